```python
import math
import jax, jax.numpy as jnp
from jax import lax
import numpy as np

D_MODEL = 1024
BATCH = 16
SEQ = 2048
DEPTH = 2

PLE_DIM = 256
ATTN_HEADS = 16
HEAD_DIM = 64
ATTN_WIDTH = ATTN_HEADS * HEAD_DIM
POOL_WINDOWS = (2, 4, 8, 16)
N_POOL_GROUPS = len(POOL_WINDOWS)
POOL_GROUP_DIM = 256
POOL_WIDTH = N_POOL_GROUPS * POOL_GROUP_DIM
MIX_WIDTH = ATTN_WIDTH + POOL_WIDTH
Q_BLOCK = 128
EPS = 1e-6
SPLIT_SIZES = (ATTN_WIDTH, ATTN_WIDTH, ATTN_WIDTH, ATTN_WIDTH, ATTN_HEADS, POOL_WIDTH, POOL_WIDTH)
IN_COLS = sum(SPLIT_SIZES)
SPLIT_POINTS = tuple(int(v) for v in np.cumsum(SPLIT_SIZES)[:-1])

kernel_name = "hymba_fox_pool_hybrid"


def rms_norm(x, g):
    xf = x.astype(jnp.float32)
    y = xf * lax.rsqrt(jnp.mean(xf * xf, axis=-1, keepdims=True) + EPS)
    return (y * g.astype(jnp.float32)).astype(x.dtype)


def multi_scale_pool(u, w_pool, pool_scale):
    B, S, _ = u.shape
    ug = u.reshape(B, S, N_POOL_GROUPS, POOL_GROUP_DIM).astype(jnp.float32)
    cs = jnp.cumsum(ug, axis=1)
    pos = jnp.arange(1, S + 1, dtype=jnp.float32)
    outs = []
    for g, w in enumerate(POOL_WINDOWS):
        c = cs[:, :, g]
        shifted = jnp.pad(c, ((0, 0), (w, 0), (0, 0)))[:, :S]
        count = jnp.minimum(pos, float(w))[None, :, None]
        outs.append((c - shifted) / count - ug[:, :, g])
    pooled = jnp.stack(outs, axis=2).astype(u.dtype)
    mixed = jnp.einsum('bsgc,gcd->bsgd', pooled, w_pool)
    return mixed.reshape(B, S, POOL_WIDTH) * pool_scale


def forgetting_attention(q, k, v, log_f):
    B, S, H, Dh = q.shape
    scale = 1.0 / math.sqrt(Dh)
    c = jnp.cumsum(log_f, axis=1).transpose(0, 2, 1)
    qh = q.transpose(0, 2, 1, 3)
    kh = k.transpose(0, 2, 1, 3)
    vh = v.transpose(0, 2, 1, 3)
    tri = jnp.tril(jnp.ones((Q_BLOCK, Q_BLOCK), dtype=bool))
    outs = []
    for i in range(S // Q_BLOCK):
        q0 = i * Q_BLOCK
        end = q0 + Q_BLOCK
        qb = qh[:, :, q0:end]
        kb = kh[:, :, :end]
        vb = vh[:, :, :end]
        s = jnp.einsum('bhqd,bhkd->bhqk', qb, kb,
                       preferred_element_type=jnp.float32) * scale
        s = s + c[:, :, q0:end, None] - c[:, :, None, :end]
        mask = jnp.concatenate([jnp.ones((Q_BLOCK, q0), dtype=bool), tri], axis=1)
        s = jnp.where(mask[None, None], s, -jnp.inf)
        pr = jax.nn.softmax(s, axis=-1)
        outs.append(jnp.einsum('bhqk,bhkd->bhqd', pr.astype(v.dtype), vb))
    o = jnp.concatenate(outs, axis=2)
    return o.transpose(0, 2, 1, 3)


def _fwd_setup_inputs(seed: int = 0) -> dict:
    key = jax.random.key(seed)
    ks = jax.random.split(key, 12)
    f32 = jnp.float32
    x = jax.random.normal(ks[0], (BATCH, SEQ, D_MODEL), f32)
    p = jax.random.normal(ks[1], (DEPTH, BATCH, SEQ, PLE_DIM), f32)
    norm_pre = 1.0 + 0.05 * jax.random.normal(ks[2], (DEPTH, D_MODEL), f32)
    norm_post = 1.0 + 0.05 * jax.random.normal(ks[3], (DEPTH, D_MODEL), f32)
    w_in = jax.random.normal(ks[4], (DEPTH, D_MODEL, IN_COLS), f32) * D_MODEL ** -0.5
    b_f = 3.0 + 0.5 * jax.random.normal(ks[5], (DEPTH, ATTN_HEADS), f32)
    w_pool = jax.random.normal(ks[6], (DEPTH, N_POOL_GROUPS, POOL_GROUP_DIM, POOL_GROUP_DIM), f32) * POOL_GROUP_DIM ** -0.5
    pool_scale = 1.0 + 0.1 * jax.random.normal(ks[7], (DEPTH, POOL_WIDTH), f32)
    w_out = jax.random.normal(ks[8], (DEPTH, MIX_WIDTH, D_MODEL), f32) * MIX_WIDTH ** -0.5
    w_pg = jax.random.normal(ks[9], (DEPTH, D_MODEL, D_MODEL), f32) * D_MODEL ** -0.5
    w_pe = jax.random.normal(ks[10], (DEPTH, PLE_DIM, D_MODEL), f32) * (0.5 * PLE_DIM ** -0.5)
    return {"x": x, "p": p, "norm_pre": norm_pre, "norm_post": norm_post,
            "w_in": w_in, "b_f": b_f, "w_pool": w_pool, "pool_scale": pool_scale,
            "w_out": w_out, "w_pg": w_pg, "w_pe": w_pe}


def _fwd_reference(x, p, norm_pre, norm_post, w_in, b_f, w_pool, pool_scale, w_out, w_pg, w_pe):
    B, S, _ = x.shape
    h = x
    for i in range(DEPTH):
        hn = rms_norm(h, norm_pre[i])
        proj = hn @ w_in[i]
        q, k, v, z_attn, f_logit, u_pool, z_pool = jnp.split(proj, SPLIT_POINTS, axis=-1)
        log_f = jax.nn.log_sigmoid(f_logit.astype(jnp.float32) + b_f[i].astype(jnp.float32))
        attn = forgetting_attention(
            q.reshape(B, S, ATTN_HEADS, HEAD_DIM),
            k.reshape(B, S, ATTN_HEADS, HEAD_DIM),
            v.reshape(B, S, ATTN_HEADS, HEAD_DIM), log_f)
        attn = attn.reshape(B, S, ATTN_WIDTH) * jax.nn.silu(z_attn)
        pool = multi_scale_pool(u_pool, w_pool[i], pool_scale[i]) * jax.nn.silu(z_pool)
        mix = jnp.concatenate([attn, pool], axis=-1) @ w_out[i]
        h = h + rms_norm(mix, norm_post[i])
        gate = jax.nn.sigmoid(h @ w_pg[i])
        h = h + gate * (p[i] @ w_pe[i])
    return h


import jax as _jax
import jax.numpy as _jnp

TWIN_FORMAT = 'train_step'
FWD_PARAMS = ['x', 'p', 'norm_pre', 'norm_post', 'w_in', 'b_f', 'w_pool', 'pool_scale', 'w_out', 'w_pg', 'w_pe']
TWIN_WEIGHTS = ['norm_pre', 'norm_post', 'w_in', 'b_f', 'w_pool', 'pool_scale', 'w_out', 'w_pg', 'w_pe']
TWIN_DIFF_INPUT = 'x'
TWIN_INPUTS = ['x', 'p', 'norm_pre', 'norm_post', 'w_in', 'b_f', 'w_pool', 'pool_scale', 'w_out', 'w_pg', 'w_pe', 'loss_target', 'm_norm_pre', 'm_norm_post', 'm_w_in', 'm_b_f', 'm_w_pool', 'm_pool_scale', 'm_w_out', 'm_w_pg', 'm_w_pe', 'v_norm_pre', 'v_norm_post', 'v_w_in', 'v_b_f', 'v_w_pool', 'v_pool_scale', 'v_w_out', 'v_w_pg', 'v_w_pe']
TWIN_OUTPUTS = ['loss', 'grad_x', 'grad_norm_pre', 'grad_norm_post', 'grad_w_in', 'grad_b_f', 'grad_w_pool', 'grad_pool_scale', 'grad_w_out', 'grad_w_pg', 'grad_w_pe', 'delta_norm_pre', 'delta_norm_post', 'delta_w_in', 'delta_b_f', 'delta_w_pool', 'delta_pool_scale', 'delta_w_out', 'delta_w_pg', 'delta_w_pe', 'new_m_norm_pre', 'new_m_norm_post', 'new_m_w_in', 'new_m_b_f', 'new_m_w_pool', 'new_m_pool_scale', 'new_m_w_out', 'new_m_w_pg', 'new_m_w_pe', 'new_v_norm_pre', 'new_v_norm_post', 'new_v_w_in', 'new_v_b_f', 'new_v_w_pool', 'new_v_pool_scale', 'new_v_w_out', 'new_v_w_pg', 'new_v_w_pe']
TWIN_LEAF_KINDS = {'loss': 'loss', 'grad_x': 'grad_x', 'grad_norm_pre': 'grad_w', 'grad_norm_post': 'grad_w', 'grad_w_in': 'grad_w', 'grad_b_f': 'grad_w', 'grad_w_pool': 'grad_w', 'grad_pool_scale': 'grad_w', 'grad_w_out': 'grad_w', 'grad_w_pg': 'grad_w', 'grad_w_pe': 'grad_w', 'delta_norm_pre': 'delta_w', 'delta_norm_post': 'delta_w', 'delta_w_in': 'delta_w', 'delta_b_f': 'delta_w', 'delta_w_pool': 'delta_w', 'delta_pool_scale': 'delta_w', 'delta_w_out': 'delta_w', 'delta_w_pg': 'delta_w', 'delta_w_pe': 'delta_w', 'new_m_norm_pre': 'new_m', 'new_m_norm_post': 'new_m', 'new_m_w_in': 'new_m', 'new_m_b_f': 'new_m', 'new_m_w_pool': 'new_m', 'new_m_pool_scale': 'new_m', 'new_m_w_out': 'new_m', 'new_m_w_pg': 'new_m', 'new_m_w_pe': 'new_m', 'new_v_norm_pre': 'new_v', 'new_v_norm_post': 'new_v', 'new_v_w_in': 'new_v', 'new_v_b_f': 'new_v', 'new_v_w_pool': 'new_v', 'new_v_pool_scale': 'new_v', 'new_v_w_out': 'new_v', 'new_v_w_pg': 'new_v', 'new_v_w_pe': 'new_v'}


def _forward(args):
    return _fwd_reference(*[args[k] for k in FWD_PARAMS])


def _output_shape():
    out = _jax.eval_shape(lambda: _forward(_fwd_setup_inputs(0)))
    return out.shape, out.dtype

N_MICROBATCH = 1
ADAM_LR = 0.001
ADAM_B1 = 0.9
ADAM_B2 = 0.999
ADAM_EPS = 1e-08
ADAM_WD = 0.01
ADAM_STEP = 10
PER_EXAMPLE_BATCH_AXIS = {'x': 0, 'p': 1, 'loss_target': 0}
SHARED_INPUTS = []
_WEIGHT_DTYPES = {'norm_pre': _jnp.float32, 'norm_post': _jnp.float32, 'w_in': _jnp.float32, 'b_f': _jnp.float32, 'w_pool': _jnp.float32, 'pool_scale': _jnp.float32, 'w_out': _jnp.float32, 'w_pg': _jnp.float32, 'w_pe': _jnp.float32}
MOMENT_SCALE = {'norm_pre': 6.414080e-01, 'norm_post': 3.204787e+01, 'w_in': 2.533954e-01, 'b_f': 6.012361e-01, 'w_pool': 4.363067e-01, 'pool_scale': 4.697627e-01, 'w_out': 4.678998e-01, 'w_pg': 6.343538e-02, 'w_pe': 3.054865e-01}


def _to_microbatches(a, axis):
    t = _jnp.moveaxis(a, axis, 0)
    t = t.reshape((N_MICROBATCH, t.shape[0] // N_MICROBATCH) + t.shape[1:])
    return _jnp.moveaxis(t, 1, axis + 1)


def setup_inputs(seed: int = 0) -> dict:
    inp = _fwd_setup_inputs(seed)
    key = _jax.random.fold_in(_jax.random.key(seed), 7919)
    shape, _ = _output_shape()
    out = dict(inp)
    out["loss_target"] = _jax.random.normal(_jax.random.fold_in(key, 0), shape, _jnp.float32)
    for i, name in enumerate(TWIN_WEIGHTS):
        w = inp[name].astype(_jnp.float32)
        if MOMENT_SCALE is None:
            s = _jnp.sqrt(_jnp.mean(_jnp.square(w)) + 1e-30)
        else:
            s = MOMENT_SCALE[name]
        km, kv = _jax.random.split(_jax.random.fold_in(key, i + 1))
        out[name] = w
        out["m_" + name] = s * _jax.random.normal(km, w.shape, _jnp.float32)
        out["v_" + name] = (s * s) * _jax.random.uniform(kv, w.shape, _jnp.float32, 0.5, 1.5)
    if N_MICROBATCH > 1:
        for name, axis in PER_EXAMPLE_BATCH_AXIS.items():
            out[name] = _to_microbatches(out[name], axis)
    return {'x': out['x'], 'p': out['p'], 'norm_pre': out['norm_pre'], 'norm_post': out['norm_post'], 'w_in': out['w_in'], 'b_f': out['b_f'], 'w_pool': out['w_pool'], 'pool_scale': out['pool_scale'], 'w_out': out['w_out'], 'w_pg': out['w_pg'], 'w_pe': out['w_pe'], 'loss_target': out['loss_target'], 'm_norm_pre': out['m_norm_pre'], 'm_norm_post': out['m_norm_post'], 'm_w_in': out['m_w_in'], 'm_b_f': out['m_b_f'], 'm_w_pool': out['m_w_pool'], 'm_pool_scale': out['m_pool_scale'], 'm_w_out': out['m_w_out'], 'm_w_pg': out['m_w_pg'], 'm_w_pe': out['m_w_pe'], 'v_norm_pre': out['v_norm_pre'], 'v_norm_post': out['v_norm_post'], 'v_w_in': out['v_w_in'], 'v_b_f': out['v_b_f'], 'v_w_pool': out['v_w_pool'], 'v_pool_scale': out['v_pool_scale'], 'v_w_out': out['v_w_out'], 'v_w_pg': out['v_w_pg'], 'v_w_pe': out['v_w_pe']}


def _loss(weights, diff, rest, loss_target):
    with _jax.named_scope("forward"):
        args = {**rest, TWIN_DIFF_INPUT: diff, **{k: w.astype(_WEIGHT_DTYPES[k]) for k, w in weights.items()}}
        y = _forward(args)
    with _jax.named_scope("loss_head"):
        err = _jnp.square(y.astype(_jnp.float32) - loss_target)
        return 0.5 * _jnp.sum(_jnp.mean(err, axis=-1)) if err.ndim else 0.5 * err


def _adamw(w, g, m, v):
    m = ADAM_B1 * m + (1.0 - ADAM_B1) * g
    v = ADAM_B2 * v + (1.0 - ADAM_B2) * _jnp.square(g)
    m_hat = m / (1.0 - ADAM_B1 ** ADAM_STEP)
    v_hat = v / (1.0 - ADAM_B2 ** ADAM_STEP)
    delta = -ADAM_LR * (m_hat / (_jnp.sqrt(v_hat) + ADAM_EPS) + ADAM_WD * w)
    return delta, m, v


def reference(x, p, norm_pre, norm_post, w_in, b_f, w_pool, pool_scale, w_out, w_pg, w_pe, loss_target, m_norm_pre, m_norm_post, m_w_in, m_b_f, m_w_pool, m_pool_scale, m_w_out, m_w_pg, m_w_pe, v_norm_pre, v_norm_post, v_w_in, v_b_f, v_w_pool, v_pool_scale, v_w_out, v_w_pg, v_w_pe):
    given = dict(x=x, p=p, norm_pre=norm_pre, norm_post=norm_post, w_in=w_in, b_f=b_f, w_pool=w_pool, pool_scale=pool_scale, w_out=w_out, w_pg=w_pg, w_pe=w_pe, loss_target=loss_target, m_norm_pre=m_norm_pre, m_norm_post=m_norm_post, m_w_in=m_w_in, m_b_f=m_b_f, m_w_pool=m_w_pool, m_pool_scale=m_pool_scale, m_w_out=m_w_out, m_w_pg=m_w_pg, m_w_pe=m_w_pe, v_norm_pre=v_norm_pre, v_norm_post=v_norm_post, v_w_in=v_w_in, v_b_f=v_b_f, v_w_pool=v_w_pool, v_pool_scale=v_pool_scale, v_w_out=v_w_out, v_w_pg=v_w_pg, v_w_pe=v_w_pe)
    weights = {n: given[n] for n in TWIN_WEIGHTS}
    shared = {n: given[n] for n in SHARED_INPUTS}
    per_example = {n: given[n] for n in ['x', 'p']}
    grad_fn = _jax.value_and_grad(_loss, argnums=(0, 1))

    def one_microbatch(ex, loss_target):
        ex = dict(ex)
        diff = ex.pop(TWIN_DIFF_INPUT)
        return grad_fn(weights, diff, {**shared, **ex}, loss_target)

    if N_MICROBATCH == 1:
        loss, (grad_w, grad_x) = one_microbatch(per_example, given["loss_target"])
    else:
        def body(carry, xs):
            loss_sum, grad_sum = carry
            l_k, (gw_k, gx_k) = one_microbatch(xs[0], xs[1])
            with _jax.named_scope("update"):
                return (loss_sum + l_k, _jax.tree.map(_jnp.add, grad_sum, gw_k)), gx_k

        init = (_jnp.zeros((), _jnp.float32), _jax.tree.map(_jnp.zeros_like, weights))
        (loss, grad_w), grad_x = _jax.lax.scan(body, init, (per_example, given["loss_target"]))
    with _jax.named_scope("update"):
        delta_w, new_m, new_v = {}, {}, {}
        for n in TWIN_WEIGHTS:
            delta_w[n], new_m[n], new_v[n] = _adamw(weights[n], grad_w[n], given["m_" + n], given["v_" + n])
    return (loss, grad_x, *[grad_w[n] for n in TWIN_WEIGHTS], *[delta_w[n] for n in TWIN_WEIGHTS],
            *[new_m[n] for n in TWIN_WEIGHTS], *[new_v[n] for n in TWIN_WEIGHTS])
```

```python
import functools
import math

import jax
import jax.numpy as jnp
from jax import lax
from jax.experimental import pallas as pl
from jax.experimental.pallas import tpu as pltpu

N_DEV = 8
MESH_AXES = ("x", "y", "c")
HEAD_DIM = 64
LANES = 128
N_POOL_GROUPS = 4
EPS = 1e-6
ADAM_LR = 0.001
ADAM_B1 = 0.9
ADAM_B2 = 0.999
ADAM_EPS = 1e-08
ADAM_WD = 0.01
ADAM_STEP = 10
VMEM_LIMIT_BYTES = 56 * 1024 * 1024
F32 = jnp.float32
BF16 = jnp.bfloat16
NEG_INF = float("-inf")


def _call(body, *, name, grid, in_specs, out_specs, out_shape, scratch_shapes=(), semantics=None):
    return pl.pallas_call(
        body, name=name, grid=grid, in_specs=in_specs, out_specs=out_specs, out_shape=out_shape,
        scratch_shapes=list(scratch_shapes),
        compiler_params=pltpu.CompilerParams(dimension_semantics=semantics,
                                             vmem_limit_bytes=VMEM_LIMIT_BYTES))


def _sigmoid(z):
    return 1.0 / (1.0 + jnp.exp(-z))


def _dot(a, b, dims):
    return lax.dot_general(a, b, (dims, ((), ())), preferred_element_type=F32)


NN = ((1,), (0,))
NT = ((1,), (1,))
TN = ((0,), (0,))


def _exchange(items, name):
    n = len(items)
    modes = [s for _, s in items]
    out_shapes = []
    for a, s in items:
        shp = a.shape[1:] if s else a.shape
        out_shapes.append(jax.ShapeDtypeStruct((N_DEV,) + tuple(shp), a.dtype))

    def body(*refs):
        ins = refs[:n]
        outs = refs[n:2 * n]
        send_sems, recv_sems, local_sems = refs[2 * n:]
        x, y, c = (lax.axis_index(ax) for ax in MESH_AXES)
        me = 4 * x + 2 * y + c
        started = []
        for i in range(n):
            mine = ins[i].at[me] if modes[i] else ins[i]
            loc = pltpu.make_async_copy(mine, outs[i].at[me], local_sems.at[i])
            loc.start()
            started.append(loc)
        remote = []
        for k in range(1, N_DEV):
            px = x ^ ((k >> 2) & 1)
            py = y ^ ((k >> 1) & 1)
            pc = c ^ (k & 1)
            peer = me ^ k
            for i in range(n):
                src = ins[i].at[peer] if modes[i] else ins[i]
                cp = pltpu.make_async_remote_copy(
                    src_ref=src, dst_ref=outs[i].at[me],
                    send_sem=send_sems.at[i, k - 1], recv_sem=recv_sems.at[i, k - 1],
                    device_id=(px, py, pc), device_id_type=pl.DeviceIdType.MESH)
                cp.start()
                remote.append(cp)
        for cp in remote:
            cp.wait()
        for loc in started:
            loc.wait()

    hbm = pl.BlockSpec(memory_space=pltpu.HBM)
    return pl.pallas_call(
        body, name=name, out_shape=out_shapes,
        in_specs=[hbm] * n, out_specs=[hbm] * n,
        scratch_shapes=[pltpu.SemaphoreType.DMA((n, N_DEV - 1)),
                        pltpu.SemaphoreType.DMA((n, N_DEV - 1)),
                        pltpu.SemaphoreType.DMA((n,))],
    )(*[a for a, _ in items])


def _matmul(pairs, mode, out_dtype, name, tm=512, tn=1024, tk=512):
    dims = {"nn": NN, "nt": NT, "tn": TN}[mode]
    a0, b0 = pairs[0]
    m_dim = a0.shape[1] if mode == "tn" else a0.shape[0]
    n_dim = b0.shape[0] if mode == "nt" else b0.shape[1]
    tm = min(tm, m_dim)
    tn = min(tn, n_dim)
    segs = []
    off = 0
    for a, _ in pairs:
        k_dim = a.shape[0] if mode == "tn" else a.shape[1]
        t = min(tk, k_dim)
        segs.append((off, k_dim // t, t))
        off += k_dim // t
    nk = off
    n_pairs = len(pairs)

    in_specs = []
    for (o, cnt, t) in segs:
        def kc(kk, o=o, cnt=cnt):
            return jnp.clip(kk - o, 0, cnt - 1)
        if mode == "tn":
            in_specs.append(pl.BlockSpec((t, tm), lambda i, j, kk, kc=kc: (kc(kk), i)))
        else:
            in_specs.append(pl.BlockSpec((tm, t), lambda i, j, kk, kc=kc: (i, kc(kk))))
        if mode == "nt":
            in_specs.append(pl.BlockSpec((tn, t), lambda i, j, kk, kc=kc: (j, kc(kk))))
        else:
            in_specs.append(pl.BlockSpec((t, tn), lambda i, j, kk, kc=kc: (kc(kk), j)))

    def body(*refs):
        out_ref = refs[2 * n_pairs]
        acc = refs[2 * n_pairs + 1]
        kk = pl.program_id(2)

        @pl.when(kk == 0)
        def _():
            acc[...] = jnp.zeros_like(acc)

        for idx, (o, cnt, _) in enumerate(segs):
            @pl.when((kk >= o) & (kk < o + cnt))
            def _(idx=idx):
                acc[...] += _dot(refs[2 * idx][...], refs[2 * idx + 1][...], dims)

        @pl.when(kk == nk - 1)
        def _():
            out_ref[...] = acc[...].astype(out_dtype)

    flat = [t for pr in pairs for t in pr]
    return _call(body, name=name, grid=(m_dim // tm, n_dim // tn, nk), in_specs=in_specs,
                 out_specs=pl.BlockSpec((tm, tn), lambda i, j, kk: (i, j)),
                 out_shape=jax.ShapeDtypeStruct((m_dim, n_dim), out_dtype),
                 scratch_shapes=[pltpu.VMEM((tm, tn), F32)],
                 semantics=("parallel", "parallel", "arbitrary"))(*flat)


def _row_tile(t):
    return min(256, t)


def _rms_fwd(h, g, name):
    t, d = h.shape
    tt = _row_tile(t)

    def body(h_ref, g_ref, o_ref):
        hv = h_ref[...]
        r = lax.rsqrt(jnp.mean(hv * hv, axis=-1, keepdims=True) + EPS)
        o_ref[...] = (hv * r * g_ref[...]).astype(BF16)

    row = pl.BlockSpec((tt, d), lambda i: (i, 0))
    vec = pl.BlockSpec((1, d), lambda i: (0, 0))
    return _call(body, name=name, grid=(t // tt,), in_specs=[row, vec], out_specs=row,
                 out_shape=jax.ShapeDtypeStruct((t, d), BF16), semantics=("parallel",))(h, g)


def _post_fwd(h, mix, g, name):
    t, d = h.shape
    tt = _row_tile(t)

    def body(h_ref, m_ref, g_ref, o_ref, ob_ref):
        mv = m_ref[...]
        r = lax.rsqrt(jnp.mean(mv * mv, axis=-1, keepdims=True) + EPS)
        h1 = h_ref[...] + mv * r * g_ref[...]
        o_ref[...] = h1
        ob_ref[...] = h1.astype(BF16)

    row = pl.BlockSpec((tt, d), lambda i: (i, 0))
    vec = pl.BlockSpec((1, d), lambda i: (0, 0))
    return _call(body, name=name, grid=(t // tt,), in_specs=[row, row, vec], out_specs=[row, row],
                 out_shape=[jax.ShapeDtypeStruct((t, d), F32), jax.ShapeDtypeStruct((t, d), BF16)],
                 semantics=("parallel",))(h, mix, g)


def _ple_fwd(h1, gpre, e, name):
    t, d = h1.shape
    tt = _row_tile(t)

    def body(h_ref, g_ref, e_ref, o_ref):
        o_ref[...] = h_ref[...] + _sigmoid(g_ref[...]) * e_ref[...]

    row = pl.BlockSpec((tt, d), lambda i: (i, 0))
    return _call(body, name=name, grid=(t // tt,), in_specs=[row, row, row], out_specs=row,
                 out_shape=jax.ShapeDtypeStruct((t, d), F32), semantics=("parallel",))(h1, gpre, e)


def _loss_bwd(y, target, name):
    t, d = y.shape
    tt = _row_tile(t)

    def body(y_ref, t_ref, dy_ref, s_ref):
        @pl.when(pl.program_id(0) == 0)
        def _():
            s_ref[...] = jnp.zeros_like(s_ref)
        diff = y_ref[...] - t_ref[...]
        dy_ref[...] = diff * (1.0 / d)
        s_ref[...] += jnp.sum(diff * diff, axis=0, keepdims=True)

    row = pl.BlockSpec((tt, d), lambda i: (i, 0))
    vec = pl.BlockSpec((1, d), lambda i: (0, 0))
    return _call(body, name=name, grid=(t // tt,), in_specs=[row, row], out_specs=[row, vec],
                 out_shape=[jax.ShapeDtypeStruct((t, d), F32), jax.ShapeDtypeStruct((1, d), F32)],
                 semantics=("arbitrary",))(y, target)


def _ple_bwd(dh2, gpre, e, name):
    t, d = dh2.shape
    tt = _row_tile(t)

    def body(d_ref, g_ref, e_ref, de_ref, dp_ref):
        gate = _sigmoid(g_ref[...])
        dv = d_ref[...]
        de_ref[...] = (dv * gate).astype(BF16)
        dp_ref[...] = (dv * e_ref[...] * gate * (1.0 - gate)).astype(BF16)

    row = pl.BlockSpec((tt, d), lambda i: (i, 0))
    return _call(body, name=name, grid=(t // tt,), in_specs=[row, row, row], out_specs=[row, row],
                 out_shape=[jax.ShapeDtypeStruct((t, d), BF16)] * 2,
                 semantics=("parallel",))(dh2, gpre, e)


def _post_bwd(dh2, t1, mix, g, name):
    t, d = dh2.shape
    tt = _row_tile(t)

    def body(d_ref, t_ref, m_ref, g_ref, dh_ref, dm_ref, dg_ref):
        @pl.when(pl.program_id(0) == 0)
        def _():
            dg_ref[...] = jnp.zeros_like(dg_ref)
        dh1 = d_ref[...] + t_ref[...]
        mv = m_ref[...]
        r = lax.rsqrt(jnp.mean(mv * mv, axis=-1, keepdims=True) + EPS)
        dh_ref[...] = dh1
        dg_ref[...] += jnp.sum(dh1 * mv * r, axis=0, keepdims=True)
        w = dh1 * g_ref[...]
        dot = jnp.mean(w * mv, axis=-1, keepdims=True)
        dm_ref[...] = (r * w - mv * (r * r * r) * dot).astype(BF16)

    row = pl.BlockSpec((tt, d), lambda i: (i, 0))
    vec = pl.BlockSpec((1, d), lambda i: (0, 0))
    return _call(body, name=name, grid=(t // tt,), in_specs=[row, row, row, vec],
                 out_specs=[row, row, vec],
                 out_shape=[jax.ShapeDtypeStruct((t, d), F32), jax.ShapeDtypeStruct((t, d), BF16),
                            jax.ShapeDtypeStruct((1, d), F32)],
                 semantics=("arbitrary",))(dh2, t1, mix, g)


def _pre_bwd(h, dhn, dh1, g, name):
    t, d = h.shape
    tt = _row_tile(t)

    def body(h_ref, dn_ref, d1_ref, g_ref, dh_ref, dg_ref):
        @pl.when(pl.program_id(0) == 0)
        def _():
            dg_ref[...] = jnp.zeros_like(dg_ref)
        hv = h_ref[...]
        dn = dn_ref[...]
        r = lax.rsqrt(jnp.mean(hv * hv, axis=-1, keepdims=True) + EPS)
        dg_ref[...] += jnp.sum(dn * hv * r, axis=0, keepdims=True)
        w = dn * g_ref[...]
        dot = jnp.mean(w * hv, axis=-1, keepdims=True)
        dh_ref[...] = d1_ref[...] + r * w - hv * (r * r * r) * dot

    row = pl.BlockSpec((tt, d), lambda i: (i, 0))
    vec = pl.BlockSpec((1, d), lambda i: (0, 0))
    return _call(body, name=name, grid=(t // tt,), in_specs=[row, row, row, vec],
                 out_specs=[row, vec],
                 out_shape=[jax.ShapeDtypeStruct((t, d), F32), jax.ShapeDtypeStruct((1, d), F32)],
                 semantics=("arbitrary",))(h, dhn, dh1, g)


def _split3(v):
    hi = v.astype(BF16)
    r1 = v - hi.astype(F32)
    mid = r1.astype(BF16)
    lo = (r1 - mid.astype(F32)).astype(BF16)
    return hi, mid, lo


def _gates_fwd(fl, bf, name):
    b, s, _ = fl.shape

    def body(f_ref, b_ref, c_ref):
        xv = f_ref[0] + b_ref[...]
        lf = jnp.minimum(xv, 0.0) - jnp.log(1.0 + jnp.exp(-jnp.abs(xv)))
        src = lax.broadcasted_iota(jnp.int32, (s, s), 0)
        dst = lax.broadcasted_iota(jnp.int32, (s, s), 1)
        upper = (src <= dst).astype(BF16)
        acc = jnp.zeros((LANES, s), F32)
        for part in _split3(lf):
            acc = acc + _dot(part, upper, TN)
        c_ref[0] = acc

    return _call(body, name=name, grid=(b,),
                 in_specs=[pl.BlockSpec((1, s, LANES), lambda i: (i, 0, 0)),
                           pl.BlockSpec((1, LANES), lambda i: (0, 0))],
                 out_specs=pl.BlockSpec((1, LANES, s), lambda i: (i, 0, 0)),
                 out_shape=jax.ShapeDtypeStruct((b, LANES, s), F32),
                 semantics=("parallel",))(fl, bf)


def _gates_bwd(dct, fl, bf, heads, name):
    b, s, _ = fl.shape

    def body(d_ref, f_ref, b_ref, o_ref, db_ref):
        xv = f_ref[0] + b_ref[...]
        dst = lax.broadcasted_iota(jnp.int32, (s, s), 0)
        src = lax.broadcasted_iota(jnp.int32, (s, s), 1)
        later = (src >= dst).astype(BF16)
        dlf = jnp.zeros((s, LANES), F32)
        for part in _split3(d_ref[0]):
            dlf = dlf + _dot(later, part, NT)
        lane = lax.broadcasted_iota(jnp.int32, (s, LANES), 1)
        dfl = jnp.where(lane < heads, dlf * _sigmoid(-xv), 0.0)
        o_ref[0] = dfl.astype(BF16)
        db_ref[0] = jnp.sum(dfl, axis=0, keepdims=True)

    return _call(body, name=name, grid=(b,),
                 in_specs=[pl.BlockSpec((1, LANES, s), lambda i: (i, 0, 0)),
                           pl.BlockSpec((1, s, LANES), lambda i: (i, 0, 0)),
                           pl.BlockSpec((1, LANES), lambda i: (0, 0))],
                 out_specs=[pl.BlockSpec((1, s, LANES), lambda i: (i, 0, 0)),
                            pl.BlockSpec((1, 1, LANES), lambda i: (i, 0, 0))],
                 out_shape=[jax.ShapeDtypeStruct((b, s, LANES), BF16),
                            jax.ShapeDtypeStruct((b, 1, LANES), F32)],
                 semantics=("parallel",))(dct, fl, bf)


def _attn_tile(s):
    return min(256, s)


def _row_to_col(row, eye):
    return jnp.sum(jnp.where(eye, row, 0.0), axis=1, keepdims=True)


def _col_to_row(col, eye):
    return jnp.sum(jnp.where(eye, col, 0.0), axis=0, keepdims=True)


def _attn_fwd(pa, c4, name):
    b, s, a4 = pa.shape
    a = a4 // 4
    pairs = a // LANES
    tq = _attn_tile(s)
    scale = 1.0 / math.sqrt(HEAD_DIM)

    def body(q_ref, k_ref, v_ref, z_ref, c_ref, o_ref, g_ref, lse_ref):
        qi = pl.program_id(2)
        q0 = pl.multiple_of(qi * tq, tq)
        qv = q_ref[0]
        lane = lax.broadcasted_iota(jnp.int32, (tq, LANES), 1)
        ri = lax.broadcasted_iota(jnp.int32, (tq, tq), 0)
        ci = lax.broadcasted_iota(jnp.int32, (tq, tq), 1)
        eye = ri == ci
        causal = ci <= ri
        heads_out = []
        for j in range(2):
            in_head = (lane >= HEAD_DIM * j) & (lane < HEAD_DIM * (j + 1))
            qm = jnp.where(in_head, qv, jnp.zeros_like(qv))
            cq = _row_to_col(c_ref[0, 0, j:j + 1, pl.ds(q0, tq)], eye)

            def step(kj, carry, masked, j=j, qm=qm, cq=cq):
                m, l, acc = carry
                k0 = pl.multiple_of(kj * tq, tq)
                kb = k_ref[0, pl.ds(k0, tq), :]
                vb = v_ref[0, pl.ds(k0, tq), :]
                ck = c_ref[0, 0, j:j + 1, pl.ds(k0, tq)]
                sc = _dot(qm, kb, NT) * scale + cq - ck
                if masked:
                    sc = jnp.where(causal, sc, NEG_INF)
                m_new = jnp.maximum(m, jnp.max(sc, axis=1, keepdims=True))
                alpha = jnp.exp(m - m_new)
                pr = jnp.exp(sc - m_new)
                l_new = alpha * l + jnp.sum(pr, axis=1, keepdims=True)
                acc_new = alpha * acc + _dot(pr.astype(BF16), vb, NN)
                return m_new, l_new, acc_new

            init = (jnp.full((tq, 1), NEG_INF, F32), jnp.zeros((tq, 1), F32),
                    jnp.zeros((tq, LANES), F32))
            carry = lax.fori_loop(0, qi, lambda kj, cr: step(kj, cr, False), init)
            m, l, acc = step(qi, carry, True)
            heads_out.append(acc / l)
            lse_ref[0, 0, j:j + 1, :] = _col_to_row(m + jnp.log(l), eye)
        ov = jnp.where(lane < HEAD_DIM, heads_out[0], heads_out[1])
        o_ref[0] = ov.astype(BF16)
        zv = z_ref[0].astype(F32)
        g_ref[0] = (ov * zv * _sigmoid(zv)).astype(BF16)

    blk = lambda col: (lambda bi, hp, qi: (bi, qi, 4 * hp + col))
    full = lambda col: (lambda bi, hp, qi: (bi, 0, 4 * hp + col))
    return _call(
        body, name=name, grid=(b, pairs, s // tq),
        in_specs=[pl.BlockSpec((1, tq, LANES), blk(0)),
                  pl.BlockSpec((1, s, LANES), full(1)),
                  pl.BlockSpec((1, s, LANES), full(2)),
                  pl.BlockSpec((1, tq, LANES), blk(3)),
                  pl.BlockSpec((1, 1, 2, s), lambda bi, hp, qi: (bi, hp, 0, 0))],
        out_specs=[pl.BlockSpec((1, tq, LANES), lambda bi, hp, qi: (bi, qi, hp)),
                   pl.BlockSpec((1, tq, LANES), lambda bi, hp, qi: (bi, qi, hp)),
                   pl.BlockSpec((1, 1, 2, tq), lambda bi, hp, qi: (bi, hp, 0, qi))],
        out_shape=[jax.ShapeDtypeStruct((b, s, a), BF16), jax.ShapeDtypeStruct((b, s, a), BF16),
                   jax.ShapeDtypeStruct((b, pairs, 2, s), F32)],
        semantics=("parallel", "parallel", "arbitrary"))(pa, pa, pa, pa, c4)


def _attn_bwd(pa, o, dcat, c4, lse, name):
    b, s, a4 = pa.shape
    a = a4 // 4
    pairs = a // LANES
    tb = _attn_tile(s)
    nb = s // tb
    scale = 1.0 / math.sqrt(HEAD_DIM)

    def body(q_ref, k_ref, v_ref, z_ref, o_ref, da_ref, c_ref, lse_ref, dp_ref, dc_ref,
             dq_acc, do_s, dsum_s, dcq_s):
        lane = lax.broadcasted_iota(jnp.int32, (tb, LANES), 1)
        ri = lax.broadcasted_iota(jnp.int32, (tb, tb), 0)
        ci = lax.broadcasted_iota(jnp.int32, (tb, tb), 1)
        eye = ri == ci
        causal_t = ri <= ci
        head_masks = [(lane >= HEAD_DIM * j) & (lane < HEAD_DIM * (j + 1)) for j in range(2)]

        def prologue(i, _):
            r0 = pl.multiple_of(i * tb, tb)
            rows = pl.ds(r0, tb)
            zv = z_ref[0, rows, :].astype(F32)
            dav = da_ref[0, rows, :].astype(F32)
            ov = o_ref[0, rows, :].astype(F32)
            sg = _sigmoid(zv)
            dov = dav * zv * sg
            do_s[rows, :] = dov.astype(BF16)
            dp_ref[0, rows, 3 * LANES:4 * LANES] = (
                dav * ov * sg * (1.0 + zv * (1.0 - sg))).astype(BF16)
            prod = dov * ov
            for j in range(2):
                dcol = jnp.sum(jnp.where(head_masks[j], prod, 0.0), axis=1, keepdims=True)
                dsum_s[j:j + 1, rows] = _col_to_row(dcol, eye)
            return 0

        lax.fori_loop(0, nb, prologue, 0)

        for j in range(2):
            in_head = head_masks[j]
            dq_acc[...] = jnp.zeros_like(dq_acc)
            dcq_s[...] = jnp.zeros_like(dcq_s)

            def key_block(kj, _, j=j, in_head=in_head):
                k0 = pl.multiple_of(kj * tb, tb)
                krows = pl.ds(k0, tb)
                kb = k_ref[0, krows, :]
                vb = v_ref[0, krows, :]
                kbm = jnp.where(in_head, kb, jnp.zeros_like(kb))
                vbm = jnp.where(in_head, vb, jnp.zeros_like(vb))
                ck = _row_to_col(c_ref[0, 0, j:j + 1, krows], eye)

                def query_block(qi, carry, masked):
                    dk, dv, dcs = carry
                    r0 = pl.multiple_of(qi * tb, tb)
                    qrows = pl.ds(r0, tb)
                    qb = q_ref[0, qrows, :]
                    dob = do_s[qrows, :]
                    e_row = c_ref[0, 0, j:j + 1, qrows] - lse_ref[0, 0, j:j + 1, qrows]
                    d_row = dsum_s[j:j + 1, qrows]
                    st = _dot(kbm, qb, NT) * scale + e_row - ck
                    pt = jnp.exp(st)
                    if masked:
                        pt = jnp.where(causal_t, pt, 0.0)
                    dpt = _dot(vbm, dob, NT)
                    dst = pt * (dpt - d_row)
                    dsb = dst.astype(BF16)
                    dv = dv + _dot(pt.astype(BF16), dob, NN)
                    dk = dk + _dot(dsb, qb, NN)
                    dq_acc[qrows, :] += _dot(dsb, kb, TN)
                    dcq_s[:, qrows] += jnp.sum(dst, axis=0, keepdims=True)
                    for t in range(tb // LANES):
                        dcs = dcs + dst[:, t * LANES:(t + 1) * LANES]
                    return dk, dv, dcs

                zero = jnp.zeros((tb, LANES), F32)
                carry = query_block(kj, (zero, zero, zero), True)
                dk, dv, dcs = lax.fori_loop(kj + 1, nb, lambda qi, cr: query_block(qi, cr, False),
                                            carry)
                dkb = (dk * scale).astype(BF16)
                dvb = dv.astype(BF16)
                if j == 0:
                    dp_ref[0, krows, LANES:2 * LANES] = dkb
                    dp_ref[0, krows, 2 * LANES:3 * LANES] = dvb
                else:
                    dp_ref[0, krows, LANES:2 * LANES] = jnp.where(
                        in_head, dkb, dp_ref[0, krows, LANES:2 * LANES])
                    dp_ref[0, krows, 2 * LANES:3 * LANES] = jnp.where(
                        in_head, dvb, dp_ref[0, krows, 2 * LANES:3 * LANES])
                dccol = -jnp.sum(dcs, axis=1, keepdims=True)
                dc_ref[0, 0, j:j + 1, krows] = _col_to_row(dccol, eye)
                return 0

            lax.fori_loop(0, nb, key_block, 0)
            dc_ref[0, 0, j:j + 1, :] += dcq_s[...]

            def write_dq(i, _, j=j, in_head=in_head):
                rows = pl.ds(pl.multiple_of(i * tb, tb), tb)
                dqb = (dq_acc[rows, :] * scale).astype(BF16)
                if j == 0:
                    dp_ref[0, rows, 0:LANES] = dqb
                else:
                    dp_ref[0, rows, 0:LANES] = jnp.where(in_head, dqb, dp_ref[0, rows, 0:LANES])
                return 0

            lax.fori_loop(0, nb, write_dq, 0)

    full = lambda col: (lambda bi, hp: (bi, 0, 4 * hp + col))
    return _call(
        body, name=name, grid=(b, pairs),
        in_specs=[pl.BlockSpec((1, s, LANES), full(0)),
                  pl.BlockSpec((1, s, LANES), full(1)),
                  pl.BlockSpec((1, s, LANES), full(2)),
                  pl.BlockSpec((1, s, LANES), full(3)),
                  pl.BlockSpec((1, s, LANES), lambda bi, hp: (bi, 0, hp)),
                  pl.BlockSpec((1, s, LANES), lambda bi, hp: (bi, 0, hp)),
                  pl.BlockSpec((1, 1, 2, s), lambda bi, hp: (bi, hp, 0, 0)),
                  pl.BlockSpec((1, 1, 2, s), lambda bi, hp: (bi, hp, 0, 0))],
        out_specs=[pl.BlockSpec((1, s, 4 * LANES), lambda bi, hp: (bi, 0, hp)),
                   pl.BlockSpec((1, 1, 2, s), lambda bi, hp: (bi, hp, 0, 0))],
        out_shape=[jax.ShapeDtypeStruct((b, s, a4), BF16),
                   jax.ShapeDtypeStruct((b, pairs, 2, s), F32)],
        scratch_shapes=[pltpu.VMEM((s, LANES), F32), pltpu.VMEM((s, LANES), BF16),
                        pltpu.VMEM((2, s), F32), pltpu.VMEM((1, s), F32)],
        semantics=("parallel", "parallel"))(pa, pa, pa, pa, o, dcat, c4, lse)


def _pool_tile(s):
    return min(256, s)


def _band(tb, window, shift):
    tgt = lax.broadcasted_iota(jnp.int32, (tb, tb), 0)
    src = lax.broadcasted_iota(jnp.int32, (tb, tb), 1) + shift
    return ((src <= tgt) & (src > tgt - window)).astype(BF16)


def _band_t(tb, window, shift):
    src = lax.broadcasted_iota(jnp.int32, (tb, tb), 0)
    tgt = lax.broadcasted_iota(jnp.int32, (tb, tb), 1) + shift
    return ((src <= tgt) & (src > tgt - window)).astype(BF16)


def _pool_fwd(pp, w_pool, scale, name):
    b, s, pw2 = pp.shape
    pw = pw2 // 2
    pg = pw // N_POOL_GROUPS
    tb = _pool_tile(s)
    nb = s // tb

    def body(u_ref, z_ref, w_ref, s_ref, o_ref):
        window = 2 << pl.program_id(1)
        band0 = _band(tb, window, 0)
        band1 = _band(tb, window, -tb)
        pos = lax.broadcasted_iota(jnp.int32, (tb, pg), 0)

        def block(i, _):
            rows = pl.ds(pl.multiple_of(i * tb, tb), tb)
            prev = pl.ds(pl.multiple_of(jnp.maximum(i - 1, 0) * tb, tb), tb)
            ub = u_ref[0, rows, :]
            up = u_ref[0, prev, :]
            up = jnp.where(i > 0, up, jnp.zeros_like(up))
            count = jnp.minimum(pos + i * tb + 1, window).astype(F32)
            pooled = (_dot(band0, ub, NN) + _dot(band1, up, NN)) / count - ub.astype(F32)
            mixed = _dot(pooled.astype(BF16), w_ref[0], NN) * s_ref[...]
            zv = z_ref[0, rows, :].astype(F32)
            o_ref[0, rows, :] = (mixed * zv * _sigmoid(zv)).astype(BF16)
            return 0

        lax.fori_loop(0, nb, block, 0)

    return _call(
        body, name=name, grid=(b, N_POOL_GROUPS),
        in_specs=[pl.BlockSpec((1, s, pg), lambda bi, g: (bi, 0, 2 * g)),
                  pl.BlockSpec((1, s, pg), lambda bi, g: (bi, 0, 2 * g + 1)),
                  pl.BlockSpec((1, pg, pg), lambda bi, g: (g, 0, 0)),
                  pl.BlockSpec((1, pg), lambda bi, g: (0, g))],
        out_specs=pl.BlockSpec((1, s, pg), lambda bi, g: (bi, 0, g)),
        out_shape=jax.ShapeDtypeStruct((b, s, pw), BF16),
        semantics=("parallel", "parallel"))(pp, pp, w_pool, scale)


def _pool_bwd(pp, dcat, w_pool, scale, first_block, name):
    b, s, pw2 = pp.shape
    pw = pw2 // 2
    pg = pw // N_POOL_GROUPS
    tb = _pool_tile(s)
    nb = s // tb

    def body(u_ref, z_ref, d_ref, w_ref, s_ref, dp_ref, dw_ref, ds_ref, dpool_s):
        @pl.when(pl.program_id(1) == 0)
        def _():
            dw_ref[...] = jnp.zeros_like(dw_ref)
            ds_ref[...] = jnp.zeros_like(ds_ref)

        window = 2 << pl.program_id(0)
        band0 = _band(tb, window, 0)
        band1 = _band(tb, window, -tb)
        band0_t = _band_t(tb, window, 0)
        band1_t = _band_t(tb, window, tb)
        pos = lax.broadcasted_iota(jnp.int32, (tb, pg), 0)

        def first(i, _):
            rows = pl.ds(pl.multiple_of(i * tb, tb), tb)
            prev = pl.ds(pl.multiple_of(jnp.maximum(i - 1, 0) * tb, tb), tb)
            ub = u_ref[0, rows, :]
            up = u_ref[0, prev, :]
            up = jnp.where(i > 0, up, jnp.zeros_like(up))
            count = jnp.minimum(pos + i * tb + 1, window).astype(F32)
            pooled = ((_dot(band0, ub, NN) + _dot(band1, up, NN)) / count
                      - ub.astype(F32)).astype(BF16)
            mixed = _dot(pooled, w_ref[0], NN)
            pm = mixed * s_ref[...]
            zv = z_ref[0, rows, :].astype(F32)
            sg = _sigmoid(zv)
            dpl = d_ref[0, rows, :].astype(F32)
            dpm = dpl * zv * sg
            dp_ref[0, rows, pg:2 * pg] = (dpl * pm * sg * (1.0 + zv * (1.0 - sg))).astype(BF16)
            ds_ref[...] += jnp.sum(dpm * mixed, axis=0, keepdims=True)
            dmixed = (dpm * s_ref[...]).astype(BF16)
            dw_ref[0] += _dot(pooled, dmixed, TN)
            dpool_s[rows, :] = _dot(dmixed, w_ref[0], NT)
            return 0

        lax.fori_loop(0, nb, first, 0)

        def second(i, _):
            rows = pl.ds(pl.multiple_of(i * tb, tb), tb)
            nxt_i = jnp.minimum(i + 1, nb - 1)
            nxt = pl.ds(pl.multiple_of(nxt_i * tb, tb), tb)
            count = jnp.minimum(pos + i * tb + 1, window).astype(F32)
            count_n = jnp.minimum(pos + nxt_i * tb + 1, window).astype(F32)
            dpb = dpool_s[rows, :]
            cur = (dpb / count).astype(BF16)
            nx = dpool_s[nxt, :] / count_n
            nx = jnp.where(i < nb - 1, nx, 0.0).astype(BF16)
            du = _dot(band0_t, cur, NN) + _dot(band1_t, nx, NN) - dpb
            dp_ref[0, rows, 0:pg] = du.astype(BF16)
            return 0

        lax.fori_loop(0, nb, second, 0)

    return _call(
        body, name=name, grid=(N_POOL_GROUPS, b),
        in_specs=[pl.BlockSpec((1, s, pg), lambda g, bi: (bi, 0, 2 * g)),
                  pl.BlockSpec((1, s, pg), lambda g, bi: (bi, 0, 2 * g + 1)),
                  pl.BlockSpec((1, s, pg), lambda g, bi: (bi, 0, first_block + g)),
                  pl.BlockSpec((1, pg, pg), lambda g, bi: (g, 0, 0)),
                  pl.BlockSpec((1, pg), lambda g, bi: (0, g))],
        out_specs=[pl.BlockSpec((1, s, 2 * pg), lambda g, bi: (bi, 0, g)),
                   pl.BlockSpec((1, pg, pg), lambda g, bi: (g, 0, 0)),
                   pl.BlockSpec((1, pg), lambda g, bi: (0, g))],
        out_shape=[jax.ShapeDtypeStruct((b, s, pw2), BF16),
                   jax.ShapeDtypeStruct((N_POOL_GROUPS, pg, pg), F32),
                   jax.ShapeDtypeStruct((1, pw), F32)],
        scratch_shapes=[pltpu.VMEM((s, pg), F32)],
        semantics=("parallel", "arbitrary"))(pp, pp, dcat, w_pool, scale)


def _adamw(recv, w, m, v, name):
    r, c = w.shape
    tr = min(128, r)
    c1 = 1.0 - ADAM_B1 ** ADAM_STEP
    c2 = 1.0 - ADAM_B2 ** ADAM_STEP

    def body(r_ref, w_ref, m_ref, v_ref, g_ref, d_ref, nm_ref, nv_ref):
        g = r_ref[0].astype(F32)
        for sl in range(1, N_DEV):
            g = g + r_ref[sl].astype(F32)
        mn = ADAM_B1 * m_ref[...] + (1.0 - ADAM_B1) * g
        vn = ADAM_B2 * v_ref[...] + (1.0 - ADAM_B2) * (g * g)
        m_hat = mn / c1
        v_hat = vn / c2
        g_ref[...] = g
        d_ref[...] = -ADAM_LR * (m_hat / (jnp.sqrt(v_hat) + ADAM_EPS) + ADAM_WD * w_ref[...])
        nm_ref[...] = mn
        nv_ref[...] = vn

    row = pl.BlockSpec((tr, c), lambda i: (i, 0))
    return _call(body, name=name, grid=(r // tr,),
                 in_specs=[pl.BlockSpec((N_DEV, tr, c), lambda i: (0, i, 0)), row, row, row],
                 out_specs=[row] * 4, out_shape=[jax.ShapeDtypeStruct((r, c), F32)] * 4,
                 semantics=("parallel",))(recv, w, m, v)


def _split_cols(w_full, a, heads, pw):
    sizes = (a, a, a, a, heads, pw, pw)
    out, o = [], 0
    for sz in sizes:
        out.append(w_full[:, o:o + sz])
        o += sz
    return out


def _pack_w_in(gathered, a, heads, pw):
    d = gathered.shape[1]
    w_full = jnp.transpose(gathered, (1, 0, 2)).reshape(d, -1)
    q, k, v, z, f, u, zp = _split_cols(w_full, a, heads, pw)
    pairs = a // LANES
    pg = pw // N_POOL_GROUPS
    wa = jnp.stack([t.reshape(d, pairs, LANES) for t in (q, k, v, z)], axis=2).reshape(d, 4 * a)
    wp = jnp.stack([t.reshape(d, N_POOL_GROUPS, pg) for t in (u, zp)], axis=2).reshape(d, 2 * pw)
    wf = jnp.pad(f, ((0, 0), (0, LANES - heads)))
    return wa, wp, wf


def _unpack_dw_in(dwa, dwp, dwf, a, heads, pw):
    d = dwa.shape[0]
    pairs = a // LANES
    pg = pw // N_POOL_GROUPS
    ra = dwa.reshape(d, pairs, 4, LANES)
    q, k, v, z = (ra[:, :, i, :].reshape(d, a) for i in range(4))
    rp = dwp.reshape(d, N_POOL_GROUPS, 2, pg)
    u, zp = (rp[:, :, i, :].reshape(d, pw) for i in range(2))
    full = jnp.concatenate([q, k, v, z, dwf[:, :heads], u, zp], axis=1)
    return jnp.transpose(full.reshape(d, N_DEV, -1), (1, 0, 2))


def kernel(x, p, norm_pre, norm_post, w_in, b_f, w_pool, pool_scale, w_out, w_pg, w_pe, loss_target, m_norm_pre, m_norm_post, m_w_in, m_b_f, m_w_pool, m_pool_scale, m_w_out, m_w_pg, m_w_pe, v_norm_pre, v_norm_post, v_w_in, v_b_f, v_w_pool, v_pool_scale, v_w_out, v_w_pg, v_w_pe):
    depth = w_in.shape[0]
    b, s, d = x.shape
    t = b * s
    heads = b_f.shape[1]
    a = heads * HEAD_DIM
    pairs = a // LANES
    pw = pool_scale.shape[1]
    pg = pw // N_POOL_GROUPS
    ple = p.shape[-1]
    mix_w = a + pw

    shard2d = {
        "w_in": w_in.reshape(depth * d, -1),
        "w_pool": w_pool.reshape(depth * N_POOL_GROUPS * (pg // N_DEV), pg),
        "w_out": w_out.reshape(depth * (mix_w // N_DEV), d),
        "w_pg": w_pg.reshape(depth * (d // N_DEV), d),
        "w_pe": w_pe.reshape(depth * ple, d // N_DEV),
    }
    names = list(shard2d)
    gathered = _exchange([(shard2d[nm].astype(BF16), False) for nm in names], "gather_weights")
    gw = dict(zip(names, gathered))

    layers = []
    for i in range(depth):
        g_in = gw["w_in"].reshape(N_DEV, depth, d, -1)[:, i]
        wa, wp, wf = _pack_w_in(g_in, a, heads, pw)
        g_pool = gw["w_pool"].reshape(N_DEV, depth, N_POOL_GROUPS, pg // N_DEV, pg)[:, i]
        wpool = jnp.transpose(g_pool, (1, 0, 2, 3)).reshape(N_POOL_GROUPS, pg, pg)
        wout = gw["w_out"].reshape(N_DEV, depth, mix_w // N_DEV, d)[:, i].reshape(mix_w, d)
        wpg = gw["w_pg"].reshape(N_DEV, depth, d // N_DEV, d)[:, i].reshape(d, d)
        g_pe = gw["w_pe"].reshape(N_DEV, depth, ple, d // N_DEV)[:, i]
        wpe = jnp.transpose(g_pe, (1, 0, 2)).reshape(ple, d)
        layers.append(dict(wa=wa, wp=wp, wf=wf, wpool=wpool, wout=wout, wpg=wpg, wpe=wpe))

    h = x.reshape(t, d)
    saved = []
    for i in range(depth):
        lw = layers[i]
        sv = dict(h=h)
        g_pre = norm_pre[i:i + 1]
        g_post = norm_post[i:i + 1]
        bf = jnp.pad(b_f[i:i + 1], ((0, 0), (0, LANES - heads)))
        scale = pool_scale[i:i + 1]
        hn = _rms_fwd(h, g_pre, "rms_pre")
        pa = _matmul([(hn, lw["wa"])], "nn", BF16, "proj_attn").reshape(b, s, 4 * a)
        pp = _matmul([(hn, lw["wp"])], "nn", BF16, "proj_pool").reshape(b, s, 2 * pw)
        fl = _matmul([(hn, lw["wf"])], "nn", F32, "proj_gate").reshape(b, s, LANES)
        ct = _gates_fwd(fl, bf, "gates_fwd")
        c4 = ct[:, :heads].reshape(b, pairs, 2, s)
        o, ga, lse = _attn_fwd(pa, c4, "attn_fwd")
        gp = _pool_fwd(pp, lw["wpool"], scale, "pool_fwd")
        ga2 = ga.reshape(t, a)
        gp2 = gp.reshape(t, pw)
        mix = _matmul([(ga2, lw["wout"][:a]), (gp2, lw["wout"][a:])], "nn", F32, "mix_out")
        h1, h1b = _post_fwd(h, mix, g_post, "post_fwd")
        pb = p[i].reshape(t, ple).astype(BF16)
        gpre = _matmul([(h1b, lw["wpg"])], "nn", F32, "ple_gate")
        e = _matmul([(pb, lw["wpe"])], "nn", F32, "ple_embed")
        h = _ple_fwd(h1, gpre, e, "ple_fwd")
        sv.update(hn=hn, pa=pa, pp=pp, fl=fl, bf=bf, c4=c4, o=o, lse=lse, ga=ga2, gp=gp2, mix=mix,
                  h1b=h1b, pb=pb, gpre=gpre, e=e, g_pre=g_pre, g_post=g_post, scale=scale)
        saved.append(sv)

    dh, sq = _loss_bwd(h, loss_target.reshape(t, d), "loss")
    loss = lax.psum(0.5 * jnp.sum(sq) / d, MESH_AXES)

    big = {nm: [None] * depth for nm in names}
    small = {nm: [None] * depth for nm in ("norm_pre", "norm_post", "b_f", "pool_scale")}
    for i in reversed(range(depth)):
        lw, sv = layers[i], saved[i]
        de, dpre = _ple_bwd(dh, sv["gpre"], sv["e"], "ple_bwd")
        dwpe = _matmul([(sv["pb"], de)], "tn", BF16, "dw_pe")
        dwpg = _matmul([(sv["h1b"], dpre)], "tn", BF16, "dw_pg")
        t1 = _matmul([(dpre, lw["wpg"])], "nt", F32, "d_h1")
        dh1, dmix, dg_post = _post_bwd(dh, t1, sv["mix"], sv["g_post"], "post_bwd")
        dwout = jnp.concatenate([_matmul([(sv["ga"], dmix)], "tn", BF16, "dw_out_attn"),
                                 _matmul([(sv["gp"], dmix)], "tn", BF16, "dw_out_pool")], axis=0)
        dcat = _matmul([(dmix, lw["wout"])], "nt", BF16, "d_cat").reshape(b, s, mix_w)
        dpp, dwpool, dscale = _pool_bwd(sv["pp"], dcat, lw["wpool"], sv["scale"], a // pg, "pool_bwd")
        dpp = dpp.reshape(t, 2 * pw)
        dpa, dc4 = _attn_bwd(sv["pa"], sv["o"], dcat, sv["c4"], sv["lse"], "attn_bwd")
        dct = jnp.pad(dc4.reshape(b, heads, s), ((0, 0), (0, LANES - heads), (0, 0)))
        dfl, dbf = _gates_bwd(dct, sv["fl"], sv["bf"], heads, "gates_bwd")
        dpa2 = dpa.reshape(t, 4 * a)
        dfl2 = dfl.reshape(t, LANES)
        dwa = _matmul([(sv["hn"], dpa2)], "tn", BF16, "dw_attn")
        dwp = _matmul([(sv["hn"], dpp)], "tn", BF16, "dw_pool_proj")
        dwf = _matmul([(sv["hn"], dfl2)], "tn", BF16, "dw_gate")
        dhn = _matmul([(dpa2, lw["wa"]), (dpp, lw["wp"]), (dfl2, lw["wf"])], "nt", F32, "d_hn")
        dh, dg_pre = _pre_bwd(sv["h"], dhn, dh1, sv["g_pre"], "pre_bwd")

        big["w_in"][i] = _unpack_dw_in(dwa, dwp, dwf, a, heads, pw)
        big["w_pool"][i] = jnp.transpose(
            dwpool.astype(BF16).reshape(N_POOL_GROUPS, N_DEV, pg // N_DEV, pg), (1, 0, 2, 3)
        ).reshape(N_DEV, N_POOL_GROUPS * (pg // N_DEV), pg)
        big["w_out"][i] = dwout.reshape(N_DEV, mix_w // N_DEV, d)
        big["w_pg"][i] = dwpg.reshape(N_DEV, d // N_DEV, d)
        big["w_pe"][i] = jnp.transpose(dwpe.reshape(ple, N_DEV, d // N_DEV), (1, 0, 2))
        small["norm_pre"][i] = dg_pre
        small["norm_post"][i] = dg_post
        small["b_f"][i] = jnp.sum(dbf, axis=0)
        small["pool_scale"][i] = dscale
    grad_x = dh.reshape(b, s, d)

    width = max(d, pw)
    small_names = ("norm_pre", "norm_post", "pool_scale", "b_f")

    def small_rows(get):
        rows = []
        for nm in small_names:
            for i in range(depth):
                v_ = get(nm, i)
                rows.append(jnp.pad(v_, ((0, 0), (0, width - v_.shape[1]))))
        return jnp.concatenate(rows, axis=0)

    small_g = small_rows(lambda nm, i: small[nm][i])
    items = []
    for nm in names:
        stacked = jnp.stack(big[nm], axis=1)
        items.append((stacked.reshape(N_DEV, -1, stacked.shape[-1]), True))
    items.append((small_g, False))
    received = _exchange(items, "exchange_grads")

    weights = dict(norm_pre=norm_pre, norm_post=norm_post, w_in=w_in, b_f=b_f, w_pool=w_pool,
                   pool_scale=pool_scale, w_out=w_out, w_pg=w_pg, w_pe=w_pe)
    mom1 = dict(norm_pre=m_norm_pre, norm_post=m_norm_post, w_in=m_w_in, b_f=m_b_f, w_pool=m_w_pool,
                pool_scale=m_pool_scale, w_out=m_w_out, w_pg=m_w_pg, w_pe=m_w_pe)
    mom2 = dict(norm_pre=v_norm_pre, norm_post=v_norm_post, w_in=v_w_in, b_f=v_b_f, w_pool=v_w_pool,
                pool_scale=v_pool_scale, w_out=v_w_out, w_pg=v_w_pg, w_pe=v_w_pe)

    results = {}
    for nm, recv in zip(names, received[:-1]):
        shp = weights[nm].shape
        flat = lambda arr: arr.reshape(recv.shape[1], recv.shape[2])
        outs = _adamw(recv, flat(weights[nm]), flat(mom1[nm]), flat(mom2[nm]), "adamw_" + nm)
        results[nm] = [o_.reshape(shp) for o_ in outs]

    small_w = small_rows(lambda nm, i: weights[nm][i:i + 1])
    small_m = small_rows(lambda nm, i: mom1[nm][i:i + 1])
    small_v = small_rows(lambda nm, i: mom2[nm][i:i + 1])
    outs = _adamw(received[-1], small_w, small_m, small_v, "adamw_small")
    for j, nm in enumerate(small_names):
        cols = weights[nm].shape[1]
        results[nm] = [o_[j * depth:(j + 1) * depth, :cols] for o_ in outs]

    order = ("norm_pre", "norm_post", "w_in", "b_f", "w_pool", "pool_scale", "w_out", "w_pg", "w_pe")
    return (loss, grad_x, *[results[nm][0] for nm in order], *[results[nm][1] for nm in order],
            *[results[nm][2] for nm in order], *[results[nm][3] for nm in order])
```

```python
import functools
import math

import jax
import jax.numpy as jnp
from jax import lax
from jax.experimental import pallas as pl
from jax.experimental.pallas import tpu as pltpu

N_DEV = 8
MESH_AXES = ("x", "y", "c")
HEAD_DIM = 64
LANES = 128
N_POOL_GROUPS = 4
EPS = 1e-6
ADAM_LR = 0.001
ADAM_B1 = 0.9
ADAM_B2 = 0.999
ADAM_EPS = 1e-08
ADAM_WD = 0.01
ADAM_STEP = 10
VMEM_LIMIT_BYTES = 56 * 1024 * 1024
F32 = jnp.float32
BF16 = jnp.bfloat16
NEG_INF = float("-inf")


def _call(body, *, name, grid, in_specs, out_specs, out_shape, scratch_shapes=(), semantics=None):
    return pl.pallas_call(
        body, name=name, grid=grid, in_specs=in_specs, out_specs=out_specs, out_shape=out_shape,
        scratch_shapes=list(scratch_shapes),
        compiler_params=pltpu.CompilerParams(dimension_semantics=semantics,
                                             vmem_limit_bytes=VMEM_LIMIT_BYTES))


def _sigmoid(z):
    return 1.0 / (1.0 + jnp.exp(-z))


def _dot(a, b, dims):
    return lax.dot_general(a, b, (dims, ((), ())), preferred_element_type=F32)


NN = ((1,), (0,))
NT = ((1,), (1,))
TN = ((0,), (0,))


def _exchange(items, name):
    n = len(items)
    modes = [s for _, s in items]
    out_shapes = []
    for a, s in items:
        shp = a.shape[1:] if s else a.shape
        out_shapes.append(jax.ShapeDtypeStruct((N_DEV,) + tuple(shp), a.dtype))

    def body(*refs):
        ins = refs[:n]
        outs = refs[n:2 * n]
        send_sems, recv_sems, local_sems = refs[2 * n:]
        x, y, c = (lax.axis_index(ax) for ax in MESH_AXES)
        me = 4 * x + 2 * y + c
        started = []
        for i in range(n):
            mine = ins[i].at[me] if modes[i] else ins[i]
            loc = pltpu.make_async_copy(mine, outs[i].at[me], local_sems.at[i])
            loc.start()
            started.append(loc)
        remote = []
        for k in range(1, N_DEV):
            px = x ^ ((k >> 2) & 1)
            py = y ^ ((k >> 1) & 1)
            pc = c ^ (k & 1)
            peer = me ^ k
            for i in range(n):
                src = ins[i].at[peer] if modes[i] else ins[i]
                cp = pltpu.make_async_remote_copy(
                    src_ref=src, dst_ref=outs[i].at[me],
                    send_sem=send_sems.at[i, k - 1], recv_sem=recv_sems.at[i, k - 1],
                    device_id=(px, py, pc), device_id_type=pl.DeviceIdType.MESH)
                cp.start()
                remote.append(cp)
        for cp in remote:
            cp.wait()
        for loc in started:
            loc.wait()

    hbm = pl.BlockSpec(memory_space=pltpu.HBM)
    return pl.pallas_call(
        body, name=name, out_shape=out_shapes,
        in_specs=[hbm] * n, out_specs=[hbm] * n,
        scratch_shapes=[pltpu.SemaphoreType.DMA((n, N_DEV - 1)),
                        pltpu.SemaphoreType.DMA((n, N_DEV - 1)),
                        pltpu.SemaphoreType.DMA((n,))],
    )(*[a for a, _ in items])


def _matmul(pairs, mode, out_dtype, name, tm=512, tn=1024, tk=512):
    dims = {"nn": NN, "nt": NT, "tn": TN}[mode]
    a0, b0 = pairs[0]
    m_dim = a0.shape[1] if mode == "tn" else a0.shape[0]
    n_dim = b0.shape[0] if mode == "nt" else b0.shape[1]
    tm = min(tm, m_dim)
    tn = min(tn, n_dim)
    segs = []
    off = 0
    for a, _ in pairs:
        k_dim = a.shape[0] if mode == "tn" else a.shape[1]
        t = min(tk, k_dim)
        segs.append((off, k_dim // t, t))
        off += k_dim // t
    nk = off
    n_pairs = len(pairs)

    in_specs = []
    for (o, cnt, t) in segs:
        def kc(kk, o=o, cnt=cnt):
            return jnp.clip(kk - o, 0, cnt - 1)
        if mode == "tn":
            in_specs.append(pl.BlockSpec((t, tm), lambda i, j, kk, kc=kc: (kc(kk), i)))
        else:
            in_specs.append(pl.BlockSpec((tm, t), lambda i, j, kk, kc=kc: (i, kc(kk))))
        if mode == "nt":
            in_specs.append(pl.BlockSpec((tn, t), lambda i, j, kk, kc=kc: (j, kc(kk))))
        else:
            in_specs.append(pl.BlockSpec((t, tn), lambda i, j, kk, kc=kc: (kc(kk), j)))

    def body(*refs):
        out_ref = refs[2 * n_pairs]
        acc = refs[2 * n_pairs + 1]
        kk = pl.program_id(2)

        @pl.when(kk == 0)
        def _():
            acc[...] = jnp.zeros_like(acc)

        for idx, (o, cnt, _) in enumerate(segs):
            @pl.when((kk >= o) & (kk < o + cnt))
            def _(idx=idx):
                acc[...] += _dot(refs[2 * idx][...], refs[2 * idx + 1][...], dims)

        @pl.when(kk == nk - 1)
        def _():
            out_ref[...] = acc[...].astype(out_dtype)

    flat = [t for pr in pairs for t in pr]
    return _call(body, name=name, grid=(m_dim // tm, n_dim // tn, nk), in_specs=in_specs,
                 out_specs=pl.BlockSpec((tm, tn), lambda i, j, kk: (i, j)),
                 out_shape=jax.ShapeDtypeStruct((m_dim, n_dim), out_dtype),
                 scratch_shapes=[pltpu.VMEM((tm, tn), F32)],
                 semantics=("parallel", "parallel", "arbitrary"))(*flat)


def _row_tile(t):
    return min(256, t)


def _rms_fwd(h, g, name):
    t, d = h.shape
    tt = _row_tile(t)

    def body(h_ref, g_ref, o_ref):
        hv = h_ref[...]
        r = lax.rsqrt(jnp.mean(hv * hv, axis=-1, keepdims=True) + EPS)
        o_ref[...] = (hv * r * g_ref[...]).astype(BF16)

    row = pl.BlockSpec((tt, d), lambda i: (i, 0))
    vec = pl.BlockSpec((1, d), lambda i: (0, 0))
    return _call(body, name=name, grid=(t // tt,), in_specs=[row, vec], out_specs=row,
                 out_shape=jax.ShapeDtypeStruct((t, d), BF16), semantics=("parallel",))(h, g)


def _post_fwd(h, mix, g, name):
    t, d = h.shape
    tt = _row_tile(t)

    def body(h_ref, m_ref, g_ref, o_ref, ob_ref):
        mv = m_ref[...]
        r = lax.rsqrt(jnp.mean(mv * mv, axis=-1, keepdims=True) + EPS)
        h1 = h_ref[...] + mv * r * g_ref[...]
        o_ref[...] = h1
        ob_ref[...] = h1.astype(BF16)

    row = pl.BlockSpec((tt, d), lambda i: (i, 0))
    vec = pl.BlockSpec((1, d), lambda i: (0, 0))
    return _call(body, name=name, grid=(t // tt,), in_specs=[row, row, vec], out_specs=[row, row],
                 out_shape=[jax.ShapeDtypeStruct((t, d), F32), jax.ShapeDtypeStruct((t, d), BF16)],
                 semantics=("parallel",))(h, mix, g)


def _ple_fwd(h1, gpre, e, name):
    t, d = h1.shape
    tt = _row_tile(t)

    def body(h_ref, g_ref, e_ref, o_ref):
        o_ref[...] = h_ref[...] + _sigmoid(g_ref[...]) * e_ref[...]

    row = pl.BlockSpec((tt, d), lambda i: (i, 0))
    return _call(body, name=name, grid=(t // tt,), in_specs=[row, row, row], out_specs=row,
                 out_shape=jax.ShapeDtypeStruct((t, d), F32), semantics=("parallel",))(h1, gpre, e)


def _loss_bwd(y, target, name):
    t, d = y.shape
    tt = _row_tile(t)

    def body(y_ref, t_ref, dy_ref, s_ref):
        @pl.when(pl.program_id(0) == 0)
        def _():
            s_ref[...] = jnp.zeros_like(s_ref)
        diff = y_ref[...] - t_ref[...]
        dy_ref[...] = diff * (1.0 / d)
        s_ref[...] += jnp.sum(diff * diff, axis=0, keepdims=True)

    row = pl.BlockSpec((tt, d), lambda i: (i, 0))
    vec = pl.BlockSpec((1, d), lambda i: (0, 0))
    return _call(body, name=name, grid=(t // tt,), in_specs=[row, row], out_specs=[row, vec],
                 out_shape=[jax.ShapeDtypeStruct((t, d), F32), jax.ShapeDtypeStruct((1, d), F32)],
                 semantics=("arbitrary",))(y, target)


def _ple_bwd(dh2, gpre, e, name):
    t, d = dh2.shape
    tt = _row_tile(t)

    def body(d_ref, g_ref, e_ref, de_ref, dp_ref):
        gate = _sigmoid(g_ref[...])
        dv = d_ref[...]
        de_ref[...] = (dv * gate).astype(BF16)
        dp_ref[...] = (dv * e_ref[...] * gate * (1.0 - gate)).astype(BF16)

    row = pl.BlockSpec((tt, d), lambda i: (i, 0))
    return _call(body, name=name, grid=(t // tt,), in_specs=[row, row, row], out_specs=[row, row],
                 out_shape=[jax.ShapeDtypeStruct((t, d), BF16)] * 2,
                 semantics=("parallel",))(dh2, gpre, e)


def _post_bwd(dh2, t1, mix, g, name):
    t, d = dh2.shape
    tt = _row_tile(t)

    def body(d_ref, t_ref, m_ref, g_ref, dh_ref, dm_ref, dg_ref):
        @pl.when(pl.program_id(0) == 0)
        def _():
            dg_ref[...] = jnp.zeros_like(dg_ref)
        dh1 = d_ref[...] + t_ref[...]
        mv = m_ref[...]
        r = lax.rsqrt(jnp.mean(mv * mv, axis=-1, keepdims=True) + EPS)
        dh_ref[...] = dh1
        dg_ref[...] += jnp.sum(dh1 * mv * r, axis=0, keepdims=True)
        w = dh1 * g_ref[...]
        dot = jnp.mean(w * mv, axis=-1, keepdims=True)
        dm_ref[...] = (r * w - mv * (r * r * r) * dot).astype(BF16)

    row = pl.BlockSpec((tt, d), lambda i: (i, 0))
    vec = pl.BlockSpec((1, d), lambda i: (0, 0))
    return _call(body, name=name, grid=(t // tt,), in_specs=[row, row, row, vec],
                 out_specs=[row, row, vec],
                 out_shape=[jax.ShapeDtypeStruct((t, d), F32), jax.ShapeDtypeStruct((t, d), BF16),
                            jax.ShapeDtypeStruct((1, d), F32)],
                 semantics=("arbitrary",))(dh2, t1, mix, g)


def _pre_bwd(h, dhn, dh1, g, name):
    t, d = h.shape
    tt = _row_tile(t)

    def body(h_ref, dn_ref, d1_ref, g_ref, dh_ref, dg_ref):
        @pl.when(pl.program_id(0) == 0)
        def _():
            dg_ref[...] = jnp.zeros_like(dg_ref)
        hv = h_ref[...]
        dn = dn_ref[...]
        r = lax.rsqrt(jnp.mean(hv * hv, axis=-1, keepdims=True) + EPS)
        dg_ref[...] += jnp.sum(dn * hv * r, axis=0, keepdims=True)
        w = dn * g_ref[...]
        dot = jnp.mean(w * hv, axis=-1, keepdims=True)
        dh_ref[...] = d1_ref[...] + r * w - hv * (r * r * r) * dot

    row = pl.BlockSpec((tt, d), lambda i: (i, 0))
    vec = pl.BlockSpec((1, d), lambda i: (0, 0))
    return _call(body, name=name, grid=(t // tt,), in_specs=[row, row, row, vec],
                 out_specs=[row, vec],
                 out_shape=[jax.ShapeDtypeStruct((t, d), F32), jax.ShapeDtypeStruct((1, d), F32)],
                 semantics=("arbitrary",))(h, dhn, dh1, g)


def _split3(v):
    hi = v.astype(BF16)
    r1 = v - hi.astype(F32)
    mid = r1.astype(BF16)
    lo = (r1 - mid.astype(F32)).astype(BF16)
    return hi, mid, lo


def _gates_fwd(fl, bf, name):
    b, s, _ = fl.shape

    def body(f_ref, b_ref, c_ref):
        xv = f_ref[0] + b_ref[...]
        lf = jnp.minimum(xv, 0.0) - jnp.log(1.0 + jnp.exp(-jnp.abs(xv)))
        dst = lax.broadcasted_iota(jnp.int32, (s, s), 0)
        src = lax.broadcasted_iota(jnp.int32, (s, s), 1)
        lower = (src <= dst).astype(BF16)
        acc = jnp.zeros((s, LANES), F32)
        for part in _split3(lf):
            acc = acc + _dot(lower, part, NN)
        c_ref[0] = acc

    blk = pl.BlockSpec((1, s, LANES), lambda i: (i, 0, 0))
    return _call(body, name=name, grid=(b,),
                 in_specs=[blk, pl.BlockSpec((1, LANES), lambda i: (0, 0))],
                 out_specs=blk, out_shape=jax.ShapeDtypeStruct((b, s, LANES), F32),
                 semantics=("parallel",))(fl, bf)


def _gates_bwd(dc, fl, bf, heads, name):
    b, s, _ = fl.shape

    def body(d_ref, f_ref, b_ref, o_ref, db_ref):
        xv = f_ref[0] + b_ref[...]
        dst = lax.broadcasted_iota(jnp.int32, (s, s), 0)
        src = lax.broadcasted_iota(jnp.int32, (s, s), 1)
        later = (src >= dst).astype(BF16)
        dlf = jnp.zeros((s, LANES), F32)
        for part in _split3(d_ref[0]):
            dlf = dlf + _dot(later, part, NN)
        lane = lax.broadcasted_iota(jnp.int32, (s, LANES), 1)
        dfl = jnp.where(lane < heads, dlf * _sigmoid(-xv), 0.0)
        o_ref[0] = dfl.astype(BF16)
        db_ref[0] = jnp.sum(dfl, axis=0, keepdims=True)

    blk = pl.BlockSpec((1, s, LANES), lambda i: (i, 0, 0))
    return _call(body, name=name, grid=(b,),
                 in_specs=[blk, blk, pl.BlockSpec((1, LANES), lambda i: (0, 0))],
                 out_specs=[blk, pl.BlockSpec((1, 1, LANES), lambda i: (i, 0, 0))],
                 out_shape=[jax.ShapeDtypeStruct((b, s, LANES), BF16),
                            jax.ShapeDtypeStruct((b, 1, LANES), F32)],
                 semantics=("parallel",))(dc, fl, bf)


LANE_CQ = 64
LANE_CK = 67
LANE_LSE = 70
LANE_D = 64
N_PARTS = 3


def _attn_tiles(s):
    return min(512, s), min(256, s)


def _lanes_in(lane, first):
    return (lane >= first) & (lane < first + N_PARTS)


def _attn_prep_fwd(pa, c, name):
    b, s, a4 = pa.shape
    pairs = a4 // (4 * LANES)
    scale = 1.0 / math.sqrt(HEAD_DIM)
    wide = (1 + N_PARTS) * LANES

    def body(q_ref, k_ref, v_ref, c_ref, qa_ref, ka_ref, kat_ref, va_ref, vt_ref):
        hp = pl.program_id(1)
        c3 = jnp.concatenate(_split3(c_ref[0]), axis=1)
        qx = jnp.concatenate([q_ref[0], c3], axis=1)
        kx = jnp.concatenate([k_ref[0], c3], axis=1)
        vv = v_ref[0]
        lane = lax.broadcasted_iota(jnp.int32, (s, LANES), 1)
        row = lax.broadcasted_iota(jnp.int32, (wide, LANES), 0)
        col = lax.broadcasted_iota(jnp.int32, (wide, LANES), 1)
        r128 = lax.broadcasted_iota(jnp.int32, (LANES, LANES), 0)
        c128 = lax.broadcasted_iota(jnp.int32, (LANES, LANES), 1)
        ident = (r128 == c128).astype(BF16)
        for j in range(2):
            head = 2 * hp + j
            move = (row == col + HEAD_DIM * j) & (col < HEAD_DIM)
            move128 = (r128 == c128 + HEAD_DIM * j) & (c128 < HEAD_DIM)

            def pick(first, head=head):
                hit = (row == LANES + head) & (col == first)
                for i in range(1, N_PARTS):
                    hit = hit | ((row == LANES * (i + 1) + head) & (col == first + i))
                return hit

            mq = (jnp.where(move, scale, 0.0) + jnp.where(pick(LANE_CQ), 1.0, 0.0)).astype(BF16)
            mk = (jnp.where(move, 1.0, 0.0) - jnp.where(pick(LANE_CK), 1.0, 0.0)).astype(BF16)
            qa = _dot(qx, mq, NN) + jnp.where(_lanes_in(lane, LANE_CK), 1.0, 0.0)
            ka = _dot(kx, mk, NN) + jnp.where(
                _lanes_in(lane, LANE_CQ) | _lanes_in(lane, LANE_LSE), 1.0, 0.0)
            va = _dot(vv, move128.astype(BF16), NN) + jnp.where(_lanes_in(lane, LANE_D), 1.0, 0.0)
            kab = ka.astype(BF16)
            qa_ref[0, 0, j] = qa.astype(BF16)
            ka_ref[0, 0, j] = kab
            kat_ref[0, 0, j] = _dot(ident, kab, NT).astype(BF16)
            va_ref[0, 0, j] = va.astype(BF16)
        vt_ref[0, 0] = _dot(ident, vv, NT).astype(BF16)

    col_blk = lambda cidx: pl.BlockSpec((1, s, LANES), lambda bi, hp: (bi, 0, 4 * hp + cidx))
    tok = pl.BlockSpec((1, 1, 2, s, LANES), lambda bi, hp: (bi, hp, 0, 0, 0))
    tok_t = pl.BlockSpec((1, 1, 2, LANES, s), lambda bi, hp: (bi, hp, 0, 0, 0))
    tok_shape = jax.ShapeDtypeStruct((b, pairs, 2, s, LANES), BF16)
    return _call(
        body, name=name, grid=(b, pairs),
        in_specs=[col_blk(0), col_blk(1), col_blk(2),
                  pl.BlockSpec((1, s, LANES), lambda bi, hp: (bi, 0, 0))],
        out_specs=[tok, tok, tok_t, tok,
                   pl.BlockSpec((1, 1, LANES, s), lambda bi, hp: (bi, hp, 0, 0))],
        out_shape=[tok_shape, tok_shape, jax.ShapeDtypeStruct((b, pairs, 2, LANES, s), BF16),
                   tok_shape, jax.ShapeDtypeStruct((b, pairs, LANES, s), BF16)],
        semantics=("parallel", "parallel"))(pa, pa, pa, c)


def _attn_fwd(qa, ka, vt, pa, name):
    b, pairs, _, s, _ = qa.shape
    a = pairs * LANES
    tq, tk = _attn_tiles(s)
    ratio = tq // tk

    def body(q_ref, k_ref, vt_ref, z_ref, o_ref, g_ref, lse_ref):
        qi = pl.program_id(2)
        key_i = lax.broadcasted_iota(jnp.int32, (tk, tq), 0)
        qry_i = lax.broadcasted_iota(jnp.int32, (tk, tq), 1)
        qv = [q_ref[0, 0, j] for j in range(2)]

        def step(kj, carry, diag):
            k0 = pl.multiple_of(kj * tk, tk)
            out = []
            for j in range(2):
                m, l, acc = carry[j]
                st = _dot(k_ref[0, 0, j, pl.ds(k0, tk), :], qv[j], NT)
                if diag is not None:
                    st = jnp.where(key_i + diag * tk <= qry_i, st, NEG_INF)
                m_new = jnp.maximum(m, jnp.max(st, axis=0, keepdims=True))
                alpha = jnp.exp(m - m_new)
                pt = jnp.exp(st - m_new)
                l_new = alpha * l + jnp.sum(pt, axis=0, keepdims=True)
                vb = vt_ref[0, 0, HEAD_DIM * j:HEAD_DIM * (j + 1), pl.ds(k0, tk)]
                out.append((m_new, l_new, alpha * acc + _dot(vb, pt.astype(BF16), NN)))
            return tuple(out)

        init = (jnp.full((1, tq), NEG_INF, F32), jnp.zeros((1, tq), F32),
                jnp.zeros((HEAD_DIM, tq), F32))
        carry = lax.fori_loop(0, ratio * qi, lambda kj, cr: step(kj, cr, None), (init, init))
        for i in range(ratio):
            carry = step(ratio * qi + i, carry, i)
        heads_out = []
        for j in range(2):
            m, l, acc = carry[j]
            heads_out.append(acc / l)
            lse_ref[0, 0, j:j + 1, :] = m + jnp.log(l)
        ov = jnp.transpose(jnp.concatenate(heads_out, axis=0))
        o_ref[0] = ov.astype(BF16)
        zv = z_ref[0].astype(F32)
        g_ref[0] = (ov * zv * _sigmoid(zv)).astype(BF16)

    return _call(
        body, name=name, grid=(b, pairs, s // tq),
        in_specs=[pl.BlockSpec((1, 1, 2, tq, LANES), lambda bi, hp, qi: (bi, hp, 0, qi, 0)),
                  pl.BlockSpec((1, 1, 2, s, LANES), lambda bi, hp, qi: (bi, hp, 0, 0, 0)),
                  pl.BlockSpec((1, 1, LANES, s), lambda bi, hp, qi: (bi, hp, 0, 0)),
                  pl.BlockSpec((1, tq, LANES), lambda bi, hp, qi: (bi, qi, 4 * hp + 3))],
        out_specs=[pl.BlockSpec((1, tq, LANES), lambda bi, hp, qi: (bi, qi, hp)),
                   pl.BlockSpec((1, tq, LANES), lambda bi, hp, qi: (bi, qi, hp)),
                   pl.BlockSpec((1, 1, 2, tq), lambda bi, hp, qi: (bi, hp, 0, qi))],
        out_shape=[jax.ShapeDtypeStruct((b, s, a), BF16), jax.ShapeDtypeStruct((b, s, a), BF16),
                   jax.ShapeDtypeStruct((b, pairs, 2, s), F32)],
        semantics=("parallel", "parallel", "arbitrary"))(qa, ka, vt, pa)


def _attn_prep_bwd(dcat, pa, o, lse, qa, name):
    b, pairs, _, s, _ = qa.shape
    a = pairs * LANES
    sub = 16

    def body(da_ref, z_ref, o_ref, lse_ref, qa_ref, qab_ref, doa_ref, dz_ref):
        zv = z_ref[0].astype(F32)
        dav = da_ref[0].astype(F32)
        ov = o_ref[0].astype(F32)
        sg = _sigmoid(zv)
        dov = dav * zv * sg
        dz_ref[0] = (dav * ov * sg * (1.0 + zv * (1.0 - sg))).astype(BF16)
        prod = dov * ov
        dob = dov.astype(BF16)
        lane = lax.broadcasted_iota(jnp.int32, (s, LANES), 1)
        r128 = lax.broadcasted_iota(jnp.int32, (LANES, LANES), 0)
        c128 = lax.broadcasted_iota(jnp.int32, (LANES, LANES), 1)
        prow = lax.broadcasted_iota(jnp.int32, (sub, s), 0)
        srow = lax.broadcasted_iota(jnp.int32, (sub, LANES), 0)
        scol = lax.broadcasted_iota(jnp.int32, (sub, LANES), 1)
        place = ((scol == srow + LANE_LSE) & (srow < N_PARTS)).astype(BF16)
        for j in range(2):
            in_head = (lane >= HEAD_DIM * j) & (lane < HEAD_DIM * (j + 1))
            dparts = _split3(jnp.sum(jnp.where(in_head, prod, 0.0), axis=1, keepdims=True))
            move128 = ((r128 == c128 + HEAD_DIM * j) & (c128 < HEAD_DIM)).astype(BF16)
            doa = _dot(dob, move128, NN)
            for i in range(N_PARTS):
                doa = jnp.where(lane == LANE_D + i, -dparts[i].astype(F32), doa)
            doa_ref[0, 0, j] = doa.astype(BF16)
            lparts = _split3(lse_ref[0, 0, j:j + 1, :])
            pmat = jnp.zeros((sub, s), BF16)
            for i in range(N_PARTS):
                pmat = jnp.where(prow == i, lparts[i], pmat)
            lcol = _dot(pmat, place, TN)
            qab_ref[0, 0, j] = (qa_ref[0, 0, j].astype(F32) - lcol).astype(BF16)

    tok = pl.BlockSpec((1, 1, 2, s, LANES), lambda bi, hp: (bi, hp, 0, 0, 0))
    tok_shape = jax.ShapeDtypeStruct((b, pairs, 2, s, LANES), BF16)
    pair_blk = pl.BlockSpec((1, s, LANES), lambda bi, hp: (bi, 0, hp))
    return _call(
        body, name=name, grid=(b, pairs),
        in_specs=[pair_blk,
                  pl.BlockSpec((1, s, LANES), lambda bi, hp: (bi, 0, 4 * hp + 3)),
                  pair_blk,
                  pl.BlockSpec((1, 1, 2, s), lambda bi, hp: (bi, hp, 0, 0)),
                  tok],
        out_specs=[tok, tok, pair_blk],
        out_shape=[tok_shape, tok_shape, jax.ShapeDtypeStruct((b, s, a), BF16)],
        semantics=("parallel", "parallel"))(dcat, pa, o, lse, qa)


def _attn_bwd(ka, kat, va, qab, doa, dz, name):
    b, pairs, _, s, _ = ka.shape
    a4 = 4 * pairs * LANES
    tq, tk = _attn_tiles(s)
    ratio = tq // tk
    nq, nk = s // tq, s // tk
    scale = 1.0 / math.sqrt(HEAD_DIM)

    def body(k_ref, kt_ref, v_ref, q_ref, do_ref, dz_ref, dp_ref, dc_ref, dqt_acc, dk_s, dv_s):
        key_i = lax.broadcasted_iota(jnp.int32, (tk, tq), 0)
        qry_i = lax.broadcasted_iota(jnp.int32, (tk, tq), 1)
        lane = lax.broadcasted_iota(jnp.int32, (tq, LANES), 1)
        low = lane < HEAD_DIM

        dqt_acc[...] = jnp.zeros_like(dqt_acc)

        def key_block(kj, _):
            krows = pl.ds(pl.multiple_of(kj * tk, tk), tk)
            kb = [k_ref[0, 0, j, krows, :] for j in range(2)]
            vb = [v_ref[0, 0, j, krows, :] for j in range(2)]
            ktb = [kt_ref[0, 0, j, :, krows] for j in range(2)]
            qd = kj // ratio

            def query_block(qi, carry, masked):
                qrows = pl.ds(pl.multiple_of(qi * tq, tq), tq)
                out = []
                for j in range(2):
                    dk, dv = carry[j]
                    qb = q_ref[0, 0, j, qrows, :]
                    dob = do_ref[0, 0, j, qrows, :]
                    pt = jnp.exp(_dot(kb[j], qb, NT))
                    if masked:
                        pt = jnp.where(key_i + (kj * tk - qi * tq) <= qry_i, pt, 0.0)
                    dst = pt * _dot(vb[j], dob, NT)
                    dsb = dst.astype(BF16)
                    dv = dv + _dot(pt.astype(BF16), dob, NN)
                    dk = dk + _dot(dsb, qb, NN)
                    dqt_acc[j, :, qrows] += _dot(ktb[j], dsb, NN)
                    out.append((dk, dv))
                return tuple(out)

            zero = jnp.zeros((tk, LANES), F32)
            carry = query_block(qd, ((zero, zero), (zero, zero)), True)
            carry = lax.fori_loop(qd + 1, nq, lambda qi, cr: query_block(qi, cr, False), carry)
            for j in range(2):
                dk_s[j, krows, :] = carry[j][0]
                dv_s[j, krows, :] = carry[j][1]
            return 0

        lax.fori_loop(0, nk, key_block, 0)

        def finish(i, _):
            rows = pl.ds(pl.multiple_of(i * tq, tq), tq)
            dq = [jnp.transpose(dqt_acc[j, :, rows]) for j in range(2)]
            dk = [dk_s[j, rows, :] for j in range(2)]
            dv = [dv_s[j, rows, :] for j in range(2)]
            dcol = [dq[j][:, LANE_CQ:LANE_CQ + 1] - dk[j][:, LANE_CK:LANE_CK + 1] for j in range(2)]
            dq = [dq[j] * scale for j in range(2)]
            for t, val in enumerate((dq, dk, dv)):
                merged = jnp.where(low, val[0], pltpu.roll(val[1], HEAD_DIM, 1))
                dp_ref[0, rows, t * LANES:(t + 1) * LANES] = merged.astype(BF16)
            dc_ref[0, 0, rows, :] = jnp.where(lane == 0, dcol[0], jnp.where(lane == 1, dcol[1], 0.0))
            return 0

        lax.fori_loop(0, nq, finish, 0)
        dp_ref[0, :, 3 * LANES:4 * LANES] = dz_ref[0]

    tok = pl.BlockSpec((1, 1, 2, s, LANES), lambda bi, hp: (bi, hp, 0, 0, 0))
    tok_t = pl.BlockSpec((1, 1, 2, LANES, s), lambda bi, hp: (bi, hp, 0, 0, 0))
    return _call(
        body, name=name, grid=(b, pairs),
        in_specs=[tok, tok_t, tok, tok, tok,
                  pl.BlockSpec((1, s, LANES), lambda bi, hp: (bi, 0, hp))],
        out_specs=[pl.BlockSpec((1, s, 4 * LANES), lambda bi, hp: (bi, 0, hp)),
                   pl.BlockSpec((1, 1, s, LANES), lambda bi, hp: (bi, hp, 0, 0))],
        out_shape=[jax.ShapeDtypeStruct((b, s, a4), BF16),
                   jax.ShapeDtypeStruct((b, pairs, s, LANES), F32)],
        scratch_shapes=[pltpu.VMEM((2, LANES, s), F32), pltpu.VMEM((2, s, LANES), F32),
                        pltpu.VMEM((2, s, LANES), F32)],
        semantics=("parallel", "parallel"))(ka, kat, va, qab, doa, dz)


def _pool_tile(s):
    return min(256, s)


def _band(tb, window, shift):
    tgt = lax.broadcasted_iota(jnp.int32, (tb, tb), 0)
    src = lax.broadcasted_iota(jnp.int32, (tb, tb), 1) + shift
    return ((src <= tgt) & (src > tgt - window)).astype(BF16)


def _band_t(tb, window, shift):
    src = lax.broadcasted_iota(jnp.int32, (tb, tb), 0)
    tgt = lax.broadcasted_iota(jnp.int32, (tb, tb), 1) + shift
    return ((src <= tgt) & (src > tgt - window)).astype(BF16)


def _pool_fwd(pp, w_pool, scale, name):
    b, s, pw2 = pp.shape
    pw = pw2 // 2
    pg = pw // N_POOL_GROUPS
    tb = _pool_tile(s)
    nb = s // tb

    def body(u_ref, z_ref, w_ref, s_ref, o_ref):
        window = 2 << pl.program_id(1)
        band0 = _band(tb, window, 0)
        band1 = _band(tb, window, -tb)
        pos = lax.broadcasted_iota(jnp.int32, (tb, pg), 0)

        def block(i, _):
            rows = pl.ds(pl.multiple_of(i * tb, tb), tb)
            prev = pl.ds(pl.multiple_of(jnp.maximum(i - 1, 0) * tb, tb), tb)
            ub = u_ref[0, rows, :]
            up = u_ref[0, prev, :]
            up = jnp.where(i > 0, up, jnp.zeros_like(up))
            count = jnp.minimum(pos + i * tb + 1, window).astype(F32)
            pooled = (_dot(band0, ub, NN) + _dot(band1, up, NN)) / count - ub.astype(F32)
            mixed = _dot(pooled.astype(BF16), w_ref[0], NN) * s_ref[...]
            zv = z_ref[0, rows, :].astype(F32)
            o_ref[0, rows, :] = (mixed * zv * _sigmoid(zv)).astype(BF16)
            return 0

        lax.fori_loop(0, nb, block, 0)

    return _call(
        body, name=name, grid=(b, N_POOL_GROUPS),
        in_specs=[pl.BlockSpec((1, s, pg), lambda bi, g: (bi, 0, 2 * g)),
                  pl.BlockSpec((1, s, pg), lambda bi, g: (bi, 0, 2 * g + 1)),
                  pl.BlockSpec((1, pg, pg), lambda bi, g: (g, 0, 0)),
                  pl.BlockSpec((1, pg), lambda bi, g: (0, g))],
        out_specs=pl.BlockSpec((1, s, pg), lambda bi, g: (bi, 0, g)),
        out_shape=jax.ShapeDtypeStruct((b, s, pw), BF16),
        semantics=("parallel", "parallel"))(pp, pp, w_pool, scale)


def _pool_bwd(pp, dcat, w_pool, scale, first_block, name):
    b, s, pw2 = pp.shape
    pw = pw2 // 2
    pg = pw // N_POOL_GROUPS
    tb = _pool_tile(s)
    nb = s // tb

    def body(u_ref, z_ref, d_ref, w_ref, s_ref, dp_ref, dw_ref, ds_ref, dpool_s):
        @pl.when(pl.program_id(1) == 0)
        def _():
            dw_ref[...] = jnp.zeros_like(dw_ref)
            ds_ref[...] = jnp.zeros_like(ds_ref)

        window = 2 << pl.program_id(0)
        band0 = _band(tb, window, 0)
        band1 = _band(tb, window, -tb)
        band0_t = _band_t(tb, window, 0)
        band1_t = _band_t(tb, window, tb)
        pos = lax.broadcasted_iota(jnp.int32, (tb, pg), 0)

        def first(i, _):
            rows = pl.ds(pl.multiple_of(i * tb, tb), tb)
            prev = pl.ds(pl.multiple_of(jnp.maximum(i - 1, 0) * tb, tb), tb)
            ub = u_ref[0, rows, :]
            up = u_ref[0, prev, :]
            up = jnp.where(i > 0, up, jnp.zeros_like(up))
            count = jnp.minimum(pos + i * tb + 1, window).astype(F32)
            pooled = ((_dot(band0, ub, NN) + _dot(band1, up, NN)) / count
                      - ub.astype(F32)).astype(BF16)
            mixed = _dot(pooled, w_ref[0], NN)
            pm = mixed * s_ref[...]
            zv = z_ref[0, rows, :].astype(F32)
            sg = _sigmoid(zv)
            dpl = d_ref[0, rows, :].astype(F32)
            dpm = dpl * zv * sg
            dp_ref[0, rows, pg:2 * pg] = (dpl * pm * sg * (1.0 + zv * (1.0 - sg))).astype(BF16)
            ds_ref[...] += jnp.sum(dpm * mixed, axis=0, keepdims=True)
            dmixed = (dpm * s_ref[...]).astype(BF16)
            dw_ref[0] += _dot(pooled, dmixed, TN)
            dpool_s[rows, :] = _dot(dmixed, w_ref[0], NT)
            return 0

        lax.fori_loop(0, nb, first, 0)

        def second(i, _):
            rows = pl.ds(pl.multiple_of(i * tb, tb), tb)
            nxt_i = jnp.minimum(i + 1, nb - 1)
            nxt = pl.ds(pl.multiple_of(nxt_i * tb, tb), tb)
            count = jnp.minimum(pos + i * tb + 1, window).astype(F32)
            count_n = jnp.minimum(pos + nxt_i * tb + 1, window).astype(F32)
            dpb = dpool_s[rows, :]
            cur = (dpb / count).astype(BF16)
            nx = dpool_s[nxt, :] / count_n
            nx = jnp.where(i < nb - 1, nx, 0.0).astype(BF16)
            du = _dot(band0_t, cur, NN) + _dot(band1_t, nx, NN) - dpb
            dp_ref[0, rows, 0:pg] = du.astype(BF16)
            return 0

        lax.fori_loop(0, nb, second, 0)

    return _call(
        body, name=name, grid=(N_POOL_GROUPS, b),
        in_specs=[pl.BlockSpec((1, s, pg), lambda g, bi: (bi, 0, 2 * g)),
                  pl.BlockSpec((1, s, pg), lambda g, bi: (bi, 0, 2 * g + 1)),
                  pl.BlockSpec((1, s, pg), lambda g, bi: (bi, 0, first_block + g)),
                  pl.BlockSpec((1, pg, pg), lambda g, bi: (g, 0, 0)),
                  pl.BlockSpec((1, pg), lambda g, bi: (0, g))],
        out_specs=[pl.BlockSpec((1, s, 2 * pg), lambda g, bi: (bi, 0, g)),
                   pl.BlockSpec((1, pg, pg), lambda g, bi: (g, 0, 0)),
                   pl.BlockSpec((1, pg), lambda g, bi: (0, g))],
        out_shape=[jax.ShapeDtypeStruct((b, s, pw2), BF16),
                   jax.ShapeDtypeStruct((N_POOL_GROUPS, pg, pg), F32),
                   jax.ShapeDtypeStruct((1, pw), F32)],
        scratch_shapes=[pltpu.VMEM((s, pg), F32)],
        semantics=("parallel", "arbitrary"))(pp, pp, dcat, w_pool, scale)


def _adamw(recv, w, m, v, name):
    r, c = w.shape
    tr = min(128, r)
    c1 = 1.0 - ADAM_B1 ** ADAM_STEP
    c2 = 1.0 - ADAM_B2 ** ADAM_STEP

    def body(r_ref, w_ref, m_ref, v_ref, g_ref, d_ref, nm_ref, nv_ref):
        g = r_ref[0].astype(F32)
        for sl in range(1, N_DEV):
            g = g + r_ref[sl].astype(F32)
        mn = ADAM_B1 * m_ref[...] + (1.0 - ADAM_B1) * g
        vn = ADAM_B2 * v_ref[...] + (1.0 - ADAM_B2) * (g * g)
        m_hat = mn / c1
        v_hat = vn / c2
        g_ref[...] = g
        d_ref[...] = -ADAM_LR * (m_hat / (jnp.sqrt(v_hat) + ADAM_EPS) + ADAM_WD * w_ref[...])
        nm_ref[...] = mn
        nv_ref[...] = vn

    row = pl.BlockSpec((tr, c), lambda i: (i, 0))
    return _call(body, name=name, grid=(r // tr,),
                 in_specs=[pl.BlockSpec((N_DEV, tr, c), lambda i: (0, i, 0)), row, row, row],
                 out_specs=[row] * 4, out_shape=[jax.ShapeDtypeStruct((r, c), F32)] * 4,
                 semantics=("parallel",))(recv, w, m, v)


def _split_cols(w_full, a, heads, pw):
    sizes = (a, a, a, a, heads, pw, pw)
    out, o = [], 0
    for sz in sizes:
        out.append(w_full[:, o:o + sz])
        o += sz
    return out


def _pack_w_in(gathered, a, heads, pw):
    d = gathered.shape[1]
    w_full = jnp.transpose(gathered, (1, 0, 2)).reshape(d, -1)
    q, k, v, z, f, u, zp = _split_cols(w_full, a, heads, pw)
    pairs = a // LANES
    pg = pw // N_POOL_GROUPS
    wa = jnp.stack([t.reshape(d, pairs, LANES) for t in (q, k, v, z)], axis=2).reshape(d, 4 * a)
    wp = jnp.stack([t.reshape(d, N_POOL_GROUPS, pg) for t in (u, zp)], axis=2).reshape(d, 2 * pw)
    wf = jnp.pad(f, ((0, 0), (0, LANES - heads)))
    return wa, wp, wf


def _unpack_dw_in(dwa, dwp, dwf, a, heads, pw):
    d = dwa.shape[0]
    pairs = a // LANES
    pg = pw // N_POOL_GROUPS
    ra = dwa.reshape(d, pairs, 4, LANES)
    q, k, v, z = (ra[:, :, i, :].reshape(d, a) for i in range(4))
    rp = dwp.reshape(d, N_POOL_GROUPS, 2, pg)
    u, zp = (rp[:, :, i, :].reshape(d, pw) for i in range(2))
    full = jnp.concatenate([q, k, v, z, dwf[:, :heads], u, zp], axis=1)
    return jnp.transpose(full.reshape(d, N_DEV, -1), (1, 0, 2))


def kernel(x, p, norm_pre, norm_post, w_in, b_f, w_pool, pool_scale, w_out, w_pg, w_pe, loss_target, m_norm_pre, m_norm_post, m_w_in, m_b_f, m_w_pool, m_pool_scale, m_w_out, m_w_pg, m_w_pe, v_norm_pre, v_norm_post, v_w_in, v_b_f, v_w_pool, v_pool_scale, v_w_out, v_w_pg, v_w_pe):
    depth = w_in.shape[0]
    b, s, d = x.shape
    t = b * s
    heads = b_f.shape[1]
    a = heads * HEAD_DIM
    pairs = a // LANES
    pw = pool_scale.shape[1]
    pg = pw // N_POOL_GROUPS
    ple = p.shape[-1]
    mix_w = a + pw

    shard2d = {
        "w_in": w_in.reshape(depth * d, -1),
        "w_pool": w_pool.reshape(depth * N_POOL_GROUPS * (pg // N_DEV), pg),
        "w_out": w_out.reshape(depth * (mix_w // N_DEV), d),
        "w_pg": w_pg.reshape(depth * (d // N_DEV), d),
        "w_pe": w_pe.reshape(depth * ple, d // N_DEV),
    }
    names = list(shard2d)
    gathered = _exchange([(shard2d[nm].astype(BF16), False) for nm in names], "gather_weights")
    gw = dict(zip(names, gathered))

    layers = []
    for i in range(depth):
        g_in = gw["w_in"].reshape(N_DEV, depth, d, -1)[:, i]
        wa, wp, wf = _pack_w_in(g_in, a, heads, pw)
        g_pool = gw["w_pool"].reshape(N_DEV, depth, N_POOL_GROUPS, pg // N_DEV, pg)[:, i]
        wpool = jnp.transpose(g_pool, (1, 0, 2, 3)).reshape(N_POOL_GROUPS, pg, pg)
        wout = gw["w_out"].reshape(N_DEV, depth, mix_w // N_DEV, d)[:, i].reshape(mix_w, d)
        wpg = gw["w_pg"].reshape(N_DEV, depth, d // N_DEV, d)[:, i].reshape(d, d)
        g_pe = gw["w_pe"].reshape(N_DEV, depth, ple, d // N_DEV)[:, i]
        wpe = jnp.transpose(g_pe, (1, 0, 2)).reshape(ple, d)
        layers.append(dict(wa=wa, wp=wp, wf=wf, wpool=wpool, wout=wout, wpg=wpg, wpe=wpe))

    h = x.reshape(t, d)
    saved = []
    for i in range(depth):
        lw = layers[i]
        sv = dict(h=h)
        g_pre = norm_pre[i:i + 1]
        g_post = norm_post[i:i + 1]
        bf = jnp.pad(b_f[i:i + 1], ((0, 0), (0, LANES - heads)))
        scale = pool_scale[i:i + 1]
        hn = _rms_fwd(h, g_pre, "rms_pre")
        pa = _matmul([(hn, lw["wa"])], "nn", BF16, "proj_attn").reshape(b, s, 4 * a)
        pp = _matmul([(hn, lw["wp"])], "nn", BF16, "proj_pool").reshape(b, s, 2 * pw)
        fl = _matmul([(hn, lw["wf"])], "nn", F32, "proj_gate").reshape(b, s, LANES)
        c = _gates_fwd(fl, bf, "gates_fwd")
        qa, ka, kat, va, vt = _attn_prep_fwd(pa, c, "attn_prep_fwd")
        o, ga, lse = _attn_fwd(qa, ka, vt, pa, "attn_fwd")
        gp = _pool_fwd(pp, lw["wpool"], scale, "pool_fwd")
        ga2 = ga.reshape(t, a)
        gp2 = gp.reshape(t, pw)
        mix = _matmul([(ga2, lw["wout"][:a]), (gp2, lw["wout"][a:])], "nn", F32, "mix_out")
        h1, h1b = _post_fwd(h, mix, g_post, "post_fwd")
        pb = p[i].reshape(t, ple).astype(BF16)
        gpre = _matmul([(h1b, lw["wpg"])], "nn", F32, "ple_gate")
        e = _matmul([(pb, lw["wpe"])], "nn", F32, "ple_embed")
        h = _ple_fwd(h1, gpre, e, "ple_fwd")
        sv.update(hn=hn, pa=pa, pp=pp, fl=fl, bf=bf, qa=qa, ka=ka, kat=kat, va=va, o=o, lse=lse, ga=ga2,
                  gp=gp2, mix=mix,
                  h1b=h1b, pb=pb, gpre=gpre, e=e, g_pre=g_pre, g_post=g_post, scale=scale)
        saved.append(sv)

    dh, sq = _loss_bwd(h, loss_target.reshape(t, d), "loss")
    loss = lax.psum(0.5 * jnp.sum(sq) / d, MESH_AXES)

    big = {nm: [None] * depth for nm in names}
    small = {nm: [None] * depth for nm in ("norm_pre", "norm_post", "b_f", "pool_scale")}
    for i in reversed(range(depth)):
        lw, sv = layers[i], saved[i]
        de, dpre = _ple_bwd(dh, sv["gpre"], sv["e"], "ple_bwd")
        dwpe = _matmul([(sv["pb"], de)], "tn", BF16, "dw_pe")
        dwpg = _matmul([(sv["h1b"], dpre)], "tn", BF16, "dw_pg")
        t1 = _matmul([(dpre, lw["wpg"])], "nt", F32, "d_h1")
        dh1, dmix, dg_post = _post_bwd(dh, t1, sv["mix"], sv["g_post"], "post_bwd")
        dwout = jnp.concatenate([_matmul([(sv["ga"], dmix)], "tn", BF16, "dw_out_attn"),
                                 _matmul([(sv["gp"], dmix)], "tn", BF16, "dw_out_pool")], axis=0)
        dcat = _matmul([(dmix, lw["wout"])], "nt", BF16, "d_cat").reshape(b, s, mix_w)
        dpp, dwpool, dscale = _pool_bwd(sv["pp"], dcat, lw["wpool"], sv["scale"], a // pg, "pool_bwd")
        dpp = dpp.reshape(t, 2 * pw)
        qab, doa, dz = _attn_prep_bwd(dcat, sv["pa"], sv["o"], sv["lse"], sv["qa"], "attn_prep_bwd")
        dpa, dcp = _attn_bwd(sv["ka"], sv["kat"], sv["va"], qab, doa, dz, "attn_bwd")
        dc = jnp.transpose(dcp[..., :2], (0, 2, 1, 3)).reshape(b, s, heads)
        dc = jnp.pad(dc, ((0, 0), (0, 0), (0, LANES - heads)))
        dfl, dbf = _gates_bwd(dc, sv["fl"], sv["bf"], heads, "gates_bwd")
        dpa2 = dpa.reshape(t, 4 * a)
        dfl2 = dfl.reshape(t, LANES)
        dwa = _matmul([(sv["hn"], dpa2)], "tn", BF16, "dw_attn")
        dwp = _matmul([(sv["hn"], dpp)], "tn", BF16, "dw_pool_proj")
        dwf = _matmul([(sv["hn"], dfl2)], "tn", BF16, "dw_gate")
        dhn = _matmul([(dpa2, lw["wa"]), (dpp, lw["wp"]), (dfl2, lw["wf"])], "nt", F32, "d_hn")
        dh, dg_pre = _pre_bwd(sv["h"], dhn, dh1, sv["g_pre"], "pre_bwd")

        big["w_in"][i] = _unpack_dw_in(dwa, dwp, dwf, a, heads, pw)
        big["w_pool"][i] = jnp.transpose(
            dwpool.astype(BF16).reshape(N_POOL_GROUPS, N_DEV, pg // N_DEV, pg), (1, 0, 2, 3)
        ).reshape(N_DEV, N_POOL_GROUPS * (pg // N_DEV), pg)
        big["w_out"][i] = dwout.reshape(N_DEV, mix_w // N_DEV, d)
        big["w_pg"][i] = dwpg.reshape(N_DEV, d // N_DEV, d)
        big["w_pe"][i] = jnp.transpose(dwpe.reshape(ple, N_DEV, d // N_DEV), (1, 0, 2))
        small["norm_pre"][i] = dg_pre
        small["norm_post"][i] = dg_post
        small["b_f"][i] = jnp.sum(dbf, axis=0)
        small["pool_scale"][i] = dscale
    grad_x = dh.reshape(b, s, d)

    width = max(d, pw)
    small_names = ("norm_pre", "norm_post", "pool_scale", "b_f")

    def small_rows(get):
        rows = []
        for nm in small_names:
            for i in range(depth):
                v_ = get(nm, i)
                rows.append(jnp.pad(v_, ((0, 0), (0, width - v_.shape[1]))))
        return jnp.concatenate(rows, axis=0)

    small_g = small_rows(lambda nm, i: small[nm][i])
    items = []
    for nm in names:
        stacked = jnp.stack(big[nm], axis=1)
        items.append((stacked.reshape(N_DEV, -1, stacked.shape[-1]), True))
    items.append((small_g, False))
    received = _exchange(items, "exchange_grads")

    weights = dict(norm_pre=norm_pre, norm_post=norm_post, w_in=w_in, b_f=b_f, w_pool=w_pool,
                   pool_scale=pool_scale, w_out=w_out, w_pg=w_pg, w_pe=w_pe)
    mom1 = dict(norm_pre=m_norm_pre, norm_post=m_norm_post, w_in=m_w_in, b_f=m_b_f, w_pool=m_w_pool,
                pool_scale=m_pool_scale, w_out=m_w_out, w_pg=m_w_pg, w_pe=m_w_pe)
    mom2 = dict(norm_pre=v_norm_pre, norm_post=v_norm_post, w_in=v_w_in, b_f=v_b_f, w_pool=v_w_pool,
                pool_scale=v_pool_scale, w_out=v_w_out, w_pg=v_w_pg, w_pe=v_w_pe)

    results = {}
    for nm, recv in zip(names, received[:-1]):
        shp = weights[nm].shape
        flat = lambda arr: arr.reshape(recv.shape[1], recv.shape[2])
        outs = _adamw(recv, flat(weights[nm]), flat(mom1[nm]), flat(mom2[nm]), "adamw_" + nm)
        results[nm] = [o_.reshape(shp) for o_ in outs]

    small_w = small_rows(lambda nm, i: weights[nm][i:i + 1])
    small_m = small_rows(lambda nm, i: mom1[nm][i:i + 1])
    small_v = small_rows(lambda nm, i: mom2[nm][i:i + 1])
    outs = _adamw(received[-1], small_w, small_m, small_v, "adamw_small")
    for j, nm in enumerate(small_names):
        cols = weights[nm].shape[1]
        results[nm] = [o_[j * depth:(j + 1) * depth, :cols] for o_ in outs]

    order = ("norm_pre", "norm_post", "w_in", "b_f", "w_pool", "pool_scale", "w_out", "w_pg", "w_pe")
    return (loss, grad_x, *[results[nm][0] for nm in order], *[results[nm][1] for nm in order],
            *[results[nm][2] for nm in order], *[results[nm][3] for nm in order])
```

```python
import functools
import math

import jax
import jax.numpy as jnp
from jax import lax
from jax.experimental import pallas as pl
from jax.experimental.pallas import tpu as pltpu

N_DEV = 8
MESH_AXES = ("x", "y", "c")
HEAD_DIM = 64
LANES = 128
N_POOL_GROUPS = 4
EPS = 1e-6
ADAM_LR = 0.001
ADAM_B1 = 0.9
ADAM_B2 = 0.999
ADAM_EPS = 1e-08
ADAM_WD = 0.01
ADAM_STEP = 10
VMEM_LIMIT_BYTES = 56 * 1024 * 1024
F32 = jnp.float32
BF16 = jnp.bfloat16
NEG_INF = float("-inf")


def _call(body, *, name, grid, in_specs, out_specs, out_shape, scratch_shapes=(), semantics=None):
    return pl.pallas_call(
        body, name=name, grid=grid, in_specs=in_specs, out_specs=out_specs, out_shape=out_shape,
        scratch_shapes=list(scratch_shapes),
        compiler_params=pltpu.CompilerParams(dimension_semantics=semantics,
                                             vmem_limit_bytes=VMEM_LIMIT_BYTES))


def _sigmoid(z):
    return 1.0 / (1.0 + jnp.exp(-z))


def _dot(a, b, dims):
    return lax.dot_general(a, b, (dims, ((), ())), preferred_element_type=F32)


NN = ((1,), (0,))
NT = ((1,), (1,))
TN = ((0,), (0,))


def _exchange(items, name):
    n = len(items)
    modes = [s for _, s in items]
    out_shapes = []
    for a, s in items:
        shp = a.shape[1:] if s else a.shape
        out_shapes.append(jax.ShapeDtypeStruct((N_DEV,) + tuple(shp), a.dtype))

    def body(*refs):
        ins = refs[:n]
        outs = refs[n:2 * n]
        send_sems, recv_sems, local_sems = refs[2 * n:]
        x, y, c = (lax.axis_index(ax) for ax in MESH_AXES)
        me = 4 * x + 2 * y + c
        started = []
        for i in range(n):
            mine = ins[i].at[me] if modes[i] else ins[i]
            loc = pltpu.make_async_copy(mine, outs[i].at[me], local_sems.at[i])
            loc.start()
            started.append(loc)
        remote = []
        for k in range(1, N_DEV):
            px = x ^ ((k >> 2) & 1)
            py = y ^ ((k >> 1) & 1)
            pc = c ^ (k & 1)
            peer = me ^ k
            for i in range(n):
                src = ins[i].at[peer] if modes[i] else ins[i]
                cp = pltpu.make_async_remote_copy(
                    src_ref=src, dst_ref=outs[i].at[me],
                    send_sem=send_sems.at[i, k - 1], recv_sem=recv_sems.at[i, k - 1],
                    device_id=(px, py, pc), device_id_type=pl.DeviceIdType.MESH)
                cp.start()
                remote.append(cp)
        for cp in remote:
            cp.wait()
        for loc in started:
            loc.wait()

    hbm = pl.BlockSpec(memory_space=pltpu.HBM)
    return pl.pallas_call(
        body, name=name, out_shape=out_shapes,
        in_specs=[hbm] * n, out_specs=[hbm] * n,
        scratch_shapes=[pltpu.SemaphoreType.DMA((n, N_DEV - 1)),
                        pltpu.SemaphoreType.DMA((n, N_DEV - 1)),
                        pltpu.SemaphoreType.DMA((n,))],
    )(*[a for a, _ in items])


def _peer_copies(srcs, lands, modes, send_sems, recv_sems):
    x, y, c = (lax.axis_index(ax) for ax in MESH_AXES)
    me = 4 * x + 2 * y + c
    copies = []
    for k in range(1, N_DEV):
        peer_id = (x ^ ((k >> 2) & 1), y ^ ((k >> 1) & 1), c ^ (k & 1))
        for i, scatter in enumerate(modes):
            src = srcs[i].at[me ^ k] if scatter else srcs[i]
            pair = i * (N_DEV - 1) + k - 1
            copies.append(pltpu.make_async_remote_copy(
                src_ref=src, dst_ref=lands[i].at[me],
                send_sem=send_sems.at[pair], recv_sem=recv_sems.at[pair],
                device_id=peer_id, device_id_type=pl.DeviceIdType.MESH))
    return copies


def _exchange_start(items, after, name):
    n = len(items)
    modes = [s for _, s in items]
    srcs = [pltpu.with_memory_space_constraint(a, pltpu.HBM) for a, _ in items]
    lands = []
    for a, s in items:
        shp = (N_DEV,) + tuple(a.shape[1:] if s else a.shape)
        lands.append(pltpu.with_memory_space_constraint(lax.empty(shp, a.dtype), pltpu.HBM))

    def body(*refs):
        send_sems, recv_sems = refs[2 * n + 1], refs[2 * n + 2]
        token = refs[-1]
        for cp in _peer_copies(refs[:n], refs[n:2 * n], modes, send_sems, recv_sems):
            cp.start()
        token[...] = jnp.zeros_like(token)

    hbm = pl.BlockSpec(memory_space=pltpu.HBM)
    sem = pl.BlockSpec(memory_space=pltpu.SEMAPHORE)
    outs = pl.pallas_call(
        body, name=name,
        out_shape=(pltpu.SemaphoreType.DMA((n * (N_DEV - 1),)),
                   pltpu.SemaphoreType.DMA((n * (N_DEV - 1),)),
                   *[pltpu.HBM(a.shape, a.dtype) for a in srcs + lands],
                   jax.ShapeDtypeStruct((8, LANES), F32)),
        in_specs=[hbm] * (2 * n) + [pl.BlockSpec(memory_space=pl.ANY)],
        out_specs=(sem, sem, *([hbm] * (2 * n)), pl.BlockSpec(memory_space=pltpu.VMEM)),
        input_output_aliases={i: 2 + i for i in range(2 * n)},
        compiler_params=pltpu.CompilerParams(
            has_side_effects=pltpu.SideEffectType.DATAFLOW_SIDE_EFFECTING),
    )(*srcs, *lands, after)
    handle = (modes, outs[0], outs[1], list(outs[2:2 + n]), list(outs[2 + n:2 + 2 * n]))
    return handle, outs[-1][0, 0]


def _exchange_wait(handle, after, name):
    modes, send_sems, recv_sems, srcs, lands = handle
    n = len(modes)

    def body(*refs):
        for cp in _peer_copies(refs[:n], refs[n:2 * n], modes, refs[2 * n], refs[2 * n + 1]):
            cp.wait_send()
            cp.wait_recv()

    hbm = pl.BlockSpec(memory_space=pltpu.HBM)
    sem = pl.BlockSpec(memory_space=pltpu.SEMAPHORE)
    outs = pl.pallas_call(
        body, name=name,
        out_shape=tuple(pltpu.HBM(a.shape, a.dtype) for a in srcs + lands),
        in_specs=[hbm] * (2 * n) + [sem, sem, pl.BlockSpec(memory_space=pl.ANY)],
        out_specs=tuple([hbm] * (2 * n)),
        input_output_aliases={i: i for i in range(2 * n)},
        compiler_params=pltpu.CompilerParams(
            has_side_effects=pltpu.SideEffectType.DATAFLOW_SIDE_EFFECTING),
    )(*srcs, *lands, send_sems, recv_sems, after)
    return list(outs[n:])


def _with_own(slots, own, me):
    idx = lax.broadcasted_iota(jnp.int32, (N_DEV,) + (1,) * own.ndim, 0)
    return jnp.where(idx == me, own[None], slots)


def _matmul(pairs, mode, out_dtype, name, tm=512, tn=1024, tk=512):
    dims = {"nn": NN, "nt": NT, "tn": TN}[mode]
    a0, b0 = pairs[0]
    m_dim = a0.shape[1] if mode == "tn" else a0.shape[0]
    n_dim = b0.shape[0] if mode == "nt" else b0.shape[1]
    tm = min(tm, m_dim)
    tn = min(tn, n_dim)
    segs = []
    off = 0
    for a, _ in pairs:
        k_dim = a.shape[0] if mode == "tn" else a.shape[1]
        t = min(tk, k_dim)
        segs.append((off, k_dim // t, t))
        off += k_dim // t
    nk = off
    n_pairs = len(pairs)

    in_specs = []
    for (o, cnt, t) in segs:
        def kc(kk, o=o, cnt=cnt):
            return jnp.clip(kk - o, 0, cnt - 1)
        if mode == "tn":
            in_specs.append(pl.BlockSpec((t, tm), lambda i, j, kk, kc=kc: (kc(kk), i)))
        else:
            in_specs.append(pl.BlockSpec((tm, t), lambda i, j, kk, kc=kc: (i, kc(kk))))
        if mode == "nt":
            in_specs.append(pl.BlockSpec((tn, t), lambda i, j, kk, kc=kc: (j, kc(kk))))
        else:
            in_specs.append(pl.BlockSpec((t, tn), lambda i, j, kk, kc=kc: (kc(kk), j)))

    def body(*refs):
        out_ref = refs[2 * n_pairs]
        acc = refs[2 * n_pairs + 1]
        kk = pl.program_id(2)

        @pl.when(kk == 0)
        def _():
            acc[...] = jnp.zeros_like(acc)

        for idx, (o, cnt, _) in enumerate(segs):
            @pl.when((kk >= o) & (kk < o + cnt))
            def _(idx=idx):
                acc[...] += _dot(refs[2 * idx][...], refs[2 * idx + 1][...], dims)

        @pl.when(kk == nk - 1)
        def _():
            out_ref[...] = acc[...].astype(out_dtype)

    flat = [t for pr in pairs for t in pr]
    return _call(body, name=name, grid=(m_dim // tm, n_dim // tn, nk), in_specs=in_specs,
                 out_specs=pl.BlockSpec((tm, tn), lambda i, j, kk: (i, j)),
                 out_shape=jax.ShapeDtypeStruct((m_dim, n_dim), out_dtype),
                 scratch_shapes=[pltpu.VMEM((tm, tn), F32)],
                 semantics=("parallel", "parallel", "arbitrary"))(*flat)


def _row_tile(t):
    return min(256, t)


def _rms_fwd(h, g, name):
    t, d = h.shape
    tt = _row_tile(t)

    def body(h_ref, g_ref, o_ref):
        hv = h_ref[...]
        r = lax.rsqrt(jnp.mean(hv * hv, axis=-1, keepdims=True) + EPS)
        o_ref[...] = (hv * r * g_ref[...]).astype(BF16)

    row = pl.BlockSpec((tt, d), lambda i: (i, 0))
    vec = pl.BlockSpec((1, d), lambda i: (0, 0))
    return _call(body, name=name, grid=(t // tt,), in_specs=[row, vec], out_specs=row,
                 out_shape=jax.ShapeDtypeStruct((t, d), BF16), semantics=("parallel",))(h, g)


def _post_fwd(h, mix, g, name):
    t, d = h.shape
    tt = _row_tile(t)

    def body(h_ref, m_ref, g_ref, o_ref, ob_ref):
        mv = m_ref[...]
        r = lax.rsqrt(jnp.mean(mv * mv, axis=-1, keepdims=True) + EPS)
        h1 = h_ref[...] + mv * r * g_ref[...]
        o_ref[...] = h1
        ob_ref[...] = h1.astype(BF16)

    row = pl.BlockSpec((tt, d), lambda i: (i, 0))
    vec = pl.BlockSpec((1, d), lambda i: (0, 0))
    return _call(body, name=name, grid=(t // tt,), in_specs=[row, row, vec], out_specs=[row, row],
                 out_shape=[jax.ShapeDtypeStruct((t, d), F32), jax.ShapeDtypeStruct((t, d), BF16)],
                 semantics=("parallel",))(h, mix, g)


def _ple_fwd(h1, gpre, e, name):
    t, d = h1.shape
    tt = _row_tile(t)

    def body(h_ref, g_ref, e_ref, o_ref):
        o_ref[...] = h_ref[...] + _sigmoid(g_ref[...]) * e_ref[...]

    row = pl.BlockSpec((tt, d), lambda i: (i, 0))
    return _call(body, name=name, grid=(t // tt,), in_specs=[row, row, row], out_specs=row,
                 out_shape=jax.ShapeDtypeStruct((t, d), F32), semantics=("parallel",))(h1, gpre, e)


def _loss_bwd(y, target, name):
    t, d = y.shape
    tt = _row_tile(t)

    def body(y_ref, t_ref, dy_ref, s_ref):
        @pl.when(pl.program_id(0) == 0)
        def _():
            s_ref[...] = jnp.zeros_like(s_ref)
        diff = y_ref[...] - t_ref[...]
        dy_ref[...] = diff * (1.0 / d)
        s_ref[...] += jnp.sum(diff * diff, axis=0, keepdims=True)

    row = pl.BlockSpec((tt, d), lambda i: (i, 0))
    vec = pl.BlockSpec((1, d), lambda i: (0, 0))
    return _call(body, name=name, grid=(t // tt,), in_specs=[row, row], out_specs=[row, vec],
                 out_shape=[jax.ShapeDtypeStruct((t, d), F32), jax.ShapeDtypeStruct((1, d), F32)],
                 semantics=("arbitrary",))(y, target)


def _ple_bwd(dh2, gpre, e, name):
    t, d = dh2.shape
    tt = _row_tile(t)

    def body(d_ref, g_ref, e_ref, de_ref, dp_ref):
        gate = _sigmoid(g_ref[...])
        dv = d_ref[...]
        de_ref[...] = (dv * gate).astype(BF16)
        dp_ref[...] = (dv * e_ref[...] * gate * (1.0 - gate)).astype(BF16)

    row = pl.BlockSpec((tt, d), lambda i: (i, 0))
    return _call(body, name=name, grid=(t // tt,), in_specs=[row, row, row], out_specs=[row, row],
                 out_shape=[jax.ShapeDtypeStruct((t, d), BF16)] * 2,
                 semantics=("parallel",))(dh2, gpre, e)


def _post_bwd(dh2, t1, mix, g, name):
    t, d = dh2.shape
    tt = _row_tile(t)

    def body(d_ref, t_ref, m_ref, g_ref, dh_ref, dm_ref, dg_ref):
        @pl.when(pl.program_id(0) == 0)
        def _():
            dg_ref[...] = jnp.zeros_like(dg_ref)
        dh1 = d_ref[...] + t_ref[...]
        mv = m_ref[...]
        r = lax.rsqrt(jnp.mean(mv * mv, axis=-1, keepdims=True) + EPS)
        dh_ref[...] = dh1
        dg_ref[...] += jnp.sum(dh1 * mv * r, axis=0, keepdims=True)
        w = dh1 * g_ref[...]
        dot = jnp.mean(w * mv, axis=-1, keepdims=True)
        dm_ref[...] = (r * w - mv * (r * r * r) * dot).astype(BF16)

    row = pl.BlockSpec((tt, d), lambda i: (i, 0))
    vec = pl.BlockSpec((1, d), lambda i: (0, 0))
    return _call(body, name=name, grid=(t // tt,), in_specs=[row, row, row, vec],
                 out_specs=[row, row, vec],
                 out_shape=[jax.ShapeDtypeStruct((t, d), F32), jax.ShapeDtypeStruct((t, d), BF16),
                            jax.ShapeDtypeStruct((1, d), F32)],
                 semantics=("arbitrary",))(dh2, t1, mix, g)


def _pre_bwd(h, dhn, dh1, g, name):
    t, d = h.shape
    tt = _row_tile(t)

    def body(h_ref, dn_ref, d1_ref, g_ref, dh_ref, dg_ref):
        @pl.when(pl.program_id(0) == 0)
        def _():
            dg_ref[...] = jnp.zeros_like(dg_ref)
        hv = h_ref[...]
        dn = dn_ref[...]
        r = lax.rsqrt(jnp.mean(hv * hv, axis=-1, keepdims=True) + EPS)
        dg_ref[...] += jnp.sum(dn * hv * r, axis=0, keepdims=True)
        w = dn * g_ref[...]
        dot = jnp.mean(w * hv, axis=-1, keepdims=True)
        dh_ref[...] = d1_ref[...] + r * w - hv * (r * r * r) * dot

    row = pl.BlockSpec((tt, d), lambda i: (i, 0))
    vec = pl.BlockSpec((1, d), lambda i: (0, 0))
    return _call(body, name=name, grid=(t // tt,), in_specs=[row, row, row, vec],
                 out_specs=[row, vec],
                 out_shape=[jax.ShapeDtypeStruct((t, d), F32), jax.ShapeDtypeStruct((1, d), F32)],
                 semantics=("arbitrary",))(h, dhn, dh1, g)


def _split3(v):
    hi = v.astype(BF16)
    r1 = v - hi.astype(F32)
    mid = r1.astype(BF16)
    lo = (r1 - mid.astype(F32)).astype(BF16)
    return hi, mid, lo


def _gates_fwd(fl, bf, name):
    b, s, _ = fl.shape

    def body(f_ref, b_ref, c_ref):
        xv = f_ref[0] + b_ref[...]
        lf = jnp.minimum(xv, 0.0) - jnp.log(1.0 + jnp.exp(-jnp.abs(xv)))
        dst = lax.broadcasted_iota(jnp.int32, (s, s), 0)
        src = lax.broadcasted_iota(jnp.int32, (s, s), 1)
        lower = (src <= dst).astype(BF16)
        acc = jnp.zeros((s, LANES), F32)
        for part in _split3(lf):
            acc = acc + _dot(lower, part, NN)
        c_ref[0] = acc

    blk = pl.BlockSpec((1, s, LANES), lambda i: (i, 0, 0))
    return _call(body, name=name, grid=(b,),
                 in_specs=[blk, pl.BlockSpec((1, LANES), lambda i: (0, 0))],
                 out_specs=blk, out_shape=jax.ShapeDtypeStruct((b, s, LANES), F32),
                 semantics=("parallel",))(fl, bf)


def _gates_bwd(dc, fl, bf, heads, name):
    b, s, _ = fl.shape

    def body(d_ref, f_ref, b_ref, o_ref, db_ref):
        xv = f_ref[0] + b_ref[...]
        dst = lax.broadcasted_iota(jnp.int32, (s, s), 0)
        src = lax.broadcasted_iota(jnp.int32, (s, s), 1)
        later = (src >= dst).astype(BF16)
        dlf = jnp.zeros((s, LANES), F32)
        for part in _split3(d_ref[0]):
            dlf = dlf + _dot(later, part, NN)
        lane = lax.broadcasted_iota(jnp.int32, (s, LANES), 1)
        dfl = jnp.where(lane < heads, dlf * _sigmoid(-xv), 0.0)
        o_ref[0] = dfl.astype(BF16)
        db_ref[0] = jnp.sum(dfl, axis=0, keepdims=True)

    blk = pl.BlockSpec((1, s, LANES), lambda i: (i, 0, 0))
    return _call(body, name=name, grid=(b,),
                 in_specs=[blk, blk, pl.BlockSpec((1, LANES), lambda i: (0, 0))],
                 out_specs=[blk, pl.BlockSpec((1, 1, LANES), lambda i: (i, 0, 0))],
                 out_shape=[jax.ShapeDtypeStruct((b, s, LANES), BF16),
                            jax.ShapeDtypeStruct((b, 1, LANES), F32)],
                 semantics=("parallel",))(dc, fl, bf)


LANE_CQ = 64
LANE_CK = 67
LANE_LSE = 70
LANE_D = 64
N_PARTS = 3


def _attn_tiles(s):
    return min(512, s), min(256, s)


def _lanes_in(lane, first):
    return (lane >= first) & (lane < first + N_PARTS)


def _attn_prep_fwd(pa, c, name):
    b, s, a4 = pa.shape
    pairs = a4 // (4 * LANES)
    scale = 1.0 / math.sqrt(HEAD_DIM)
    wide = (1 + N_PARTS) * LANES

    def body(q_ref, k_ref, v_ref, c_ref, qa_ref, ka_ref, kat_ref, va_ref, vt_ref):
        hp = pl.program_id(1)
        c3 = jnp.concatenate(_split3(c_ref[0]), axis=1)
        qx = jnp.concatenate([q_ref[0], c3], axis=1)
        kx = jnp.concatenate([k_ref[0], c3], axis=1)
        vv = v_ref[0]
        lane = lax.broadcasted_iota(jnp.int32, (s, LANES), 1)
        row = lax.broadcasted_iota(jnp.int32, (wide, LANES), 0)
        col = lax.broadcasted_iota(jnp.int32, (wide, LANES), 1)
        r128 = lax.broadcasted_iota(jnp.int32, (LANES, LANES), 0)
        c128 = lax.broadcasted_iota(jnp.int32, (LANES, LANES), 1)
        ident = (r128 == c128).astype(BF16)
        for j in range(2):
            head = 2 * hp + j
            move = (row == col + HEAD_DIM * j) & (col < HEAD_DIM)
            move128 = (r128 == c128 + HEAD_DIM * j) & (c128 < HEAD_DIM)

            def pick(first, head=head):
                hit = (row == LANES + head) & (col == first)
                for i in range(1, N_PARTS):
                    hit = hit | ((row == LANES * (i + 1) + head) & (col == first + i))
                return hit

            mq = (jnp.where(move, scale, 0.0) + jnp.where(pick(LANE_CQ), 1.0, 0.0)).astype(BF16)
            mk = (jnp.where(move, 1.0, 0.0) - jnp.where(pick(LANE_CK), 1.0, 0.0)).astype(BF16)
            qa = _dot(qx, mq, NN) + jnp.where(_lanes_in(lane, LANE_CK), 1.0, 0.0)
            ka = _dot(kx, mk, NN) + jnp.where(
                _lanes_in(lane, LANE_CQ) | _lanes_in(lane, LANE_LSE), 1.0, 0.0)
            va = _dot(vv, move128.astype(BF16), NN) + jnp.where(_lanes_in(lane, LANE_D), 1.0, 0.0)
            kab = ka.astype(BF16)
            qa_ref[0, 0, j] = qa.astype(BF16)
            ka_ref[0, 0, j] = kab
            kat_ref[0, 0, j] = _dot(ident, kab, NT).astype(BF16)
            va_ref[0, 0, j] = va.astype(BF16)
        vt_ref[0, 0] = _dot(ident, vv, NT).astype(BF16)

    col_blk = lambda cidx: pl.BlockSpec((1, s, LANES), lambda bi, hp: (bi, 0, 4 * hp + cidx))
    tok = pl.BlockSpec((1, 1, 2, s, LANES), lambda bi, hp: (bi, hp, 0, 0, 0))
    tok_t = pl.BlockSpec((1, 1, 2, LANES, s), lambda bi, hp: (bi, hp, 0, 0, 0))
    tok_shape = jax.ShapeDtypeStruct((b, pairs, 2, s, LANES), BF16)
    return _call(
        body, name=name, grid=(b, pairs),
        in_specs=[col_blk(0), col_blk(1), col_blk(2),
                  pl.BlockSpec((1, s, LANES), lambda bi, hp: (bi, 0, 0))],
        out_specs=[tok, tok, tok_t, tok,
                   pl.BlockSpec((1, 1, LANES, s), lambda bi, hp: (bi, hp, 0, 0))],
        out_shape=[tok_shape, tok_shape, jax.ShapeDtypeStruct((b, pairs, 2, LANES, s), BF16),
                   tok_shape, jax.ShapeDtypeStruct((b, pairs, LANES, s), BF16)],
        semantics=("parallel", "parallel"))(pa, pa, pa, c)


def _attn_fwd(qa, ka, vt, pa, name):
    b, pairs, _, s, _ = qa.shape
    a = pairs * LANES
    tq, tk = _attn_tiles(s)
    ratio = tq // tk

    def body(q_ref, k_ref, vt_ref, z_ref, o_ref, g_ref, lse_ref):
        qi = pl.program_id(2)
        key_i = lax.broadcasted_iota(jnp.int32, (tk, tq), 0)
        qry_i = lax.broadcasted_iota(jnp.int32, (tk, tq), 1)
        qv = [q_ref[0, 0, j] for j in range(2)]

        def step(kj, carry, diag):
            k0 = pl.multiple_of(kj * tk, tk)
            out = []
            for j in range(2):
                m, l, acc = carry[j]
                st = _dot(k_ref[0, 0, j, pl.ds(k0, tk), :], qv[j], NT)
                if diag is not None:
                    st = jnp.where(key_i + diag * tk <= qry_i, st, NEG_INF)
                m_new = jnp.maximum(m, jnp.max(st, axis=0, keepdims=True))
                alpha = jnp.exp(m - m_new)
                pt = jnp.exp(st - m_new)
                l_new = alpha * l + jnp.sum(pt, axis=0, keepdims=True)
                vb = vt_ref[0, 0, HEAD_DIM * j:HEAD_DIM * (j + 1), pl.ds(k0, tk)]
                out.append((m_new, l_new, alpha * acc + _dot(vb, pt.astype(BF16), NN)))
            return tuple(out)

        init = (jnp.full((1, tq), NEG_INF, F32), jnp.zeros((1, tq), F32),
                jnp.zeros((HEAD_DIM, tq), F32))
        carry = lax.fori_loop(0, ratio * qi, lambda kj, cr: step(kj, cr, None), (init, init))
        for i in range(ratio):
            carry = step(ratio * qi + i, carry, i)
        heads_out = []
        for j in range(2):
            m, l, acc = carry[j]
            heads_out.append(acc / l)
            lse_ref[0, 0, j:j + 1, :] = m + jnp.log(l)
        ov = jnp.transpose(jnp.concatenate(heads_out, axis=0))
        o_ref[0] = ov.astype(BF16)
        zv = z_ref[0].astype(F32)
        g_ref[0] = (ov * zv * _sigmoid(zv)).astype(BF16)

    return _call(
        body, name=name, grid=(b, pairs, s // tq),
        in_specs=[pl.BlockSpec((1, 1, 2, tq, LANES), lambda bi, hp, qi: (bi, hp, 0, qi, 0)),
                  pl.BlockSpec((1, 1, 2, s, LANES), lambda bi, hp, qi: (bi, hp, 0, 0, 0)),
                  pl.BlockSpec((1, 1, LANES, s), lambda bi, hp, qi: (bi, hp, 0, 0)),
                  pl.BlockSpec((1, tq, LANES), lambda bi, hp, qi: (bi, qi, 4 * hp + 3))],
        out_specs=[pl.BlockSpec((1, tq, LANES), lambda bi, hp, qi: (bi, qi, hp)),
                   pl.BlockSpec((1, tq, LANES), lambda bi, hp, qi: (bi, qi, hp)),
                   pl.BlockSpec((1, 1, 2, tq), lambda bi, hp, qi: (bi, hp, 0, qi))],
        out_shape=[jax.ShapeDtypeStruct((b, s, a), BF16), jax.ShapeDtypeStruct((b, s, a), BF16),
                   jax.ShapeDtypeStruct((b, pairs, 2, s), F32)],
        semantics=("parallel", "parallel", "arbitrary"))(qa, ka, vt, pa)


def _attn_prep_bwd(dcat, pa, o, lse, qa, name):
    b, pairs, _, s, _ = qa.shape
    a = pairs * LANES
    sub = 16

    def body(da_ref, z_ref, o_ref, lse_ref, qa_ref, qab_ref, doa_ref, dz_ref):
        zv = z_ref[0].astype(F32)
        dav = da_ref[0].astype(F32)
        ov = o_ref[0].astype(F32)
        sg = _sigmoid(zv)
        dov = dav * zv * sg
        dz_ref[0] = (dav * ov * sg * (1.0 + zv * (1.0 - sg))).astype(BF16)
        prod = dov * ov
        dob = dov.astype(BF16)
        lane = lax.broadcasted_iota(jnp.int32, (s, LANES), 1)
        r128 = lax.broadcasted_iota(jnp.int32, (LANES, LANES), 0)
        c128 = lax.broadcasted_iota(jnp.int32, (LANES, LANES), 1)
        prow = lax.broadcasted_iota(jnp.int32, (sub, s), 0)
        srow = lax.broadcasted_iota(jnp.int32, (sub, LANES), 0)
        scol = lax.broadcasted_iota(jnp.int32, (sub, LANES), 1)
        place = ((scol == srow + LANE_LSE) & (srow < N_PARTS)).astype(BF16)
        for j in range(2):
            in_head = (lane >= HEAD_DIM * j) & (lane < HEAD_DIM * (j + 1))
            dparts = _split3(jnp.sum(jnp.where(in_head, prod, 0.0), axis=1, keepdims=True))
            move128 = ((r128 == c128 + HEAD_DIM * j) & (c128 < HEAD_DIM)).astype(BF16)
            doa = _dot(dob, move128, NN)
            for i in range(N_PARTS):
                doa = jnp.where(lane == LANE_D + i, -dparts[i].astype(F32), doa)
            doa_ref[0, 0, j] = doa.astype(BF16)
            lparts = _split3(lse_ref[0, 0, j:j + 1, :])
            pmat = jnp.zeros((sub, s), BF16)
            for i in range(N_PARTS):
                pmat = jnp.where(prow == i, lparts[i], pmat)
            lcol = _dot(pmat, place, TN)
            qab_ref[0, 0, j] = (qa_ref[0, 0, j].astype(F32) - lcol).astype(BF16)

    tok = pl.BlockSpec((1, 1, 2, s, LANES), lambda bi, hp: (bi, hp, 0, 0, 0))
    tok_shape = jax.ShapeDtypeStruct((b, pairs, 2, s, LANES), BF16)
    pair_blk = pl.BlockSpec((1, s, LANES), lambda bi, hp: (bi, 0, hp))
    return _call(
        body, name=name, grid=(b, pairs),
        in_specs=[pair_blk,
                  pl.BlockSpec((1, s, LANES), lambda bi, hp: (bi, 0, 4 * hp + 3)),
                  pair_blk,
                  pl.BlockSpec((1, 1, 2, s), lambda bi, hp: (bi, hp, 0, 0)),
                  tok],
        out_specs=[tok, tok, pair_blk],
        out_shape=[tok_shape, tok_shape, jax.ShapeDtypeStruct((b, s, a), BF16)],
        semantics=("parallel", "parallel"))(dcat, pa, o, lse, qa)


def _attn_bwd(ka, kat, va, qab, doa, dz, name):
    b, pairs, _, s, _ = ka.shape
    a4 = 4 * pairs * LANES
    tq, tk = _attn_tiles(s)
    ratio = tq // tk
    nq, nk = s // tq, s // tk
    scale = 1.0 / math.sqrt(HEAD_DIM)

    def body(k_ref, kt_ref, v_ref, q_ref, do_ref, dz_ref, dp_ref, dc_ref, dqt_acc, dk_s, dv_s):
        key_i = lax.broadcasted_iota(jnp.int32, (tk, tq), 0)
        qry_i = lax.broadcasted_iota(jnp.int32, (tk, tq), 1)
        lane = lax.broadcasted_iota(jnp.int32, (tq, LANES), 1)
        low = lane < HEAD_DIM

        dqt_acc[...] = jnp.zeros_like(dqt_acc)

        def key_block(kj, _):
            krows = pl.ds(pl.multiple_of(kj * tk, tk), tk)
            kb = [k_ref[0, 0, j, krows, :] for j in range(2)]
            vb = [v_ref[0, 0, j, krows, :] for j in range(2)]
            ktb = [kt_ref[0, 0, j, :, krows] for j in range(2)]
            qd = kj // ratio

            def query_block(qi, carry, masked):
                qrows = pl.ds(pl.multiple_of(qi * tq, tq), tq)
                out = []
                for j in range(2):
                    dk, dv = carry[j]
                    qb = q_ref[0, 0, j, qrows, :]
                    dob = do_ref[0, 0, j, qrows, :]
                    pt = jnp.exp(_dot(kb[j], qb, NT))
                    if masked:
                        pt = jnp.where(key_i + (kj * tk - qi * tq) <= qry_i, pt, 0.0)
                    dst = pt * _dot(vb[j], dob, NT)
                    dsb = dst.astype(BF16)
                    dv = dv + _dot(pt.astype(BF16), dob, NN)
                    dk = dk + _dot(dsb, qb, NN)
                    dqt_acc[j, :, qrows] += _dot(ktb[j], dsb, NN)
                    out.append((dk, dv))
                return tuple(out)

            zero = jnp.zeros((tk, LANES), F32)
            carry = query_block(qd, ((zero, zero), (zero, zero)), True)
            carry = lax.fori_loop(qd + 1, nq, lambda qi, cr: query_block(qi, cr, False), carry)
            for j in range(2):
                dk_s[j, krows, :] = carry[j][0]
                dv_s[j, krows, :] = carry[j][1]
            return 0

        lax.fori_loop(0, nk, key_block, 0)

        def finish(i, _):
            rows = pl.ds(pl.multiple_of(i * tq, tq), tq)
            dq = [jnp.transpose(dqt_acc[j, :, rows]) for j in range(2)]
            dk = [dk_s[j, rows, :] for j in range(2)]
            dv = [dv_s[j, rows, :] for j in range(2)]
            dcol = [dq[j][:, LANE_CQ:LANE_CQ + 1] - dk[j][:, LANE_CK:LANE_CK + 1] for j in range(2)]
            dq = [dq[j] * scale for j in range(2)]
            for t, val in enumerate((dq, dk, dv)):
                merged = jnp.where(low, val[0], pltpu.roll(val[1], HEAD_DIM, 1))
                dp_ref[0, rows, t * LANES:(t + 1) * LANES] = merged.astype(BF16)
            dc_ref[0, 0, rows, :] = jnp.where(lane == 0, dcol[0], jnp.where(lane == 1, dcol[1], 0.0))
            return 0

        lax.fori_loop(0, nq, finish, 0)
        dp_ref[0, :, 3 * LANES:4 * LANES] = dz_ref[0]

    tok = pl.BlockSpec((1, 1, 2, s, LANES), lambda bi, hp: (bi, hp, 0, 0, 0))
    tok_t = pl.BlockSpec((1, 1, 2, LANES, s), lambda bi, hp: (bi, hp, 0, 0, 0))
    return _call(
        body, name=name, grid=(b, pairs),
        in_specs=[tok, tok_t, tok, tok, tok,
                  pl.BlockSpec((1, s, LANES), lambda bi, hp: (bi, 0, hp))],
        out_specs=[pl.BlockSpec((1, s, 4 * LANES), lambda bi, hp: (bi, 0, hp)),
                   pl.BlockSpec((1, 1, s, LANES), lambda bi, hp: (bi, hp, 0, 0))],
        out_shape=[jax.ShapeDtypeStruct((b, s, a4), BF16),
                   jax.ShapeDtypeStruct((b, pairs, s, LANES), F32)],
        scratch_shapes=[pltpu.VMEM((2, LANES, s), F32), pltpu.VMEM((2, s, LANES), F32),
                        pltpu.VMEM((2, s, LANES), F32)],
        semantics=("parallel", "parallel"))(ka, kat, va, qab, doa, dz)


def _pool_tile(s):
    return min(256, s)


def _band(tb, window, shift):
    tgt = lax.broadcasted_iota(jnp.int32, (tb, tb), 0)
    src = lax.broadcasted_iota(jnp.int32, (tb, tb), 1) + shift
    return ((src <= tgt) & (src > tgt - window)).astype(BF16)


def _band_t(tb, window, shift):
    src = lax.broadcasted_iota(jnp.int32, (tb, tb), 0)
    tgt = lax.broadcasted_iota(jnp.int32, (tb, tb), 1) + shift
    return ((src <= tgt) & (src > tgt - window)).astype(BF16)


def _pool_fwd(pp, w_pool, scale, name):
    b, s, pw2 = pp.shape
    pw = pw2 // 2
    pg = pw // N_POOL_GROUPS
    tb = _pool_tile(s)
    nb = s // tb

    def body(u_ref, z_ref, w_ref, s_ref, o_ref):
        window = 2 << pl.program_id(1)
        band0 = _band(tb, window, 0)
        band1 = _band(tb, window, -tb)
        pos = lax.broadcasted_iota(jnp.int32, (tb, pg), 0)

        def block(i, _):
            rows = pl.ds(pl.multiple_of(i * tb, tb), tb)
            prev = pl.ds(pl.multiple_of(jnp.maximum(i - 1, 0) * tb, tb), tb)
            ub = u_ref[0, rows, :]
            up = u_ref[0, prev, :]
            up = jnp.where(i > 0, up, jnp.zeros_like(up))
            count = jnp.minimum(pos + i * tb + 1, window).astype(F32)
            pooled = (_dot(band0, ub, NN) + _dot(band1, up, NN)) / count - ub.astype(F32)
            mixed = _dot(pooled.astype(BF16), w_ref[0], NN) * s_ref[...]
            zv = z_ref[0, rows, :].astype(F32)
            o_ref[0, rows, :] = (mixed * zv * _sigmoid(zv)).astype(BF16)
            return 0

        lax.fori_loop(0, nb, block, 0)

    return _call(
        body, name=name, grid=(b, N_POOL_GROUPS),
        in_specs=[pl.BlockSpec((1, s, pg), lambda bi, g: (bi, 0, 2 * g)),
                  pl.BlockSpec((1, s, pg), lambda bi, g: (bi, 0, 2 * g + 1)),
                  pl.BlockSpec((1, pg, pg), lambda bi, g: (g, 0, 0)),
                  pl.BlockSpec((1, pg), lambda bi, g: (0, g))],
        out_specs=pl.BlockSpec((1, s, pg), lambda bi, g: (bi, 0, g)),
        out_shape=jax.ShapeDtypeStruct((b, s, pw), BF16),
        semantics=("parallel", "parallel"))(pp, pp, w_pool, scale)


def _pool_bwd(pp, dcat, w_pool, scale, first_block, name):
    b, s, pw2 = pp.shape
    pw = pw2 // 2
    pg = pw // N_POOL_GROUPS
    tb = _pool_tile(s)
    nb = s // tb

    def body(u_ref, z_ref, d_ref, w_ref, s_ref, dp_ref, dw_ref, ds_ref, dpool_s):
        @pl.when(pl.program_id(1) == 0)
        def _():
            dw_ref[...] = jnp.zeros_like(dw_ref)
            ds_ref[...] = jnp.zeros_like(ds_ref)

        window = 2 << pl.program_id(0)
        band0 = _band(tb, window, 0)
        band1 = _band(tb, window, -tb)
        band0_t = _band_t(tb, window, 0)
        band1_t = _band_t(tb, window, tb)
        pos = lax.broadcasted_iota(jnp.int32, (tb, pg), 0)

        def first(i, _):
            rows = pl.ds(pl.multiple_of(i * tb, tb), tb)
            prev = pl.ds(pl.multiple_of(jnp.maximum(i - 1, 0) * tb, tb), tb)
            ub = u_ref[0, rows, :]
            up = u_ref[0, prev, :]
            up = jnp.where(i > 0, up, jnp.zeros_like(up))
            count = jnp.minimum(pos + i * tb + 1, window).astype(F32)
            pooled = ((_dot(band0, ub, NN) + _dot(band1, up, NN)) / count
                      - ub.astype(F32)).astype(BF16)
            mixed = _dot(pooled, w_ref[0], NN)
            pm = mixed * s_ref[...]
            zv = z_ref[0, rows, :].astype(F32)
            sg = _sigmoid(zv)
            dpl = d_ref[0, rows, :].astype(F32)
            dpm = dpl * zv * sg
            dp_ref[0, rows, pg:2 * pg] = (dpl * pm * sg * (1.0 + zv * (1.0 - sg))).astype(BF16)
            ds_ref[...] += jnp.sum(dpm * mixed, axis=0, keepdims=True)
            dmixed = (dpm * s_ref[...]).astype(BF16)
            dw_ref[0] += _dot(pooled, dmixed, TN)
            dpool_s[rows, :] = _dot(dmixed, w_ref[0], NT)
            return 0

        lax.fori_loop(0, nb, first, 0)

        def second(i, _):
            rows = pl.ds(pl.multiple_of(i * tb, tb), tb)
            nxt_i = jnp.minimum(i + 1, nb - 1)
            nxt = pl.ds(pl.multiple_of(nxt_i * tb, tb), tb)
            count = jnp.minimum(pos + i * tb + 1, window).astype(F32)
            count_n = jnp.minimum(pos + nxt_i * tb + 1, window).astype(F32)
            dpb = dpool_s[rows, :]
            cur = (dpb / count).astype(BF16)
            nx = dpool_s[nxt, :] / count_n
            nx = jnp.where(i < nb - 1, nx, 0.0).astype(BF16)
            du = _dot(band0_t, cur, NN) + _dot(band1_t, nx, NN) - dpb
            dp_ref[0, rows, 0:pg] = du.astype(BF16)
            return 0

        lax.fori_loop(0, nb, second, 0)

    return _call(
        body, name=name, grid=(N_POOL_GROUPS, b),
        in_specs=[pl.BlockSpec((1, s, pg), lambda g, bi: (bi, 0, 2 * g)),
                  pl.BlockSpec((1, s, pg), lambda g, bi: (bi, 0, 2 * g + 1)),
                  pl.BlockSpec((1, s, pg), lambda g, bi: (bi, 0, first_block + g)),
                  pl.BlockSpec((1, pg, pg), lambda g, bi: (g, 0, 0)),
                  pl.BlockSpec((1, pg), lambda g, bi: (0, g))],
        out_specs=[pl.BlockSpec((1, s, 2 * pg), lambda g, bi: (bi, 0, g)),
                   pl.BlockSpec((1, pg, pg), lambda g, bi: (g, 0, 0)),
                   pl.BlockSpec((1, pg), lambda g, bi: (0, g))],
        out_shape=[jax.ShapeDtypeStruct((b, s, pw2), BF16),
                   jax.ShapeDtypeStruct((N_POOL_GROUPS, pg, pg), F32),
                   jax.ShapeDtypeStruct((1, pw), F32)],
        scratch_shapes=[pltpu.VMEM((s, pg), F32)],
        semantics=("parallel", "arbitrary"))(pp, pp, dcat, w_pool, scale)


def _adamw(recvs, owns, w, m, v, name):
    depth = len(recvs)
    r, c = owns[0].shape
    tr = min(128, r)
    nb = r // tr
    c1 = 1.0 - ADAM_B1 ** ADAM_STEP
    c2 = 1.0 - ADAM_B2 ** ADAM_STEP

    def body(*refs):
        recv_refs, own_refs = refs[:depth], refs[depth:2 * depth]
        w_ref, m_ref, v_ref, g_ref, d_ref, nm_ref, nv_ref = refs[2 * depth:]
        x, y, core = (lax.axis_index(ax) for ax in MESH_AXES)
        me = 4 * x + 2 * y + core
        for layer in range(depth):
            @pl.when(pl.program_id(0) == layer)
            def _(layer=layer):
                own = own_refs[layer][...].astype(F32)
                g = jnp.where(me == 0, own, recv_refs[layer][0].astype(F32))
                for sl in range(1, N_DEV):
                    g = g + jnp.where(me == sl, own, recv_refs[layer][sl].astype(F32))
                mn = ADAM_B1 * m_ref[...] + (1.0 - ADAM_B1) * g
                vn = ADAM_B2 * v_ref[...] + (1.0 - ADAM_B2) * (g * g)
                m_hat = mn / c1
                v_hat = vn / c2
                g_ref[...] = g
                d_ref[...] = -ADAM_LR * (m_hat / (jnp.sqrt(v_hat) + ADAM_EPS) + ADAM_WD * w_ref[...])
                nm_ref[...] = mn
                nv_ref[...] = vn

    def blk(layer):
        return lambda l, i: jnp.clip(i + (l - layer) * nb, 0, nb - 1)

    in_specs = [pl.BlockSpec((N_DEV, tr, c), lambda l, i, f=blk(layer): (0, f(l, i), 0))
                for layer in range(depth)]
    in_specs += [pl.BlockSpec((tr, c), lambda l, i, f=blk(layer): (f(l, i), 0))
                 for layer in range(depth)]
    row = pl.BlockSpec((tr, c), lambda l, i: (l * nb + i, 0))
    return _call(body, name=name, grid=(depth, nb), in_specs=in_specs + [row, row, row],
                 out_specs=[row] * 4, out_shape=[jax.ShapeDtypeStruct((depth * r, c), F32)] * 4,
                 semantics=("arbitrary", "arbitrary"))(*recvs, *owns, w, m, v)


def _split_cols(w_full, a, heads, pw):
    sizes = (a, a, a, a, heads, pw, pw)
    out, o = [], 0
    for sz in sizes:
        out.append(w_full[:, o:o + sz])
        o += sz
    return out


def _pack_w_in(gathered, a, heads, pw):
    d = gathered.shape[1]
    w_full = jnp.transpose(gathered, (1, 0, 2)).reshape(d, -1)
    q, k, v, z, f, u, zp = _split_cols(w_full, a, heads, pw)
    pairs = a // LANES
    pg = pw // N_POOL_GROUPS
    wa = jnp.stack([t.reshape(d, pairs, LANES) for t in (q, k, v, z)], axis=2).reshape(d, 4 * a)
    wp = jnp.stack([t.reshape(d, N_POOL_GROUPS, pg) for t in (u, zp)], axis=2).reshape(d, 2 * pw)
    wf = jnp.pad(f, ((0, 0), (0, LANES - heads)))
    return wa, wp, wf


def _unpack_dw_in(dwa, dwp, dwf, a, heads, pw):
    d = dwa.shape[0]
    pairs = a // LANES
    pg = pw // N_POOL_GROUPS
    ra = dwa.reshape(d, pairs, 4, LANES)
    q, k, v, z = (ra[:, :, i, :].reshape(d, a) for i in range(4))
    rp = dwp.reshape(d, N_POOL_GROUPS, 2, pg)
    u, zp = (rp[:, :, i, :].reshape(d, pw) for i in range(2))
    full = jnp.concatenate([q, k, v, z, dwf[:, :heads], u, zp], axis=1)
    return jnp.transpose(full.reshape(d, N_DEV, -1), (1, 0, 2))


def kernel(x, p, norm_pre, norm_post, w_in, b_f, w_pool, pool_scale, w_out, w_pg, w_pe, loss_target, m_norm_pre, m_norm_post, m_w_in, m_b_f, m_w_pool, m_pool_scale, m_w_out, m_w_pg, m_w_pe, v_norm_pre, v_norm_post, v_w_in, v_b_f, v_w_pool, v_pool_scale, v_w_out, v_w_pg, v_w_pe):
    depth = w_in.shape[0]
    b, s, d = x.shape
    t = b * s
    heads = b_f.shape[1]
    a = heads * HEAD_DIM
    pairs = a // LANES
    pw = pool_scale.shape[1]
    pg = pw // N_POOL_GROUPS
    ple = p.shape[-1]
    mix_w = a + pw

    me = 4 * lax.axis_index("x") + 2 * lax.axis_index("y") + lax.axis_index("c")
    shard = {
        "w_in": [w_in[i].astype(BF16) for i in range(depth)],
        "w_pool": [w_pool[i].reshape(N_POOL_GROUPS * (pg // N_DEV), pg).astype(BF16)
                   for i in range(depth)],
        "w_out": [w_out[i].astype(BF16) for i in range(depth)],
        "w_pg": [w_pg[i].astype(BF16) for i in range(depth)],
        "w_pe": [w_pe[i].astype(BF16) for i in range(depth)],
    }
    names = list(shard)
    rest = names[1:]

    def unpack_rest(lands, layer, which):
        g = {nm: _with_own(ld, shard[nm][layer], me) for nm, ld in zip(which, lands)}
        g_pool = g["w_pool"].reshape(N_DEV, N_POOL_GROUPS, pg // N_DEV, pg)
        return dict(wpool=jnp.transpose(g_pool, (1, 0, 2, 3)).reshape(N_POOL_GROUPS, pg, pg),
                    wout=g["w_out"].reshape(mix_w, d), wpg=g["w_pg"].reshape(d, d),
                    wpe=jnp.transpose(g["w_pe"], (1, 0, 2)).reshape(ple, d))

    (g_in0,) = _exchange([(shard["w_in"][0], False)], "gather_w_in0")
    rest0, tok_rest0 = _exchange_start([(shard[nm][0], False) for nm in rest], g_in0,
                                       "gather_rest0_start")
    later, tok = [], tok_rest0
    for i in range(1, depth):
        hdl, tk_i = _exchange_start([(shard[nm][i], False) for nm in names], g_in0,
                                    "gather_layer%d_start" % i)
        later.append(hdl)
        tok = tok + tk_i

    h = x.reshape(t, d)
    saved = []
    layers = []
    for i in range(depth):
        sv = dict(h=h)
        g_pre = norm_pre[i:i + 1]
        g_post = norm_post[i:i + 1]
        bf = jnp.pad(b_f[i:i + 1], ((0, 0), (0, LANES - heads)))
        scale = pool_scale[i:i + 1]
        if i == 0:
            lw = dict(zip(("wa", "wp", "wf"), _pack_w_in(g_in0, a, heads, pw)))
            g_pre = g_pre + tok
        else:
            lands = _exchange_wait(later[i - 1], h, "gather_layer%d_wait" % i)
            g_in = _with_own(lands[0], shard["w_in"][i], me)
            lw = dict(zip(("wa", "wp", "wf"), _pack_w_in(g_in, a, heads, pw)))
            lw.update(unpack_rest(lands[1:], i, rest))
        hn = _rms_fwd(h, g_pre, "rms_pre")
        pa = _matmul([(hn, lw["wa"])], "nn", BF16, "proj_attn").reshape(b, s, 4 * a)
        pp = _matmul([(hn, lw["wp"])], "nn", BF16, "proj_pool").reshape(b, s, 2 * pw)
        fl = _matmul([(hn, lw["wf"])], "nn", F32, "proj_gate").reshape(b, s, LANES)
        c = _gates_fwd(fl, bf, "gates_fwd")
        qa, ka, kat, va, vt = _attn_prep_fwd(pa, c, "attn_prep_fwd")
        o, ga, lse = _attn_fwd(qa, ka, vt, pa, "attn_fwd")
        if i == 0:
            lw.update(unpack_rest(_exchange_wait(rest0, lse, "gather_rest0_wait"), 0, rest))
        layers.append(lw)
        gp = _pool_fwd(pp, lw["wpool"], scale, "pool_fwd")
        ga2 = ga.reshape(t, a)
        gp2 = gp.reshape(t, pw)
        mix = _matmul([(ga2, lw["wout"][:a]), (gp2, lw["wout"][a:])], "nn", F32, "mix_out")
        h1, h1b = _post_fwd(h, mix, g_post, "post_fwd")
        pb = p[i].reshape(t, ple).astype(BF16)
        gpre = _matmul([(h1b, lw["wpg"])], "nn", F32, "ple_gate")
        e = _matmul([(pb, lw["wpe"])], "nn", F32, "ple_embed")
        h = _ple_fwd(h1, gpre, e, "ple_fwd")
        sv.update(hn=hn, pa=pa, pp=pp, fl=fl, bf=bf, qa=qa, ka=ka, kat=kat, va=va, o=o, lse=lse, ga=ga2,
                  gp=gp2, mix=mix,
                  h1b=h1b, pb=pb, gpre=gpre, e=e, g_pre=g_pre, g_post=g_post, scale=scale)
        saved.append(sv)

    dh, sq = _loss_bwd(h, loss_target.reshape(t, d), "loss")
    loss = lax.psum(0.5 * jnp.sum(sq) / d, MESH_AXES)

    big = {nm: [None] * depth for nm in names}
    small = {nm: [None] * depth for nm in ("norm_pre", "norm_post", "b_f", "pool_scale")}
    grad_handles = [None] * depth
    for i in reversed(range(depth)):
        lw, sv = layers[i], saved[i]
        de, dpre = _ple_bwd(dh, sv["gpre"], sv["e"], "ple_bwd")
        dwpe = _matmul([(sv["pb"], de)], "tn", BF16, "dw_pe")
        dwpg = _matmul([(sv["h1b"], dpre)], "tn", BF16, "dw_pg")
        t1 = _matmul([(dpre, lw["wpg"])], "nt", F32, "d_h1")
        dh1, dmix, dg_post = _post_bwd(dh, t1, sv["mix"], sv["g_post"], "post_bwd")
        dwout = jnp.concatenate([_matmul([(sv["ga"], dmix)], "tn", BF16, "dw_out_attn"),
                                 _matmul([(sv["gp"], dmix)], "tn", BF16, "dw_out_pool")], axis=0)
        dcat = _matmul([(dmix, lw["wout"])], "nt", BF16, "d_cat").reshape(b, s, mix_w)
        dpp, dwpool, dscale = _pool_bwd(sv["pp"], dcat, lw["wpool"], sv["scale"], a // pg, "pool_bwd")
        dpp = dpp.reshape(t, 2 * pw)
        qab, doa, dz = _attn_prep_bwd(dcat, sv["pa"], sv["o"], sv["lse"], sv["qa"], "attn_prep_bwd")
        dpa, dcp = _attn_bwd(sv["ka"], sv["kat"], sv["va"], qab, doa, dz, "attn_bwd")
        dc = jnp.transpose(dcp[..., :2], (0, 2, 1, 3)).reshape(b, s, heads)
        dc = jnp.pad(dc, ((0, 0), (0, 0), (0, LANES - heads)))
        dfl, dbf = _gates_bwd(dc, sv["fl"], sv["bf"], heads, "gates_bwd")
        dpa2 = dpa.reshape(t, 4 * a)
        dfl2 = dfl.reshape(t, LANES)
        dwa = _matmul([(sv["hn"], dpa2)], "tn", BF16, "dw_attn")
        dwp = _matmul([(sv["hn"], dpp)], "tn", BF16, "dw_pool_proj")
        dwf = _matmul([(sv["hn"], dfl2)], "tn", BF16, "dw_gate")

        big["w_in"][i] = _unpack_dw_in(dwa, dwp, dwf, a, heads, pw)
        big["w_pool"][i] = jnp.transpose(
            dwpool.astype(BF16).reshape(N_POOL_GROUPS, N_DEV, pg // N_DEV, pg), (1, 0, 2, 3)
        ).reshape(N_DEV, N_POOL_GROUPS * (pg // N_DEV), pg)
        big["w_out"][i] = dwout.reshape(N_DEV, mix_w // N_DEV, d)
        big["w_pg"][i] = dwpg.reshape(N_DEV, d // N_DEV, d)
        big["w_pe"][i] = jnp.transpose(dwpe.reshape(ple, N_DEV, d // N_DEV), (1, 0, 2))
        grad_handles[i], tok = _exchange_start([(big[nm][i], True) for nm in names], dwf,
                                               "grads_layer%d_start" % i)

        dhn = _matmul([(dpa2, lw["wa"]), (dpp, lw["wp"]), (dfl2, lw["wf"] + tok.astype(BF16))],
                      "nt", F32, "d_hn")
        dh, dg_pre = _pre_bwd(sv["h"], dhn, dh1, sv["g_pre"] + tok, "pre_bwd")
        small["norm_pre"][i] = dg_pre
        small["norm_post"][i] = dg_post
        small["b_f"][i] = jnp.sum(dbf, axis=0)
        small["pool_scale"][i] = dscale
    grad_x = dh.reshape(b, s, d)

    width = max(d, pw)
    small_names = ("norm_pre", "norm_post", "pool_scale", "b_f")

    def small_rows(get):
        rows = []
        for nm in small_names:
            for i in range(depth):
                v_ = get(nm, i)
                rows.append(jnp.pad(v_, ((0, 0), (0, width - v_.shape[1]))))
        return jnp.concatenate(rows, axis=0)

    small_g = small_rows(lambda nm, i: small[nm][i])
    (small_recv,) = _exchange([(small_g, False)], "exchange_small")
    received = [_exchange_wait(grad_handles[i], dh, "grads_layer%d_wait" % i) for i in range(depth)]

    weights = dict(norm_pre=norm_pre, norm_post=norm_post, w_in=w_in, b_f=b_f, w_pool=w_pool,
                   pool_scale=pool_scale, w_out=w_out, w_pg=w_pg, w_pe=w_pe)
    mom1 = dict(norm_pre=m_norm_pre, norm_post=m_norm_post, w_in=m_w_in, b_f=m_b_f, w_pool=m_w_pool,
                pool_scale=m_pool_scale, w_out=m_w_out, w_pg=m_w_pg, w_pe=m_w_pe)
    mom2 = dict(norm_pre=v_norm_pre, norm_post=v_norm_post, w_in=v_w_in, b_f=v_b_f, w_pool=v_w_pool,
                pool_scale=v_pool_scale, w_out=v_w_out, w_pg=v_w_pg, w_pe=v_w_pe)

    results = {}
    for j, nm in enumerate(names):
        shp = weights[nm].shape
        recvs = [received[i][j] for i in range(depth)]
        owns = [lax.dynamic_index_in_dim(big[nm][i], me, 0, keepdims=False) for i in range(depth)]
        flat = lambda arr: arr.reshape(depth * owns[0].shape[0], owns[0].shape[1])
        outs = _adamw(recvs, owns, flat(weights[nm]), flat(mom1[nm]), flat(mom2[nm]), "adamw_" + nm)
        results[nm] = [o_.reshape(shp) for o_ in outs]

    small_w = small_rows(lambda nm, i: weights[nm][i:i + 1])
    small_m = small_rows(lambda nm, i: mom1[nm][i:i + 1])
    small_v = small_rows(lambda nm, i: mom2[nm][i:i + 1])
    outs = _adamw([small_recv], [small_g], small_w, small_m, small_v, "adamw_small")
    for j, nm in enumerate(small_names):
        cols = weights[nm].shape[1]
        results[nm] = [o_[j * depth:(j + 1) * depth, :cols] for o_ in outs]

    order = ("norm_pre", "norm_post", "w_in", "b_f", "w_pool", "pool_scale", "w_out", "w_pg", "w_pe")
    return (loss, grad_x, *[results[nm][0] for nm in order], *[results[nm][1] for nm in order],
            *[results[nm][2] for nm in order], *[results[nm][3] for nm in order])
```

```python
import functools
import math

import jax
import jax.numpy as jnp
from jax import lax
from jax.experimental import pallas as pl
from jax.experimental.pallas import tpu as pltpu

N_DEV = 8
MESH_AXES = ("x", "y", "c")
HEAD_DIM = 64
LANES = 128
N_POOL_GROUPS = 4
EPS = 1e-6
ADAM_LR = 0.001
ADAM_B1 = 0.9
ADAM_B2 = 0.999
ADAM_EPS = 1e-08
ADAM_WD = 0.01
ADAM_STEP = 10
VMEM_LIMIT_BYTES = 56 * 1024 * 1024
F32 = jnp.float32
BF16 = jnp.bfloat16
NEG_INF = float("-inf")


def _call(body, *, name, grid, in_specs, out_specs, out_shape, scratch_shapes=(), semantics=None):
    return pl.pallas_call(
        body, name=name, grid=grid, in_specs=in_specs, out_specs=out_specs, out_shape=out_shape,
        scratch_shapes=list(scratch_shapes),
        compiler_params=pltpu.CompilerParams(dimension_semantics=semantics,
                                             vmem_limit_bytes=VMEM_LIMIT_BYTES))


def _sigmoid(z):
    return 1.0 / (1.0 + jnp.exp(-z))


def _dot(a, b, dims):
    return lax.dot_general(a, b, (dims, ((), ())), preferred_element_type=F32)


NN = ((1,), (0,))
NT = ((1,), (1,))
TN = ((0,), (0,))


def _exchange(items, name):
    n = len(items)
    modes = [s for _, s in items]
    out_shapes = []
    for a, s in items:
        shp = a.shape[1:] if s else a.shape
        out_shapes.append(jax.ShapeDtypeStruct((N_DEV,) + tuple(shp), a.dtype))

    def body(*refs):
        ins = refs[:n]
        outs = refs[n:2 * n]
        send_sems, recv_sems, local_sems = refs[2 * n:]
        x, y, c = (lax.axis_index(ax) for ax in MESH_AXES)
        me = 4 * x + 2 * y + c
        started = []
        for i in range(n):
            mine = ins[i].at[me] if modes[i] else ins[i]
            loc = pltpu.make_async_copy(mine, outs[i].at[me], local_sems.at[i])
            loc.start()
            started.append(loc)
        remote = []
        for k in range(1, N_DEV):
            px = x ^ ((k >> 2) & 1)
            py = y ^ ((k >> 1) & 1)
            pc = c ^ (k & 1)
            peer = me ^ k
            for i in range(n):
                src = ins[i].at[peer] if modes[i] else ins[i]
                cp = pltpu.make_async_remote_copy(
                    src_ref=src, dst_ref=outs[i].at[me],
                    send_sem=send_sems.at[i, k - 1], recv_sem=recv_sems.at[i, k - 1],
                    device_id=(px, py, pc), device_id_type=pl.DeviceIdType.MESH)
                cp.start()
                remote.append(cp)
        for cp in remote:
            cp.wait()
        for loc in started:
            loc.wait()

    hbm = pl.BlockSpec(memory_space=pltpu.HBM)
    return pl.pallas_call(
        body, name=name, out_shape=out_shapes,
        in_specs=[hbm] * n, out_specs=[hbm] * n,
        scratch_shapes=[pltpu.SemaphoreType.DMA((n, N_DEV - 1)),
                        pltpu.SemaphoreType.DMA((n, N_DEV - 1)),
                        pltpu.SemaphoreType.DMA((n,))],
    )(*[a for a, _ in items])


def _peer_copies(srcs, lands, modes, send_sems, recv_sems):
    x, y, c = (lax.axis_index(ax) for ax in MESH_AXES)
    me = 4 * x + 2 * y + c
    copies = []
    for k in range(1, N_DEV):
        peer_id = (x ^ ((k >> 2) & 1), y ^ ((k >> 1) & 1), c ^ (k & 1))
        for i, scatter in enumerate(modes):
            src = srcs[i].at[me ^ k] if scatter else srcs[i]
            pair = i * (N_DEV - 1) + k - 1
            copies.append(pltpu.make_async_remote_copy(
                src_ref=src, dst_ref=lands[i].at[me],
                send_sem=send_sems.at[pair], recv_sem=recv_sems.at[pair],
                device_id=peer_id, device_id_type=pl.DeviceIdType.MESH))
    return copies


def _exchange_start(items, after, name):
    n = len(items)
    modes = [s for _, s in items]
    srcs = [pltpu.with_memory_space_constraint(a, pltpu.HBM) for a, _ in items]
    lands = []
    for a, s in items:
        shp = (N_DEV,) + tuple(a.shape[1:] if s else a.shape)
        lands.append(pltpu.with_memory_space_constraint(lax.empty(shp, a.dtype), pltpu.HBM))

    def body(*refs):
        send_sems, recv_sems = refs[2 * n + 1], refs[2 * n + 2]
        token = refs[-1]
        for cp in _peer_copies(refs[:n], refs[n:2 * n], modes, send_sems, recv_sems):
            cp.start()
        token[...] = jnp.zeros_like(token)

    hbm = pl.BlockSpec(memory_space=pltpu.HBM)
    sem = pl.BlockSpec(memory_space=pltpu.SEMAPHORE)
    outs = pl.pallas_call(
        body, name=name,
        out_shape=(pltpu.SemaphoreType.DMA((n * (N_DEV - 1),)),
                   pltpu.SemaphoreType.DMA((n * (N_DEV - 1),)),
                   *[pltpu.HBM(a.shape, a.dtype) for a in srcs + lands],
                   jax.ShapeDtypeStruct((8, LANES), F32)),
        in_specs=[hbm] * (2 * n) + [pl.BlockSpec(memory_space=pl.ANY)],
        out_specs=(sem, sem, *([hbm] * (2 * n)), pl.BlockSpec(memory_space=pltpu.VMEM)),
        input_output_aliases={i: 2 + i for i in range(2 * n)},
        compiler_params=pltpu.CompilerParams(
            has_side_effects=pltpu.SideEffectType.DATAFLOW_SIDE_EFFECTING),
    )(*srcs, *lands, after)
    handle = (modes, outs[0], outs[1], list(outs[2:2 + n]), list(outs[2 + n:2 + 2 * n]))
    return handle, outs[-1][0, 0]


def _exchange_wait(handle, after, name):
    modes, send_sems, recv_sems, srcs, lands = handle
    n = len(modes)

    def body(*refs):
        for cp in _peer_copies(refs[:n], refs[n:2 * n], modes, refs[2 * n], refs[2 * n + 1]):
            cp.wait_send()
            cp.wait_recv()

    hbm = pl.BlockSpec(memory_space=pltpu.HBM)
    sem = pl.BlockSpec(memory_space=pltpu.SEMAPHORE)
    outs = pl.pallas_call(
        body, name=name,
        out_shape=tuple(pltpu.HBM(a.shape, a.dtype) for a in srcs + lands),
        in_specs=[hbm] * (2 * n) + [sem, sem, pl.BlockSpec(memory_space=pl.ANY)],
        out_specs=tuple([hbm] * (2 * n)),
        input_output_aliases={i: i for i in range(2 * n)},
        compiler_params=pltpu.CompilerParams(
            has_side_effects=pltpu.SideEffectType.DATAFLOW_SIDE_EFFECTING),
    )(*srcs, *lands, send_sems, recv_sems, after)
    return list(outs[n:])


def _with_own(slots, own, me):
    idx = lax.broadcasted_iota(jnp.int32, (N_DEV,) + (1,) * own.ndim, 0)
    return jnp.where(idx == me, own[None], slots)


def _matmul(pairs, mode, out_dtype, name, n_dim=None, tm=512, tn=1024, tk=1024, n_outer=False):
    dims = {"nn": NN, "nt": NT, "tn": TN}[mode]
    pairs = [tuple(pr) + (0, 0) * (len(pr) == 2) for pr in pairs]
    a0, b0 = pairs[0][:2]
    m_dim = a0.shape[1] if mode == "tn" else a0.shape[0]
    if n_dim is None:
        n_dim = b0.shape[0] if mode == "nt" else b0.shape[1]
    tm = min(tm, m_dim)
    tn = min(tn, n_dim)
    segs = []
    off = 0
    for a, _, k0, n0 in pairs:
        k_dim = a.shape[0] if mode == "tn" else a.shape[1]
        t = min(tk, k_dim)
        segs.append((off, k_dim // t, t, k0 // t, n0 // tn))
        off += k_dim // t
    nk = off
    n_pairs = len(pairs)

    def ij(g0, g1):
        return (g1, g0) if n_outer else (g0, g1)

    in_specs = []
    for (o, cnt, t, kb, nb) in segs:
        def kc(kk, o=o, cnt=cnt):
            return jnp.clip(kk - o, 0, cnt - 1)
        if mode == "tn":
            in_specs.append(pl.BlockSpec((t, tm), lambda g0, g1, kk, kc=kc: (kc(kk), ij(g0, g1)[0])))
        else:
            in_specs.append(pl.BlockSpec((tm, t), lambda g0, g1, kk, kc=kc: (ij(g0, g1)[0], kc(kk))))
        if mode == "nt":
            in_specs.append(pl.BlockSpec((tn, t), lambda g0, g1, kk, kc=kc, kb=kb, nb=nb:
                                         (nb + ij(g0, g1)[1], kb + kc(kk))))
        else:
            in_specs.append(pl.BlockSpec((t, tn), lambda g0, g1, kk, kc=kc, kb=kb, nb=nb:
                                         (kb + kc(kk), nb + ij(g0, g1)[1])))

    def body_single(a_ref, b_ref, out_ref):
        out_ref[...] = _dot(a_ref[...], b_ref[...], dims).astype(out_dtype)

    def body(*refs):
        out_ref = refs[2 * n_pairs]
        acc = refs[2 * n_pairs + 1]
        kk = pl.program_id(2)

        @pl.when(kk == 0)
        def _():
            acc[...] = jnp.zeros_like(acc)

        for idx, (o, cnt) in enumerate(sg[:2] for sg in segs):
            @pl.when((kk >= o) & (kk < o + cnt))
            def _(idx=idx):
                acc[...] += _dot(refs[2 * idx][...], refs[2 * idx + 1][...], dims)

        @pl.when(kk == nk - 1)
        def _():
            out_ref[...] = acc[...].astype(out_dtype)

    flat = [t for pr in pairs for t in pr[:2]]
    tiles = (m_dim // tm, n_dim // tn)
    return _call(body_single if nk == 1 else body, name=name,
                 grid=ij(*tiles) + (nk,), in_specs=in_specs,
                 out_specs=pl.BlockSpec((tm, tn), lambda g0, g1, kk: ij(g0, g1)),
                 out_shape=jax.ShapeDtypeStruct((m_dim, n_dim), out_dtype),
                 scratch_shapes=[] if nk == 1 else [pltpu.VMEM((tm, tn), F32)],
                 semantics=("parallel", "parallel", "arbitrary"))(*flat)


def _row_tile(t):
    return min(256, t)


def _rms_fwd(h, g, name):
    t, d = h.shape
    tt = _row_tile(t)

    def body(h_ref, g_ref, o_ref):
        hv = h_ref[...]
        r = lax.rsqrt(jnp.mean(hv * hv, axis=-1, keepdims=True) + EPS)
        o_ref[...] = (hv * r * g_ref[...]).astype(BF16)

    row = pl.BlockSpec((tt, d), lambda i: (i, 0))
    vec = pl.BlockSpec((1, d), lambda i: (0, 0))
    return _call(body, name=name, grid=(t // tt,), in_specs=[row, vec], out_specs=row,
                 out_shape=jax.ShapeDtypeStruct((t, d), BF16), semantics=("parallel",))(h, g)


def _post_fwd(h, mix, g, name):
    t, d = h.shape
    tt = _row_tile(t)

    def body(h_ref, m_ref, g_ref, o_ref, ob_ref):
        mv = m_ref[...]
        r = lax.rsqrt(jnp.mean(mv * mv, axis=-1, keepdims=True) + EPS)
        h1 = h_ref[...] + mv * r * g_ref[...]
        o_ref[...] = h1
        ob_ref[...] = h1.astype(BF16)

    row = pl.BlockSpec((tt, d), lambda i: (i, 0))
    vec = pl.BlockSpec((1, d), lambda i: (0, 0))
    return _call(body, name=name, grid=(t // tt,), in_specs=[row, row, vec], out_specs=[row, row],
                 out_shape=[jax.ShapeDtypeStruct((t, d), F32), jax.ShapeDtypeStruct((t, d), BF16)],
                 semantics=("parallel",))(h, mix, g)


def _ple_fwd(h1, gpre, e, name):
    t, d = h1.shape
    tt = _row_tile(t)

    def body(h_ref, g_ref, e_ref, o_ref):
        o_ref[...] = h_ref[...] + _sigmoid(g_ref[...]) * e_ref[...]

    row = pl.BlockSpec((tt, d), lambda i: (i, 0))
    return _call(body, name=name, grid=(t // tt,), in_specs=[row, row, row], out_specs=row,
                 out_shape=jax.ShapeDtypeStruct((t, d), F32), semantics=("parallel",))(h1, gpre, e)


def _loss_bwd(y, target, name):
    t, d = y.shape
    tt = _row_tile(t)

    def body(y_ref, t_ref, dy_ref, s_ref):
        @pl.when(pl.program_id(0) == 0)
        def _():
            s_ref[...] = jnp.zeros_like(s_ref)
        diff = y_ref[...] - t_ref[...]
        dy_ref[...] = diff * (1.0 / d)
        s_ref[...] += jnp.sum(diff * diff, axis=0, keepdims=True)

    row = pl.BlockSpec((tt, d), lambda i: (i, 0))
    vec = pl.BlockSpec((1, d), lambda i: (0, 0))
    return _call(body, name=name, grid=(t // tt,), in_specs=[row, row], out_specs=[row, vec],
                 out_shape=[jax.ShapeDtypeStruct((t, d), F32), jax.ShapeDtypeStruct((1, d), F32)],
                 semantics=("arbitrary",))(y, target)


def _ple_bwd(dh2, gpre, e, name):
    t, d = dh2.shape
    tt = _row_tile(t)

    def body(d_ref, g_ref, e_ref, de_ref, dp_ref):
        gate = _sigmoid(g_ref[...])
        dv = d_ref[...]
        de_ref[...] = (dv * gate).astype(BF16)
        dp_ref[...] = (dv * e_ref[...] * gate * (1.0 - gate)).astype(BF16)

    row = pl.BlockSpec((tt, d), lambda i: (i, 0))
    return _call(body, name=name, grid=(t // tt,), in_specs=[row, row, row], out_specs=[row, row],
                 out_shape=[jax.ShapeDtypeStruct((t, d), BF16)] * 2,
                 semantics=("parallel",))(dh2, gpre, e)


def _post_bwd(dh2, t1, mix, g, name):
    t, d = dh2.shape
    tt = _row_tile(t)

    def body(d_ref, t_ref, m_ref, g_ref, dh_ref, dm_ref, dg_ref):
        @pl.when(pl.program_id(0) == 0)
        def _():
            dg_ref[...] = jnp.zeros_like(dg_ref)
        dh1 = d_ref[...] + t_ref[...]
        mv = m_ref[...]
        r = lax.rsqrt(jnp.mean(mv * mv, axis=-1, keepdims=True) + EPS)
        dh_ref[...] = dh1
        dg_ref[...] += jnp.sum(dh1 * mv * r, axis=0, keepdims=True)
        w = dh1 * g_ref[...]
        dot = jnp.mean(w * mv, axis=-1, keepdims=True)
        dm_ref[...] = (r * w - mv * (r * r * r) * dot).astype(BF16)

    row = pl.BlockSpec((tt, d), lambda i: (i, 0))
    vec = pl.BlockSpec((1, d), lambda i: (0, 0))
    return _call(body, name=name, grid=(t // tt,), in_specs=[row, row, row, vec],
                 out_specs=[row, row, vec],
                 out_shape=[jax.ShapeDtypeStruct((t, d), F32), jax.ShapeDtypeStruct((t, d), BF16),
                            jax.ShapeDtypeStruct((1, d), F32)],
                 semantics=("arbitrary",))(dh2, t1, mix, g)


def _pre_bwd(h, dhn, dh1, g, name):
    t, d = h.shape
    tt = _row_tile(t)

    def body(h_ref, dn_ref, d1_ref, g_ref, dh_ref, dg_ref):
        @pl.when(pl.program_id(0) == 0)
        def _():
            dg_ref[...] = jnp.zeros_like(dg_ref)
        hv = h_ref[...]
        dn = dn_ref[...]
        r = lax.rsqrt(jnp.mean(hv * hv, axis=-1, keepdims=True) + EPS)
        dg_ref[...] += jnp.sum(dn * hv * r, axis=0, keepdims=True)
        w = dn * g_ref[...]
        dot = jnp.mean(w * hv, axis=-1, keepdims=True)
        dh_ref[...] = d1_ref[...] + r * w - hv * (r * r * r) * dot

    row = pl.BlockSpec((tt, d), lambda i: (i, 0))
    vec = pl.BlockSpec((1, d), lambda i: (0, 0))
    return _call(body, name=name, grid=(t // tt,), in_specs=[row, row, row, vec],
                 out_specs=[row, vec],
                 out_shape=[jax.ShapeDtypeStruct((t, d), F32), jax.ShapeDtypeStruct((1, d), F32)],
                 semantics=("arbitrary",))(h, dhn, dh1, g)


def _split3(v):
    hi = v.astype(BF16)
    r1 = v - hi.astype(F32)
    mid = r1.astype(BF16)
    lo = (r1 - mid.astype(F32)).astype(BF16)
    return hi, mid, lo


def _gates_fwd(fl, bf, name):
    b, s, _ = fl.shape

    def body(f_ref, b_ref, c_ref):
        xv = f_ref[0] + b_ref[...]
        lf = jnp.minimum(xv, 0.0) - jnp.log(1.0 + jnp.exp(-jnp.abs(xv)))
        dst = lax.broadcasted_iota(jnp.int32, (s, s), 0)
        src = lax.broadcasted_iota(jnp.int32, (s, s), 1)
        lower = (src <= dst).astype(BF16)
        acc = jnp.zeros((s, LANES), F32)
        for part in _split3(lf):
            acc = acc + _dot(lower, part, NN)
        c_ref[0] = acc

    blk = pl.BlockSpec((1, s, LANES), lambda i: (i, 0, 0))
    return _call(body, name=name, grid=(b,),
                 in_specs=[blk, pl.BlockSpec((1, LANES), lambda i: (0, 0))],
                 out_specs=blk, out_shape=jax.ShapeDtypeStruct((b, s, LANES), F32),
                 semantics=("parallel",))(fl, bf)


def _gates_bwd(dc, fl, bf, heads, name):
    b, s, _ = fl.shape

    def body(d_ref, f_ref, b_ref, o_ref, db_ref):
        xv = f_ref[0] + b_ref[...]
        dst = lax.broadcasted_iota(jnp.int32, (s, s), 0)
        src = lax.broadcasted_iota(jnp.int32, (s, s), 1)
        later = (src >= dst).astype(BF16)
        dlf = jnp.zeros((s, LANES), F32)
        for part in _split3(d_ref[0]):
            dlf = dlf + _dot(later, part, NN)
        lane = lax.broadcasted_iota(jnp.int32, (s, LANES), 1)
        dfl = jnp.where(lane < heads, dlf * _sigmoid(-xv), 0.0)
        o_ref[0] = dfl.astype(BF16)
        db_ref[0] = jnp.sum(dfl, axis=0, keepdims=True)

    blk = pl.BlockSpec((1, s, LANES), lambda i: (i, 0, 0))
    return _call(body, name=name, grid=(b,),
                 in_specs=[blk, blk, pl.BlockSpec((1, LANES), lambda i: (0, 0))],
                 out_specs=[blk, pl.BlockSpec((1, 1, LANES), lambda i: (i, 0, 0))],
                 out_shape=[jax.ShapeDtypeStruct((b, s, LANES), BF16),
                            jax.ShapeDtypeStruct((b, 1, LANES), F32)],
                 semantics=("parallel",))(dc, fl, bf)


LANE_CQ = 64
LANE_CK = 67
LANE_LSE = 70
LANE_D = 64
N_PARTS = 3


def _attn_tiles(s):
    return min(512, s), min(256, s)


def _lanes_in(lane, first):
    return (lane >= first) & (lane < first + N_PARTS)


def _attn_prep_fwd(pa, c, name):
    b, s, a4 = pa.shape
    pairs = a4 // (4 * LANES)
    scale = 1.0 / math.sqrt(HEAD_DIM)
    wide = (1 + N_PARTS) * LANES

    def body(q_ref, k_ref, v_ref, c_ref, qa_ref, ka_ref, kat_ref, va_ref, vt_ref):
        hp = pl.program_id(1)
        c3 = jnp.concatenate(_split3(c_ref[0]), axis=1)
        qx = jnp.concatenate([q_ref[0], c3], axis=1)
        kx = jnp.concatenate([k_ref[0], c3], axis=1)
        vv = v_ref[0]
        lane = lax.broadcasted_iota(jnp.int32, (s, LANES), 1)
        row = lax.broadcasted_iota(jnp.int32, (wide, LANES), 0)
        col = lax.broadcasted_iota(jnp.int32, (wide, LANES), 1)
        r128 = lax.broadcasted_iota(jnp.int32, (LANES, LANES), 0)
        c128 = lax.broadcasted_iota(jnp.int32, (LANES, LANES), 1)
        ident = (r128 == c128).astype(BF16)
        for j in range(2):
            head = 2 * hp + j
            move = (row == col + HEAD_DIM * j) & (col < HEAD_DIM)
            move128 = (r128 == c128 + HEAD_DIM * j) & (c128 < HEAD_DIM)

            def pick(first, head=head):
                hit = (row == LANES + head) & (col == first)
                for i in range(1, N_PARTS):
                    hit = hit | ((row == LANES * (i + 1) + head) & (col == first + i))
                return hit

            mq = (jnp.where(move, scale, 0.0) + jnp.where(pick(LANE_CQ), 1.0, 0.0)).astype(BF16)
            mk = (jnp.where(move, 1.0, 0.0) - jnp.where(pick(LANE_CK), 1.0, 0.0)).astype(BF16)
            qa = _dot(qx, mq, NN) + jnp.where(_lanes_in(lane, LANE_CK), 1.0, 0.0)
            ka = _dot(kx, mk, NN) + jnp.where(
                _lanes_in(lane, LANE_CQ) | _lanes_in(lane, LANE_LSE), 1.0, 0.0)
            va = _dot(vv, move128.astype(BF16), NN) + jnp.where(_lanes_in(lane, LANE_D), 1.0, 0.0)
            kab = ka.astype(BF16)
            qa_ref[0, 0, j] = qa.astype(BF16)
            ka_ref[0, 0, j] = kab
            kat_ref[0, 0, j] = _dot(ident, kab, NT).astype(BF16)
            va_ref[0, 0, j] = va.astype(BF16)
        vt_ref[0, 0] = _dot(ident, vv, NT).astype(BF16)

    col_blk = lambda cidx: pl.BlockSpec((1, s, LANES), lambda bi, hp: (bi, 0, cidx * pairs + hp))
    tok = pl.BlockSpec((1, 1, 2, s, LANES), lambda bi, hp: (bi, hp, 0, 0, 0))
    tok_t = pl.BlockSpec((1, 1, 2, LANES, s), lambda bi, hp: (bi, hp, 0, 0, 0))
    tok_shape = jax.ShapeDtypeStruct((b, pairs, 2, s, LANES), BF16)
    return _call(
        body, name=name, grid=(b, pairs),
        in_specs=[col_blk(0), col_blk(1), col_blk(2),
                  pl.BlockSpec((1, s, LANES), lambda bi, hp: (bi, 0, 0))],
        out_specs=[tok, tok, tok_t, tok,
                   pl.BlockSpec((1, 1, LANES, s), lambda bi, hp: (bi, hp, 0, 0))],
        out_shape=[tok_shape, tok_shape, jax.ShapeDtypeStruct((b, pairs, 2, LANES, s), BF16),
                   tok_shape, jax.ShapeDtypeStruct((b, pairs, LANES, s), BF16)],
        semantics=("parallel", "parallel"))(pa, pa, pa, c)


def _attn_fwd(qa, ka, vt, pa, name):
    b, pairs, _, s, _ = qa.shape
    a = pairs * LANES
    tq, tk = _attn_tiles(s)
    ratio = tq // tk

    def body(q_ref, k_ref, vt_ref, z_ref, o_ref, g_ref, lse_ref):
        qi = pl.program_id(2)
        key_i = lax.broadcasted_iota(jnp.int32, (tk, tq), 0)
        qry_i = lax.broadcasted_iota(jnp.int32, (tk, tq), 1)
        qv = [q_ref[0, 0, j] for j in range(2)]

        def step(kj, carry, diag):
            k0 = pl.multiple_of(kj * tk, tk)
            out = []
            for j in range(2):
                m, l, acc = carry[j]
                st = _dot(k_ref[0, 0, j, pl.ds(k0, tk), :], qv[j], NT)
                if diag is not None:
                    st = jnp.where(key_i + diag * tk <= qry_i, st, NEG_INF)
                m_new = jnp.maximum(m, jnp.max(st, axis=0, keepdims=True))
                alpha = jnp.exp(m - m_new)
                pt = jnp.exp(st - m_new)
                l_new = alpha * l + jnp.sum(pt, axis=0, keepdims=True)
                vb = vt_ref[0, 0, HEAD_DIM * j:HEAD_DIM * (j + 1), pl.ds(k0, tk)]
                out.append((m_new, l_new, alpha * acc + _dot(vb, pt.astype(BF16), NN)))
            return tuple(out)

        init = (jnp.full((1, tq), NEG_INF, F32), jnp.zeros((1, tq), F32),
                jnp.zeros((HEAD_DIM, tq), F32))
        carry = lax.fori_loop(0, ratio * qi, lambda kj, cr: step(kj, cr, None), (init, init))
        for i in range(ratio):
            carry = step(ratio * qi + i, carry, i)
        heads_out = []
        for j in range(2):
            m, l, acc = carry[j]
            heads_out.append(acc / l)
            lse_ref[0, 0, j:j + 1, :] = m + jnp.log(l)
        ov = jnp.transpose(jnp.concatenate(heads_out, axis=0))
        o_ref[0] = ov.astype(BF16)
        zv = z_ref[0].astype(F32)
        g_ref[0] = (ov * zv * _sigmoid(zv)).astype(BF16)

    return _call(
        body, name=name, grid=(b, pairs, s // tq),
        in_specs=[pl.BlockSpec((1, 1, 2, tq, LANES), lambda bi, hp, qi: (bi, hp, 0, qi, 0)),
                  pl.BlockSpec((1, 1, 2, s, LANES), lambda bi, hp, qi: (bi, hp, 0, 0, 0)),
                  pl.BlockSpec((1, 1, LANES, s), lambda bi, hp, qi: (bi, hp, 0, 0)),
                  pl.BlockSpec((1, tq, LANES), lambda bi, hp, qi: (bi, qi, 3 * pairs + hp))],
        out_specs=[pl.BlockSpec((1, tq, LANES), lambda bi, hp, qi: (bi, qi, hp)),
                   pl.BlockSpec((1, tq, LANES), lambda bi, hp, qi: (bi, qi, hp)),
                   pl.BlockSpec((1, 1, 2, tq), lambda bi, hp, qi: (bi, hp, 0, qi))],
        out_shape=[jax.ShapeDtypeStruct((b, s, a), BF16), jax.ShapeDtypeStruct((b, s, a), BF16),
                   jax.ShapeDtypeStruct((b, pairs, 2, s), F32)],
        semantics=("parallel", "parallel", "arbitrary"))(qa, ka, vt, pa)


def _attn_prep_bwd(dcat, pa, o, lse, qa, name):
    b, pairs, _, s, _ = qa.shape
    a = pairs * LANES
    sub = 16

    def body(da_ref, z_ref, o_ref, lse_ref, qa_ref, qab_ref, doa_ref, dz_ref):
        zv = z_ref[0].astype(F32)
        dav = da_ref[0].astype(F32)
        ov = o_ref[0].astype(F32)
        sg = _sigmoid(zv)
        dov = dav * zv * sg
        dz_ref[0] = (dav * ov * sg * (1.0 + zv * (1.0 - sg))).astype(BF16)
        prod = dov * ov
        dob = dov.astype(BF16)
        lane = lax.broadcasted_iota(jnp.int32, (s, LANES), 1)
        r128 = lax.broadcasted_iota(jnp.int32, (LANES, LANES), 0)
        c128 = lax.broadcasted_iota(jnp.int32, (LANES, LANES), 1)
        prow = lax.broadcasted_iota(jnp.int32, (sub, s), 0)
        srow = lax.broadcasted_iota(jnp.int32, (sub, LANES), 0)
        scol = lax.broadcasted_iota(jnp.int32, (sub, LANES), 1)
        place = ((scol == srow + LANE_LSE) & (srow < N_PARTS)).astype(BF16)
        for j in range(2):
            in_head = (lane >= HEAD_DIM * j) & (lane < HEAD_DIM * (j + 1))
            dparts = _split3(jnp.sum(jnp.where(in_head, prod, 0.0), axis=1, keepdims=True))
            move128 = ((r128 == c128 + HEAD_DIM * j) & (c128 < HEAD_DIM)).astype(BF16)
            doa = _dot(dob, move128, NN)
            for i in range(N_PARTS):
                doa = jnp.where(lane == LANE_D + i, -dparts[i].astype(F32), doa)
            doa_ref[0, 0, j] = doa.astype(BF16)
            lparts = _split3(lse_ref[0, 0, j:j + 1, :])
            pmat = jnp.zeros((sub, s), BF16)
            for i in range(N_PARTS):
                pmat = jnp.where(prow == i, lparts[i], pmat)
            lcol = _dot(pmat, place, TN)
            qab_ref[0, 0, j] = (qa_ref[0, 0, j].astype(F32) - lcol).astype(BF16)

    tok = pl.BlockSpec((1, 1, 2, s, LANES), lambda bi, hp: (bi, hp, 0, 0, 0))
    tok_shape = jax.ShapeDtypeStruct((b, pairs, 2, s, LANES), BF16)
    pair_blk = pl.BlockSpec((1, s, LANES), lambda bi, hp: (bi, 0, hp))
    return _call(
        body, name=name, grid=(b, pairs),
        in_specs=[pair_blk,
                  pl.BlockSpec((1, s, LANES), lambda bi, hp: (bi, 0, 3 * pairs + hp)),
                  pair_blk,
                  pl.BlockSpec((1, 1, 2, s), lambda bi, hp: (bi, hp, 0, 0)),
                  tok],
        out_specs=[tok, tok, pair_blk],
        out_shape=[tok_shape, tok_shape, jax.ShapeDtypeStruct((b, s, a), BF16)],
        semantics=("parallel", "parallel"))(dcat, pa, o, lse, qa)


def _attn_bwd(ka, kat, va, qab, doa, name):
    b, pairs, _, s, _ = ka.shape
    a = pairs * LANES
    tq, tk = _attn_tiles(s)
    ratio = tq // tk
    nq, nk = s // tq, s // tk
    scale = 1.0 / math.sqrt(HEAD_DIM)

    def body(k_ref, kt_ref, v_ref, q_ref, do_ref, dq_ref, dk_ref, dv_ref, dc_ref,
             dqt_acc, dk_s, dv_s):
        key_i = lax.broadcasted_iota(jnp.int32, (tk, tq), 0)
        qry_i = lax.broadcasted_iota(jnp.int32, (tk, tq), 1)
        lane = lax.broadcasted_iota(jnp.int32, (tq, LANES), 1)
        low = lane < HEAD_DIM

        dqt_acc[...] = jnp.zeros_like(dqt_acc)

        def key_block(kj, _):
            krows = pl.ds(pl.multiple_of(kj * tk, tk), tk)
            kb = [k_ref[0, 0, j, krows, :] for j in range(2)]
            vb = [v_ref[0, 0, j, krows, :] for j in range(2)]
            ktb = [kt_ref[0, 0, j, :, krows] for j in range(2)]
            qd = kj // ratio

            def query_block(qi, carry, masked):
                qrows = pl.ds(pl.multiple_of(qi * tq, tq), tq)
                out = []
                for j in range(2):
                    dk, dv = carry[j]
                    qb = q_ref[0, 0, j, qrows, :]
                    dob = do_ref[0, 0, j, qrows, :]
                    pt = jnp.exp(_dot(kb[j], qb, NT))
                    if masked:
                        pt = jnp.where(key_i + (kj * tk - qi * tq) <= qry_i, pt, 0.0)
                    dst = pt * _dot(vb[j], dob, NT)
                    dsb = dst.astype(BF16)
                    dv = dv + _dot(pt.astype(BF16), dob, NN)
                    dk = dk + _dot(dsb, qb, NN)
                    dqt_acc[j, :, qrows] += _dot(ktb[j], dsb, NN)
                    out.append((dk, dv))
                return tuple(out)

            zero = jnp.zeros((tk, LANES), F32)
            carry = query_block(qd, ((zero, zero), (zero, zero)), True)
            carry = lax.fori_loop(qd + 1, nq, lambda qi, cr: query_block(qi, cr, False), carry)
            for j in range(2):
                dk_s[j, krows, :] = carry[j][0]
                dv_s[j, krows, :] = carry[j][1]
            return 0

        lax.fori_loop(0, nk, key_block, 0)

        def finish(i, _):
            rows = pl.ds(pl.multiple_of(i * tq, tq), tq)
            dq = [jnp.transpose(dqt_acc[j, :, rows]) for j in range(2)]
            dk = [dk_s[j, rows, :] for j in range(2)]
            dv = [dv_s[j, rows, :] for j in range(2)]
            dcol = [dq[j][:, LANE_CQ:LANE_CQ + 1] - dk[j][:, LANE_CK:LANE_CK + 1] for j in range(2)]
            dq = [dq[j] * scale for j in range(2)]
            for out_ref, val in ((dq_ref, dq), (dk_ref, dk), (dv_ref, dv)):
                merged = jnp.where(low, val[0], pltpu.roll(val[1], HEAD_DIM, 1))
                out_ref[0, rows, :] = merged.astype(BF16)
            dc_ref[0, 0, rows, :] = jnp.where(lane == 0, dcol[0], jnp.where(lane == 1, dcol[1], 0.0))
            return 0

        lax.fori_loop(0, nq, finish, 0)

    tok = pl.BlockSpec((1, 1, 2, s, LANES), lambda bi, hp: (bi, hp, 0, 0, 0))
    tok_t = pl.BlockSpec((1, 1, 2, LANES, s), lambda bi, hp: (bi, hp, 0, 0, 0))
    pair_blk = pl.BlockSpec((1, s, LANES), lambda bi, hp: (bi, 0, hp))
    pair_shape = jax.ShapeDtypeStruct((b, s, a), BF16)
    return _call(
        body, name=name, grid=(b, pairs),
        in_specs=[tok, tok_t, tok, tok, tok],
        out_specs=[pair_blk, pair_blk, pair_blk,
                   pl.BlockSpec((1, 1, s, LANES), lambda bi, hp: (bi, hp, 0, 0))],
        out_shape=[pair_shape, pair_shape, pair_shape,
                   jax.ShapeDtypeStruct((b, pairs, s, LANES), F32)],
        scratch_shapes=[pltpu.VMEM((2, LANES, s), F32), pltpu.VMEM((2, s, LANES), F32),
                        pltpu.VMEM((2, s, LANES), F32)],
        semantics=("parallel", "parallel"))(ka, kat, va, qab, doa)


def _pool_tile(s):
    return min(256, s)


def _band(tb, window, shift):
    tgt = lax.broadcasted_iota(jnp.int32, (tb, tb), 0)
    src = lax.broadcasted_iota(jnp.int32, (tb, tb), 1) + shift
    return ((src <= tgt) & (src > tgt - window)).astype(BF16)


def _band_t(tb, window, shift):
    src = lax.broadcasted_iota(jnp.int32, (tb, tb), 0)
    tgt = lax.broadcasted_iota(jnp.int32, (tb, tb), 1) + shift
    return ((src <= tgt) & (src > tgt - window)).astype(BF16)


def _pool_fwd(pp, w_pool, scale, name):
    b, s, pw2 = pp.shape
    pw = pw2 // 2
    pg = pw // N_POOL_GROUPS
    tb = _pool_tile(s)
    nb = s // tb

    def body(u_ref, z_ref, w_ref, s_ref, o_ref):
        window = 2 << pl.program_id(1)
        band0 = _band(tb, window, 0)
        band1 = _band(tb, window, -tb)
        pos = lax.broadcasted_iota(jnp.int32, (tb, pg), 0)

        def block(i, _):
            rows = pl.ds(pl.multiple_of(i * tb, tb), tb)
            prev = pl.ds(pl.multiple_of(jnp.maximum(i - 1, 0) * tb, tb), tb)
            ub = u_ref[0, rows, :]
            up = u_ref[0, prev, :]
            up = jnp.where(i > 0, up, jnp.zeros_like(up))
            count = jnp.minimum(pos + i * tb + 1, window).astype(F32)
            pooled = (_dot(band0, ub, NN) + _dot(band1, up, NN)) / count - ub.astype(F32)
            mixed = _dot(pooled.astype(BF16), w_ref[0], NN) * s_ref[...]
            zv = z_ref[0, rows, :].astype(F32)
            o_ref[0, rows, :] = (mixed * zv * _sigmoid(zv)).astype(BF16)
            return 0

        lax.fori_loop(0, nb, block, 0)

    return _call(
        body, name=name, grid=(b, N_POOL_GROUPS),
        in_specs=[pl.BlockSpec((1, s, pg), lambda bi, g: (bi, 0, g)),
                  pl.BlockSpec((1, s, pg), lambda bi, g: (bi, 0, N_POOL_GROUPS + g)),
                  pl.BlockSpec((1, pg, pg), lambda bi, g: (g, 0, 0)),
                  pl.BlockSpec((1, pg), lambda bi, g: (0, g))],
        out_specs=pl.BlockSpec((1, s, pg), lambda bi, g: (bi, 0, g)),
        out_shape=jax.ShapeDtypeStruct((b, s, pw), BF16),
        semantics=("parallel", "parallel"))(pp, pp, w_pool, scale)


def _pool_bwd(pp, dcat, w_pool, scale, first_block, name):
    b, s, pw2 = pp.shape
    pw = pw2 // 2
    pg = pw // N_POOL_GROUPS
    tb = _pool_tile(s)
    nb = s // tb

    def body(u_ref, z_ref, d_ref, w_ref, s_ref, du_ref, dz_ref, dw_ref, ds_ref, dpool_s):
        @pl.when(pl.program_id(1) == 0)
        def _():
            dw_ref[...] = jnp.zeros_like(dw_ref)
            ds_ref[...] = jnp.zeros_like(ds_ref)

        window = 2 << pl.program_id(0)
        band0 = _band(tb, window, 0)
        band1 = _band(tb, window, -tb)
        band0_t = _band_t(tb, window, 0)
        band1_t = _band_t(tb, window, tb)
        pos = lax.broadcasted_iota(jnp.int32, (tb, pg), 0)

        def first(i, _):
            rows = pl.ds(pl.multiple_of(i * tb, tb), tb)
            prev = pl.ds(pl.multiple_of(jnp.maximum(i - 1, 0) * tb, tb), tb)
            ub = u_ref[0, rows, :]
            up = u_ref[0, prev, :]
            up = jnp.where(i > 0, up, jnp.zeros_like(up))
            count = jnp.minimum(pos + i * tb + 1, window).astype(F32)
            pooled = ((_dot(band0, ub, NN) + _dot(band1, up, NN)) / count
                      - ub.astype(F32)).astype(BF16)
            mixed = _dot(pooled, w_ref[0], NN)
            pm = mixed * s_ref[...]
            zv = z_ref[0, rows, :].astype(F32)
            sg = _sigmoid(zv)
            dpl = d_ref[0, rows, :].astype(F32)
            dpm = dpl * zv * sg
            dz_ref[0, rows, :] = (dpl * pm * sg * (1.0 + zv * (1.0 - sg))).astype(BF16)
            ds_ref[...] += jnp.sum(dpm * mixed, axis=0, keepdims=True)
            dmixed = (dpm * s_ref[...]).astype(BF16)
            dw_ref[0] += _dot(pooled, dmixed, TN)
            dpool_s[rows, :] = _dot(dmixed, w_ref[0], NT)
            return 0

        lax.fori_loop(0, nb, first, 0)

        def second(i, _):
            rows = pl.ds(pl.multiple_of(i * tb, tb), tb)
            nxt_i = jnp.minimum(i + 1, nb - 1)
            nxt = pl.ds(pl.multiple_of(nxt_i * tb, tb), tb)
            count = jnp.minimum(pos + i * tb + 1, window).astype(F32)
            count_n = jnp.minimum(pos + nxt_i * tb + 1, window).astype(F32)
            dpb = dpool_s[rows, :]
            cur = (dpb / count).astype(BF16)
            nx = dpool_s[nxt, :] / count_n
            nx = jnp.where(i < nb - 1, nx, 0.0).astype(BF16)
            du = _dot(band0_t, cur, NN) + _dot(band1_t, nx, NN) - dpb
            du_ref[0, rows, :] = du.astype(BF16)
            return 0

        lax.fori_loop(0, nb, second, 0)

    return _call(
        body, name=name, grid=(N_POOL_GROUPS, b),
        in_specs=[pl.BlockSpec((1, s, pg), lambda g, bi: (bi, 0, g)),
                  pl.BlockSpec((1, s, pg), lambda g, bi: (bi, 0, N_POOL_GROUPS + g)),
                  pl.BlockSpec((1, s, pg), lambda g, bi: (bi, 0, first_block + g)),
                  pl.BlockSpec((1, pg, pg), lambda g, bi: (g, 0, 0)),
                  pl.BlockSpec((1, pg), lambda g, bi: (0, g))],
        out_specs=[pl.BlockSpec((1, s, pg), lambda g, bi: (bi, 0, g)),
                   pl.BlockSpec((1, s, pg), lambda g, bi: (bi, 0, g)),
                   pl.BlockSpec((1, pg, pg), lambda g, bi: (g, 0, 0)),
                   pl.BlockSpec((1, pg), lambda g, bi: (0, g))],
        out_shape=[jax.ShapeDtypeStruct((b, s, pw), BF16), jax.ShapeDtypeStruct((b, s, pw), BF16),
                   jax.ShapeDtypeStruct((N_POOL_GROUPS, pg, pg), F32),
                   jax.ShapeDtypeStruct((1, pw), F32)],
        scratch_shapes=[pltpu.VMEM((s, pg), F32)],
        semantics=("parallel", "arbitrary"))(pp, pp, dcat, w_pool, scale)


def _adamw(recvs, owns, w, m, v, name):
    depth = len(recvs)
    r, c = owns[0].shape
    tr = min(128, r)
    nb = r // tr
    c1 = 1.0 - ADAM_B1 ** ADAM_STEP
    c2 = 1.0 - ADAM_B2 ** ADAM_STEP

    def body(*refs):
        recv_refs, own_refs = refs[:depth], refs[depth:2 * depth]
        w_ref, m_ref, v_ref, g_ref, d_ref, nm_ref, nv_ref = refs[2 * depth:]
        x, y, core = (lax.axis_index(ax) for ax in MESH_AXES)
        me = 4 * x + 2 * y + core
        for layer in range(depth):
            @pl.when(pl.program_id(0) == layer)
            def _(layer=layer):
                own = own_refs[layer][...].astype(F32)
                g = jnp.where(me == 0, own, recv_refs[layer][0].astype(F32))
                for sl in range(1, N_DEV):
                    g = g + jnp.where(me == sl, own, recv_refs[layer][sl].astype(F32))
                mn = ADAM_B1 * m_ref[0] + (1.0 - ADAM_B1) * g
                vn = ADAM_B2 * v_ref[0] + (1.0 - ADAM_B2) * (g * g)
                m_hat = mn / c1
                v_hat = vn / c2
                g_ref[0] = g
                d_ref[0] = -ADAM_LR * (m_hat / (jnp.sqrt(v_hat) + ADAM_EPS) + ADAM_WD * w_ref[0])
                nm_ref[0] = mn
                nv_ref[0] = vn

    def blk(layer):
        return lambda l, i: jnp.clip(i + (l - layer) * nb, 0, nb - 1)

    in_specs = [pl.BlockSpec((N_DEV, tr, c), lambda l, i, f=blk(layer): (0, f(l, i), 0))
                for layer in range(depth)]
    in_specs += [pl.BlockSpec((tr, c), lambda l, i, f=blk(layer): (f(l, i), 0))
                 for layer in range(depth)]
    row = pl.BlockSpec((1, tr, c), lambda l, i: (l, i, 0))
    return _call(body, name=name, grid=(depth, nb), in_specs=in_specs + [row, row, row],
                 out_specs=[row] * 4, out_shape=[jax.ShapeDtypeStruct((depth, r, c), F32)] * 4,
                 semantics=("arbitrary", "arbitrary"))(*recvs, *owns, w, m, v)


def _pack_w_in(gathered, a, heads, pw):
    d = gathered.shape[1]
    w_full = jnp.transpose(gathered, (1, 0, 2)).reshape(d, -1)
    wf = jnp.pad(w_full[:, 4 * a:4 * a + heads], ((0, 0), (0, LANES - heads)))
    return w_full, w_full[:, 4 * a + heads:], wf


def _unpack_dw_in(parts, heads):
    dq, dk, dv, dz, dwf, du, dzp = parts
    d = dq.shape[0]
    full = jnp.concatenate([dq, dk, dv, dz, dwf[:, :heads], du, dzp], axis=1)
    return jnp.transpose(full.reshape(d, N_DEV, -1), (1, 0, 2))


def kernel(x, p, norm_pre, norm_post, w_in, b_f, w_pool, pool_scale, w_out, w_pg, w_pe, loss_target, m_norm_pre, m_norm_post, m_w_in, m_b_f, m_w_pool, m_pool_scale, m_w_out, m_w_pg, m_w_pe, v_norm_pre, v_norm_post, v_w_in, v_b_f, v_w_pool, v_pool_scale, v_w_out, v_w_pg, v_w_pe):
    depth = w_in.shape[0]
    b, s, d = x.shape
    t = b * s
    heads = b_f.shape[1]
    a = heads * HEAD_DIM
    pairs = a // LANES
    pw = pool_scale.shape[1]
    pg = pw // N_POOL_GROUPS
    ple = p.shape[-1]
    mix_w = a + pw

    me = 4 * lax.axis_index("x") + 2 * lax.axis_index("y") + lax.axis_index("c")
    shard = {
        "w_in": [w_in[i].astype(BF16) for i in range(depth)],
        "w_pool": [w_pool[i].reshape(N_POOL_GROUPS * (pg // N_DEV), pg).astype(BF16)
                   for i in range(depth)],
        "w_out": [w_out[i].astype(BF16) for i in range(depth)],
        "w_pg": [w_pg[i].astype(BF16) for i in range(depth)],
        "w_pe": [w_pe[i].astype(BF16) for i in range(depth)],
    }
    names = list(shard)
    rest = names[1:]

    def unpack_rest(lands, layer, which):
        g = {nm: _with_own(ld, shard[nm][layer], me) for nm, ld in zip(which, lands)}
        g_pool = g["w_pool"].reshape(N_DEV, N_POOL_GROUPS, pg // N_DEV, pg)
        return dict(wpool=jnp.transpose(g_pool, (1, 0, 2, 3)).reshape(N_POOL_GROUPS, pg, pg),
                    wout=g["w_out"].reshape(mix_w, d), wpg=g["w_pg"].reshape(d, d),
                    wpe=jnp.transpose(g["w_pe"], (1, 0, 2)).reshape(ple, d))

    (g_in0,) = _exchange([(shard["w_in"][0], False)], "gather_w_in0")
    rest0, tok_rest0 = _exchange_start([(shard[nm][0], False) for nm in rest], g_in0,
                                       "gather_rest0_start")
    later, tok = [], tok_rest0
    for i in range(1, depth):
        hdl, tk_i = _exchange_start([(shard[nm][i], False) for nm in names], g_in0,
                                    "gather_layer%d_start" % i)
        later.append(hdl)
        tok = tok + tk_i

    h = x.reshape(t, d)
    saved = []
    layers = []
    for i in range(depth):
        sv = dict(h=h)
        g_pre = norm_pre[i:i + 1]
        g_post = norm_post[i:i + 1]
        bf = jnp.pad(b_f[i:i + 1], ((0, 0), (0, LANES - heads)))
        scale = pool_scale[i:i + 1]
        if i == 0:
            lw = dict(zip(("wa", "wp", "wf"), _pack_w_in(g_in0, a, heads, pw)))
            g_pre = g_pre + tok
        else:
            lands = _exchange_wait(later[i - 1], h, "gather_layer%d_wait" % i)
            g_in = _with_own(lands[0], shard["w_in"][i], me)
            lw = dict(zip(("wa", "wp", "wf"), _pack_w_in(g_in, a, heads, pw)))
            lw.update(unpack_rest(lands[1:], i, rest))
        hn = _rms_fwd(h, g_pre, "rms_pre")
        pa = _matmul([(hn, lw["wa"])], "nn", BF16, "proj_attn", n_dim=4 * a, tn=2048,
                     n_outer=True).reshape(b, s, 4 * a)
        pp = _matmul([(hn, lw["wp"])], "nn", BF16, "proj_pool", tn=2048,
                     n_outer=True).reshape(b, s, 2 * pw)
        fl = _matmul([(hn, lw["wf"])], "nn", F32, "proj_gate").reshape(b, s, LANES)
        c = _gates_fwd(fl, bf, "gates_fwd")
        qa, ka, kat, va, vt = _attn_prep_fwd(pa, c, "attn_prep_fwd")
        o, ga, lse = _attn_fwd(qa, ka, vt, pa, "attn_fwd")
        if i == 0:
            lw.update(unpack_rest(_exchange_wait(rest0, lse, "gather_rest0_wait"), 0, rest))
        layers.append(lw)
        gp = _pool_fwd(pp, lw["wpool"], scale, "pool_fwd")
        ga2 = ga.reshape(t, a)
        gp2 = gp.reshape(t, pw)
        mix = _matmul([(ga2, lw["wout"], 0, 0), (gp2, lw["wout"], a, 0)], "nn", F32, "mix_out")
        h1, h1b = _post_fwd(h, mix, g_post, "post_fwd")
        pb = p[i].reshape(t, ple).astype(BF16)
        gpre = _matmul([(h1b, lw["wpg"])], "nn", F32, "ple_gate")
        e = _matmul([(pb, lw["wpe"])], "nn", F32, "ple_embed")
        h = _ple_fwd(h1, gpre, e, "ple_fwd")
        sv.update(hn=hn, pa=pa, pp=pp, fl=fl, bf=bf, qa=qa, ka=ka, kat=kat, va=va, o=o, lse=lse, ga=ga2,
                  gp=gp2, mix=mix,
                  h1b=h1b, pb=pb, gpre=gpre, e=e, g_pre=g_pre, g_post=g_post, scale=scale)
        saved.append(sv)

    dh, sq = _loss_bwd(h, loss_target.reshape(t, d), "loss")
    loss = lax.psum(0.5 * jnp.sum(sq) / d, MESH_AXES)

    big = {nm: [None] * depth for nm in names}
    small = {nm: [None] * depth for nm in ("norm_pre", "norm_post", "b_f", "pool_scale")}
    grad_handles = [None] * depth
    for i in reversed(range(depth)):
        lw, sv = layers[i], saved[i]
        de, dpre = _ple_bwd(dh, sv["gpre"], sv["e"], "ple_bwd")
        dwpe = _matmul([(sv["pb"], de)], "tn", BF16, "dw_pe", tm=1024)
        dwpg = _matmul([(sv["h1b"], dpre)], "tn", BF16, "dw_pg", tm=1024)
        t1 = _matmul([(dpre, lw["wpg"])], "nt", F32, "d_h1")
        dh1, dmix, dg_post = _post_bwd(dh, t1, sv["mix"], sv["g_post"], "post_bwd")
        dwout = jnp.concatenate(
            [_matmul([(sv["ga"], dmix)], "tn", BF16, "dw_out_attn", tm=1024),
             _matmul([(sv["gp"], dmix)], "tn", BF16, "dw_out_pool", tm=1024)], axis=0)
        dcat = _matmul([(dmix, lw["wout"])], "nt", BF16, "d_cat", tn=2048).reshape(b, s, mix_w)
        du, dzp, dwpool, dscale = _pool_bwd(sv["pp"], dcat, lw["wpool"], sv["scale"], a // pg,
                                            "pool_bwd")
        qab, doa, dz = _attn_prep_bwd(dcat, sv["pa"], sv["o"], sv["lse"], sv["qa"], "attn_prep_bwd")
        dq, dk, dv, dcp = _attn_bwd(sv["ka"], sv["kat"], sv["va"], qab, doa, "attn_bwd")
        dc = jnp.transpose(dcp[..., :2], (0, 2, 1, 3)).reshape(b, s, heads)
        dc = jnp.pad(dc, ((0, 0), (0, 0), (0, LANES - heads)))
        dfl, dbf = _gates_bwd(dc, sv["fl"], sv["bf"], heads, "gates_bwd")
        dproj = [g_.reshape(t, -1) for g_ in (dq, dk, dv, dz, dfl, du, dzp)]
        dw_parts = [_matmul([(sv["hn"], g_)], "tn", BF16, "dw_in_%d" % n_, tm=1024)
                    for n_, g_ in enumerate(dproj)]

        big["w_in"][i] = _unpack_dw_in(dw_parts, heads)
        big["w_pool"][i] = jnp.transpose(
            dwpool.astype(BF16).reshape(N_POOL_GROUPS, N_DEV, pg // N_DEV, pg), (1, 0, 2, 3)
        ).reshape(N_DEV, N_POOL_GROUPS * (pg // N_DEV), pg)
        big["w_out"][i] = dwout.reshape(N_DEV, mix_w // N_DEV, d)
        big["w_pg"][i] = dwpg.reshape(N_DEV, d // N_DEV, d)
        big["w_pe"][i] = jnp.transpose(dwpe.reshape(ple, N_DEV, d // N_DEV), (1, 0, 2))
        grad_handles[i], tok = _exchange_start([(big[nm][i], True) for nm in names], dw_parts[-1],
                                               "grads_layer%d_start" % i)

        dq2, dk2, dv2, dz2, dfl2, du2, dzp2 = dproj
        dhn = _matmul([(dq2, lw["wa"], 0, 0), (dk2, lw["wa"], a, 0), (dv2, lw["wa"], 2 * a, 0),
                       (dz2, lw["wa"], 3 * a, 0), (du2, lw["wp"], 0, 0), (dzp2, lw["wp"], pw, 0),
                       (dfl2, lw["wf"] + tok.astype(BF16), 0, 0)], "nt", F32, "d_hn")
        dh, dg_pre = _pre_bwd(sv["h"], dhn, dh1, sv["g_pre"] + tok, "pre_bwd")
        small["norm_pre"][i] = dg_pre
        small["norm_post"][i] = dg_post
        small["b_f"][i] = jnp.sum(dbf, axis=0)
        small["pool_scale"][i] = dscale
    grad_x = dh.reshape(b, s, d)

    width = max(d, pw)
    small_names = ("norm_pre", "norm_post", "pool_scale", "b_f")

    def small_rows(get):
        rows = []
        for nm in small_names:
            for i in range(depth):
                v_ = get(nm, i)
                rows.append(jnp.pad(v_, ((0, 0), (0, width - v_.shape[1]))))
        return jnp.concatenate(rows, axis=0)

    small_g = small_rows(lambda nm, i: small[nm][i])
    (small_recv,) = _exchange([(small_g, False)], "exchange_small")
    received = [_exchange_wait(grad_handles[i], dh, "grads_layer%d_wait" % i) for i in range(depth)]

    weights = dict(norm_pre=norm_pre, norm_post=norm_post, w_in=w_in, b_f=b_f, w_pool=w_pool,
                   pool_scale=pool_scale, w_out=w_out, w_pg=w_pg, w_pe=w_pe)
    mom1 = dict(norm_pre=m_norm_pre, norm_post=m_norm_post, w_in=m_w_in, b_f=m_b_f, w_pool=m_w_pool,
                pool_scale=m_pool_scale, w_out=m_w_out, w_pg=m_w_pg, w_pe=m_w_pe)
    mom2 = dict(norm_pre=v_norm_pre, norm_post=v_norm_post, w_in=v_w_in, b_f=v_b_f, w_pool=v_w_pool,
                pool_scale=v_pool_scale, w_out=v_w_out, w_pg=v_w_pg, w_pe=v_w_pe)

    results = {}
    for j, nm in enumerate(names):
        shp = weights[nm].shape
        recvs = [received[i][j] for i in range(depth)]
        owns = [lax.dynamic_index_in_dim(big[nm][i], me, 0, keepdims=False) for i in range(depth)]
        flat = lambda arr: arr.reshape((depth,) + owns[0].shape)
        outs = _adamw(recvs, owns, flat(weights[nm]), flat(mom1[nm]), flat(mom2[nm]), "adamw_" + nm)
        results[nm] = [o_.reshape(shp) for o_ in outs]

    small_w = small_rows(lambda nm, i: weights[nm][i:i + 1])[None]
    small_m = small_rows(lambda nm, i: mom1[nm][i:i + 1])[None]
    small_v = small_rows(lambda nm, i: mom2[nm][i:i + 1])[None]
    outs = _adamw([small_recv], [small_g], small_w, small_m, small_v, "adamw_small")
    for j, nm in enumerate(small_names):
        cols = weights[nm].shape[1]
        results[nm] = [o_[0, j * depth:(j + 1) * depth, :cols] for o_ in outs]

    order = ("norm_pre", "norm_post", "w_in", "b_f", "w_pool", "pool_scale", "w_out", "w_pg", "w_pe")
    return (loss, grad_x, *[results[nm][0] for nm in order], *[results[nm][1] for nm in order],
            *[results[nm][2] for nm in order], *[results[nm][3] for nm in order])
```

```python
import functools
import math

import jax
import jax.numpy as jnp
from jax import lax
from jax.experimental import pallas as pl
from jax.experimental.pallas import tpu as pltpu

N_DEV = 8
MESH_AXES = ("x", "y", "c")
HEAD_DIM = 64
LANES = 128
N_POOL_GROUPS = 4
EPS = 1e-6
ADAM_LR = 0.001
ADAM_B1 = 0.9
ADAM_B2 = 0.999
ADAM_EPS = 1e-08
ADAM_WD = 0.01
ADAM_STEP = 10
VMEM_LIMIT_BYTES = 56 * 1024 * 1024
F32 = jnp.float32
BF16 = jnp.bfloat16
NEG_INF = float("-inf")


def _call(body, *, name, grid, in_specs, out_specs, out_shape, scratch_shapes=(), semantics=None):
    return pl.pallas_call(
        body, name=name, grid=grid, in_specs=in_specs, out_specs=out_specs, out_shape=out_shape,
        scratch_shapes=list(scratch_shapes),
        compiler_params=pltpu.CompilerParams(dimension_semantics=semantics,
                                             vmem_limit_bytes=VMEM_LIMIT_BYTES))


def _sigmoid(z):
    return 1.0 / (1.0 + jnp.exp(-z))


def _dot(a, b, dims):
    return lax.dot_general(a, b, (dims, ((), ())), preferred_element_type=F32)


NN = ((1,), (0,))
NT = ((1,), (1,))
TN = ((0,), (0,))


def _exchange(items, name):
    n = len(items)
    modes = [s for _, s in items]
    out_shapes = []
    for a, s in items:
        shp = a.shape[1:] if s else a.shape
        out_shapes.append(jax.ShapeDtypeStruct((N_DEV,) + tuple(shp), a.dtype))

    def body(*refs):
        ins = refs[:n]
        outs = refs[n:2 * n]
        send_sems, recv_sems, local_sems = refs[2 * n:]
        x, y, c = (lax.axis_index(ax) for ax in MESH_AXES)
        me = 4 * x + 2 * y + c
        started = []
        for i in range(n):
            mine = ins[i].at[me] if modes[i] else ins[i]
            loc = pltpu.make_async_copy(mine, outs[i].at[me], local_sems.at[i])
            loc.start()
            started.append(loc)
        remote = []
        for k in range(1, N_DEV):
            px = x ^ ((k >> 2) & 1)
            py = y ^ ((k >> 1) & 1)
            pc = c ^ (k & 1)
            peer = me ^ k
            for i in range(n):
                src = ins[i].at[peer] if modes[i] else ins[i]
                cp = pltpu.make_async_remote_copy(
                    src_ref=src, dst_ref=outs[i].at[me],
                    send_sem=send_sems.at[i, k - 1], recv_sem=recv_sems.at[i, k - 1],
                    device_id=(px, py, pc), device_id_type=pl.DeviceIdType.MESH)
                cp.start()
                remote.append(cp)
        for cp in remote:
            cp.wait()
        for loc in started:
            loc.wait()

    hbm = pl.BlockSpec(memory_space=pltpu.HBM)
    return pl.pallas_call(
        body, name=name, out_shape=out_shapes,
        in_specs=[hbm] * n, out_specs=[hbm] * n,
        scratch_shapes=[pltpu.SemaphoreType.DMA((n, N_DEV - 1)),
                        pltpu.SemaphoreType.DMA((n, N_DEV - 1)),
                        pltpu.SemaphoreType.DMA((n,))],
    )(*[a for a, _ in items])


def _peer_copies(srcs, lands, modes, send_sems, recv_sems):
    x, y, c = (lax.axis_index(ax) for ax in MESH_AXES)
    me = 4 * x + 2 * y + c
    copies = []
    for k in range(1, N_DEV):
        peer_id = (x ^ ((k >> 2) & 1), y ^ ((k >> 1) & 1), c ^ (k & 1))
        for i, scatter in enumerate(modes):
            src = srcs[i].at[me ^ k] if scatter else srcs[i]
            pair = i * (N_DEV - 1) + k - 1
            copies.append(pltpu.make_async_remote_copy(
                src_ref=src, dst_ref=lands[i].at[me],
                send_sem=send_sems.at[pair], recv_sem=recv_sems.at[pair],
                device_id=peer_id, device_id_type=pl.DeviceIdType.MESH))
    return copies


def _exchange_start(items, after, name):
    n = len(items)
    modes = [s for _, s in items]
    srcs = [pltpu.with_memory_space_constraint(a, pltpu.HBM) for a, _ in items]
    lands = []
    for a, s in items:
        shp = (N_DEV,) + tuple(a.shape[1:] if s else a.shape)
        lands.append(pltpu.with_memory_space_constraint(lax.empty(shp, a.dtype), pltpu.HBM))

    def body(*refs):
        send_sems, recv_sems = refs[2 * n + 1], refs[2 * n + 2]
        token = refs[-1]
        for cp in _peer_copies(refs[:n], refs[n:2 * n], modes, send_sems, recv_sems):
            cp.start()
        token[...] = jnp.zeros_like(token)

    hbm = pl.BlockSpec(memory_space=pltpu.HBM)
    sem = pl.BlockSpec(memory_space=pltpu.SEMAPHORE)
    outs = pl.pallas_call(
        body, name=name,
        out_shape=(pltpu.SemaphoreType.DMA((n * (N_DEV - 1),)),
                   pltpu.SemaphoreType.DMA((n * (N_DEV - 1),)),
                   *[pltpu.HBM(a.shape, a.dtype) for a in srcs + lands],
                   jax.ShapeDtypeStruct((8, LANES), F32)),
        in_specs=[hbm] * (2 * n) + [pl.BlockSpec(memory_space=pl.ANY)],
        out_specs=(sem, sem, *([hbm] * (2 * n)), pl.BlockSpec(memory_space=pltpu.VMEM)),
        input_output_aliases={i: 2 + i for i in range(2 * n)},
        compiler_params=pltpu.CompilerParams(
            has_side_effects=pltpu.SideEffectType.DATAFLOW_SIDE_EFFECTING),
    )(*srcs, *lands, after)
    handle = (modes, outs[0], outs[1], list(outs[2:2 + n]), list(outs[2 + n:2 + 2 * n]))
    return handle, outs[-1][0, 0]


def _exchange_wait(handle, after, name):
    modes, send_sems, recv_sems, srcs, lands = handle
    n = len(modes)

    def body(*refs):
        for cp in _peer_copies(refs[:n], refs[n:2 * n], modes, refs[2 * n], refs[2 * n + 1]):
            cp.wait_send()
            cp.wait_recv()

    hbm = pl.BlockSpec(memory_space=pltpu.HBM)
    sem = pl.BlockSpec(memory_space=pltpu.SEMAPHORE)
    outs = pl.pallas_call(
        body, name=name,
        out_shape=tuple(pltpu.HBM(a.shape, a.dtype) for a in srcs + lands),
        in_specs=[hbm] * (2 * n) + [sem, sem, pl.BlockSpec(memory_space=pl.ANY)],
        out_specs=tuple([hbm] * (2 * n)),
        input_output_aliases={i: i for i in range(2 * n)},
        compiler_params=pltpu.CompilerParams(
            has_side_effects=pltpu.SideEffectType.DATAFLOW_SIDE_EFFECTING),
    )(*srcs, *lands, send_sems, recv_sems, after)
    return list(outs[n:])


def _with_own(slots, own, me):
    idx = lax.broadcasted_iota(jnp.int32, (N_DEV,) + (1,) * own.ndim, 0)
    return jnp.where(idx == me, own[None], slots)


def _matmul(pairs, mode, out_dtype, name, n_dim=None, tm=512, tn=1024, tk=1024, n_outer=False):
    dims = {"nn": NN, "nt": NT, "tn": TN}[mode]
    pairs = [tuple(pr) + (0, 0) * (len(pr) == 2) for pr in pairs]
    a0, b0 = pairs[0][:2]
    m_dim = a0.shape[1] if mode == "tn" else a0.shape[0]
    if n_dim is None:
        n_dim = b0.shape[0] if mode == "nt" else b0.shape[1]
    tm = min(tm, m_dim)
    tn = min(tn, n_dim)
    segs = []
    off = 0
    for a, _, k0, n0 in pairs:
        k_dim = a.shape[0] if mode == "tn" else a.shape[1]
        t = min(tk, k_dim)
        segs.append((off, k_dim // t, t, k0 // t, n0 // tn))
        off += k_dim // t
    nk = off
    n_pairs = len(pairs)

    def ij(g0, g1):
        return (g1, g0) if n_outer else (g0, g1)

    in_specs = []
    for (o, cnt, t, kb, nb) in segs:
        def kc(kk, o=o, cnt=cnt):
            return jnp.clip(kk - o, 0, cnt - 1)
        if mode == "tn":
            in_specs.append(pl.BlockSpec((t, tm), lambda g0, g1, kk, kc=kc: (kc(kk), ij(g0, g1)[0])))
        else:
            in_specs.append(pl.BlockSpec((tm, t), lambda g0, g1, kk, kc=kc: (ij(g0, g1)[0], kc(kk))))
        if mode == "nt":
            in_specs.append(pl.BlockSpec((tn, t), lambda g0, g1, kk, kc=kc, kb=kb, nb=nb:
                                         (nb + ij(g0, g1)[1], kb + kc(kk))))
        else:
            in_specs.append(pl.BlockSpec((t, tn), lambda g0, g1, kk, kc=kc, kb=kb, nb=nb:
                                         (kb + kc(kk), nb + ij(g0, g1)[1])))

    def body_single(a_ref, b_ref, out_ref):
        out_ref[...] = _dot(a_ref[...], b_ref[...], dims).astype(out_dtype)

    def body(*refs):
        out_ref = refs[2 * n_pairs]
        acc = refs[2 * n_pairs + 1]
        kk = pl.program_id(2)

        @pl.when(kk == 0)
        def _():
            acc[...] = jnp.zeros_like(acc)

        for idx, (o, cnt) in enumerate(sg[:2] for sg in segs):
            @pl.when((kk >= o) & (kk < o + cnt))
            def _(idx=idx):
                acc[...] += _dot(refs[2 * idx][...], refs[2 * idx + 1][...], dims)

        @pl.when(kk == nk - 1)
        def _():
            out_ref[...] = acc[...].astype(out_dtype)

    flat = [t for pr in pairs for t in pr[:2]]
    tiles = (m_dim // tm, n_dim // tn)
    return _call(body_single if nk == 1 else body, name=name,
                 grid=ij(*tiles) + (nk,), in_specs=in_specs,
                 out_specs=pl.BlockSpec((tm, tn), lambda g0, g1, kk: ij(g0, g1)),
                 out_shape=jax.ShapeDtypeStruct((m_dim, n_dim), out_dtype),
                 scratch_shapes=[] if nk == 1 else [pltpu.VMEM((tm, tn), F32)],
                 semantics=("parallel", "parallel", "arbitrary"))(*flat)


def _row_tile(t):
    return min(256, t)


def _rms_fwd(h, g, name):
    t, d = h.shape
    tt = _row_tile(t)

    def body(h_ref, g_ref, o_ref):
        hv = h_ref[...]
        r = lax.rsqrt(jnp.mean(hv * hv, axis=-1, keepdims=True) + EPS)
        o_ref[...] = (hv * r * g_ref[...]).astype(BF16)

    row = pl.BlockSpec((tt, d), lambda i: (i, 0))
    vec = pl.BlockSpec((1, d), lambda i: (0, 0))
    return _call(body, name=name, grid=(t // tt,), in_specs=[row, vec], out_specs=row,
                 out_shape=jax.ShapeDtypeStruct((t, d), BF16), semantics=("parallel",))(h, g)


def _post_fwd(h, mix, g, name):
    t, d = h.shape
    tt = _row_tile(t)

    def body(h_ref, m_ref, g_ref, o_ref, ob_ref):
        mv = m_ref[...]
        r = lax.rsqrt(jnp.mean(mv * mv, axis=-1, keepdims=True) + EPS)
        h1 = h_ref[...] + mv * r * g_ref[...]
        o_ref[...] = h1
        ob_ref[...] = h1.astype(BF16)

    row = pl.BlockSpec((tt, d), lambda i: (i, 0))
    vec = pl.BlockSpec((1, d), lambda i: (0, 0))
    return _call(body, name=name, grid=(t // tt,), in_specs=[row, row, vec], out_specs=[row, row],
                 out_shape=[jax.ShapeDtypeStruct((t, d), F32), jax.ShapeDtypeStruct((t, d), BF16)],
                 semantics=("parallel",))(h, mix, g)


def _ple_fwd(h1, gpre, e, name):
    t, d = h1.shape
    tt = _row_tile(t)

    def body(h_ref, g_ref, e_ref, o_ref):
        o_ref[...] = h_ref[...] + _sigmoid(g_ref[...]) * e_ref[...]

    row = pl.BlockSpec((tt, d), lambda i: (i, 0))
    return _call(body, name=name, grid=(t // tt,), in_specs=[row, row, row], out_specs=row,
                 out_shape=jax.ShapeDtypeStruct((t, d), F32), semantics=("parallel",))(h1, gpre, e)


def _loss_bwd(y, target, name):
    t, d = y.shape
    tt = _row_tile(t)

    def body(y_ref, t_ref, dy_ref, s_ref):
        @pl.when(pl.program_id(0) == 0)
        def _():
            s_ref[...] = jnp.zeros_like(s_ref)
        diff = y_ref[...] - t_ref[...]
        dy_ref[...] = diff * (1.0 / d)
        s_ref[...] += jnp.sum(diff * diff, axis=0, keepdims=True)

    row = pl.BlockSpec((tt, d), lambda i: (i, 0))
    vec = pl.BlockSpec((1, d), lambda i: (0, 0))
    return _call(body, name=name, grid=(t // tt,), in_specs=[row, row], out_specs=[row, vec],
                 out_shape=[jax.ShapeDtypeStruct((t, d), F32), jax.ShapeDtypeStruct((1, d), F32)],
                 semantics=("arbitrary",))(y, target)


def _ple_bwd(dh2, gpre, e, name):
    t, d = dh2.shape
    tt = _row_tile(t)

    def body(d_ref, g_ref, e_ref, de_ref, dp_ref):
        gate = _sigmoid(g_ref[...])
        dv = d_ref[...]
        de_ref[...] = (dv * gate).astype(BF16)
        dp_ref[...] = (dv * e_ref[...] * gate * (1.0 - gate)).astype(BF16)

    row = pl.BlockSpec((tt, d), lambda i: (i, 0))
    return _call(body, name=name, grid=(t // tt,), in_specs=[row, row, row], out_specs=[row, row],
                 out_shape=[jax.ShapeDtypeStruct((t, d), BF16)] * 2,
                 semantics=("parallel",))(dh2, gpre, e)


def _post_bwd(dh2, t1, mix, g, name):
    t, d = dh2.shape
    tt = _row_tile(t)

    def body(d_ref, t_ref, m_ref, g_ref, dh_ref, dm_ref, dg_ref):
        @pl.when(pl.program_id(0) == 0)
        def _():
            dg_ref[...] = jnp.zeros_like(dg_ref)
        dh1 = d_ref[...] + t_ref[...]
        mv = m_ref[...]
        r = lax.rsqrt(jnp.mean(mv * mv, axis=-1, keepdims=True) + EPS)
        dh_ref[...] = dh1
        dg_ref[...] += jnp.sum(dh1 * mv * r, axis=0, keepdims=True)
        w = dh1 * g_ref[...]
        dot = jnp.mean(w * mv, axis=-1, keepdims=True)
        dm_ref[...] = (r * w - mv * (r * r * r) * dot).astype(BF16)

    row = pl.BlockSpec((tt, d), lambda i: (i, 0))
    vec = pl.BlockSpec((1, d), lambda i: (0, 0))
    return _call(body, name=name, grid=(t // tt,), in_specs=[row, row, row, vec],
                 out_specs=[row, row, vec],
                 out_shape=[jax.ShapeDtypeStruct((t, d), F32), jax.ShapeDtypeStruct((t, d), BF16),
                            jax.ShapeDtypeStruct((1, d), F32)],
                 semantics=("arbitrary",))(dh2, t1, mix, g)


def _pre_bwd(h, dhn, dh1, g, name):
    t, d = h.shape
    tt = _row_tile(t)

    def body(h_ref, dn_ref, d1_ref, g_ref, dh_ref, dg_ref):
        @pl.when(pl.program_id(0) == 0)
        def _():
            dg_ref[...] = jnp.zeros_like(dg_ref)
        hv = h_ref[...]
        dn = dn_ref[...]
        r = lax.rsqrt(jnp.mean(hv * hv, axis=-1, keepdims=True) + EPS)
        dg_ref[...] += jnp.sum(dn * hv * r, axis=0, keepdims=True)
        w = dn * g_ref[...]
        dot = jnp.mean(w * hv, axis=-1, keepdims=True)
        dh_ref[...] = d1_ref[...] + r * w - hv * (r * r * r) * dot

    row = pl.BlockSpec((tt, d), lambda i: (i, 0))
    vec = pl.BlockSpec((1, d), lambda i: (0, 0))
    return _call(body, name=name, grid=(t // tt,), in_specs=[row, row, row, vec],
                 out_specs=[row, vec],
                 out_shape=[jax.ShapeDtypeStruct((t, d), F32), jax.ShapeDtypeStruct((1, d), F32)],
                 semantics=("arbitrary",))(h, dhn, dh1, g)


def _split3(v):
    hi = v.astype(BF16)
    r1 = v - hi.astype(F32)
    mid = r1.astype(BF16)
    lo = (r1 - mid.astype(F32)).astype(BF16)
    return hi, mid, lo


def _gates_fwd(fl, bf, name):
    b, s, _ = fl.shape

    def body(f_ref, b_ref, c_ref):
        xv = f_ref[0] + b_ref[...]
        lf = jnp.minimum(xv, 0.0) - jnp.log(1.0 + jnp.exp(-jnp.abs(xv)))
        dst = lax.broadcasted_iota(jnp.int32, (s, s), 0)
        src = lax.broadcasted_iota(jnp.int32, (s, s), 1)
        lower = (src <= dst).astype(BF16)
        acc = jnp.zeros((s, LANES), F32)
        for part in _split3(lf):
            acc = acc + _dot(lower, part, NN)
        c_ref[0] = acc

    blk = pl.BlockSpec((1, s, LANES), lambda i: (i, 0, 0))
    return _call(body, name=name, grid=(b,),
                 in_specs=[blk, pl.BlockSpec((1, LANES), lambda i: (0, 0))],
                 out_specs=blk, out_shape=jax.ShapeDtypeStruct((b, s, LANES), F32),
                 semantics=("parallel",))(fl, bf)


def _gates_bwd(dc, fl, bf, heads, name):
    b, s, _ = fl.shape

    def body(d_ref, f_ref, b_ref, o_ref, db_ref):
        xv = f_ref[0] + b_ref[...]
        dst = lax.broadcasted_iota(jnp.int32, (s, s), 0)
        src = lax.broadcasted_iota(jnp.int32, (s, s), 1)
        later = (src >= dst).astype(BF16)
        dlf = jnp.zeros((s, LANES), F32)
        for part in _split3(d_ref[0]):
            dlf = dlf + _dot(later, part, NN)
        lane = lax.broadcasted_iota(jnp.int32, (s, LANES), 1)
        dfl = jnp.where(lane < heads, dlf * _sigmoid(-xv), 0.0)
        o_ref[0] = dfl.astype(BF16)
        db_ref[0] = jnp.sum(dfl, axis=0, keepdims=True)

    blk = pl.BlockSpec((1, s, LANES), lambda i: (i, 0, 0))
    return _call(body, name=name, grid=(b,),
                 in_specs=[blk, blk, pl.BlockSpec((1, LANES), lambda i: (0, 0))],
                 out_specs=[blk, pl.BlockSpec((1, 1, LANES), lambda i: (i, 0, 0))],
                 out_shape=[jax.ShapeDtypeStruct((b, s, LANES), BF16),
                            jax.ShapeDtypeStruct((b, 1, LANES), F32)],
                 semantics=("parallel",))(dc, fl, bf)


LANE_CQ = 64
LANE_CK = 67
LANE_LSE = 70
LANE_D = 64
N_PARTS = 3


def _attn_tiles(s):
    return min(512, s), min(256, s)


def _lanes_in(lane, first):
    return (lane >= first) & (lane < first + N_PARTS)


def _attn_prep_fwd(pa, c, name):
    b, s, a4 = pa.shape
    pairs = a4 // (4 * LANES)
    scale = 1.0 / math.sqrt(HEAD_DIM)
    wide = (1 + N_PARTS) * LANES

    def body(q_ref, k_ref, v_ref, c_ref, qa_ref, ka_ref, kat_ref, va_ref, vt_ref):
        hp = pl.program_id(1)
        c3 = jnp.concatenate(_split3(c_ref[0]), axis=1)
        qx = jnp.concatenate([q_ref[0], c3], axis=1)
        kx = jnp.concatenate([k_ref[0], c3], axis=1)
        vv = v_ref[0]
        lane = lax.broadcasted_iota(jnp.int32, (s, LANES), 1)
        row = lax.broadcasted_iota(jnp.int32, (wide, LANES), 0)
        col = lax.broadcasted_iota(jnp.int32, (wide, LANES), 1)
        r128 = lax.broadcasted_iota(jnp.int32, (LANES, LANES), 0)
        c128 = lax.broadcasted_iota(jnp.int32, (LANES, LANES), 1)
        ident = (r128 == c128).astype(BF16)
        for j in range(2):
            head = 2 * hp + j
            move = (row == col + HEAD_DIM * j) & (col < HEAD_DIM)
            move128 = (r128 == c128 + HEAD_DIM * j) & (c128 < HEAD_DIM)

            def pick(first, head=head):
                hit = (row == LANES + head) & (col == first)
                for i in range(1, N_PARTS):
                    hit = hit | ((row == LANES * (i + 1) + head) & (col == first + i))
                return hit

            mq = (jnp.where(move, scale, 0.0) + jnp.where(pick(LANE_CQ), 1.0, 0.0)).astype(BF16)
            mk = (jnp.where(move, 1.0, 0.0) - jnp.where(pick(LANE_CK), 1.0, 0.0)).astype(BF16)
            qa = _dot(qx, mq, NN) + jnp.where(_lanes_in(lane, LANE_CK), 1.0, 0.0)
            ka = _dot(kx, mk, NN) + jnp.where(
                _lanes_in(lane, LANE_CQ) | _lanes_in(lane, LANE_LSE), 1.0, 0.0)
            va = _dot(vv, move128.astype(BF16), NN) + jnp.where(_lanes_in(lane, LANE_D), 1.0, 0.0)
            kab = ka.astype(BF16)
            qa_ref[0, 0, j] = qa.astype(BF16)
            ka_ref[0, 0, j] = kab
            kat_ref[0, 0, j] = _dot(ident, kab, NT).astype(BF16)
            va_ref[0, 0, j] = va.astype(BF16)
        vt_ref[0, 0] = _dot(ident, vv, NT).astype(BF16)

    col_blk = lambda cidx: pl.BlockSpec((1, s, LANES), lambda bi, hp: (bi, 0, cidx * pairs + hp))
    tok = pl.BlockSpec((1, 1, 2, s, LANES), lambda bi, hp: (bi, hp, 0, 0, 0))
    tok_t = pl.BlockSpec((1, 1, 2, LANES, s), lambda bi, hp: (bi, hp, 0, 0, 0))
    tok_shape = jax.ShapeDtypeStruct((b, pairs, 2, s, LANES), BF16)
    return _call(
        body, name=name, grid=(b, pairs),
        in_specs=[col_blk(0), col_blk(1), col_blk(2),
                  pl.BlockSpec((1, s, LANES), lambda bi, hp: (bi, 0, 0))],
        out_specs=[tok, tok, tok_t, tok,
                   pl.BlockSpec((1, 1, LANES, s), lambda bi, hp: (bi, hp, 0, 0))],
        out_shape=[tok_shape, tok_shape, jax.ShapeDtypeStruct((b, pairs, 2, LANES, s), BF16),
                   tok_shape, jax.ShapeDtypeStruct((b, pairs, LANES, s), BF16)],
        semantics=("parallel", "parallel"))(pa, pa, pa, c)


def _attn_fwd(qa, ka, vt, pa, name):
    b, pairs, _, s, _ = qa.shape
    a = pairs * LANES
    tq = _attn_tiles(s)[0]
    nq = s // tq

    def body(q_ref, k_ref, vt_ref, z_ref, o_ref, g_ref, lse_ref):
        key_i = lax.broadcasted_iota(jnp.int32, (tq, tq), 0)
        qry_i = lax.broadcasted_iota(jnp.int32, (tq, tq), 1)

        def query_block(c):
            past = tq * c
            heads_out = []
            for j in range(2):
                qv = q_ref[0, 0, j]
                vrows = slice(HEAD_DIM * j, HEAD_DIM * (j + 1))
                sd = _dot(k_ref[0, 0, j, past:past + tq, :], qv, NT)
                sd = jnp.where(key_i <= qry_i, sd, NEG_INF)
                m = jnp.max(sd, axis=0, keepdims=True)
                if c > 0:
                    sp = _dot(k_ref[0, 0, j, 0:past, :], qv, NT)
                    m = jnp.maximum(m, jnp.max(sp, axis=0, keepdims=True))
                pd = jnp.exp(sd - m)
                l = jnp.sum(pd, axis=0, keepdims=True)
                acc = _dot(vt_ref[0, 0, vrows, past:past + tq], pd.astype(BF16), NN)
                if c > 0:
                    pp = jnp.exp(sp - m)
                    l = l + jnp.sum(pp, axis=0, keepdims=True)
                    acc = acc + _dot(vt_ref[0, 0, vrows, 0:past], pp.astype(BF16), NN)
                heads_out.append(acc / l)
                lse_ref[0, 0, j:j + 1, :] = m + jnp.log(l)
            ov = jnp.transpose(jnp.concatenate(heads_out, axis=0))
            o_ref[0] = ov.astype(BF16)
            zv = z_ref[0].astype(F32)
            g_ref[0] = (ov * zv * _sigmoid(zv)).astype(BF16)

        for c in range(nq):
            pl.when(pl.program_id(2) == c)(functools.partial(query_block, c))

    return _call(
        body, name=name, grid=(b, pairs, s // tq),
        in_specs=[pl.BlockSpec((1, 1, 2, tq, LANES), lambda bi, hp, qi: (bi, hp, 0, qi, 0)),
                  pl.BlockSpec((1, 1, 2, s, LANES), lambda bi, hp, qi: (bi, hp, 0, 0, 0)),
                  pl.BlockSpec((1, 1, LANES, s), lambda bi, hp, qi: (bi, hp, 0, 0)),
                  pl.BlockSpec((1, tq, LANES), lambda bi, hp, qi: (bi, qi, 3 * pairs + hp))],
        out_specs=[pl.BlockSpec((1, tq, LANES), lambda bi, hp, qi: (bi, qi, hp)),
                   pl.BlockSpec((1, tq, LANES), lambda bi, hp, qi: (bi, qi, hp)),
                   pl.BlockSpec((1, 1, 2, tq), lambda bi, hp, qi: (bi, hp, 0, qi))],
        out_shape=[jax.ShapeDtypeStruct((b, s, a), BF16), jax.ShapeDtypeStruct((b, s, a), BF16),
                   jax.ShapeDtypeStruct((b, pairs, 2, s), F32)],
        semantics=("parallel", "parallel", "arbitrary"))(qa, ka, vt, pa)


def _attn_prep_bwd(dcat, pa, o, lse, qa, name):
    b, pairs, _, s, _ = qa.shape
    a = pairs * LANES
    sub = 16

    def body(da_ref, z_ref, o_ref, lse_ref, qa_ref, qab_ref, doa_ref, dz_ref):
        zv = z_ref[0].astype(F32)
        dav = da_ref[0].astype(F32)
        ov = o_ref[0].astype(F32)
        sg = _sigmoid(zv)
        dov = dav * zv * sg
        dz_ref[0] = (dav * ov * sg * (1.0 + zv * (1.0 - sg))).astype(BF16)
        prod = dov * ov
        dob = dov.astype(BF16)
        lane = lax.broadcasted_iota(jnp.int32, (s, LANES), 1)
        r128 = lax.broadcasted_iota(jnp.int32, (LANES, LANES), 0)
        c128 = lax.broadcasted_iota(jnp.int32, (LANES, LANES), 1)
        prow = lax.broadcasted_iota(jnp.int32, (sub, s), 0)
        srow = lax.broadcasted_iota(jnp.int32, (sub, LANES), 0)
        scol = lax.broadcasted_iota(jnp.int32, (sub, LANES), 1)
        place = ((scol == srow + LANE_LSE) & (srow < N_PARTS)).astype(BF16)
        for j in range(2):
            in_head = (lane >= HEAD_DIM * j) & (lane < HEAD_DIM * (j + 1))
            dparts = _split3(jnp.sum(jnp.where(in_head, prod, 0.0), axis=1, keepdims=True))
            move128 = ((r128 == c128 + HEAD_DIM * j) & (c128 < HEAD_DIM)).astype(BF16)
            doa = _dot(dob, move128, NN)
            for i in range(N_PARTS):
                doa = jnp.where(lane == LANE_D + i, -dparts[i].astype(F32), doa)
            doa_ref[0, 0, j] = doa.astype(BF16)
            lparts = _split3(lse_ref[0, 0, j:j + 1, :])
            pmat = jnp.zeros((sub, s), BF16)
            for i in range(N_PARTS):
                pmat = jnp.where(prow == i, lparts[i], pmat)
            lcol = _dot(pmat, place, TN)
            qab_ref[0, 0, j] = (qa_ref[0, 0, j].astype(F32) - lcol).astype(BF16)

    tok = pl.BlockSpec((1, 1, 2, s, LANES), lambda bi, hp: (bi, hp, 0, 0, 0))
    tok_shape = jax.ShapeDtypeStruct((b, pairs, 2, s, LANES), BF16)
    pair_blk = pl.BlockSpec((1, s, LANES), lambda bi, hp: (bi, 0, hp))
    return _call(
        body, name=name, grid=(b, pairs),
        in_specs=[pair_blk,
                  pl.BlockSpec((1, s, LANES), lambda bi, hp: (bi, 0, 3 * pairs + hp)),
                  pair_blk,
                  pl.BlockSpec((1, 1, 2, s), lambda bi, hp: (bi, hp, 0, 0)),
                  tok],
        out_specs=[tok, tok, pair_blk],
        out_shape=[tok_shape, tok_shape, jax.ShapeDtypeStruct((b, s, a), BF16)],
        semantics=("parallel", "parallel"))(dcat, pa, o, lse, qa)


def _attn_bwd(ka, kat, va, qab, doa, name):
    b, pairs, _, s, _ = ka.shape
    a = pairs * LANES
    tq, tk = _attn_tiles(s)
    ratio = tq // tk
    nq, nk = s // tq, s // tk
    scale = 1.0 / math.sqrt(HEAD_DIM)

    def body(k_ref, kt_ref, v_ref, q_ref, do_ref, dq_ref, dk_ref, dv_ref, dc_ref,
             dqt_acc, dk_s, dv_s):
        key_i = lax.broadcasted_iota(jnp.int32, (tk, tq), 0)
        qry_i = lax.broadcasted_iota(jnp.int32, (tk, tq), 1)
        lane = lax.broadcasted_iota(jnp.int32, (tq, LANES), 1)
        low = lane < HEAD_DIM

        dqt_acc[...] = jnp.zeros_like(dqt_acc)

        def key_block(kj, _):
            krows = pl.ds(pl.multiple_of(kj * tk, tk), tk)
            kb = [k_ref[0, 0, j, krows, :] for j in range(2)]
            vb = [v_ref[0, 0, j, krows, :] for j in range(2)]
            ktb = [kt_ref[0, 0, j, :, krows] for j in range(2)]
            qd = kj // ratio

            def query_block(qi, carry, masked):
                qrows = pl.ds(pl.multiple_of(qi * tq, tq), tq)
                out = []
                for j in range(2):
                    dk, dv = carry[j]
                    qb = q_ref[0, 0, j, qrows, :]
                    dob = do_ref[0, 0, j, qrows, :]
                    pt = jnp.exp(_dot(kb[j], qb, NT))
                    if masked:
                        pt = jnp.where(key_i + (kj * tk - qi * tq) <= qry_i, pt, 0.0)
                    dst = pt * _dot(vb[j], dob, NT)
                    dsb = dst.astype(BF16)
                    dv = dv + _dot(pt.astype(BF16), dob, NN)
                    dk = dk + _dot(dsb, qb, NN)
                    dqt_acc[j, :, qrows] += _dot(ktb[j], dsb, NN)
                    out.append((dk, dv))
                return tuple(out)

            zero = jnp.zeros((tk, LANES), F32)
            carry = query_block(qd, ((zero, zero), (zero, zero)), True)
            carry = lax.fori_loop(qd + 1, nq, lambda qi, cr: query_block(qi, cr, False), carry)
            for j in range(2):
                dk_s[j, krows, :] = carry[j][0]
                dv_s[j, krows, :] = carry[j][1]
            return 0

        lax.fori_loop(0, nk, key_block, 0)

        def finish(i, _):
            rows = pl.ds(pl.multiple_of(i * tq, tq), tq)
            dq = [jnp.transpose(dqt_acc[j, :, rows]) for j in range(2)]
            dk = [dk_s[j, rows, :] for j in range(2)]
            dv = [dv_s[j, rows, :] for j in range(2)]
            dcol = [dq[j][:, LANE_CQ:LANE_CQ + 1] - dk[j][:, LANE_CK:LANE_CK + 1] for j in range(2)]
            dq = [dq[j] * scale for j in range(2)]
            for out_ref, val in ((dq_ref, dq), (dk_ref, dk), (dv_ref, dv)):
                merged = jnp.where(low, val[0], pltpu.roll(val[1], HEAD_DIM, 1))
                out_ref[0, rows, :] = merged.astype(BF16)
            dc_ref[0, 0, rows, :] = jnp.where(lane == 0, dcol[0], jnp.where(lane == 1, dcol[1], 0.0))
            return 0

        lax.fori_loop(0, nq, finish, 0)

    tok = pl.BlockSpec((1, 1, 2, s, LANES), lambda bi, hp: (bi, hp, 0, 0, 0))
    tok_t = pl.BlockSpec((1, 1, 2, LANES, s), lambda bi, hp: (bi, hp, 0, 0, 0))
    pair_blk = pl.BlockSpec((1, s, LANES), lambda bi, hp: (bi, 0, hp))
    pair_shape = jax.ShapeDtypeStruct((b, s, a), BF16)
    return _call(
        body, name=name, grid=(b, pairs),
        in_specs=[tok, tok_t, tok, tok, tok],
        out_specs=[pair_blk, pair_blk, pair_blk,
                   pl.BlockSpec((1, 1, s, LANES), lambda bi, hp: (bi, hp, 0, 0))],
        out_shape=[pair_shape, pair_shape, pair_shape,
                   jax.ShapeDtypeStruct((b, pairs, s, LANES), F32)],
        scratch_shapes=[pltpu.VMEM((2, LANES, s), F32), pltpu.VMEM((2, s, LANES), F32),
                        pltpu.VMEM((2, s, LANES), F32)],
        semantics=("parallel", "parallel"))(ka, kat, va, qab, doa)


def _pool_tile(s):
    return min(256, s)


def _band(tb, window, shift):
    tgt = lax.broadcasted_iota(jnp.int32, (tb, tb), 0)
    src = lax.broadcasted_iota(jnp.int32, (tb, tb), 1) + shift
    return ((src <= tgt) & (src > tgt - window)).astype(BF16)


def _band_t(tb, window, shift):
    src = lax.broadcasted_iota(jnp.int32, (tb, tb), 0)
    tgt = lax.broadcasted_iota(jnp.int32, (tb, tb), 1) + shift
    return ((src <= tgt) & (src > tgt - window)).astype(BF16)


def _pool_fwd(pp, w_pool, scale, name):
    b, s, pw2 = pp.shape
    pw = pw2 // 2
    pg = pw // N_POOL_GROUPS
    tb = _pool_tile(s)
    nb = s // tb

    def body(u_ref, z_ref, w_ref, s_ref, o_ref):
        window = 2 << pl.program_id(1)
        band0 = _band(tb, window, 0)
        band1 = _band(tb, window, -tb)
        pos = lax.broadcasted_iota(jnp.int32, (tb, pg), 0)

        def block(i, _):
            rows = pl.ds(pl.multiple_of(i * tb, tb), tb)
            prev = pl.ds(pl.multiple_of(jnp.maximum(i - 1, 0) * tb, tb), tb)
            ub = u_ref[0, rows, :]
            up = u_ref[0, prev, :]
            up = jnp.where(i > 0, up, jnp.zeros_like(up))
            count = jnp.minimum(pos + i * tb + 1, window).astype(F32)
            pooled = (_dot(band0, ub, NN) + _dot(band1, up, NN)) / count - ub.astype(F32)
            mixed = _dot(pooled.astype(BF16), w_ref[0], NN) * s_ref[...]
            zv = z_ref[0, rows, :].astype(F32)
            o_ref[0, rows, :] = (mixed * zv * _sigmoid(zv)).astype(BF16)
            return 0

        lax.fori_loop(0, nb, block, 0)

    return _call(
        body, name=name, grid=(b, N_POOL_GROUPS),
        in_specs=[pl.BlockSpec((1, s, pg), lambda bi, g: (bi, 0, g)),
                  pl.BlockSpec((1, s, pg), lambda bi, g: (bi, 0, N_POOL_GROUPS + g)),
                  pl.BlockSpec((1, pg, pg), lambda bi, g: (g, 0, 0)),
                  pl.BlockSpec((1, pg), lambda bi, g: (0, g))],
        out_specs=pl.BlockSpec((1, s, pg), lambda bi, g: (bi, 0, g)),
        out_shape=jax.ShapeDtypeStruct((b, s, pw), BF16),
        semantics=("parallel", "parallel"))(pp, pp, w_pool, scale)


def _pool_bwd(pp, dcat, w_pool, scale, first_block, name):
    b, s, pw2 = pp.shape
    pw = pw2 // 2
    pg = pw // N_POOL_GROUPS
    tb = _pool_tile(s)
    nb = s // tb

    def body(u_ref, z_ref, d_ref, w_ref, s_ref, du_ref, dz_ref, dw_ref, ds_ref, dpool_s):
        @pl.when(pl.program_id(1) == 0)
        def _():
            dw_ref[...] = jnp.zeros_like(dw_ref)
            ds_ref[...] = jnp.zeros_like(ds_ref)

        window = 2 << pl.program_id(0)
        band0 = _band(tb, window, 0)
        band1 = _band(tb, window, -tb)
        band0_t = _band_t(tb, window, 0)
        band1_t = _band_t(tb, window, tb)
        pos = lax.broadcasted_iota(jnp.int32, (tb, pg), 0)

        def first(i, _):
            rows = pl.ds(pl.multiple_of(i * tb, tb), tb)
            prev = pl.ds(pl.multiple_of(jnp.maximum(i - 1, 0) * tb, tb), tb)
            ub = u_ref[0, rows, :]
            up = u_ref[0, prev, :]
            up = jnp.where(i > 0, up, jnp.zeros_like(up))
            count = jnp.minimum(pos + i * tb + 1, window).astype(F32)
            pooled = ((_dot(band0, ub, NN) + _dot(band1, up, NN)) / count
                      - ub.astype(F32)).astype(BF16)
            mixed = _dot(pooled, w_ref[0], NN)
            pm = mixed * s_ref[...]
            zv = z_ref[0, rows, :].astype(F32)
            sg = _sigmoid(zv)
            dpl = d_ref[0, rows, :].astype(F32)
            dpm = dpl * zv * sg
            dz_ref[0, rows, :] = (dpl * pm * sg * (1.0 + zv * (1.0 - sg))).astype(BF16)
            ds_ref[...] += jnp.sum(dpm * mixed, axis=0, keepdims=True)
            dmixed = (dpm * s_ref[...]).astype(BF16)
            dw_ref[0] += _dot(pooled, dmixed, TN)
            dpool_s[rows, :] = _dot(dmixed, w_ref[0], NT)
            return 0

        lax.fori_loop(0, nb, first, 0)

        def second(i, _):
            rows = pl.ds(pl.multiple_of(i * tb, tb), tb)
            nxt_i = jnp.minimum(i + 1, nb - 1)
            nxt = pl.ds(pl.multiple_of(nxt_i * tb, tb), tb)
            count = jnp.minimum(pos + i * tb + 1, window).astype(F32)
            count_n = jnp.minimum(pos + nxt_i * tb + 1, window).astype(F32)
            dpb = dpool_s[rows, :]
            cur = (dpb / count).astype(BF16)
            nx = dpool_s[nxt, :] / count_n
            nx = jnp.where(i < nb - 1, nx, 0.0).astype(BF16)
            du = _dot(band0_t, cur, NN) + _dot(band1_t, nx, NN) - dpb
            du_ref[0, rows, :] = du.astype(BF16)
            return 0

        lax.fori_loop(0, nb, second, 0)

    return _call(
        body, name=name, grid=(N_POOL_GROUPS, b),
        in_specs=[pl.BlockSpec((1, s, pg), lambda g, bi: (bi, 0, g)),
                  pl.BlockSpec((1, s, pg), lambda g, bi: (bi, 0, N_POOL_GROUPS + g)),
                  pl.BlockSpec((1, s, pg), lambda g, bi: (bi, 0, first_block + g)),
                  pl.BlockSpec((1, pg, pg), lambda g, bi: (g, 0, 0)),
                  pl.BlockSpec((1, pg), lambda g, bi: (0, g))],
        out_specs=[pl.BlockSpec((1, s, pg), lambda g, bi: (bi, 0, g)),
                   pl.BlockSpec((1, s, pg), lambda g, bi: (bi, 0, g)),
                   pl.BlockSpec((1, pg, pg), lambda g, bi: (g, 0, 0)),
                   pl.BlockSpec((1, pg), lambda g, bi: (0, g))],
        out_shape=[jax.ShapeDtypeStruct((b, s, pw), BF16), jax.ShapeDtypeStruct((b, s, pw), BF16),
                   jax.ShapeDtypeStruct((N_POOL_GROUPS, pg, pg), F32),
                   jax.ShapeDtypeStruct((1, pw), F32)],
        scratch_shapes=[pltpu.VMEM((s, pg), F32)],
        semantics=("parallel", "arbitrary"))(pp, pp, dcat, w_pool, scale)


def _adamw(recvs, owns, w, m, v, name):
    depth = len(recvs)
    r, c = owns[0].shape
    tr = min(128, r)
    nb = r // tr
    c1 = 1.0 - ADAM_B1 ** ADAM_STEP
    c2 = 1.0 - ADAM_B2 ** ADAM_STEP

    def body(*refs):
        recv_refs, own_refs = refs[:depth], refs[depth:2 * depth]
        w_ref, m_ref, v_ref, g_ref, d_ref, nm_ref, nv_ref = refs[2 * depth:]
        x, y, core = (lax.axis_index(ax) for ax in MESH_AXES)
        me = 4 * x + 2 * y + core
        for layer in range(depth):
            @pl.when(pl.program_id(0) == layer)
            def _(layer=layer):
                own = own_refs[layer][...].astype(F32)
                g = jnp.where(me == 0, own, recv_refs[layer][0].astype(F32))
                for sl in range(1, N_DEV):
                    g = g + jnp.where(me == sl, own, recv_refs[layer][sl].astype(F32))
                mn = ADAM_B1 * m_ref[0] + (1.0 - ADAM_B1) * g
                vn = ADAM_B2 * v_ref[0] + (1.0 - ADAM_B2) * (g * g)
                m_hat = mn / c1
                v_hat = vn / c2
                g_ref[0] = g
                d_ref[0] = -ADAM_LR * (m_hat / (jnp.sqrt(v_hat) + ADAM_EPS) + ADAM_WD * w_ref[0])
                nm_ref[0] = mn
                nv_ref[0] = vn

    def blk(layer):
        return lambda l, i: jnp.clip(i + (l - layer) * nb, 0, nb - 1)

    in_specs = [pl.BlockSpec((N_DEV, tr, c), lambda l, i, f=blk(layer): (0, f(l, i), 0))
                for layer in range(depth)]
    in_specs += [pl.BlockSpec((tr, c), lambda l, i, f=blk(layer): (f(l, i), 0))
                 for layer in range(depth)]
    row = pl.BlockSpec((1, tr, c), lambda l, i: (l, i, 0))
    return _call(body, name=name, grid=(depth, nb), in_specs=in_specs + [row, row, row],
                 out_specs=[row] * 4, out_shape=[jax.ShapeDtypeStruct((depth, r, c), F32)] * 4,
                 semantics=("arbitrary", "arbitrary"))(*recvs, *owns, w, m, v)


def _pack_w_in(gathered, a, heads, pw):
    d = gathered.shape[1]
    w_full = jnp.transpose(gathered, (1, 0, 2)).reshape(d, -1)
    wf = jnp.pad(w_full[:, 4 * a:4 * a + heads], ((0, 0), (0, LANES - heads)))
    return w_full, w_full[:, 4 * a + heads:], wf


def _unpack_dw_in(parts, heads):
    dq, dk, dv, dz, dwf, du, dzp = parts
    d = dq.shape[0]
    full = jnp.concatenate([dq, dk, dv, dz, dwf[:, :heads], du, dzp], axis=1)
    return jnp.transpose(full.reshape(d, N_DEV, -1), (1, 0, 2))


def kernel(x, p, norm_pre, norm_post, w_in, b_f, w_pool, pool_scale, w_out, w_pg, w_pe, loss_target, m_norm_pre, m_norm_post, m_w_in, m_b_f, m_w_pool, m_pool_scale, m_w_out, m_w_pg, m_w_pe, v_norm_pre, v_norm_post, v_w_in, v_b_f, v_w_pool, v_pool_scale, v_w_out, v_w_pg, v_w_pe):
    depth = w_in.shape[0]
    b, s, d = x.shape
    t = b * s
    heads = b_f.shape[1]
    a = heads * HEAD_DIM
    pairs = a // LANES
    pw = pool_scale.shape[1]
    pg = pw // N_POOL_GROUPS
    ple = p.shape[-1]
    mix_w = a + pw

    me = 4 * lax.axis_index("x") + 2 * lax.axis_index("y") + lax.axis_index("c")
    shard = {
        "w_in": [w_in[i].astype(BF16) for i in range(depth)],
        "w_pool": [w_pool[i].reshape(N_POOL_GROUPS * (pg // N_DEV), pg).astype(BF16)
                   for i in range(depth)],
        "w_out": [w_out[i].astype(BF16) for i in range(depth)],
        "w_pg": [w_pg[i].astype(BF16) for i in range(depth)],
        "w_pe": [w_pe[i].astype(BF16) for i in range(depth)],
    }
    names = list(shard)
    rest = names[1:]

    def unpack_rest(lands, layer, which):
        g = {nm: _with_own(ld, shard[nm][layer], me) for nm, ld in zip(which, lands)}
        g_pool = g["w_pool"].reshape(N_DEV, N_POOL_GROUPS, pg // N_DEV, pg)
        return dict(wpool=jnp.transpose(g_pool, (1, 0, 2, 3)).reshape(N_POOL_GROUPS, pg, pg),
                    wout=g["w_out"].reshape(mix_w, d), wpg=g["w_pg"].reshape(d, d),
                    wpe=jnp.transpose(g["w_pe"], (1, 0, 2)).reshape(ple, d))

    (g_in0,) = _exchange([(shard["w_in"][0], False)], "gather_w_in0")
    rest0, tok_rest0 = _exchange_start([(shard[nm][0], False) for nm in rest], g_in0,
                                       "gather_rest0_start")
    later, tok = [], tok_rest0
    for i in range(1, depth):
        hdl, tk_i = _exchange_start([(shard[nm][i], False) for nm in names], g_in0,
                                    "gather_layer%d_start" % i)
        later.append(hdl)
        tok = tok + tk_i

    h = x.reshape(t, d)
    saved = []
    layers = []
    for i in range(depth):
        sv = dict(h=h)
        g_pre = norm_pre[i:i + 1]
        g_post = norm_post[i:i + 1]
        bf = jnp.pad(b_f[i:i + 1], ((0, 0), (0, LANES - heads)))
        scale = pool_scale[i:i + 1]
        if i == 0:
            lw = dict(zip(("wa", "wp", "wf"), _pack_w_in(g_in0, a, heads, pw)))
            g_pre = g_pre + tok
        else:
            lands = _exchange_wait(later[i - 1], h, "gather_layer%d_wait" % i)
            g_in = _with_own(lands[0], shard["w_in"][i], me)
            lw = dict(zip(("wa", "wp", "wf"), _pack_w_in(g_in, a, heads, pw)))
            lw.update(unpack_rest(lands[1:], i, rest))
        hn = _rms_fwd(h, g_pre, "rms_pre")
        pa = _matmul([(hn, lw["wa"])], "nn", BF16, "proj_attn", n_dim=4 * a, tn=2048,
                     n_outer=True).reshape(b, s, 4 * a)
        pp = _matmul([(hn, lw["wp"])], "nn", BF16, "proj_pool", tn=2048,
                     n_outer=True).reshape(b, s, 2 * pw)
        fl = _matmul([(hn, lw["wf"])], "nn", F32, "proj_gate").reshape(b, s, LANES)
        c = _gates_fwd(fl, bf, "gates_fwd")
        qa, ka, kat, va, vt = _attn_prep_fwd(pa, c, "attn_prep_fwd")
        o, ga, lse = _attn_fwd(qa, ka, vt, pa, "attn_fwd")
        if i == 0:
            lw.update(unpack_rest(_exchange_wait(rest0, lse, "gather_rest0_wait"), 0, rest))
        layers.append(lw)
        gp = _pool_fwd(pp, lw["wpool"], scale, "pool_fwd")
        ga2 = ga.reshape(t, a)
        gp2 = gp.reshape(t, pw)
        mix = _matmul([(ga2, lw["wout"], 0, 0), (gp2, lw["wout"], a, 0)], "nn", F32, "mix_out")
        h1, h1b = _post_fwd(h, mix, g_post, "post_fwd")
        pb = p[i].reshape(t, ple).astype(BF16)
        gpre = _matmul([(h1b, lw["wpg"])], "nn", F32, "ple_gate")
        e = _matmul([(pb, lw["wpe"])], "nn", F32, "ple_embed")
        h = _ple_fwd(h1, gpre, e, "ple_fwd")
        sv.update(hn=hn, pa=pa, pp=pp, fl=fl, bf=bf, qa=qa, ka=ka, kat=kat, va=va, o=o, lse=lse, ga=ga2,
                  gp=gp2, mix=mix,
                  h1b=h1b, pb=pb, gpre=gpre, e=e, g_pre=g_pre, g_post=g_post, scale=scale)
        saved.append(sv)

    dh, sq = _loss_bwd(h, loss_target.reshape(t, d), "loss")
    loss = lax.psum(0.5 * jnp.sum(sq) / d, MESH_AXES)

    big = {nm: [None] * depth for nm in names}
    small = {nm: [None] * depth for nm in ("norm_pre", "norm_post", "b_f", "pool_scale")}
    grad_handles = [None] * depth
    rest_handles = [None] * depth
    for i in reversed(range(depth)):
        lw, sv = layers[i], saved[i]
        de, dpre = _ple_bwd(dh, sv["gpre"], sv["e"], "ple_bwd")
        dwpe = _matmul([(sv["pb"], de)], "tn", BF16, "dw_pe", tm=1024)
        dwpg = _matmul([(sv["h1b"], dpre)], "tn", BF16, "dw_pg", tm=1024)
        t1 = _matmul([(dpre, lw["wpg"])], "nt", F32, "d_h1")
        dh1, dmix, dg_post = _post_bwd(dh, t1, sv["mix"], sv["g_post"], "post_bwd")
        dwout = jnp.concatenate(
            [_matmul([(sv["ga"], dmix)], "tn", BF16, "dw_out_attn", tm=1024),
             _matmul([(sv["gp"], dmix)], "tn", BF16, "dw_out_pool", tm=1024)], axis=0)
        dcat = _matmul([(dmix, lw["wout"])], "nt", BF16, "d_cat", tn=2048).reshape(b, s, mix_w)
        du, dzp, dwpool, dscale = _pool_bwd(sv["pp"], dcat, lw["wpool"], sv["scale"], a // pg,
                                            "pool_bwd")
        big["w_pool"][i] = jnp.transpose(
            dwpool.astype(BF16).reshape(N_POOL_GROUPS, N_DEV, pg // N_DEV, pg), (1, 0, 2, 3)
        ).reshape(N_DEV, N_POOL_GROUPS * (pg // N_DEV), pg)
        big["w_out"][i] = dwout.reshape(N_DEV, mix_w // N_DEV, d)
        big["w_pg"][i] = dwpg.reshape(N_DEV, d // N_DEV, d)
        big["w_pe"][i] = jnp.transpose(dwpe.reshape(ple, N_DEV, d // N_DEV), (1, 0, 2))
        rest_handles[i], tok = _exchange_start([(big[nm][i], True) for nm in rest], du,
                                               "grads_rest%d_start" % i)
        qab, doa, dz = _attn_prep_bwd(dcat, sv["pa"], sv["o"], sv["lse"] + tok, sv["qa"],
                                      "attn_prep_bwd")
        dq, dk, dv, dcp = _attn_bwd(sv["ka"], sv["kat"], sv["va"], qab, doa, "attn_bwd")
        dc = jnp.transpose(dcp[..., :2], (0, 2, 1, 3)).reshape(b, s, heads)
        dc = jnp.pad(dc, ((0, 0), (0, 0), (0, LANES - heads)))
        dfl, dbf = _gates_bwd(dc, sv["fl"], sv["bf"], heads, "gates_bwd")
        dproj = [g_.reshape(t, -1) for g_ in (dq, dk, dv, dz, dfl, du, dzp)]
        dw_parts = [_matmul([(sv["hn"], g_)], "tn", BF16, "dw_in_%d" % n_, tm=1024)
                    for n_, g_ in enumerate(dproj)]

        big["w_in"][i] = _unpack_dw_in(dw_parts, heads)
        grad_handles[i], tok = _exchange_start([(big["w_in"][i], True)], dw_parts[-1],
                                               "grads_w_in%d_start" % i)

        dq2, dk2, dv2, dz2, dfl2, du2, dzp2 = dproj
        dhn = _matmul([(dq2, lw["wa"], 0, 0), (dk2, lw["wa"], a, 0), (dv2, lw["wa"], 2 * a, 0),
                       (dz2, lw["wa"], 3 * a, 0), (du2, lw["wp"], 0, 0), (dzp2, lw["wp"], pw, 0),
                       (dfl2, lw["wf"] + tok.astype(BF16), 0, 0)], "nt", F32, "d_hn")
        dh, dg_pre = _pre_bwd(sv["h"], dhn, dh1, sv["g_pre"] + tok, "pre_bwd")
        small["norm_pre"][i] = dg_pre
        small["norm_post"][i] = dg_post
        small["b_f"][i] = jnp.sum(dbf, axis=0)
        small["pool_scale"][i] = dscale
    grad_x = dh.reshape(b, s, d)

    width = max(d, pw)
    small_names = ("norm_pre", "norm_post", "pool_scale", "b_f")

    def small_rows(get):
        rows = []
        for nm in small_names:
            for i in range(depth):
                v_ = get(nm, i)
                rows.append(jnp.pad(v_, ((0, 0), (0, width - v_.shape[1]))))
        return jnp.concatenate(rows, axis=0)

    small_g = small_rows(lambda nm, i: small[nm][i])
    (small_recv,) = _exchange([(small_g, False)], "exchange_small")
    received = []
    for i in range(depth):
        got_rest = _exchange_wait(rest_handles[i], dh, "grads_rest%d_wait" % i)
        got_w_in = _exchange_wait(grad_handles[i], dh, "grads_w_in%d_wait" % i)
        received.append(got_w_in + got_rest)

    weights = dict(norm_pre=norm_pre, norm_post=norm_post, w_in=w_in, b_f=b_f, w_pool=w_pool,
                   pool_scale=pool_scale, w_out=w_out, w_pg=w_pg, w_pe=w_pe)
    mom1 = dict(norm_pre=m_norm_pre, norm_post=m_norm_post, w_in=m_w_in, b_f=m_b_f, w_pool=m_w_pool,
                pool_scale=m_pool_scale, w_out=m_w_out, w_pg=m_w_pg, w_pe=m_w_pe)
    mom2 = dict(norm_pre=v_norm_pre, norm_post=v_norm_post, w_in=v_w_in, b_f=v_b_f, w_pool=v_w_pool,
                pool_scale=v_pool_scale, w_out=v_w_out, w_pg=v_w_pg, w_pe=v_w_pe)

    results = {}
    for j, nm in enumerate(names):
        shp = weights[nm].shape
        recvs = [received[i][j] for i in range(depth)]
        owns = [lax.dynamic_index_in_dim(big[nm][i], me, 0, keepdims=False) for i in range(depth)]
        flat = lambda arr: arr.reshape((depth,) + owns[0].shape)
        outs = _adamw(recvs, owns, flat(weights[nm]), flat(mom1[nm]), flat(mom2[nm]), "adamw_" + nm)
        results[nm] = [o_.reshape(shp) for o_ in outs]

    small_w = small_rows(lambda nm, i: weights[nm][i:i + 1])[None]
    small_m = small_rows(lambda nm, i: mom1[nm][i:i + 1])[None]
    small_v = small_rows(lambda nm, i: mom2[nm][i:i + 1])[None]
    outs = _adamw([small_recv], [small_g], small_w, small_m, small_v, "adamw_small")
    for j, nm in enumerate(small_names):
        cols = weights[nm].shape[1]
        results[nm] = [o_[0, j * depth:(j + 1) * depth, :cols] for o_ in outs]

    order = ("norm_pre", "norm_post", "w_in", "b_f", "w_pool", "pool_scale", "w_out", "w_pg", "w_pe")
    return (loss, grad_x, *[results[nm][0] for nm in order], *[results[nm][1] for nm in order],
            *[results[nm][2] for nm in order], *[results[nm][3] for nm in order])
```

```python
import functools
import math

import jax
import jax.numpy as jnp
from jax import lax
from jax.experimental import pallas as pl
from jax.experimental.pallas import tpu as pltpu

N_DEV = 8
MESH_AXES = ("x", "y", "c")
HEAD_DIM = 64
LANES = 128
N_POOL_GROUPS = 4
EPS = 1e-6
ADAM_LR = 0.001
ADAM_B1 = 0.9
ADAM_B2 = 0.999
ADAM_EPS = 1e-08
ADAM_WD = 0.01
ADAM_STEP = 10
VMEM_LIMIT_BYTES = 56 * 1024 * 1024
F32 = jnp.float32
BF16 = jnp.bfloat16
NEG_INF = float("-inf")


def _call(body, *, name, grid, in_specs, out_specs, out_shape, scratch_shapes=(), semantics=None):
    return pl.pallas_call(
        body, name=name, grid=grid, in_specs=in_specs, out_specs=out_specs, out_shape=out_shape,
        scratch_shapes=list(scratch_shapes),
        compiler_params=pltpu.CompilerParams(dimension_semantics=semantics,
                                             vmem_limit_bytes=VMEM_LIMIT_BYTES))


def _sigmoid(z):
    return 1.0 / (1.0 + jnp.exp(-z))


def _dot(a, b, dims):
    return lax.dot_general(a, b, (dims, ((), ())), preferred_element_type=F32)


NN = ((1,), (0,))
NT = ((1,), (1,))
TN = ((0,), (0,))


def _exchange(items, name):
    n = len(items)
    modes = [s for _, s in items]
    out_shapes = []
    for a, s in items:
        shp = a.shape[1:] if s else a.shape
        out_shapes.append(jax.ShapeDtypeStruct((N_DEV,) + tuple(shp), a.dtype))

    def body(*refs):
        ins = refs[:n]
        outs = refs[n:2 * n]
        send_sems, recv_sems, local_sems = refs[2 * n:]
        x, y, c = (lax.axis_index(ax) for ax in MESH_AXES)
        me = 4 * x + 2 * y + c
        started = []
        for i in range(n):
            mine = ins[i].at[me] if modes[i] else ins[i]
            loc = pltpu.make_async_copy(mine, outs[i].at[me], local_sems.at[i])
            loc.start()
            started.append(loc)
        remote = []
        for k in range(1, N_DEV):
            px = x ^ ((k >> 2) & 1)
            py = y ^ ((k >> 1) & 1)
            pc = c ^ (k & 1)
            peer = me ^ k
            for i in range(n):
                src = ins[i].at[peer] if modes[i] else ins[i]
                cp = pltpu.make_async_remote_copy(
                    src_ref=src, dst_ref=outs[i].at[me],
                    send_sem=send_sems.at[i, k - 1], recv_sem=recv_sems.at[i, k - 1],
                    device_id=(px, py, pc), device_id_type=pl.DeviceIdType.MESH)
                cp.start()
                remote.append(cp)
        for cp in remote:
            cp.wait()
        for loc in started:
            loc.wait()

    hbm = pl.BlockSpec(memory_space=pltpu.HBM)
    return pl.pallas_call(
        body, name=name, out_shape=out_shapes,
        in_specs=[hbm] * n, out_specs=[hbm] * n,
        scratch_shapes=[pltpu.SemaphoreType.DMA((n, N_DEV - 1)),
                        pltpu.SemaphoreType.DMA((n, N_DEV - 1)),
                        pltpu.SemaphoreType.DMA((n,))],
    )(*[a for a, _ in items])


def _gather_two_level(shard, name):
    def body(x_ref, out_ref, send_sems, recv_sems, local_sem):
        x, y, c = (lax.axis_index(ax) for ax in MESH_AXES)
        sibling = (x, y, 1 - c)
        chips = [(1 - x, y), (x, 1 - y), (1 - x, 1 - y)]

        def slot(px, py, pc):
            return out_ref.at[4 * px + 2 * py + pc]

        def copy(k, block, to, src=None):
            return pltpu.make_async_remote_copy(
                src_ref=slot(*block) if src is None else src, dst_ref=slot(*block),
                send_sem=send_sems.at[k], recv_sem=recv_sems.at[k],
                device_id=to, device_id_type=pl.DeviceIdType.MESH)

        mine = pltpu.make_async_copy(x_ref, slot(x, y, c), local_sem)
        mine.start()
        first = [copy(0, (x, y, c), sibling, src=x_ref)]
        first += [copy(1 + j, (x, y, c), (*chip, c), src=x_ref) for j, chip in enumerate(chips)]
        for cp in first:
            cp.start()
        passed = [copy(4 + j, (*chip, c), sibling) for j, chip in enumerate(chips)]
        for j, chip in enumerate(chips):
            copy(1 + j, (*chip, c), (x, y, c)).wait_recv()
            passed[j].start()
        copy(0, (x, y, 1 - c), (x, y, c)).wait_recv()
        for j, chip in enumerate(chips):
            copy(4 + j, (*chip, 1 - c), (x, y, c)).wait_recv()
        for cp in first + passed:
            cp.wait_send()
        mine.wait()

    hbm = pl.BlockSpec(memory_space=pltpu.HBM)
    return pl.pallas_call(
        body, name=name, out_shape=jax.ShapeDtypeStruct((N_DEV,) + shard.shape, shard.dtype),
        in_specs=[hbm], out_specs=hbm,
        scratch_shapes=[pltpu.SemaphoreType.DMA((N_DEV - 1,)), pltpu.SemaphoreType.DMA((N_DEV - 1,)),
                        pltpu.SemaphoreType.DMA],
    )(shard)


def _peer_copies(srcs, lands, modes, send_sems, recv_sems):
    x, y, c = (lax.axis_index(ax) for ax in MESH_AXES)
    me = 4 * x + 2 * y + c
    copies = []
    for k in range(1, N_DEV):
        peer_id = (x ^ ((k >> 2) & 1), y ^ ((k >> 1) & 1), c ^ (k & 1))
        for i, scatter in enumerate(modes):
            src = srcs[i].at[me ^ k] if scatter else srcs[i]
            pair = i * (N_DEV - 1) + k - 1
            copies.append(pltpu.make_async_remote_copy(
                src_ref=src, dst_ref=lands[i].at[me],
                send_sem=send_sems.at[pair], recv_sem=recv_sems.at[pair],
                device_id=peer_id, device_id_type=pl.DeviceIdType.MESH))
    return copies


def _exchange_start(items, after, name):
    n = len(items)
    modes = [s for _, s in items]
    srcs = [pltpu.with_memory_space_constraint(a, pltpu.HBM) for a, _ in items]
    lands = []
    for a, s in items:
        shp = (N_DEV,) + tuple(a.shape[1:] if s else a.shape)
        lands.append(pltpu.with_memory_space_constraint(lax.empty(shp, a.dtype), pltpu.HBM))

    def body(*refs):
        send_sems, recv_sems = refs[2 * n + 1], refs[2 * n + 2]
        token = refs[-1]
        for cp in _peer_copies(refs[:n], refs[n:2 * n], modes, send_sems, recv_sems):
            cp.start()
        token[...] = jnp.zeros_like(token)

    hbm = pl.BlockSpec(memory_space=pltpu.HBM)
    sem = pl.BlockSpec(memory_space=pltpu.SEMAPHORE)
    outs = pl.pallas_call(
        body, name=name,
        out_shape=(pltpu.SemaphoreType.DMA((n * (N_DEV - 1),)),
                   pltpu.SemaphoreType.DMA((n * (N_DEV - 1),)),
                   *[pltpu.HBM(a.shape, a.dtype) for a in srcs + lands],
                   jax.ShapeDtypeStruct((8, LANES), F32)),
        in_specs=[hbm] * (2 * n) + [pl.BlockSpec(memory_space=pl.ANY)],
        out_specs=(sem, sem, *([hbm] * (2 * n)), pl.BlockSpec(memory_space=pltpu.VMEM)),
        input_output_aliases={i: 2 + i for i in range(2 * n)},
        compiler_params=pltpu.CompilerParams(
            has_side_effects=pltpu.SideEffectType.DATAFLOW_SIDE_EFFECTING),
    )(*srcs, *lands, after)
    handle = (modes, outs[0], outs[1], list(outs[2:2 + n]), list(outs[2 + n:2 + 2 * n]))
    return handle, outs[-1][0, 0]


def _exchange_wait(handle, after, name):
    modes, send_sems, recv_sems, srcs, lands = handle
    n = len(modes)

    def body(*refs):
        for cp in _peer_copies(refs[:n], refs[n:2 * n], modes, refs[2 * n], refs[2 * n + 1]):
            cp.wait_send()
            cp.wait_recv()

    hbm = pl.BlockSpec(memory_space=pltpu.HBM)
    sem = pl.BlockSpec(memory_space=pltpu.SEMAPHORE)
    outs = pl.pallas_call(
        body, name=name,
        out_shape=tuple(pltpu.HBM(a.shape, a.dtype) for a in srcs + lands),
        in_specs=[hbm] * (2 * n) + [sem, sem, pl.BlockSpec(memory_space=pl.ANY)],
        out_specs=tuple([hbm] * (2 * n)),
        input_output_aliases={i: i for i in range(2 * n)},
        compiler_params=pltpu.CompilerParams(
            has_side_effects=pltpu.SideEffectType.DATAFLOW_SIDE_EFFECTING),
    )(*srcs, *lands, send_sems, recv_sems, after)
    return list(outs[n:])


def _with_own(slots, own, me):
    idx = lax.broadcasted_iota(jnp.int32, (N_DEV,) + (1,) * own.ndim, 0)
    return jnp.where(idx == me, own[None], slots)


def _matmul(pairs, mode, out_dtype, name, n_dim=None, tm=512, tn=1024, tk=1024, n_outer=False):
    dims = {"nn": NN, "nt": NT, "tn": TN}[mode]
    pairs = [tuple(pr) + (0, 0) * (len(pr) == 2) for pr in pairs]
    a0, b0 = pairs[0][:2]
    m_dim = a0.shape[1] if mode == "tn" else a0.shape[0]
    if n_dim is None:
        n_dim = b0.shape[0] if mode == "nt" else b0.shape[1]
    tm = min(tm, m_dim)
    tn = min(tn, n_dim)
    segs = []
    off = 0
    for a, _, k0, n0 in pairs:
        k_dim = a.shape[0] if mode == "tn" else a.shape[1]
        t = min(tk, k_dim)
        segs.append((off, k_dim // t, t, k0 // t, n0 // tn))
        off += k_dim // t
    nk = off
    n_pairs = len(pairs)

    def ij(g0, g1):
        return (g1, g0) if n_outer else (g0, g1)

    in_specs = []
    for (o, cnt, t, kb, nb) in segs:
        def kc(kk, o=o, cnt=cnt):
            return jnp.clip(kk - o, 0, cnt - 1)
        if mode == "tn":
            in_specs.append(pl.BlockSpec((t, tm), lambda g0, g1, kk, kc=kc: (kc(kk), ij(g0, g1)[0])))
        else:
            in_specs.append(pl.BlockSpec((tm, t), lambda g0, g1, kk, kc=kc: (ij(g0, g1)[0], kc(kk))))
        if mode == "nt":
            in_specs.append(pl.BlockSpec((tn, t), lambda g0, g1, kk, kc=kc, kb=kb, nb=nb:
                                         (nb + ij(g0, g1)[1], kb + kc(kk))))
        else:
            in_specs.append(pl.BlockSpec((t, tn), lambda g0, g1, kk, kc=kc, kb=kb, nb=nb:
                                         (kb + kc(kk), nb + ij(g0, g1)[1])))

    one_shot = all(sg[1] == 1 for sg in segs)

    def body_sum(*refs):
        total = _dot(refs[0][...], refs[1][...], dims)
        for idx in range(1, n_pairs):
            total = total + _dot(refs[2 * idx][...], refs[2 * idx + 1][...], dims)
        refs[2 * n_pairs][...] = total.astype(out_dtype)

    def body(*refs):
        out_ref = refs[2 * n_pairs]
        acc = refs[2 * n_pairs + 1]
        kk = pl.program_id(2)

        @pl.when(kk == 0)
        def _():
            acc[...] = jnp.zeros_like(acc)

        for idx, (o, cnt) in enumerate(sg[:2] for sg in segs):
            @pl.when((kk >= o) & (kk < o + cnt))
            def _(idx=idx):
                acc[...] += _dot(refs[2 * idx][...], refs[2 * idx + 1][...], dims)

        @pl.when(kk == nk - 1)
        def _():
            out_ref[...] = acc[...].astype(out_dtype)

    flat = [t for pr in pairs for t in pr[:2]]
    tiles = (m_dim // tm, n_dim // tn)
    return _call(body_sum if one_shot else body, name=name,
                 grid=ij(*tiles) + (1 if one_shot else nk,), in_specs=in_specs,
                 out_specs=pl.BlockSpec((tm, tn), lambda g0, g1, kk: ij(g0, g1)),
                 out_shape=jax.ShapeDtypeStruct((m_dim, n_dim), out_dtype),
                 scratch_shapes=[] if one_shot else [pltpu.VMEM((tm, tn), F32)],
                 semantics=("parallel", "parallel", "arbitrary"))(*flat)


def _row_tile(t):
    return min(256, t)


def _rms_fwd(h, g, name):
    t, d = h.shape
    tt = _row_tile(t)

    def body(h_ref, g_ref, o_ref):
        hv = h_ref[...]
        r = lax.rsqrt(jnp.mean(hv * hv, axis=-1, keepdims=True) + EPS)
        o_ref[...] = (hv * r * g_ref[...]).astype(BF16)

    row = pl.BlockSpec((tt, d), lambda i: (i, 0))
    vec = pl.BlockSpec((1, d), lambda i: (0, 0))
    return _call(body, name=name, grid=(t // tt,), in_specs=[row, vec], out_specs=row,
                 out_shape=jax.ShapeDtypeStruct((t, d), BF16), semantics=("parallel",))(h, g)


def _post_fwd(h, mix, g, name):
    t, d = h.shape
    tt = _row_tile(t)

    def body(h_ref, m_ref, g_ref, o_ref, ob_ref):
        mv = m_ref[...]
        r = lax.rsqrt(jnp.mean(mv * mv, axis=-1, keepdims=True) + EPS)
        h1 = h_ref[...] + mv * r * g_ref[...]
        o_ref[...] = h1
        ob_ref[...] = h1.astype(BF16)

    row = pl.BlockSpec((tt, d), lambda i: (i, 0))
    vec = pl.BlockSpec((1, d), lambda i: (0, 0))
    return _call(body, name=name, grid=(t // tt,), in_specs=[row, row, vec], out_specs=[row, row],
                 out_shape=[jax.ShapeDtypeStruct((t, d), F32), jax.ShapeDtypeStruct((t, d), BF16)],
                 semantics=("parallel",))(h, mix, g)


def _ple_fwd(h1, gpre, e, name):
    t, d = h1.shape
    tt = _row_tile(t)

    def body(h_ref, g_ref, e_ref, o_ref):
        o_ref[...] = h_ref[...] + _sigmoid(g_ref[...]) * e_ref[...]

    row = pl.BlockSpec((tt, d), lambda i: (i, 0))
    return _call(body, name=name, grid=(t // tt,), in_specs=[row, row, row], out_specs=row,
                 out_shape=jax.ShapeDtypeStruct((t, d), F32), semantics=("parallel",))(h1, gpre, e)


def _loss_bwd(y, target, name):
    t, d = y.shape
    tt = _row_tile(t)

    def body(y_ref, t_ref, dy_ref, s_ref):
        @pl.when(pl.program_id(0) == 0)
        def _():
            s_ref[...] = jnp.zeros_like(s_ref)
        diff = y_ref[...] - t_ref[...]
        dy_ref[...] = diff * (1.0 / d)
        s_ref[...] += jnp.sum(diff * diff, axis=0, keepdims=True)

    row = pl.BlockSpec((tt, d), lambda i: (i, 0))
    vec = pl.BlockSpec((1, d), lambda i: (0, 0))
    return _call(body, name=name, grid=(t // tt,), in_specs=[row, row], out_specs=[row, vec],
                 out_shape=[jax.ShapeDtypeStruct((t, d), F32), jax.ShapeDtypeStruct((1, d), F32)],
                 semantics=("arbitrary",))(y, target)


def _ple_bwd(dh2, gpre, e, name):
    t, d = dh2.shape
    tt = _row_tile(t)

    def body(d_ref, g_ref, e_ref, de_ref, dp_ref):
        gate = _sigmoid(g_ref[...])
        dv = d_ref[...]
        de_ref[...] = (dv * gate).astype(BF16)
        dp_ref[...] = (dv * e_ref[...] * gate * (1.0 - gate)).astype(BF16)

    row = pl.BlockSpec((tt, d), lambda i: (i, 0))
    return _call(body, name=name, grid=(t // tt,), in_specs=[row, row, row], out_specs=[row, row],
                 out_shape=[jax.ShapeDtypeStruct((t, d), BF16)] * 2,
                 semantics=("parallel",))(dh2, gpre, e)


def _post_bwd(dh2, t1, mix, g, name):
    t, d = dh2.shape
    tt = _row_tile(t)

    def body(d_ref, t_ref, m_ref, g_ref, dh_ref, dm_ref, dg_ref):
        @pl.when(pl.program_id(0) == 0)
        def _():
            dg_ref[...] = jnp.zeros_like(dg_ref)
        dh1 = d_ref[...] + t_ref[...]
        mv = m_ref[...]
        r = lax.rsqrt(jnp.mean(mv * mv, axis=-1, keepdims=True) + EPS)
        dh_ref[...] = dh1
        dg_ref[...] += jnp.sum(dh1 * mv * r, axis=0, keepdims=True)
        w = dh1 * g_ref[...]
        dot = jnp.mean(w * mv, axis=-1, keepdims=True)
        dm_ref[...] = (r * w - mv * (r * r * r) * dot).astype(BF16)

    row = pl.BlockSpec((tt, d), lambda i: (i, 0))
    vec = pl.BlockSpec((1, d), lambda i: (0, 0))
    return _call(body, name=name, grid=(t // tt,), in_specs=[row, row, row, vec],
                 out_specs=[row, row, vec],
                 out_shape=[jax.ShapeDtypeStruct((t, d), F32), jax.ShapeDtypeStruct((t, d), BF16),
                            jax.ShapeDtypeStruct((1, d), F32)],
                 semantics=("arbitrary",))(dh2, t1, mix, g)


def _pre_bwd(h, dhn, dh1, g, name):
    t, d = h.shape
    tt = _row_tile(t)

    def body(h_ref, dn_ref, d1_ref, g_ref, dh_ref, dg_ref):
        @pl.when(pl.program_id(0) == 0)
        def _():
            dg_ref[...] = jnp.zeros_like(dg_ref)
        hv = h_ref[...]
        dn = dn_ref[...]
        r = lax.rsqrt(jnp.mean(hv * hv, axis=-1, keepdims=True) + EPS)
        dg_ref[...] += jnp.sum(dn * hv * r, axis=0, keepdims=True)
        w = dn * g_ref[...]
        dot = jnp.mean(w * hv, axis=-1, keepdims=True)
        dh_ref[...] = d1_ref[...] + r * w - hv * (r * r * r) * dot

    row = pl.BlockSpec((tt, d), lambda i: (i, 0))
    vec = pl.BlockSpec((1, d), lambda i: (0, 0))
    return _call(body, name=name, grid=(t // tt,), in_specs=[row, row, row, vec],
                 out_specs=[row, vec],
                 out_shape=[jax.ShapeDtypeStruct((t, d), F32), jax.ShapeDtypeStruct((1, d), F32)],
                 semantics=("arbitrary",))(h, dhn, dh1, g)


def _split3(v):
    hi = v.astype(BF16)
    r1 = v - hi.astype(F32)
    mid = r1.astype(BF16)
    lo = (r1 - mid.astype(F32)).astype(BF16)
    return hi, mid, lo


def _gates_fwd(fl, bf, name):
    b, s, _ = fl.shape

    def body(f_ref, b_ref, c_ref):
        xv = f_ref[0] + b_ref[...]
        lf = jnp.minimum(xv, 0.0) - jnp.log(1.0 + jnp.exp(-jnp.abs(xv)))
        dst = lax.broadcasted_iota(jnp.int32, (s, s), 0)
        src = lax.broadcasted_iota(jnp.int32, (s, s), 1)
        lower = (src <= dst).astype(BF16)
        acc = jnp.zeros((s, LANES), F32)
        for part in _split3(lf):
            acc = acc + _dot(lower, part, NN)
        c_ref[0] = acc

    blk = pl.BlockSpec((1, s, LANES), lambda i: (i, 0, 0))
    return _call(body, name=name, grid=(b,),
                 in_specs=[blk, pl.BlockSpec((1, LANES), lambda i: (0, 0))],
                 out_specs=blk, out_shape=jax.ShapeDtypeStruct((b, s, LANES), F32),
                 semantics=("parallel",))(fl, bf)


def _gates_bwd(dc, fl, bf, heads, name):
    b, s, _ = fl.shape

    def body(d_ref, f_ref, b_ref, o_ref, db_ref):
        xv = f_ref[0] + b_ref[...]
        dst = lax.broadcasted_iota(jnp.int32, (s, s), 0)
        src = lax.broadcasted_iota(jnp.int32, (s, s), 1)
        later = (src >= dst).astype(BF16)
        dlf = jnp.zeros((s, LANES), F32)
        for part in _split3(d_ref[0]):
            dlf = dlf + _dot(later, part, NN)
        lane = lax.broadcasted_iota(jnp.int32, (s, LANES), 1)
        dfl = jnp.where(lane < heads, dlf * _sigmoid(-xv), 0.0)
        o_ref[0] = dfl.astype(BF16)
        db_ref[0] = jnp.sum(dfl, axis=0, keepdims=True)

    blk = pl.BlockSpec((1, s, LANES), lambda i: (i, 0, 0))
    return _call(body, name=name, grid=(b,),
                 in_specs=[blk, blk, pl.BlockSpec((1, LANES), lambda i: (0, 0))],
                 out_specs=[blk, pl.BlockSpec((1, 1, LANES), lambda i: (i, 0, 0))],
                 out_shape=[jax.ShapeDtypeStruct((b, s, LANES), BF16),
                            jax.ShapeDtypeStruct((b, 1, LANES), F32)],
                 semantics=("parallel",))(dc, fl, bf)


LANE_CQ = 64
LANE_CK = 67
LANE_LSE = 70
LANE_D = 64
N_PARTS = 3


def _attn_tiles(s):
    return min(512, s), min(256, s)


def _lanes_in(lane, first):
    return (lane >= first) & (lane < first + N_PARTS)


def _attn_prep_fwd(pa, c, name):
    b, s, a4 = pa.shape
    pairs = a4 // (4 * LANES)
    scale = 1.0 / math.sqrt(HEAD_DIM)
    wide = (1 + N_PARTS) * LANES

    def body(q_ref, k_ref, v_ref, c_ref, qa_ref, ka_ref, kat_ref, va_ref, vt_ref):
        hp = pl.program_id(1)
        c3 = jnp.concatenate(_split3(c_ref[0]), axis=1)
        qx = jnp.concatenate([q_ref[0], c3], axis=1)
        kx = jnp.concatenate([k_ref[0], c3], axis=1)
        vv = v_ref[0]
        lane = lax.broadcasted_iota(jnp.int32, (s, LANES), 1)
        row = lax.broadcasted_iota(jnp.int32, (wide, LANES), 0)
        col = lax.broadcasted_iota(jnp.int32, (wide, LANES), 1)
        r128 = lax.broadcasted_iota(jnp.int32, (LANES, LANES), 0)
        c128 = lax.broadcasted_iota(jnp.int32, (LANES, LANES), 1)
        ident = (r128 == c128).astype(BF16)
        for j in range(2):
            head = 2 * hp + j
            move = (row == col + HEAD_DIM * j) & (col < HEAD_DIM)
            move128 = (r128 == c128 + HEAD_DIM * j) & (c128 < HEAD_DIM)

            def pick(first, head=head):
                hit = (row == LANES + head) & (col == first)
                for i in range(1, N_PARTS):
                    hit = hit | ((row == LANES * (i + 1) + head) & (col == first + i))
                return hit

            mq = (jnp.where(move, scale, 0.0) + jnp.where(pick(LANE_CQ), 1.0, 0.0)).astype(BF16)
            mk = (jnp.where(move, 1.0, 0.0) - jnp.where(pick(LANE_CK), 1.0, 0.0)).astype(BF16)
            qa = _dot(qx, mq, NN) + jnp.where(_lanes_in(lane, LANE_CK), 1.0, 0.0)
            ka = _dot(kx, mk, NN) + jnp.where(
                _lanes_in(lane, LANE_CQ) | _lanes_in(lane, LANE_LSE), 1.0, 0.0)
            va = _dot(vv, move128.astype(BF16), NN) + jnp.where(_lanes_in(lane, LANE_D), 1.0, 0.0)
            kab = ka.astype(BF16)
            qa_ref[0, 0, j] = qa.astype(BF16)
            ka_ref[0, 0, j] = kab
            kat_ref[0, 0, j] = _dot(ident, kab, NT).astype(BF16)
            va_ref[0, 0, j] = va.astype(BF16)
        vt_ref[0, 0] = _dot(ident, vv, NT).astype(BF16)

    col_blk = lambda cidx: pl.BlockSpec((1, s, LANES), lambda bi, hp: (bi, 0, cidx * pairs + hp))
    tok = pl.BlockSpec((1, 1, 2, s, LANES), lambda bi, hp: (bi, hp, 0, 0, 0))
    tok_t = pl.BlockSpec((1, 1, 2, LANES, s), lambda bi, hp: (bi, hp, 0, 0, 0))
    tok_shape = jax.ShapeDtypeStruct((b, pairs, 2, s, LANES), BF16)
    return _call(
        body, name=name, grid=(b, pairs),
        in_specs=[col_blk(0), col_blk(1), col_blk(2),
                  pl.BlockSpec((1, s, LANES), lambda bi, hp: (bi, 0, 0))],
        out_specs=[tok, tok, tok_t, tok,
                   pl.BlockSpec((1, 1, LANES, s), lambda bi, hp: (bi, hp, 0, 0))],
        out_shape=[tok_shape, tok_shape, jax.ShapeDtypeStruct((b, pairs, 2, LANES, s), BF16),
                   tok_shape, jax.ShapeDtypeStruct((b, pairs, LANES, s), BF16)],
        semantics=("parallel", "parallel"))(pa, pa, pa, c)


def _attn_fwd(qa, ka, vt, pa, name):
    b, pairs, _, s, _ = qa.shape
    a = pairs * LANES
    tq = _attn_tiles(s)[0]
    nq = s // tq

    def body(q_ref, k_ref, vt_ref, z_ref, o_ref, g_ref, lse_ref):
        key_i = lax.broadcasted_iota(jnp.int32, (tq, tq), 0)
        qry_i = lax.broadcasted_iota(jnp.int32, (tq, tq), 1)

        def query_block(c):
            past = tq * c
            heads_out = []
            for j in range(2):
                qv = q_ref[0, 0, j]
                vrows = slice(HEAD_DIM * j, HEAD_DIM * (j + 1))
                sd = _dot(k_ref[0, 0, j, past:past + tq, :], qv, NT)
                sd = jnp.where(key_i <= qry_i, sd, NEG_INF)
                m = jnp.max(sd, axis=0, keepdims=True)
                if c > 0:
                    sp = _dot(k_ref[0, 0, j, 0:past, :], qv, NT)
                    m = jnp.maximum(m, jnp.max(sp, axis=0, keepdims=True))
                pd = jnp.exp(sd - m)
                l = jnp.sum(pd, axis=0, keepdims=True)
                acc = _dot(vt_ref[0, 0, vrows, past:past + tq], pd.astype(BF16), NN)
                if c > 0:
                    pp = jnp.exp(sp - m)
                    l = l + jnp.sum(pp, axis=0, keepdims=True)
                    acc = acc + _dot(vt_ref[0, 0, vrows, 0:past], pp.astype(BF16), NN)
                heads_out.append(acc / l)
                lse_ref[0, 0, j:j + 1, :] = m + jnp.log(l)
            ov = jnp.transpose(jnp.concatenate(heads_out, axis=0))
            o_ref[0] = ov.astype(BF16)
            zv = z_ref[0].astype(F32)
            g_ref[0] = (ov * zv * _sigmoid(zv)).astype(BF16)

        for c in range(nq):
            pl.when(pl.program_id(2) == c)(functools.partial(query_block, c))

    return _call(
        body, name=name, grid=(b, pairs, s // tq),
        in_specs=[pl.BlockSpec((1, 1, 2, tq, LANES), lambda bi, hp, qi: (bi, hp, 0, qi, 0)),
                  pl.BlockSpec((1, 1, 2, s, LANES), lambda bi, hp, qi: (bi, hp, 0, 0, 0)),
                  pl.BlockSpec((1, 1, LANES, s), lambda bi, hp, qi: (bi, hp, 0, 0)),
                  pl.BlockSpec((1, tq, LANES), lambda bi, hp, qi: (bi, qi, 3 * pairs + hp))],
        out_specs=[pl.BlockSpec((1, tq, LANES), lambda bi, hp, qi: (bi, qi, hp)),
                   pl.BlockSpec((1, tq, LANES), lambda bi, hp, qi: (bi, qi, hp)),
                   pl.BlockSpec((1, 1, 2, tq), lambda bi, hp, qi: (bi, hp, 0, qi))],
        out_shape=[jax.ShapeDtypeStruct((b, s, a), BF16), jax.ShapeDtypeStruct((b, s, a), BF16),
                   jax.ShapeDtypeStruct((b, pairs, 2, s), F32)],
        semantics=("parallel", "parallel", "arbitrary"))(qa, ka, vt, pa)


def _attn_prep_bwd(dcat, pa, o, lse, qa, name):
    b, pairs, _, s, _ = qa.shape
    a = pairs * LANES
    sub = 16

    def body(da_ref, z_ref, o_ref, lse_ref, qa_ref, qab_ref, doa_ref, dz_ref):
        zv = z_ref[0].astype(F32)
        dav = da_ref[0].astype(F32)
        ov = o_ref[0].astype(F32)
        sg = _sigmoid(zv)
        dov = dav * zv * sg
        dz_ref[0] = (dav * ov * sg * (1.0 + zv * (1.0 - sg))).astype(BF16)
        prod = dov * ov
        dob = dov.astype(BF16)
        lane = lax.broadcasted_iota(jnp.int32, (s, LANES), 1)
        r128 = lax.broadcasted_iota(jnp.int32, (LANES, LANES), 0)
        c128 = lax.broadcasted_iota(jnp.int32, (LANES, LANES), 1)
        prow = lax.broadcasted_iota(jnp.int32, (sub, s), 0)
        srow = lax.broadcasted_iota(jnp.int32, (sub, LANES), 0)
        scol = lax.broadcasted_iota(jnp.int32, (sub, LANES), 1)
        place = ((scol == srow + LANE_LSE) & (srow < N_PARTS)).astype(BF16)
        for j in range(2):
            in_head = (lane >= HEAD_DIM * j) & (lane < HEAD_DIM * (j + 1))
            dparts = _split3(jnp.sum(jnp.where(in_head, prod, 0.0), axis=1, keepdims=True))
            move128 = ((r128 == c128 + HEAD_DIM * j) & (c128 < HEAD_DIM)).astype(BF16)
            doa = _dot(dob, move128, NN)
            for i in range(N_PARTS):
                doa = jnp.where(lane == LANE_D + i, -dparts[i].astype(F32), doa)
            doa_ref[0, 0, j] = doa.astype(BF16)
            lparts = _split3(lse_ref[0, 0, j:j + 1, :])
            pmat = jnp.zeros((sub, s), BF16)
            for i in range(N_PARTS):
                pmat = jnp.where(prow == i, lparts[i], pmat)
            lcol = _dot(pmat, place, TN)
            qab_ref[0, 0, j] = (qa_ref[0, 0, j].astype(F32) - lcol).astype(BF16)

    tok = pl.BlockSpec((1, 1, 2, s, LANES), lambda bi, hp: (bi, hp, 0, 0, 0))
    tok_shape = jax.ShapeDtypeStruct((b, pairs, 2, s, LANES), BF16)
    pair_blk = pl.BlockSpec((1, s, LANES), lambda bi, hp: (bi, 0, hp))
    return _call(
        body, name=name, grid=(b, pairs),
        in_specs=[pair_blk,
                  pl.BlockSpec((1, s, LANES), lambda bi, hp: (bi, 0, 3 * pairs + hp)),
                  pair_blk,
                  pl.BlockSpec((1, 1, 2, s), lambda bi, hp: (bi, hp, 0, 0)),
                  tok],
        out_specs=[tok, tok, pair_blk],
        out_shape=[tok_shape, tok_shape, jax.ShapeDtypeStruct((b, s, a), BF16)],
        semantics=("parallel", "parallel"))(dcat, pa, o, lse, qa)


def _attn_bwd(ka, kat, va, qab, doa, name):
    b, pairs, _, s, _ = ka.shape
    a = pairs * LANES
    tq, tk = _attn_tiles(s)
    ratio = tq // tk
    nq, nk = s // tq, s // tk
    scale = 1.0 / math.sqrt(HEAD_DIM)

    def body(k_ref, kt_ref, v_ref, q_ref, do_ref, dq_ref, dk_ref, dv_ref, dc_ref,
             dqt_acc, dk_s, dv_s):
        key_i = lax.broadcasted_iota(jnp.int32, (tk, tq), 0)
        qry_i = lax.broadcasted_iota(jnp.int32, (tk, tq), 1)
        lane = lax.broadcasted_iota(jnp.int32, (tq, LANES), 1)
        low = lane < HEAD_DIM

        def key_block(kj):
            krows = slice(kj * tk, (kj + 1) * tk)
            q0 = (kj // ratio) * tq
            spans = [(slice(q0, q0 + tq), kj * tk - q0)]
            if q0 + tq < s:
                spans.append((slice(q0 + tq, s), None))
            for j in range(2):
                kb = k_ref[0, 0, j, krows, :]
                vb = v_ref[0, 0, j, krows, :]
                ktb = kt_ref[0, 0, j, :, krows]
                dk = dv = None
                for qrows, diag in spans:
                    qb = q_ref[0, 0, j, qrows, :]
                    dob = do_ref[0, 0, j, qrows, :]
                    pt = jnp.exp(_dot(kb, qb, NT))
                    if diag is not None:
                        pt = jnp.where(key_i + diag <= qry_i, pt, 0.0)
                    dsb = (pt * _dot(vb, dob, NT)).astype(BF16)
                    dv_part = _dot(pt.astype(BF16), dob, NN)
                    dk_part = _dot(dsb, qb, NN)
                    dv = dv_part if dv is None else dv + dv_part
                    dk = dk_part if dk is None else dk + dk_part
                    dq_part = _dot(ktb, dsb, NN)
                    if kj == 0:
                        dqt_acc[j, :, qrows] = dq_part
                    else:
                        dqt_acc[j, :, qrows] += dq_part
                dk_s[j, krows, :] = dk
                dv_s[j, krows, :] = dv

        for kj in range(nk):
            key_block(kj)

        def finish(i, _):
            rows = pl.ds(pl.multiple_of(i * tq, tq), tq)
            dq = [jnp.transpose(dqt_acc[j, :, rows]) for j in range(2)]
            dk = [dk_s[j, rows, :] for j in range(2)]
            dv = [dv_s[j, rows, :] for j in range(2)]
            dcol = [dq[j][:, LANE_CQ:LANE_CQ + 1] - dk[j][:, LANE_CK:LANE_CK + 1] for j in range(2)]
            dq = [dq[j] * scale for j in range(2)]
            for out_ref, val in ((dq_ref, dq), (dk_ref, dk), (dv_ref, dv)):
                merged = jnp.where(low, val[0], pltpu.roll(val[1], HEAD_DIM, 1))
                out_ref[0, rows, :] = merged.astype(BF16)
            dc_ref[0, 0, rows, :] = jnp.where(lane == 0, dcol[0], jnp.where(lane == 1, dcol[1], 0.0))
            return 0

        lax.fori_loop(0, nq, finish, 0)

    tok = pl.BlockSpec((1, 1, 2, s, LANES), lambda bi, hp: (bi, hp, 0, 0, 0))
    tok_t = pl.BlockSpec((1, 1, 2, LANES, s), lambda bi, hp: (bi, hp, 0, 0, 0))
    pair_blk = pl.BlockSpec((1, s, LANES), lambda bi, hp: (bi, 0, hp))
    pair_shape = jax.ShapeDtypeStruct((b, s, a), BF16)
    return _call(
        body, name=name, grid=(b, pairs),
        in_specs=[tok, tok_t, tok, tok, tok],
        out_specs=[pair_blk, pair_blk, pair_blk,
                   pl.BlockSpec((1, 1, s, LANES), lambda bi, hp: (bi, hp, 0, 0))],
        out_shape=[pair_shape, pair_shape, pair_shape,
                   jax.ShapeDtypeStruct((b, pairs, s, LANES), F32)],
        scratch_shapes=[pltpu.VMEM((2, LANES, s), F32), pltpu.VMEM((2, s, LANES), F32),
                        pltpu.VMEM((2, s, LANES), F32)],
        semantics=("parallel", "parallel"))(ka, kat, va, qab, doa)


def _pool_tile(s):
    return min(256, s)


def _band(tb, window, shift):
    tgt = lax.broadcasted_iota(jnp.int32, (tb, tb), 0)
    src = lax.broadcasted_iota(jnp.int32, (tb, tb), 1) + shift
    return ((src <= tgt) & (src > tgt - window)).astype(BF16)


def _band_t(tb, window, shift):
    src = lax.broadcasted_iota(jnp.int32, (tb, tb), 0)
    tgt = lax.broadcasted_iota(jnp.int32, (tb, tb), 1) + shift
    return ((src <= tgt) & (src > tgt - window)).astype(BF16)


def _pool_fwd(pp, w_pool, scale, name):
    b, s, pw2 = pp.shape
    pw = pw2 // 2
    pg = pw // N_POOL_GROUPS
    tb = _pool_tile(s)
    nb = s // tb

    def body(u_ref, z_ref, w_ref, s_ref, o_ref):
        window = 2 << pl.program_id(1)
        band0 = _band(tb, window, 0)
        band1 = _band(tb, window, -tb)
        pos = lax.broadcasted_iota(jnp.int32, (tb, pg), 0)

        def block(i, _):
            rows = pl.ds(pl.multiple_of(i * tb, tb), tb)
            prev = pl.ds(pl.multiple_of(jnp.maximum(i - 1, 0) * tb, tb), tb)
            ub = u_ref[0, rows, :]
            up = u_ref[0, prev, :]
            up = jnp.where(i > 0, up, jnp.zeros_like(up))
            count = jnp.minimum(pos + i * tb + 1, window).astype(F32)
            pooled = (_dot(band0, ub, NN) + _dot(band1, up, NN)) / count - ub.astype(F32)
            mixed = _dot(pooled.astype(BF16), w_ref[0], NN) * s_ref[...]
            zv = z_ref[0, rows, :].astype(F32)
            o_ref[0, rows, :] = (mixed * zv * _sigmoid(zv)).astype(BF16)
            return 0

        lax.fori_loop(0, nb, block, 0)

    return _call(
        body, name=name, grid=(b, N_POOL_GROUPS),
        in_specs=[pl.BlockSpec((1, s, pg), lambda bi, g: (bi, 0, g)),
                  pl.BlockSpec((1, s, pg), lambda bi, g: (bi, 0, N_POOL_GROUPS + g)),
                  pl.BlockSpec((1, pg, pg), lambda bi, g: (g, 0, 0)),
                  pl.BlockSpec((1, pg), lambda bi, g: (0, g))],
        out_specs=pl.BlockSpec((1, s, pg), lambda bi, g: (bi, 0, g)),
        out_shape=jax.ShapeDtypeStruct((b, s, pw), BF16),
        semantics=("parallel", "parallel"))(pp, pp, w_pool, scale)


def _pool_bwd(pp, dcat, w_pool, scale, first_block, name):
    b, s, pw2 = pp.shape
    pw = pw2 // 2
    pg = pw // N_POOL_GROUPS
    tb = _pool_tile(s)
    nb = s // tb

    def body(u_ref, z_ref, d_ref, w_ref, s_ref, du_ref, dz_ref, dw_ref, ds_ref, dpool_s):
        @pl.when(pl.program_id(1) == 0)
        def _():
            dw_ref[...] = jnp.zeros_like(dw_ref)
            ds_ref[...] = jnp.zeros_like(ds_ref)

        window = 2 << pl.program_id(0)
        band0 = _band(tb, window, 0)
        band1 = _band(tb, window, -tb)
        band0_t = _band_t(tb, window, 0)
        band1_t = _band_t(tb, window, tb)
        pos = lax.broadcasted_iota(jnp.int32, (tb, pg), 0)

        def first(i, _):
            rows = pl.ds(pl.multiple_of(i * tb, tb), tb)
            prev = pl.ds(pl.multiple_of(jnp.maximum(i - 1, 0) * tb, tb), tb)
            ub = u_ref[0, rows, :]
            up = u_ref[0, prev, :]
            up = jnp.where(i > 0, up, jnp.zeros_like(up))
            count = jnp.minimum(pos + i * tb + 1, window).astype(F32)
            pooled = ((_dot(band0, ub, NN) + _dot(band1, up, NN)) / count
                      - ub.astype(F32)).astype(BF16)
            mixed = _dot(pooled, w_ref[0], NN)
            pm = mixed * s_ref[...]
            zv = z_ref[0, rows, :].astype(F32)
            sg = _sigmoid(zv)
            dpl = d_ref[0, rows, :].astype(F32)
            dpm = dpl * zv * sg
            dz_ref[0, rows, :] = (dpl * pm * sg * (1.0 + zv * (1.0 - sg))).astype(BF16)
            ds_ref[...] += jnp.sum(dpm * mixed, axis=0, keepdims=True)
            dmixed = (dpm * s_ref[...]).astype(BF16)
            dw_ref[0] += _dot(pooled, dmixed, TN)
            dpool_s[rows, :] = _dot(dmixed, w_ref[0], NT)
            return 0

        lax.fori_loop(0, nb, first, 0)

        def second(i, _):
            rows = pl.ds(pl.multiple_of(i * tb, tb), tb)
            nxt_i = jnp.minimum(i + 1, nb - 1)
            nxt = pl.ds(pl.multiple_of(nxt_i * tb, tb), tb)
            count = jnp.minimum(pos + i * tb + 1, window).astype(F32)
            count_n = jnp.minimum(pos + nxt_i * tb + 1, window).astype(F32)
            dpb = dpool_s[rows, :]
            cur = (dpb / count).astype(BF16)
            nx = dpool_s[nxt, :] / count_n
            nx = jnp.where(i < nb - 1, nx, 0.0).astype(BF16)
            du = _dot(band0_t, cur, NN) + _dot(band1_t, nx, NN) - dpb
            du_ref[0, rows, :] = du.astype(BF16)
            return 0

        lax.fori_loop(0, nb, second, 0)

    return _call(
        body, name=name, grid=(N_POOL_GROUPS, b),
        in_specs=[pl.BlockSpec((1, s, pg), lambda g, bi: (bi, 0, g)),
                  pl.BlockSpec((1, s, pg), lambda g, bi: (bi, 0, N_POOL_GROUPS + g)),
                  pl.BlockSpec((1, s, pg), lambda g, bi: (bi, 0, first_block + g)),
                  pl.BlockSpec((1, pg, pg), lambda g, bi: (g, 0, 0)),
                  pl.BlockSpec((1, pg), lambda g, bi: (0, g))],
        out_specs=[pl.BlockSpec((1, s, pg), lambda g, bi: (bi, 0, g)),
                   pl.BlockSpec((1, s, pg), lambda g, bi: (bi, 0, g)),
                   pl.BlockSpec((1, pg, pg), lambda g, bi: (g, 0, 0)),
                   pl.BlockSpec((1, pg), lambda g, bi: (0, g))],
        out_shape=[jax.ShapeDtypeStruct((b, s, pw), BF16), jax.ShapeDtypeStruct((b, s, pw), BF16),
                   jax.ShapeDtypeStruct((N_POOL_GROUPS, pg, pg), F32),
                   jax.ShapeDtypeStruct((1, pw), F32)],
        scratch_shapes=[pltpu.VMEM((s, pg), F32)],
        semantics=("parallel", "arbitrary"))(pp, pp, dcat, w_pool, scale)


def _adamw(recvs, owns, w, m, v, name):
    depth = len(recvs)
    r, c = owns[0].shape
    tr = min(128, r)
    nb = r // tr
    c1 = 1.0 - ADAM_B1 ** ADAM_STEP
    c2 = 1.0 - ADAM_B2 ** ADAM_STEP

    def body(*refs):
        recv_refs, own_refs = refs[:depth], refs[depth:2 * depth]
        w_ref, m_ref, v_ref, g_ref, d_ref, nm_ref, nv_ref = refs[2 * depth:]
        x, y, core = (lax.axis_index(ax) for ax in MESH_AXES)
        me = 4 * x + 2 * y + core
        for layer in range(depth):
            @pl.when(pl.program_id(0) == layer)
            def _(layer=layer):
                own = own_refs[layer][...].astype(F32)
                g = jnp.where(me == 0, own, recv_refs[layer][0].astype(F32))
                for sl in range(1, N_DEV):
                    g = g + jnp.where(me == sl, own, recv_refs[layer][sl].astype(F32))
                mn = ADAM_B1 * m_ref[0] + (1.0 - ADAM_B1) * g
                vn = ADAM_B2 * v_ref[0] + (1.0 - ADAM_B2) * (g * g)
                m_hat = mn / c1
                v_hat = vn / c2
                g_ref[0] = g
                d_ref[0] = -ADAM_LR * (m_hat / (jnp.sqrt(v_hat) + ADAM_EPS) + ADAM_WD * w_ref[0])
                nm_ref[0] = mn
                nv_ref[0] = vn

    def blk(layer):
        return lambda l, i: jnp.clip(i + (l - layer) * nb, 0, nb - 1)

    in_specs = [pl.BlockSpec((N_DEV, tr, c), lambda l, i, f=blk(layer): (0, f(l, i), 0))
                for layer in range(depth)]
    in_specs += [pl.BlockSpec((tr, c), lambda l, i, f=blk(layer): (f(l, i), 0))
                 for layer in range(depth)]
    row = pl.BlockSpec((1, tr, c), lambda l, i: (l, i, 0))
    return _call(body, name=name, grid=(depth, nb), in_specs=in_specs + [row, row, row],
                 out_specs=[row] * 4, out_shape=[jax.ShapeDtypeStruct((depth, r, c), F32)] * 4,
                 semantics=("arbitrary", "arbitrary"))(*recvs, *owns, w, m, v)


def _pack_w_in(gathered, a, heads, pw):
    d = gathered.shape[1]
    w_full = jnp.transpose(gathered, (1, 0, 2)).reshape(d, -1)
    wf = jnp.pad(w_full[:, 4 * a:4 * a + heads], ((0, 0), (0, LANES - heads)))
    return w_full, w_full[:, 4 * a + heads:], wf


def _unpack_dw_in(parts, heads):
    dq, dk, dv, dz, dwf, du, dzp = parts
    d = dq.shape[0]
    full = jnp.concatenate([dq, dk, dv, dz, dwf[:, :heads], du, dzp], axis=1)
    return jnp.transpose(full.reshape(d, N_DEV, -1), (1, 0, 2))


def kernel(x, p, norm_pre, norm_post, w_in, b_f, w_pool, pool_scale, w_out, w_pg, w_pe, loss_target, m_norm_pre, m_norm_post, m_w_in, m_b_f, m_w_pool, m_pool_scale, m_w_out, m_w_pg, m_w_pe, v_norm_pre, v_norm_post, v_w_in, v_b_f, v_w_pool, v_pool_scale, v_w_out, v_w_pg, v_w_pe):
    depth = w_in.shape[0]
    b, s, d = x.shape
    t = b * s
    heads = b_f.shape[1]
    a = heads * HEAD_DIM
    pairs = a // LANES
    pw = pool_scale.shape[1]
    pg = pw // N_POOL_GROUPS
    ple = p.shape[-1]
    mix_w = a + pw

    me = 4 * lax.axis_index("x") + 2 * lax.axis_index("y") + lax.axis_index("c")
    shard = {
        "w_in": [w_in[i].astype(BF16) for i in range(depth)],
        "w_pool": [w_pool[i].reshape(N_POOL_GROUPS * (pg // N_DEV), pg).astype(BF16)
                   for i in range(depth)],
        "w_out": [w_out[i].astype(BF16) for i in range(depth)],
        "w_pg": [w_pg[i].astype(BF16) for i in range(depth)],
        "w_pe": [w_pe[i].astype(BF16) for i in range(depth)],
    }
    names = list(shard)
    rest = names[1:]

    def unpack_rest(lands, layer, which):
        g = {nm: _with_own(ld, shard[nm][layer], me) for nm, ld in zip(which, lands)}
        g_pool = g["w_pool"].reshape(N_DEV, N_POOL_GROUPS, pg // N_DEV, pg)
        return dict(wpool=jnp.transpose(g_pool, (1, 0, 2, 3)).reshape(N_POOL_GROUPS, pg, pg),
                    wout=g["w_out"].reshape(mix_w, d), wpg=g["w_pg"].reshape(d, d),
                    wpe=jnp.transpose(g["w_pe"], (1, 0, 2)).reshape(ple, d))

    g_in0 = _gather_two_level(shard["w_in"][0], "gather_w_in0")
    rest0, tok_rest0 = _exchange_start([(shard[nm][0], False) for nm in rest], g_in0,
                                       "gather_rest0_start")
    later, tok = [], tok_rest0
    for i in range(1, depth):
        hdl, tk_i = _exchange_start([(shard[nm][i], False) for nm in names], g_in0,
                                    "gather_layer%d_start" % i)
        later.append(hdl)
        tok = tok + tk_i

    h = x.reshape(t, d)
    saved = []
    layers = []
    for i in range(depth):
        sv = dict(h=h)
        g_pre = norm_pre[i:i + 1]
        g_post = norm_post[i:i + 1]
        bf = jnp.pad(b_f[i:i + 1], ((0, 0), (0, LANES - heads)))
        scale = pool_scale[i:i + 1]
        if i == 0:
            lw = dict(zip(("wa", "wp", "wf"), _pack_w_in(g_in0, a, heads, pw)))
            g_pre = g_pre + tok
        else:
            lands = _exchange_wait(later[i - 1], h, "gather_layer%d_wait" % i)
            g_in = _with_own(lands[0], shard["w_in"][i], me)
            lw = dict(zip(("wa", "wp", "wf"), _pack_w_in(g_in, a, heads, pw)))
            lw.update(unpack_rest(lands[1:], i, rest))
        hn = _rms_fwd(h, g_pre, "rms_pre")
        pa = _matmul([(hn, lw["wa"])], "nn", BF16, "proj_attn", n_dim=4 * a, tn=2048,
                     n_outer=True).reshape(b, s, 4 * a)
        pp = _matmul([(hn, lw["wp"])], "nn", BF16, "proj_pool", tn=2048,
                     n_outer=True).reshape(b, s, 2 * pw)
        fl = _matmul([(hn, lw["wf"])], "nn", F32, "proj_gate").reshape(b, s, LANES)
        c = _gates_fwd(fl, bf, "gates_fwd")
        qa, ka, kat, va, vt = _attn_prep_fwd(pa, c, "attn_prep_fwd")
        o, ga, lse = _attn_fwd(qa, ka, vt, pa, "attn_fwd")
        if i == 0:
            lw.update(unpack_rest(_exchange_wait(rest0, lse, "gather_rest0_wait"), 0, rest))
        layers.append(lw)
        gp = _pool_fwd(pp, lw["wpool"], scale, "pool_fwd")
        ga2 = ga.reshape(t, a)
        gp2 = gp.reshape(t, pw)
        mix = _matmul([(ga2, lw["wout"], 0, 0), (gp2, lw["wout"], a, 0)], "nn", F32, "mix_out")
        h1, h1b = _post_fwd(h, mix, g_post, "post_fwd")
        pb = p[i].reshape(t, ple).astype(BF16)
        gpre = _matmul([(h1b, lw["wpg"])], "nn", F32, "ple_gate")
        e = _matmul([(pb, lw["wpe"])], "nn", F32, "ple_embed")
        h = _ple_fwd(h1, gpre, e, "ple_fwd")
        sv.update(hn=hn, pa=pa, pp=pp, fl=fl, bf=bf, qa=qa, ka=ka, kat=kat, va=va, o=o, lse=lse, ga=ga2,
                  gp=gp2, mix=mix,
                  h1b=h1b, pb=pb, gpre=gpre, e=e, g_pre=g_pre, g_post=g_post, scale=scale)
        saved.append(sv)

    dh, sq = _loss_bwd(h, loss_target.reshape(t, d), "loss")
    loss = lax.psum(0.5 * jnp.sum(sq) / d, MESH_AXES)

    big = {nm: [None] * depth for nm in names}
    small = {nm: [None] * depth for nm in ("norm_pre", "norm_post", "b_f", "pool_scale")}
    grad_handles = [None] * depth
    rest_handles = [None] * depth
    for i in reversed(range(depth)):
        lw, sv = layers[i], saved[i]
        de, dpre = _ple_bwd(dh, sv["gpre"], sv["e"], "ple_bwd")
        dwpe = _matmul([(sv["pb"], de)], "tn", BF16, "dw_pe", tm=1024)
        dwpg = _matmul([(sv["h1b"], dpre)], "tn", BF16, "dw_pg", tm=1024)
        t1 = _matmul([(dpre, lw["wpg"])], "nt", F32, "d_h1")
        dh1, dmix, dg_post = _post_bwd(dh, t1, sv["mix"], sv["g_post"], "post_bwd")
        dwout = jnp.concatenate(
            [_matmul([(sv["ga"], dmix)], "tn", BF16, "dw_out_attn", tm=1024),
             _matmul([(sv["gp"], dmix)], "tn", BF16, "dw_out_pool", tm=1024)], axis=0)
        dcat = _matmul([(dmix, lw["wout"])], "nt", BF16, "d_cat", tn=2048).reshape(b, s, mix_w)
        du, dzp, dwpool, dscale = _pool_bwd(sv["pp"], dcat, lw["wpool"], sv["scale"], a // pg,
                                            "pool_bwd")
        big["w_pool"][i] = jnp.transpose(
            dwpool.astype(BF16).reshape(N_POOL_GROUPS, N_DEV, pg // N_DEV, pg), (1, 0, 2, 3)
        ).reshape(N_DEV, N_POOL_GROUPS * (pg // N_DEV), pg)
        big["w_out"][i] = dwout.reshape(N_DEV, mix_w // N_DEV, d)
        big["w_pg"][i] = dwpg.reshape(N_DEV, d // N_DEV, d)
        big["w_pe"][i] = jnp.transpose(dwpe.reshape(ple, N_DEV, d // N_DEV), (1, 0, 2))
        rest_handles[i], tok = _exchange_start([(big[nm][i], True) for nm in rest], du,
                                               "grads_rest%d_start" % i)
        qab, doa, dz = _attn_prep_bwd(dcat, sv["pa"], sv["o"], sv["lse"] + tok, sv["qa"],
                                      "attn_prep_bwd")
        dq, dk, dv, dcp = _attn_bwd(sv["ka"], sv["kat"], sv["va"], qab, doa, "attn_bwd")
        dc = jnp.transpose(dcp[..., :2], (0, 2, 1, 3)).reshape(b, s, heads)
        dc = jnp.pad(dc, ((0, 0), (0, 0), (0, LANES - heads)))
        dfl, dbf = _gates_bwd(dc, sv["fl"], sv["bf"], heads, "gates_bwd")
        dproj = [g_.reshape(t, -1) for g_ in (dq, dk, dv, dz, dfl, du, dzp)]
        dw_parts = [_matmul([(sv["hn"], g_)], "tn", BF16, "dw_in_%d" % n_, tm=1024)
                    for n_, g_ in enumerate(dproj)]

        big["w_in"][i] = _unpack_dw_in(dw_parts, heads)
        grad_handles[i], tok = _exchange_start([(big["w_in"][i], True)], dw_parts[-1],
                                               "grads_w_in%d_start" % i)

        dq2, dk2, dv2, dz2, dfl2, du2, dzp2 = dproj
        dhn = _matmul([(dq2, lw["wa"], 0, 0), (dk2, lw["wa"], a, 0), (dv2, lw["wa"], 2 * a, 0),
                       (dz2, lw["wa"], 3 * a, 0), (du2, lw["wp"], 0, 0), (dzp2, lw["wp"], pw, 0),
                       (dfl2, lw["wf"] + tok.astype(BF16), 0, 0)], "nt", F32, "d_hn")
        dh, dg_pre = _pre_bwd(sv["h"], dhn, dh1, sv["g_pre"] + tok, "pre_bwd")
        small["norm_pre"][i] = dg_pre
        small["norm_post"][i] = dg_post
        small["b_f"][i] = jnp.sum(dbf, axis=0)
        small["pool_scale"][i] = dscale
    grad_x = dh.reshape(b, s, d)

    width = max(d, pw)
    small_names = ("norm_pre", "norm_post", "pool_scale", "b_f")

    def small_rows(get):
        rows = []
        for nm in small_names:
            for i in range(depth):
                v_ = get(nm, i)
                rows.append(jnp.pad(v_, ((0, 0), (0, width - v_.shape[1]))))
        return jnp.concatenate(rows, axis=0)

    small_g = small_rows(lambda nm, i: small[nm][i])
    (small_recv,) = _exchange([(small_g, False)], "exchange_small")
    received = []
    for i in range(depth):
        got_rest = _exchange_wait(rest_handles[i], dh, "grads_rest%d_wait" % i)
        got_w_in = _exchange_wait(grad_handles[i], dh, "grads_w_in%d_wait" % i)
        received.append(got_w_in + got_rest)

    weights = dict(norm_pre=norm_pre, norm_post=norm_post, w_in=w_in, b_f=b_f, w_pool=w_pool,
                   pool_scale=pool_scale, w_out=w_out, w_pg=w_pg, w_pe=w_pe)
    mom1 = dict(norm_pre=m_norm_pre, norm_post=m_norm_post, w_in=m_w_in, b_f=m_b_f, w_pool=m_w_pool,
                pool_scale=m_pool_scale, w_out=m_w_out, w_pg=m_w_pg, w_pe=m_w_pe)
    mom2 = dict(norm_pre=v_norm_pre, norm_post=v_norm_post, w_in=v_w_in, b_f=v_b_f, w_pool=v_w_pool,
                pool_scale=v_pool_scale, w_out=v_w_out, w_pg=v_w_pg, w_pe=v_w_pe)

    results = {}
    for j, nm in enumerate(names):
        shp = weights[nm].shape
        recvs = [received[i][j] for i in range(depth)]
        owns = [lax.dynamic_index_in_dim(big[nm][i], me, 0, keepdims=False) for i in range(depth)]
        flat = lambda arr: arr.reshape((depth,) + owns[0].shape)
        outs = _adamw(recvs, owns, flat(weights[nm]), flat(mom1[nm]), flat(mom2[nm]), "adamw_" + nm)
        results[nm] = [o_.reshape(shp) for o_ in outs]

    small_w = small_rows(lambda nm, i: weights[nm][i:i + 1])[None]
    small_m = small_rows(lambda nm, i: mom1[nm][i:i + 1])[None]
    small_v = small_rows(lambda nm, i: mom2[nm][i:i + 1])[None]
    outs = _adamw([small_recv], [small_g], small_w, small_m, small_v, "adamw_small")
    for j, nm in enumerate(small_names):
        cols = weights[nm].shape[1]
        results[nm] = [o_[0, j * depth:(j + 1) * depth, :cols] for o_ in outs]

    order = ("norm_pre", "norm_post", "w_in", "b_f", "w_pool", "pool_scale", "w_out", "w_pg", "w_pe")
    return (loss, grad_x, *[results[nm][0] for nm in order], *[results[nm][1] for nm in order],
            *[results[nm][2] for nm in order], *[results[nm][3] for nm in order])
```

```python
import functools
import math

import jax
import jax.numpy as jnp
from jax import lax
from jax.experimental import pallas as pl
from jax.experimental.pallas import tpu as pltpu

N_DEV = 8
MESH_AXES = ("x", "y", "c")
HEAD_DIM = 64
LANES = 128
N_POOL_GROUPS = 4
EPS = 1e-6
ADAM_LR = 0.001
ADAM_B1 = 0.9
ADAM_B2 = 0.999
ADAM_EPS = 1e-08
ADAM_WD = 0.01
ADAM_STEP = 10
VMEM_LIMIT_BYTES = 56 * 1024 * 1024
F32 = jnp.float32
BF16 = jnp.bfloat16
NEG_INF = float("-inf")


def _call(body, *, name, grid, in_specs, out_specs, out_shape, scratch_shapes=(), semantics=None):
    return pl.pallas_call(
        body, name=name, grid=grid, in_specs=in_specs, out_specs=out_specs, out_shape=out_shape,
        scratch_shapes=list(scratch_shapes),
        compiler_params=pltpu.CompilerParams(dimension_semantics=semantics,
                                             vmem_limit_bytes=VMEM_LIMIT_BYTES))


def _sigmoid(z):
    return 1.0 / (1.0 + jnp.exp(-z))


def _dot(a, b, dims):
    return lax.dot_general(a, b, (dims, ((), ())), preferred_element_type=F32)


NN = ((1,), (0,))
NT = ((1,), (1,))
TN = ((0,), (0,))


def _exchange(items, name):
    n = len(items)
    modes = [s for _, s in items]
    out_shapes = []
    for a, s in items:
        shp = a.shape[1:] if s else a.shape
        out_shapes.append(jax.ShapeDtypeStruct((N_DEV,) + tuple(shp), a.dtype))

    def body(*refs):
        ins = refs[:n]
        outs = refs[n:2 * n]
        send_sems, recv_sems, local_sems = refs[2 * n:]
        x, y, c = (lax.axis_index(ax) for ax in MESH_AXES)
        me = 4 * x + 2 * y + c
        started = []
        for i in range(n):
            mine = ins[i].at[me] if modes[i] else ins[i]
            loc = pltpu.make_async_copy(mine, outs[i].at[me], local_sems.at[i])
            loc.start()
            started.append(loc)
        remote = []
        for k in range(1, N_DEV):
            px = x ^ ((k >> 2) & 1)
            py = y ^ ((k >> 1) & 1)
            pc = c ^ (k & 1)
            peer = me ^ k
            for i in range(n):
                src = ins[i].at[peer] if modes[i] else ins[i]
                cp = pltpu.make_async_remote_copy(
                    src_ref=src, dst_ref=outs[i].at[me],
                    send_sem=send_sems.at[i, k - 1], recv_sem=recv_sems.at[i, k - 1],
                    device_id=(px, py, pc), device_id_type=pl.DeviceIdType.MESH)
                cp.start()
                remote.append(cp)
        for cp in remote:
            cp.wait()
        for loc in started:
            loc.wait()

    hbm = pl.BlockSpec(memory_space=pltpu.HBM)
    return pl.pallas_call(
        body, name=name, out_shape=out_shapes,
        in_specs=[hbm] * n, out_specs=[hbm] * n,
        scratch_shapes=[pltpu.SemaphoreType.DMA((n, N_DEV - 1)),
                        pltpu.SemaphoreType.DMA((n, N_DEV - 1)),
                        pltpu.SemaphoreType.DMA((n,))],
    )(*[a for a, _ in items])


def _gather_two_level(shard, name):
    def body(x_ref, out_ref, send_sems, recv_sems, local_sem):
        x, y, c = (lax.axis_index(ax) for ax in MESH_AXES)
        sibling = (x, y, 1 - c)
        chips = [(1 - x, y), (x, 1 - y), (1 - x, 1 - y)]

        def slot(px, py, pc):
            return out_ref.at[4 * px + 2 * py + pc]

        def copy(k, block, to, src=None):
            return pltpu.make_async_remote_copy(
                src_ref=slot(*block) if src is None else src, dst_ref=slot(*block),
                send_sem=send_sems.at[k], recv_sem=recv_sems.at[k],
                device_id=to, device_id_type=pl.DeviceIdType.MESH)

        mine = pltpu.make_async_copy(x_ref, slot(x, y, c), local_sem)
        mine.start()
        first = [copy(0, (x, y, c), sibling, src=x_ref)]
        first += [copy(1 + j, (x, y, c), (*chip, c), src=x_ref) for j, chip in enumerate(chips)]
        for cp in first:
            cp.start()
        passed = [copy(4 + j, (*chip, c), sibling) for j, chip in enumerate(chips)]
        for j, chip in enumerate(chips):
            copy(1 + j, (*chip, c), (x, y, c)).wait_recv()
            passed[j].start()
        copy(0, (x, y, 1 - c), (x, y, c)).wait_recv()
        for j, chip in enumerate(chips):
            copy(4 + j, (*chip, 1 - c), (x, y, c)).wait_recv()
        for cp in first + passed:
            cp.wait_send()
        mine.wait()

    hbm = pl.BlockSpec(memory_space=pltpu.HBM)
    return pl.pallas_call(
        body, name=name, out_shape=jax.ShapeDtypeStruct((N_DEV,) + shard.shape, shard.dtype),
        in_specs=[hbm], out_specs=hbm,
        scratch_shapes=[pltpu.SemaphoreType.DMA((N_DEV - 1,)), pltpu.SemaphoreType.DMA((N_DEV - 1,)),
                        pltpu.SemaphoreType.DMA],
    )(shard)


def _peer_copies(srcs, lands, modes, send_sems, recv_sems):
    x, y, c = (lax.axis_index(ax) for ax in MESH_AXES)
    me = 4 * x + 2 * y + c
    copies = []
    for k in range(1, N_DEV):
        peer_id = (x ^ ((k >> 2) & 1), y ^ ((k >> 1) & 1), c ^ (k & 1))
        for i, scatter in enumerate(modes):
            src = srcs[i].at[me ^ k] if scatter else srcs[i]
            pair = i * (N_DEV - 1) + k - 1
            copies.append(pltpu.make_async_remote_copy(
                src_ref=src, dst_ref=lands[i].at[me],
                send_sem=send_sems.at[pair], recv_sem=recv_sems.at[pair],
                device_id=peer_id, device_id_type=pl.DeviceIdType.MESH))
    return copies


def _exchange_start(items, after, name):
    n = len(items)
    modes = [s for _, s in items]
    srcs = [pltpu.with_memory_space_constraint(a, pltpu.HBM) for a, _ in items]
    lands = []
    for a, s in items:
        shp = (N_DEV,) + tuple(a.shape[1:] if s else a.shape)
        lands.append(pltpu.with_memory_space_constraint(lax.empty(shp, a.dtype), pltpu.HBM))

    def body(*refs):
        send_sems, recv_sems = refs[2 * n + 1], refs[2 * n + 2]
        token = refs[-1]
        for cp in _peer_copies(refs[:n], refs[n:2 * n], modes, send_sems, recv_sems):
            cp.start()
        token[...] = jnp.zeros_like(token)

    hbm = pl.BlockSpec(memory_space=pltpu.HBM)
    sem = pl.BlockSpec(memory_space=pltpu.SEMAPHORE)
    outs = pl.pallas_call(
        body, name=name,
        out_shape=(pltpu.SemaphoreType.DMA((n * (N_DEV - 1),)),
                   pltpu.SemaphoreType.DMA((n * (N_DEV - 1),)),
                   *[pltpu.HBM(a.shape, a.dtype) for a in srcs + lands],
                   jax.ShapeDtypeStruct((8, LANES), F32)),
        in_specs=[hbm] * (2 * n) + [pl.BlockSpec(memory_space=pl.ANY)],
        out_specs=(sem, sem, *([hbm] * (2 * n)), pl.BlockSpec(memory_space=pltpu.VMEM)),
        input_output_aliases={i: 2 + i for i in range(2 * n)},
        compiler_params=pltpu.CompilerParams(
            has_side_effects=pltpu.SideEffectType.DATAFLOW_SIDE_EFFECTING),
    )(*srcs, *lands, after)
    handle = (modes, outs[0], outs[1], list(outs[2:2 + n]), list(outs[2 + n:2 + 2 * n]))
    return handle, outs[-1][0, 0]


def _exchange_wait(handle, after, name):
    modes, send_sems, recv_sems, srcs, lands = handle
    n = len(modes)

    def body(*refs):
        for cp in _peer_copies(refs[:n], refs[n:2 * n], modes, refs[2 * n], refs[2 * n + 1]):
            cp.wait_send()
            cp.wait_recv()

    hbm = pl.BlockSpec(memory_space=pltpu.HBM)
    sem = pl.BlockSpec(memory_space=pltpu.SEMAPHORE)
    outs = pl.pallas_call(
        body, name=name,
        out_shape=tuple(pltpu.HBM(a.shape, a.dtype) for a in srcs + lands),
        in_specs=[hbm] * (2 * n) + [sem, sem, pl.BlockSpec(memory_space=pl.ANY)],
        out_specs=tuple([hbm] * (2 * n)),
        input_output_aliases={i: i for i in range(2 * n)},
        compiler_params=pltpu.CompilerParams(
            has_side_effects=pltpu.SideEffectType.DATAFLOW_SIDE_EFFECTING),
    )(*srcs, *lands, send_sems, recv_sems, after)
    return list(outs[n:])


def _with_own(slots, own, me):
    idx = lax.broadcasted_iota(jnp.int32, (N_DEV,) + (1,) * own.ndim, 0)
    return jnp.where(idx == me, own[None], slots)


def _matmul(pairs, mode, out_dtype, name, n_dim=None, tm=512, tn=1024, tk=1024, n_outer=False):
    dims = {"nn": NN, "nt": NT, "tn": TN}[mode]
    pairs = [tuple(pr) + (0, 0) * (len(pr) == 2) for pr in pairs]
    a0, b0 = pairs[0][:2]
    m_dim = a0.shape[1] if mode == "tn" else a0.shape[0]
    if n_dim is None:
        n_dim = b0.shape[0] if mode == "nt" else b0.shape[1]
    tm = min(tm, m_dim)
    tn = min(tn, n_dim)
    segs = []
    off = 0
    for a, _, k0, n0 in pairs:
        k_dim = a.shape[0] if mode == "tn" else a.shape[1]
        t = min(tk, k_dim)
        segs.append((off, k_dim // t, t, k0 // t, n0 // tn))
        off += k_dim // t
    nk = off
    n_pairs = len(pairs)

    def ij(g0, g1):
        return (g1, g0) if n_outer else (g0, g1)

    in_specs = []
    for (o, cnt, t, kb, nb) in segs:
        def kc(kk, o=o, cnt=cnt):
            return jnp.clip(kk - o, 0, cnt - 1)
        if mode == "tn":
            in_specs.append(pl.BlockSpec((t, tm), lambda g0, g1, kk, kc=kc: (kc(kk), ij(g0, g1)[0])))
        else:
            in_specs.append(pl.BlockSpec((tm, t), lambda g0, g1, kk, kc=kc: (ij(g0, g1)[0], kc(kk))))
        if mode == "nt":
            in_specs.append(pl.BlockSpec((tn, t), lambda g0, g1, kk, kc=kc, kb=kb, nb=nb:
                                         (nb + ij(g0, g1)[1], kb + kc(kk))))
        else:
            in_specs.append(pl.BlockSpec((t, tn), lambda g0, g1, kk, kc=kc, kb=kb, nb=nb:
                                         (kb + kc(kk), nb + ij(g0, g1)[1])))

    one_shot = all(sg[1] == 1 for sg in segs)

    def body_sum(*refs):
        total = _dot(refs[0][...], refs[1][...], dims)
        for idx in range(1, n_pairs):
            total = total + _dot(refs[2 * idx][...], refs[2 * idx + 1][...], dims)
        refs[2 * n_pairs][...] = total.astype(out_dtype)

    def body(*refs):
        out_ref = refs[2 * n_pairs]
        acc = refs[2 * n_pairs + 1]
        kk = pl.program_id(2)

        @pl.when(kk == 0)
        def _():
            acc[...] = jnp.zeros_like(acc)

        for idx, (o, cnt) in enumerate(sg[:2] for sg in segs):
            @pl.when((kk >= o) & (kk < o + cnt))
            def _(idx=idx):
                acc[...] += _dot(refs[2 * idx][...], refs[2 * idx + 1][...], dims)

        @pl.when(kk == nk - 1)
        def _():
            out_ref[...] = acc[...].astype(out_dtype)

    flat = [t for pr in pairs for t in pr[:2]]
    tiles = (m_dim // tm, n_dim // tn)
    return _call(body_sum if one_shot else body, name=name,
                 grid=ij(*tiles) + (1 if one_shot else nk,), in_specs=in_specs,
                 out_specs=pl.BlockSpec((tm, tn), lambda g0, g1, kk: ij(g0, g1)),
                 out_shape=jax.ShapeDtypeStruct((m_dim, n_dim), out_dtype),
                 scratch_shapes=[] if one_shot else [pltpu.VMEM((tm, tn), F32)],
                 semantics=("parallel", "parallel", "arbitrary"))(*flat)


def _row_tile(t):
    return min(512, t)


def _rms_fwd(h, g, name):
    t, d = h.shape
    tt = _row_tile(t)

    def body(h_ref, g_ref, o_ref):
        hv = h_ref[...]
        r = lax.rsqrt(jnp.mean(hv * hv, axis=-1, keepdims=True) + EPS)
        o_ref[...] = (hv * r * g_ref[...]).astype(BF16)

    row = pl.BlockSpec((tt, d), lambda i: (i, 0))
    vec = pl.BlockSpec((1, d), lambda i: (0, 0))
    return _call(body, name=name, grid=(t // tt,), in_specs=[row, vec], out_specs=row,
                 out_shape=jax.ShapeDtypeStruct((t, d), BF16), semantics=("parallel",))(h, g)


def _post_fwd(h, mix, g, name):
    t, d = h.shape
    tt = _row_tile(t)

    def body(h_ref, m_ref, g_ref, o_ref, ob_ref):
        mv = m_ref[...]
        r = lax.rsqrt(jnp.mean(mv * mv, axis=-1, keepdims=True) + EPS)
        h1 = h_ref[...] + mv * r * g_ref[...]
        o_ref[...] = h1
        ob_ref[...] = h1.astype(BF16)

    row = pl.BlockSpec((tt, d), lambda i: (i, 0))
    vec = pl.BlockSpec((1, d), lambda i: (0, 0))
    return _call(body, name=name, grid=(t // tt,), in_specs=[row, row, vec], out_specs=[row, row],
                 out_shape=[jax.ShapeDtypeStruct((t, d), F32), jax.ShapeDtypeStruct((t, d), BF16)],
                 semantics=("parallel",))(h, mix, g)


def _ple_fwd(h1, gpre, e, name):
    t, d = h1.shape
    tt = _row_tile(t)

    def body(h_ref, g_ref, e_ref, o_ref):
        o_ref[...] = h_ref[...] + _sigmoid(g_ref[...]) * e_ref[...]

    row = pl.BlockSpec((tt, d), lambda i: (i, 0))
    return _call(body, name=name, grid=(t // tt,), in_specs=[row, row, row], out_specs=row,
                 out_shape=jax.ShapeDtypeStruct((t, d), F32), semantics=("parallel",))(h1, gpre, e)


def _loss_bwd(y, target, name):
    t, d = y.shape
    tt = _row_tile(t)

    def body(y_ref, t_ref, dy_ref, s_ref):
        @pl.when(pl.program_id(0) == 0)
        def _():
            s_ref[...] = jnp.zeros_like(s_ref)
        diff = y_ref[...] - t_ref[...]
        dy_ref[...] = diff * (1.0 / d)
        s_ref[...] += jnp.sum(diff * diff, axis=0, keepdims=True)

    row = pl.BlockSpec((tt, d), lambda i: (i, 0))
    vec = pl.BlockSpec((1, d), lambda i: (0, 0))
    return _call(body, name=name, grid=(t // tt,), in_specs=[row, row], out_specs=[row, vec],
                 out_shape=[jax.ShapeDtypeStruct((t, d), F32), jax.ShapeDtypeStruct((1, d), F32)],
                 semantics=("arbitrary",))(y, target)


def _ple_bwd(dh2, gpre, e, name):
    t, d = dh2.shape
    tt = _row_tile(t)

    def body(d_ref, g_ref, e_ref, de_ref, dp_ref):
        gate = _sigmoid(g_ref[...])
        dv = d_ref[...]
        de_ref[...] = (dv * gate).astype(BF16)
        dp_ref[...] = (dv * e_ref[...] * gate * (1.0 - gate)).astype(BF16)

    row = pl.BlockSpec((tt, d), lambda i: (i, 0))
    return _call(body, name=name, grid=(t // tt,), in_specs=[row, row, row], out_specs=[row, row],
                 out_shape=[jax.ShapeDtypeStruct((t, d), BF16)] * 2,
                 semantics=("parallel",))(dh2, gpre, e)


def _post_bwd(dh2, t1, mix, g, name):
    t, d = dh2.shape
    tt = _row_tile(t)

    def body(d_ref, t_ref, m_ref, g_ref, dh_ref, dm_ref, dg_ref):
        @pl.when(pl.program_id(0) == 0)
        def _():
            dg_ref[...] = jnp.zeros_like(dg_ref)
        dh1 = d_ref[...] + t_ref[...]
        mv = m_ref[...]
        r = lax.rsqrt(jnp.mean(mv * mv, axis=-1, keepdims=True) + EPS)
        dh_ref[...] = dh1
        dg_ref[...] += jnp.sum(dh1 * mv * r, axis=0, keepdims=True)
        w = dh1 * g_ref[...]
        dot = jnp.mean(w * mv, axis=-1, keepdims=True)
        dm_ref[...] = (r * w - mv * (r * r * r) * dot).astype(BF16)

    row = pl.BlockSpec((tt, d), lambda i: (i, 0))
    vec = pl.BlockSpec((1, d), lambda i: (0, 0))
    return _call(body, name=name, grid=(t // tt,), in_specs=[row, row, row, vec],
                 out_specs=[row, row, vec],
                 out_shape=[jax.ShapeDtypeStruct((t, d), F32), jax.ShapeDtypeStruct((t, d), BF16),
                            jax.ShapeDtypeStruct((1, d), F32)],
                 semantics=("arbitrary",))(dh2, t1, mix, g)


def _pre_bwd(h, dhn, dh1, g, name):
    t, d = h.shape
    tt = _row_tile(t)

    def body(h_ref, dn_ref, d1_ref, g_ref, dh_ref, dg_ref):
        @pl.when(pl.program_id(0) == 0)
        def _():
            dg_ref[...] = jnp.zeros_like(dg_ref)
        hv = h_ref[...]
        dn = dn_ref[...]
        r = lax.rsqrt(jnp.mean(hv * hv, axis=-1, keepdims=True) + EPS)
        dg_ref[...] += jnp.sum(dn * hv * r, axis=0, keepdims=True)
        w = dn * g_ref[...]
        dot = jnp.mean(w * hv, axis=-1, keepdims=True)
        dh_ref[...] = d1_ref[...] + r * w - hv * (r * r * r) * dot

    row = pl.BlockSpec((tt, d), lambda i: (i, 0))
    vec = pl.BlockSpec((1, d), lambda i: (0, 0))
    return _call(body, name=name, grid=(t // tt,), in_specs=[row, row, row, vec],
                 out_specs=[row, vec],
                 out_shape=[jax.ShapeDtypeStruct((t, d), F32), jax.ShapeDtypeStruct((1, d), F32)],
                 semantics=("arbitrary",))(h, dhn, dh1, g)


def _split3(v):
    hi = v.astype(BF16)
    r1 = v - hi.astype(F32)
    mid = r1.astype(BF16)
    lo = (r1 - mid.astype(F32)).astype(BF16)
    return hi, mid, lo


def _gates_fwd(fl, bf, name):
    b, s, _ = fl.shape

    def body(f_ref, b_ref, c_ref):
        xv = f_ref[0] + b_ref[...]
        lf = jnp.minimum(xv, 0.0) - jnp.log(1.0 + jnp.exp(-jnp.abs(xv)))
        dst = lax.broadcasted_iota(jnp.int32, (s, s), 0)
        src = lax.broadcasted_iota(jnp.int32, (s, s), 1)
        lower = (src <= dst).astype(BF16)
        acc = jnp.zeros((s, LANES), F32)
        for part in _split3(lf):
            acc = acc + _dot(lower, part, NN)
        c_ref[0] = acc

    blk = pl.BlockSpec((1, s, LANES), lambda i: (i, 0, 0))
    return _call(body, name=name, grid=(b,),
                 in_specs=[blk, pl.BlockSpec((1, LANES), lambda i: (0, 0))],
                 out_specs=blk, out_shape=jax.ShapeDtypeStruct((b, s, LANES), F32),
                 semantics=("parallel",))(fl, bf)


def _gates_bwd(dc, fl, bf, heads, name):
    b, s, _ = fl.shape

    def body(d_ref, f_ref, b_ref, o_ref, db_ref):
        xv = f_ref[0] + b_ref[...]
        dst = lax.broadcasted_iota(jnp.int32, (s, s), 0)
        src = lax.broadcasted_iota(jnp.int32, (s, s), 1)
        later = (src >= dst).astype(BF16)
        dlf = jnp.zeros((s, LANES), F32)
        for part in _split3(d_ref[0]):
            dlf = dlf + _dot(later, part, NN)
        lane = lax.broadcasted_iota(jnp.int32, (s, LANES), 1)
        dfl = jnp.where(lane < heads, dlf * _sigmoid(-xv), 0.0)
        o_ref[0] = dfl.astype(BF16)
        db_ref[0] = jnp.sum(dfl, axis=0, keepdims=True)

    blk = pl.BlockSpec((1, s, LANES), lambda i: (i, 0, 0))
    return _call(body, name=name, grid=(b,),
                 in_specs=[blk, blk, pl.BlockSpec((1, LANES), lambda i: (0, 0))],
                 out_specs=[blk, pl.BlockSpec((1, 1, LANES), lambda i: (i, 0, 0))],
                 out_shape=[jax.ShapeDtypeStruct((b, s, LANES), BF16),
                            jax.ShapeDtypeStruct((b, 1, LANES), F32)],
                 semantics=("parallel",))(dc, fl, bf)


LANE_CQ = 64
LANE_CK = 67
LANE_LSE = 70
LANE_D = 64
N_PARTS = 3


def _attn_tiles(s):
    return min(512, s), min(256, s)


def _lanes_in(lane, first):
    return (lane >= first) & (lane < first + N_PARTS)


def _attn_prep_fwd(pa, c, name):
    b, s, a4 = pa.shape
    pairs = a4 // (4 * LANES)
    scale = 1.0 / math.sqrt(HEAD_DIM)

    def body(q_ref, k_ref, v_ref, c_ref, qa_ref, ka_ref, kat_ref, va_ref, vt_ref):
        hp = pl.program_id(1)
        cv = c_ref[0]
        vv = v_ref[0]
        lane = lax.broadcasted_iota(jnp.int32, (s, LANES), 1)
        r128 = lax.broadcasted_iota(jnp.int32, (LANES, LANES), 0)
        c128 = lax.broadcasted_iota(jnp.int32, (LANES, LANES), 1)
        ident = (r128 == c128).astype(BF16)
        for j in range(2):
            head = 2 * hp + j
            move128 = (r128 == c128 + HEAD_DIM * j) & (c128 < HEAD_DIM)
            cparts = _split3(jnp.sum(jnp.where(lane == head, cv, 0.0), axis=1, keepdims=True))
            qa = _dot(q_ref[0], jnp.where(move128, scale, 0.0).astype(BF16), NN)
            ka = _dot(k_ref[0], move128.astype(BF16), NN)
            for i in range(N_PARTS):
                qa = jnp.where(lane == LANE_CQ + i, cparts[i].astype(F32), qa)
                ka = jnp.where(lane == LANE_CK + i, -cparts[i].astype(F32), ka)
            qa = jnp.where(_lanes_in(lane, LANE_CK), 1.0, qa)
            ka = jnp.where(_lanes_in(lane, LANE_CQ) | _lanes_in(lane, LANE_LSE), 1.0, ka)
            va = _dot(vv, move128.astype(BF16), NN) + jnp.where(_lanes_in(lane, LANE_D), 1.0, 0.0)
            kab = ka.astype(BF16)
            qa_ref[0, 0, j] = qa.astype(BF16)
            ka_ref[0, 0, j] = kab
            kat_ref[0, 0, j] = _dot(ident, kab, NT).astype(BF16)
            va_ref[0, 0, j] = va.astype(BF16)
        vt_ref[0, 0] = _dot(ident, vv, NT).astype(BF16)

    col_blk = lambda cidx: pl.BlockSpec((1, s, LANES), lambda bi, hp: (bi, 0, cidx * pairs + hp))
    tok = pl.BlockSpec((1, 1, 2, s, LANES), lambda bi, hp: (bi, hp, 0, 0, 0))
    tok_t = pl.BlockSpec((1, 1, 2, LANES, s), lambda bi, hp: (bi, hp, 0, 0, 0))
    tok_shape = jax.ShapeDtypeStruct((b, pairs, 2, s, LANES), BF16)
    return _call(
        body, name=name, grid=(b, pairs),
        in_specs=[col_blk(0), col_blk(1), col_blk(2),
                  pl.BlockSpec((1, s, LANES), lambda bi, hp: (bi, 0, 0))],
        out_specs=[tok, tok, tok_t, tok,
                   pl.BlockSpec((1, 1, LANES, s), lambda bi, hp: (bi, hp, 0, 0))],
        out_shape=[tok_shape, tok_shape, jax.ShapeDtypeStruct((b, pairs, 2, LANES, s), BF16),
                   tok_shape, jax.ShapeDtypeStruct((b, pairs, LANES, s), BF16)],
        semantics=("parallel", "parallel"))(pa, pa, pa, c)


def _attn_fwd(qa, ka, vt, pa, name):
    b, pairs, _, s, _ = qa.shape
    a = pairs * LANES
    tq = _attn_tiles(s)[0]
    nq = s // tq

    def body(q_ref, k_ref, vt_ref, z_ref, o_ref, g_ref, lse_ref):
        key_i = lax.broadcasted_iota(jnp.int32, (tq, tq), 0)
        qry_i = lax.broadcasted_iota(jnp.int32, (tq, tq), 1)

        def query_block(c):
            past = tq * c
            heads_out = []
            for j in range(2):
                qv = q_ref[0, 0, j]
                vrows = slice(HEAD_DIM * j, HEAD_DIM * (j + 1))
                sd = _dot(k_ref[0, 0, j, past:past + tq, :], qv, NT)
                sd = jnp.where(key_i <= qry_i, sd, NEG_INF)
                m = jnp.max(sd, axis=0, keepdims=True)
                if c > 0:
                    sp = _dot(k_ref[0, 0, j, 0:past, :], qv, NT)
                    m = jnp.maximum(m, jnp.max(sp, axis=0, keepdims=True))
                pd = jnp.exp(sd - m)
                l = jnp.sum(pd, axis=0, keepdims=True)
                acc = _dot(vt_ref[0, 0, vrows, past:past + tq], pd.astype(BF16), NN)
                if c > 0:
                    pp = jnp.exp(sp - m)
                    l = l + jnp.sum(pp, axis=0, keepdims=True)
                    acc = acc + _dot(vt_ref[0, 0, vrows, 0:past], pp.astype(BF16), NN)
                heads_out.append(acc / l)
                lse_ref[0, 0, j:j + 1, :] = m + jnp.log(l)
            ov = jnp.transpose(jnp.concatenate(heads_out, axis=0))
            o_ref[0] = ov.astype(BF16)
            zv = z_ref[0].astype(F32)
            g_ref[0] = (ov * zv * _sigmoid(zv)).astype(BF16)

        for c in range(nq):
            pl.when(pl.program_id(2) == c)(functools.partial(query_block, c))

    return _call(
        body, name=name, grid=(b, pairs, s // tq),
        in_specs=[pl.BlockSpec((1, 1, 2, tq, LANES), lambda bi, hp, qi: (bi, hp, 0, qi, 0)),
                  pl.BlockSpec((1, 1, 2, s, LANES), lambda bi, hp, qi: (bi, hp, 0, 0, 0)),
                  pl.BlockSpec((1, 1, LANES, s), lambda bi, hp, qi: (bi, hp, 0, 0)),
                  pl.BlockSpec((1, tq, LANES), lambda bi, hp, qi: (bi, qi, 3 * pairs + hp))],
        out_specs=[pl.BlockSpec((1, tq, LANES), lambda bi, hp, qi: (bi, qi, hp)),
                   pl.BlockSpec((1, tq, LANES), lambda bi, hp, qi: (bi, qi, hp)),
                   pl.BlockSpec((1, 1, 2, tq), lambda bi, hp, qi: (bi, hp, 0, qi))],
        out_shape=[jax.ShapeDtypeStruct((b, s, a), BF16), jax.ShapeDtypeStruct((b, s, a), BF16),
                   jax.ShapeDtypeStruct((b, pairs, 2, s), F32)],
        semantics=("parallel", "parallel", "arbitrary"))(qa, ka, vt, pa)


def _attn_prep_bwd(dcat, pa, o, lse, qa, name):
    b, pairs, _, s, _ = qa.shape
    a = pairs * LANES
    sub = 16

    def body(da_ref, z_ref, o_ref, lse_ref, qa_ref, qab_ref, doa_ref, dz_ref):
        zv = z_ref[0].astype(F32)
        dav = da_ref[0].astype(F32)
        ov = o_ref[0].astype(F32)
        sg = _sigmoid(zv)
        dov = dav * zv * sg
        dz_ref[0] = (dav * ov * sg * (1.0 + zv * (1.0 - sg))).astype(BF16)
        prod = dov * ov
        dob = dov.astype(BF16)
        lane = lax.broadcasted_iota(jnp.int32, (s, LANES), 1)
        r128 = lax.broadcasted_iota(jnp.int32, (LANES, LANES), 0)
        c128 = lax.broadcasted_iota(jnp.int32, (LANES, LANES), 1)
        prow = lax.broadcasted_iota(jnp.int32, (sub, s), 0)
        srow = lax.broadcasted_iota(jnp.int32, (sub, LANES), 0)
        scol = lax.broadcasted_iota(jnp.int32, (sub, LANES), 1)
        place = ((scol == srow + LANE_LSE) & (srow < N_PARTS)).astype(BF16)
        for j in range(2):
            in_head = (lane >= HEAD_DIM * j) & (lane < HEAD_DIM * (j + 1))
            dparts = _split3(jnp.sum(jnp.where(in_head, prod, 0.0), axis=1, keepdims=True))
            move128 = ((r128 == c128 + HEAD_DIM * j) & (c128 < HEAD_DIM)).astype(BF16)
            doa = _dot(dob, move128, NN)
            for i in range(N_PARTS):
                doa = jnp.where(lane == LANE_D + i, -dparts[i].astype(F32), doa)
            doa_ref[0, 0, j] = doa.astype(BF16)
            lparts = _split3(lse_ref[0, 0, j:j + 1, :])
            pmat = jnp.zeros((sub, s), BF16)
            for i in range(N_PARTS):
                pmat = jnp.where(prow == i, lparts[i], pmat)
            lcol = _dot(pmat, place, TN)
            qab_ref[0, 0, j] = (qa_ref[0, 0, j].astype(F32) - lcol).astype(BF16)

    tok = pl.BlockSpec((1, 1, 2, s, LANES), lambda bi, hp: (bi, hp, 0, 0, 0))
    tok_shape = jax.ShapeDtypeStruct((b, pairs, 2, s, LANES), BF16)
    pair_blk = pl.BlockSpec((1, s, LANES), lambda bi, hp: (bi, 0, hp))
    return _call(
        body, name=name, grid=(b, pairs),
        in_specs=[pair_blk,
                  pl.BlockSpec((1, s, LANES), lambda bi, hp: (bi, 0, 3 * pairs + hp)),
                  pair_blk,
                  pl.BlockSpec((1, 1, 2, s), lambda bi, hp: (bi, hp, 0, 0)),
                  tok],
        out_specs=[tok, tok, pair_blk],
        out_shape=[tok_shape, tok_shape, jax.ShapeDtypeStruct((b, s, a), BF16)],
        semantics=("parallel", "parallel"))(dcat, pa, o, lse, qa)


def _attn_bwd(ka, kat, va, qab, doa, name):
    b, pairs, _, s, _ = ka.shape
    a = pairs * LANES
    tq, tk = _attn_tiles(s)
    ratio = tq // tk
    nq, nk = s // tq, s // tk
    scale = 1.0 / math.sqrt(HEAD_DIM)

    def body(k_ref, kt_ref, v_ref, q_ref, do_ref, dq_ref, dk_ref, dv_ref, dc_ref,
             dqt_acc, dk_s, dv_s):
        key_i = lax.broadcasted_iota(jnp.int32, (tk, tq), 0)
        qry_i = lax.broadcasted_iota(jnp.int32, (tk, tq), 1)
        lane = lax.broadcasted_iota(jnp.int32, (tq, LANES), 1)
        low = lane < HEAD_DIM

        def key_block(kj):
            krows = slice(kj * tk, (kj + 1) * tk)
            q0 = (kj // ratio) * tq
            spans = [(slice(q0, q0 + tq), kj * tk - q0)]
            if q0 + tq < s:
                spans.append((slice(q0 + tq, s), None))
            for j in range(2):
                kb = k_ref[0, 0, j, krows, :]
                vb = v_ref[0, 0, j, krows, :]
                ktb = kt_ref[0, 0, j, :, krows]
                dk = dv = None
                for qrows, diag in spans:
                    qb = q_ref[0, 0, j, qrows, :]
                    dob = do_ref[0, 0, j, qrows, :]
                    pt = jnp.exp(_dot(kb, qb, NT))
                    if diag is not None:
                        pt = jnp.where(key_i + diag <= qry_i, pt, 0.0)
                    dsb = (pt * _dot(vb, dob, NT)).astype(BF16)
                    dv_part = _dot(pt.astype(BF16), dob, NN)
                    dk_part = _dot(dsb, qb, NN)
                    dv = dv_part if dv is None else dv + dv_part
                    dk = dk_part if dk is None else dk + dk_part
                    dq_part = _dot(ktb, dsb, NN)
                    if kj == 0:
                        dqt_acc[j, :, qrows] = dq_part
                    else:
                        dqt_acc[j, :, qrows] += dq_part
                dk_s[j, krows, :] = dk
                dv_s[j, krows, :] = dv

        for kj in range(nk):
            key_block(kj)

        def finish(i, _):
            rows = pl.ds(pl.multiple_of(i * tq, tq), tq)
            dq = [jnp.transpose(dqt_acc[j, :, rows]) for j in range(2)]
            dk = [dk_s[j, rows, :] for j in range(2)]
            dv = [dv_s[j, rows, :] for j in range(2)]
            dcol = [dq[j][:, LANE_CQ:LANE_CQ + 1] - dk[j][:, LANE_CK:LANE_CK + 1] for j in range(2)]
            dq = [dq[j] * scale for j in range(2)]
            for out_ref, val in ((dq_ref, dq), (dk_ref, dk), (dv_ref, dv)):
                merged = jnp.where(low, val[0], pltpu.roll(val[1], HEAD_DIM, 1))
                out_ref[0, rows, :] = merged.astype(BF16)
            dc_ref[0, 0, rows, :] = jnp.where(lane == 0, dcol[0], jnp.where(lane == 1, dcol[1], 0.0))
            return 0

        lax.fori_loop(0, nq, finish, 0)

    tok = pl.BlockSpec((1, 1, 2, s, LANES), lambda bi, hp: (bi, hp, 0, 0, 0))
    tok_t = pl.BlockSpec((1, 1, 2, LANES, s), lambda bi, hp: (bi, hp, 0, 0, 0))
    pair_blk = pl.BlockSpec((1, s, LANES), lambda bi, hp: (bi, 0, hp))
    pair_shape = jax.ShapeDtypeStruct((b, s, a), BF16)
    return _call(
        body, name=name, grid=(b, pairs),
        in_specs=[tok, tok_t, tok, tok, tok],
        out_specs=[pair_blk, pair_blk, pair_blk,
                   pl.BlockSpec((1, 1, s, LANES), lambda bi, hp: (bi, hp, 0, 0))],
        out_shape=[pair_shape, pair_shape, pair_shape,
                   jax.ShapeDtypeStruct((b, pairs, s, LANES), F32)],
        scratch_shapes=[pltpu.VMEM((2, LANES, s), F32), pltpu.VMEM((2, s, LANES), F32),
                        pltpu.VMEM((2, s, LANES), F32)],
        semantics=("parallel", "parallel"))(ka, kat, va, qab, doa)


def _pool_tile(s):
    return min(256, s)


def _band(tb, window, shift):
    tgt = lax.broadcasted_iota(jnp.int32, (tb, tb), 0)
    src = lax.broadcasted_iota(jnp.int32, (tb, tb), 1) + shift
    return ((src <= tgt) & (src > tgt - window)).astype(BF16)


def _band_t(tb, window, shift):
    src = lax.broadcasted_iota(jnp.int32, (tb, tb), 0)
    tgt = lax.broadcasted_iota(jnp.int32, (tb, tb), 1) + shift
    return ((src <= tgt) & (src > tgt - window)).astype(BF16)


def _pool_fwd(pp, w_pool, scale, name):
    b, s, pw2 = pp.shape
    pw = pw2 // 2
    pg = pw // N_POOL_GROUPS
    tb = _pool_tile(s)
    nb = s // tb

    def body(u_ref, z_ref, w_ref, s_ref, o_ref):
        window = 2 << pl.program_id(1)
        band0 = _band(tb, window, 0)
        band1 = _band(tb, window, -tb)
        pos = lax.broadcasted_iota(jnp.int32, (tb, pg), 0)

        def block(i, _):
            rows = pl.ds(pl.multiple_of(i * tb, tb), tb)
            prev = pl.ds(pl.multiple_of(jnp.maximum(i - 1, 0) * tb, tb), tb)
            ub = u_ref[0, rows, :]
            up = u_ref[0, prev, :]
            up = jnp.where(i > 0, up, jnp.zeros_like(up))
            count = jnp.minimum(pos + i * tb + 1, window).astype(F32)
            pooled = (_dot(band0, ub, NN) + _dot(band1, up, NN)) / count - ub.astype(F32)
            mixed = _dot(pooled.astype(BF16), w_ref[0], NN) * s_ref[...]
            zv = z_ref[0, rows, :].astype(F32)
            o_ref[0, rows, :] = (mixed * zv * _sigmoid(zv)).astype(BF16)
            return 0

        lax.fori_loop(0, nb, block, 0)

    return _call(
        body, name=name, grid=(b, N_POOL_GROUPS),
        in_specs=[pl.BlockSpec((1, s, pg), lambda bi, g: (bi, 0, g)),
                  pl.BlockSpec((1, s, pg), lambda bi, g: (bi, 0, N_POOL_GROUPS + g)),
                  pl.BlockSpec((1, pg, pg), lambda bi, g: (g, 0, 0)),
                  pl.BlockSpec((1, pg), lambda bi, g: (0, g))],
        out_specs=pl.BlockSpec((1, s, pg), lambda bi, g: (bi, 0, g)),
        out_shape=jax.ShapeDtypeStruct((b, s, pw), BF16),
        semantics=("parallel", "parallel"))(pp, pp, w_pool, scale)


def _pool_bwd(pp, dcat, w_pool, scale, first_block, name):
    b, s, pw2 = pp.shape
    pw = pw2 // 2
    pg = pw // N_POOL_GROUPS
    tb = _pool_tile(s)
    nb = s // tb

    def body(u_ref, z_ref, d_ref, w_ref, s_ref, du_ref, dz_ref, dw_ref, ds_ref, dpool_s):
        @pl.when(pl.program_id(1) == 0)
        def _():
            dw_ref[...] = jnp.zeros_like(dw_ref)
            ds_ref[...] = jnp.zeros_like(ds_ref)

        window = 2 << pl.program_id(0)
        band0 = _band(tb, window, 0)
        band1 = _band(tb, window, -tb)
        band0_t = _band_t(tb, window, 0)
        band1_t = _band_t(tb, window, tb)
        pos = lax.broadcasted_iota(jnp.int32, (tb, pg), 0)

        def first(i, _):
            rows = pl.ds(pl.multiple_of(i * tb, tb), tb)
            prev = pl.ds(pl.multiple_of(jnp.maximum(i - 1, 0) * tb, tb), tb)
            ub = u_ref[0, rows, :]
            up = u_ref[0, prev, :]
            up = jnp.where(i > 0, up, jnp.zeros_like(up))
            count = jnp.minimum(pos + i * tb + 1, window).astype(F32)
            pooled = ((_dot(band0, ub, NN) + _dot(band1, up, NN)) / count
                      - ub.astype(F32)).astype(BF16)
            mixed = _dot(pooled, w_ref[0], NN)
            pm = mixed * s_ref[...]
            zv = z_ref[0, rows, :].astype(F32)
            sg = _sigmoid(zv)
            dpl = d_ref[0, rows, :].astype(F32)
            dpm = dpl * zv * sg
            dz_ref[0, rows, :] = (dpl * pm * sg * (1.0 + zv * (1.0 - sg))).astype(BF16)
            ds_ref[...] += jnp.sum(dpm * mixed, axis=0, keepdims=True)
            dmixed = (dpm * s_ref[...]).astype(BF16)
            dw_ref[0] += _dot(pooled, dmixed, TN)
            dpool_s[rows, :] = _dot(dmixed, w_ref[0], NT)
            return 0

        lax.fori_loop(0, nb, first, 0)

        def second(i, _):
            rows = pl.ds(pl.multiple_of(i * tb, tb), tb)
            nxt_i = jnp.minimum(i + 1, nb - 1)
            nxt = pl.ds(pl.multiple_of(nxt_i * tb, tb), tb)
            count = jnp.minimum(pos + i * tb + 1, window).astype(F32)
            count_n = jnp.minimum(pos + nxt_i * tb + 1, window).astype(F32)
            dpb = dpool_s[rows, :]
            cur = (dpb / count).astype(BF16)
            nx = dpool_s[nxt, :] / count_n
            nx = jnp.where(i < nb - 1, nx, 0.0).astype(BF16)
            du = _dot(band0_t, cur, NN) + _dot(band1_t, nx, NN) - dpb
            du_ref[0, rows, :] = du.astype(BF16)
            return 0

        lax.fori_loop(0, nb, second, 0)

    return _call(
        body, name=name, grid=(N_POOL_GROUPS, b),
        in_specs=[pl.BlockSpec((1, s, pg), lambda g, bi: (bi, 0, g)),
                  pl.BlockSpec((1, s, pg), lambda g, bi: (bi, 0, N_POOL_GROUPS + g)),
                  pl.BlockSpec((1, s, pg), lambda g, bi: (bi, 0, first_block + g)),
                  pl.BlockSpec((1, pg, pg), lambda g, bi: (g, 0, 0)),
                  pl.BlockSpec((1, pg), lambda g, bi: (0, g))],
        out_specs=[pl.BlockSpec((1, s, pg), lambda g, bi: (bi, 0, g)),
                   pl.BlockSpec((1, s, pg), lambda g, bi: (bi, 0, g)),
                   pl.BlockSpec((1, pg, pg), lambda g, bi: (g, 0, 0)),
                   pl.BlockSpec((1, pg), lambda g, bi: (0, g))],
        out_shape=[jax.ShapeDtypeStruct((b, s, pw), BF16), jax.ShapeDtypeStruct((b, s, pw), BF16),
                   jax.ShapeDtypeStruct((N_POOL_GROUPS, pg, pg), F32),
                   jax.ShapeDtypeStruct((1, pw), F32)],
        scratch_shapes=[pltpu.VMEM((s, pg), F32)],
        semantics=("parallel", "arbitrary"))(pp, pp, dcat, w_pool, scale)


def _adamw(recvs, sent, me, w, m, v, name):
    depth = len(recvs)
    r, c = w.shape[1:]
    tr = min(128, r)
    nb = r // tr
    c1 = 1.0 - ADAM_B1 ** ADAM_STEP
    c2 = 1.0 - ADAM_B2 ** ADAM_STEP
    slotted = sent[0].ndim == 3

    def body(me_ref, *refs):
        recv_refs, own_refs = refs[:depth], refs[depth:2 * depth]
        w_ref, m_ref, v_ref, g_ref, d_ref, nm_ref, nv_ref = refs[2 * depth:]
        me = me_ref[0]
        for layer in range(depth):
            @pl.when(pl.program_id(0) == layer)
            def _(layer=layer):
                own = (own_refs[layer][0] if slotted else own_refs[layer][...]).astype(F32)
                g = jnp.where(me == 0, own, recv_refs[layer][0].astype(F32))
                for sl in range(1, N_DEV):
                    g = g + jnp.where(me == sl, own, recv_refs[layer][sl].astype(F32))
                mn = ADAM_B1 * m_ref[0] + (1.0 - ADAM_B1) * g
                vn = ADAM_B2 * v_ref[0] + (1.0 - ADAM_B2) * (g * g)
                m_hat = mn / c1
                v_hat = vn / c2
                g_ref[0] = g
                d_ref[0] = -ADAM_LR * (m_hat / (jnp.sqrt(v_hat) + ADAM_EPS) + ADAM_WD * w_ref[0])
                nm_ref[0] = mn
                nv_ref[0] = vn

    def blk(layer):
        return lambda l, i: jnp.clip(i + (l - layer) * nb, 0, nb - 1)

    in_specs = [pl.BlockSpec((N_DEV, tr, c), lambda l, i, me_ref, f=blk(layer): (0, f(l, i), 0))
                for layer in range(depth)]
    if slotted:
        in_specs += [pl.BlockSpec((1, tr, c),
                                  lambda l, i, me_ref, f=blk(layer): (me_ref[0], f(l, i), 0))
                     for layer in range(depth)]
    else:
        in_specs += [pl.BlockSpec((tr, c), lambda l, i, me_ref, f=blk(layer): (f(l, i), 0))
                     for layer in range(depth)]
    row = pl.BlockSpec((1, tr, c), lambda l, i, me_ref: (l, i, 0))
    return pl.pallas_call(
        body, name=name, out_shape=[jax.ShapeDtypeStruct((depth, r, c), F32)] * 4,
        grid_spec=pltpu.PrefetchScalarGridSpec(
            num_scalar_prefetch=1, grid=(depth, nb), in_specs=in_specs + [row, row, row],
            out_specs=[row] * 4),
        compiler_params=pltpu.CompilerParams(dimension_semantics=("arbitrary", "arbitrary"),
                                             vmem_limit_bytes=VMEM_LIMIT_BYTES),
    )(me, *recvs, *sent, w, m, v)


def _pack_w_in(gathered, a, heads, pw):
    d = gathered.shape[1]
    w_full = jnp.transpose(gathered, (1, 0, 2)).reshape(d, -1)
    wf = jnp.pad(w_full[:, 4 * a:4 * a + heads], ((0, 0), (0, LANES - heads)))
    return w_full, w_full[:, 4 * a + heads:], wf


def _unpack_dw_in(parts, heads):
    dq, dk, dv, dz, dwf, du, dzp = parts
    d = dq.shape[0]
    full = jnp.concatenate([dq, dk, dv, dz, dwf[:, :heads], du, dzp], axis=1)
    return jnp.transpose(full.reshape(d, N_DEV, -1), (1, 0, 2))


def kernel(x, p, norm_pre, norm_post, w_in, b_f, w_pool, pool_scale, w_out, w_pg, w_pe, loss_target, m_norm_pre, m_norm_post, m_w_in, m_b_f, m_w_pool, m_pool_scale, m_w_out, m_w_pg, m_w_pe, v_norm_pre, v_norm_post, v_w_in, v_b_f, v_w_pool, v_pool_scale, v_w_out, v_w_pg, v_w_pe):
    depth = w_in.shape[0]
    b, s, d = x.shape
    t = b * s
    heads = b_f.shape[1]
    a = heads * HEAD_DIM
    pairs = a // LANES
    pw = pool_scale.shape[1]
    pg = pw // N_POOL_GROUPS
    ple = p.shape[-1]
    mix_w = a + pw

    me = 4 * lax.axis_index("x") + 2 * lax.axis_index("y") + lax.axis_index("c")
    shard = {
        "w_in": [w_in[i].astype(BF16) for i in range(depth)],
        "w_pool": [w_pool[i].reshape(N_POOL_GROUPS * (pg // N_DEV), pg).astype(BF16)
                   for i in range(depth)],
        "w_out": [w_out[i].astype(BF16) for i in range(depth)],
        "w_pg": [w_pg[i].astype(BF16) for i in range(depth)],
        "w_pe": [w_pe[i].astype(BF16) for i in range(depth)],
    }
    names = list(shard)
    rest = names[1:]

    def unpack_rest(lands, layer, which):
        g = {nm: _with_own(ld, shard[nm][layer], me) for nm, ld in zip(which, lands)}
        g_pool = g["w_pool"].reshape(N_DEV, N_POOL_GROUPS, pg // N_DEV, pg)
        return dict(wpool=jnp.transpose(g_pool, (1, 0, 2, 3)).reshape(N_POOL_GROUPS, pg, pg),
                    wout=g["w_out"].reshape(mix_w, d), wpg=g["w_pg"].reshape(d, d),
                    wpe=jnp.transpose(g["w_pe"], (1, 0, 2)).reshape(ple, d))

    g_in0 = _gather_two_level(shard["w_in"][0], "gather_w_in0")
    rest0, tok_rest0 = _exchange_start([(shard[nm][0], False) for nm in rest], g_in0,
                                       "gather_rest0_start")
    later, tok = [], tok_rest0
    for i in range(1, depth):
        hdl, tk_i = _exchange_start([(shard[nm][i], False) for nm in names], g_in0,
                                    "gather_layer%d_start" % i)
        later.append(hdl)
        tok = tok + tk_i

    h = x.reshape(t, d)
    saved = []
    layers = []
    for i in range(depth):
        sv = dict(h=h)
        g_pre = norm_pre[i:i + 1]
        g_post = norm_post[i:i + 1]
        bf = jnp.pad(b_f[i:i + 1], ((0, 0), (0, LANES - heads)))
        scale = pool_scale[i:i + 1]
        if i == 0:
            lw = dict(zip(("wa", "wp", "wf"), _pack_w_in(g_in0, a, heads, pw)))
            g_pre = g_pre + tok
        else:
            lands = _exchange_wait(later[i - 1], h, "gather_layer%d_wait" % i)
            g_in = _with_own(lands[0], shard["w_in"][i], me)
            lw = dict(zip(("wa", "wp", "wf"), _pack_w_in(g_in, a, heads, pw)))
            lw.update(unpack_rest(lands[1:], i, rest))
        hn = _rms_fwd(h, g_pre, "rms_pre")
        pa = _matmul([(hn, lw["wa"])], "nn", BF16, "proj_attn", n_dim=4 * a, tn=2048,
                     n_outer=True).reshape(b, s, 4 * a)
        pp = _matmul([(hn, lw["wp"])], "nn", BF16, "proj_pool", tn=2048,
                     n_outer=True).reshape(b, s, 2 * pw)
        fl = _matmul([(hn, lw["wf"])], "nn", F32, "proj_gate").reshape(b, s, LANES)
        c = _gates_fwd(fl, bf, "gates_fwd")
        qa, ka, kat, va, vt = _attn_prep_fwd(pa, c, "attn_prep_fwd")
        o, ga, lse = _attn_fwd(qa, ka, vt, pa, "attn_fwd")
        if i == 0:
            lw.update(unpack_rest(_exchange_wait(rest0, lse, "gather_rest0_wait"), 0, rest))
        layers.append(lw)
        gp = _pool_fwd(pp, lw["wpool"], scale, "pool_fwd")
        ga2 = ga.reshape(t, a)
        gp2 = gp.reshape(t, pw)
        mix = _matmul([(ga2, lw["wout"], 0, 0), (gp2, lw["wout"], a, 0)], "nn", F32, "mix_out")
        h1, h1b = _post_fwd(h, mix, g_post, "post_fwd")
        pb = p[i].reshape(t, ple).astype(BF16)
        gpre = _matmul([(h1b, lw["wpg"])], "nn", F32, "ple_gate")
        e = _matmul([(pb, lw["wpe"])], "nn", F32, "ple_embed")
        h = _ple_fwd(h1, gpre, e, "ple_fwd")
        sv.update(hn=hn, pa=pa, pp=pp, fl=fl, bf=bf, qa=qa, ka=ka, kat=kat, va=va, o=o, lse=lse, ga=ga2,
                  gp=gp2, mix=mix,
                  h1b=h1b, pb=pb, gpre=gpre, e=e, g_pre=g_pre, g_post=g_post, scale=scale)
        saved.append(sv)

    dh, sq = _loss_bwd(h, loss_target.reshape(t, d), "loss")
    loss = lax.psum(0.5 * jnp.sum(sq) / d, MESH_AXES)

    big = {nm: [None] * depth for nm in names}
    small = {nm: [None] * depth for nm in ("norm_pre", "norm_post", "b_f", "pool_scale")}
    grad_handles = [None] * depth
    rest_handles = [None] * depth
    for i in reversed(range(depth)):
        lw, sv = layers[i], saved[i]
        de, dpre = _ple_bwd(dh, sv["gpre"], sv["e"], "ple_bwd")
        dwpe = _matmul([(sv["pb"], de)], "tn", BF16, "dw_pe", tm=1024)
        dwpg = _matmul([(sv["h1b"], dpre)], "tn", BF16, "dw_pg", tm=1024)
        t1 = _matmul([(dpre, lw["wpg"])], "nt", F32, "d_h1")
        dh1, dmix, dg_post = _post_bwd(dh, t1, sv["mix"], sv["g_post"], "post_bwd")
        dwout = jnp.concatenate(
            [_matmul([(sv["ga"], dmix)], "tn", BF16, "dw_out_attn", tm=1024),
             _matmul([(sv["gp"], dmix)], "tn", BF16, "dw_out_pool", tm=1024)], axis=0)
        dcat = _matmul([(dmix, lw["wout"])], "nt", BF16, "d_cat", tn=2048).reshape(b, s, mix_w)
        du, dzp, dwpool, dscale = _pool_bwd(sv["pp"], dcat, lw["wpool"], sv["scale"], a // pg,
                                            "pool_bwd")
        big["w_pool"][i] = jnp.transpose(
            dwpool.astype(BF16).reshape(N_POOL_GROUPS, N_DEV, pg // N_DEV, pg), (1, 0, 2, 3)
        ).reshape(N_DEV, N_POOL_GROUPS * (pg // N_DEV), pg)
        big["w_out"][i] = dwout.reshape(N_DEV, mix_w // N_DEV, d)
        big["w_pg"][i] = dwpg.reshape(N_DEV, d // N_DEV, d)
        big["w_pe"][i] = jnp.transpose(dwpe.reshape(ple, N_DEV, d // N_DEV), (1, 0, 2))
        rest_handles[i], tok = _exchange_start([(big[nm][i], True) for nm in rest], du,
                                               "grads_rest%d_start" % i)
        qab, doa, dz = _attn_prep_bwd(dcat, sv["pa"], sv["o"], sv["lse"] + tok, sv["qa"],
                                      "attn_prep_bwd")
        dq, dk, dv, dcp = _attn_bwd(sv["ka"], sv["kat"], sv["va"], qab, doa, "attn_bwd")
        dc = jnp.transpose(dcp[..., :2], (0, 2, 1, 3)).reshape(b, s, heads)
        dc = jnp.pad(dc, ((0, 0), (0, 0), (0, LANES - heads)))
        dfl, dbf = _gates_bwd(dc, sv["fl"], sv["bf"], heads, "gates_bwd")
        dproj = [g_.reshape(t, -1) for g_ in (dq, dk, dv, dz, dfl, du, dzp)]
        dw_parts = [_matmul([(sv["hn"], g_)], "tn", BF16, "dw_in_%d" % n_, tm=1024)
                    for n_, g_ in enumerate(dproj)]

        big["w_in"][i] = _unpack_dw_in(dw_parts, heads)
        grad_handles[i], tok = _exchange_start([(big["w_in"][i], True)], dw_parts[-1],
                                               "grads_w_in%d_start" % i)

        dq2, dk2, dv2, dz2, dfl2, du2, dzp2 = dproj
        dhn = _matmul([(dq2, lw["wa"], 0, 0), (dk2, lw["wa"], a, 0), (dv2, lw["wa"], 2 * a, 0),
                       (dz2, lw["wa"], 3 * a, 0), (du2, lw["wp"], 0, 0), (dzp2, lw["wp"], pw, 0),
                       (dfl2, lw["wf"] + tok.astype(BF16), 0, 0)], "nt", F32, "d_hn")
        dh, dg_pre = _pre_bwd(sv["h"], dhn, dh1, sv["g_pre"] + tok, "pre_bwd")
        small["norm_pre"][i] = dg_pre
        small["norm_post"][i] = dg_post
        small["b_f"][i] = jnp.sum(dbf, axis=0)
        small["pool_scale"][i] = dscale
    grad_x = dh.reshape(b, s, d)

    width = max(d, pw)
    small_names = ("norm_pre", "norm_post", "pool_scale", "b_f")

    def small_rows(get):
        rows = []
        for nm in small_names:
            for i in range(depth):
                v_ = get(nm, i)
                rows.append(jnp.pad(v_, ((0, 0), (0, width - v_.shape[1]))))
        return jnp.concatenate(rows, axis=0)

    small_g = small_rows(lambda nm, i: small[nm][i])
    (small_recv,) = _exchange([(small_g, False)], "exchange_small")
    me1 = jnp.reshape(me, (1,)).astype(jnp.int32)

    weights = dict(norm_pre=norm_pre, norm_post=norm_post, w_in=w_in, b_f=b_f, w_pool=w_pool,
                   pool_scale=pool_scale, w_out=w_out, w_pg=w_pg, w_pe=w_pe)
    mom1 = dict(norm_pre=m_norm_pre, norm_post=m_norm_post, w_in=m_w_in, b_f=m_b_f, w_pool=m_w_pool,
                pool_scale=m_pool_scale, w_out=m_w_out, w_pg=m_w_pg, w_pe=m_w_pe)
    mom2 = dict(norm_pre=v_norm_pre, norm_post=v_norm_post, w_in=v_w_in, b_f=v_b_f, w_pool=v_w_pool,
                pool_scale=v_pool_scale, w_out=v_w_out, w_pg=v_w_pg, w_pe=v_w_pe)

    results = {}

    def update(nm, recvs):
        shp = weights[nm].shape
        sent = [big[nm][i] for i in range(depth)]
        flat = lambda arr: arr.reshape((depth,) + sent[0].shape[1:])
        outs = _adamw(recvs, sent, me1, flat(weights[nm]), flat(mom1[nm]), flat(mom2[nm]),
                      "adamw_" + nm)
        results[nm] = [o_.reshape(shp) for o_ in outs]
        return outs[0]

    got_rest = [_exchange_wait(rest_handles[i], dh, "grads_rest%d_wait" % i) for i in range(depth)]
    for j, nm in enumerate(rest):
        last = update(nm, [got_rest[i][j] for i in range(depth)])

    small_w = small_rows(lambda nm, i: weights[nm][i:i + 1])[None]
    small_m = small_rows(lambda nm, i: mom1[nm][i:i + 1])[None]
    small_v = small_rows(lambda nm, i: mom2[nm][i:i + 1])[None]
    outs = _adamw([small_recv], [small_g], me1, small_w, small_m, small_v, "adamw_small")
    for j, nm in enumerate(small_names):
        cols = weights[nm].shape[1]
        results[nm] = [o_[0, j * depth:(j + 1) * depth, :cols] for o_ in outs]

    got_w_in = [_exchange_wait(grad_handles[i], last + outs[0][0, 0, 0], "grads_w_in%d_wait" % i)[0]
                for i in range(depth)]
    update("w_in", got_w_in)

    order = ("norm_pre", "norm_post", "w_in", "b_f", "w_pool", "pool_scale", "w_out", "w_pg", "w_pe")
    return (loss, grad_x, *[results[nm][0] for nm in order], *[results[nm][1] for nm in order],
            *[results[nm][2] for nm in order], *[results[nm][3] for nm in order])
```

```python
import functools
import math

import jax
import jax.numpy as jnp
from jax import lax
from jax.experimental import pallas as pl
from jax.experimental.pallas import tpu as pltpu

N_DEV = 8
MESH_AXES = ("x", "y", "c")
HEAD_DIM = 64
LANES = 128
N_POOL_GROUPS = 4
EPS = 1e-6
ADAM_LR = 0.001
ADAM_B1 = 0.9
ADAM_B2 = 0.999
ADAM_EPS = 1e-08
ADAM_WD = 0.01
ADAM_STEP = 10
VMEM_LIMIT_BYTES = 56 * 1024 * 1024
F32 = jnp.float32
BF16 = jnp.bfloat16
NEG_INF = float("-inf")


def _call(body, *, name, grid, in_specs, out_specs, out_shape, scratch_shapes=(), semantics=None):
    return pl.pallas_call(
        body, name=name, grid=grid, in_specs=in_specs, out_specs=out_specs, out_shape=out_shape,
        scratch_shapes=list(scratch_shapes),
        compiler_params=pltpu.CompilerParams(dimension_semantics=semantics,
                                             vmem_limit_bytes=VMEM_LIMIT_BYTES))


def _sigmoid(z):
    return 1.0 / (1.0 + jnp.exp(-z))


def _dot(a, b, dims):
    return lax.dot_general(a, b, (dims, ((), ())), preferred_element_type=F32)


NN = ((1,), (0,))
NT = ((1,), (1,))
TN = ((0,), (0,))


def _exchange(items, name):
    n = len(items)
    modes = [s for _, s in items]
    out_shapes = []
    for a, s in items:
        shp = a.shape[1:] if s else a.shape
        out_shapes.append(jax.ShapeDtypeStruct((N_DEV,) + tuple(shp), a.dtype))

    def body(*refs):
        ins = refs[:n]
        outs = refs[n:2 * n]
        send_sems, recv_sems, local_sems = refs[2 * n:]
        x, y, c = (lax.axis_index(ax) for ax in MESH_AXES)
        me = 4 * x + 2 * y + c
        started = []
        for i in range(n):
            mine = ins[i].at[me] if modes[i] else ins[i]
            loc = pltpu.make_async_copy(mine, outs[i].at[me], local_sems.at[i])
            loc.start()
            started.append(loc)
        remote = []
        for k in range(1, N_DEV):
            px = x ^ ((k >> 2) & 1)
            py = y ^ ((k >> 1) & 1)
            pc = c ^ (k & 1)
            peer = me ^ k
            for i in range(n):
                src = ins[i].at[peer] if modes[i] else ins[i]
                cp = pltpu.make_async_remote_copy(
                    src_ref=src, dst_ref=outs[i].at[me],
                    send_sem=send_sems.at[i, k - 1], recv_sem=recv_sems.at[i, k - 1],
                    device_id=(px, py, pc), device_id_type=pl.DeviceIdType.MESH)
                cp.start()
                remote.append(cp)
        for cp in remote:
            cp.wait()
        for loc in started:
            loc.wait()

    hbm = pl.BlockSpec(memory_space=pltpu.HBM)
    return pl.pallas_call(
        body, name=name, out_shape=out_shapes,
        in_specs=[hbm] * n, out_specs=[hbm] * n,
        scratch_shapes=[pltpu.SemaphoreType.DMA((n, N_DEV - 1)),
                        pltpu.SemaphoreType.DMA((n, N_DEV - 1)),
                        pltpu.SemaphoreType.DMA((n,))],
    )(*[a for a, _ in items])


def _gather_two_level(shard, name):
    def body(x_ref, out_ref, send_sems, recv_sems, local_sem):
        x, y, c = (lax.axis_index(ax) for ax in MESH_AXES)
        sibling = (x, y, 1 - c)
        chips = [(1 - x, y), (x, 1 - y), (1 - x, 1 - y)]

        def slot(px, py, pc):
            return out_ref.at[4 * px + 2 * py + pc]

        def copy(k, block, to, src=None):
            return pltpu.make_async_remote_copy(
                src_ref=slot(*block) if src is None else src, dst_ref=slot(*block),
                send_sem=send_sems.at[k], recv_sem=recv_sems.at[k],
                device_id=to, device_id_type=pl.DeviceIdType.MESH)

        mine = pltpu.make_async_copy(x_ref, slot(x, y, c), local_sem)
        mine.start()
        first = [copy(0, (x, y, c), sibling, src=x_ref)]
        first += [copy(1 + j, (x, y, c), (*chip, c), src=x_ref) for j, chip in enumerate(chips)]
        for cp in first:
            cp.start()
        passed = [copy(4 + j, (*chip, c), sibling) for j, chip in enumerate(chips)]
        for j, chip in enumerate(chips):
            copy(1 + j, (*chip, c), (x, y, c)).wait_recv()
            passed[j].start()
        copy(0, (x, y, 1 - c), (x, y, c)).wait_recv()
        for j, chip in enumerate(chips):
            copy(4 + j, (*chip, 1 - c), (x, y, c)).wait_recv()
        for cp in first + passed:
            cp.wait_send()
        mine.wait()

    hbm = pl.BlockSpec(memory_space=pltpu.HBM)
    return pl.pallas_call(
        body, name=name, out_shape=jax.ShapeDtypeStruct((N_DEV,) + shard.shape, shard.dtype),
        in_specs=[hbm], out_specs=hbm,
        scratch_shapes=[pltpu.SemaphoreType.DMA((N_DEV - 1,)), pltpu.SemaphoreType.DMA((N_DEV - 1,)),
                        pltpu.SemaphoreType.DMA],
    )(shard)


def _peer_copies(srcs, lands, modes, send_sems, recv_sems):
    x, y, c = (lax.axis_index(ax) for ax in MESH_AXES)
    me = 4 * x + 2 * y + c
    copies = []
    for k in range(1, N_DEV):
        peer_id = (x ^ ((k >> 2) & 1), y ^ ((k >> 1) & 1), c ^ (k & 1))
        for i, scatter in enumerate(modes):
            src = srcs[i].at[me ^ k] if scatter else srcs[i]
            pair = i * (N_DEV - 1) + k - 1
            copies.append(pltpu.make_async_remote_copy(
                src_ref=src, dst_ref=lands[i].at[me],
                send_sem=send_sems.at[pair], recv_sem=recv_sems.at[pair],
                device_id=peer_id, device_id_type=pl.DeviceIdType.MESH))
    return copies


def _exchange_start(items, after, name):
    n = len(items)
    modes = [s for _, s in items]
    srcs = [pltpu.with_memory_space_constraint(a, pltpu.HBM) for a, _ in items]
    lands = []
    for a, s in items:
        shp = (N_DEV,) + tuple(a.shape[1:] if s else a.shape)
        lands.append(pltpu.with_memory_space_constraint(lax.empty(shp, a.dtype), pltpu.HBM))

    def body(*refs):
        send_sems, recv_sems = refs[2 * n + 1], refs[2 * n + 2]
        token = refs[-1]
        for cp in _peer_copies(refs[:n], refs[n:2 * n], modes, send_sems, recv_sems):
            cp.start()
        token[...] = jnp.zeros_like(token)

    hbm = pl.BlockSpec(memory_space=pltpu.HBM)
    sem = pl.BlockSpec(memory_space=pltpu.SEMAPHORE)
    outs = pl.pallas_call(
        body, name=name,
        out_shape=(pltpu.SemaphoreType.DMA((n * (N_DEV - 1),)),
                   pltpu.SemaphoreType.DMA((n * (N_DEV - 1),)),
                   *[pltpu.HBM(a.shape, a.dtype) for a in srcs + lands],
                   jax.ShapeDtypeStruct((8, LANES), F32)),
        in_specs=[hbm] * (2 * n) + [pl.BlockSpec(memory_space=pl.ANY)],
        out_specs=(sem, sem, *([hbm] * (2 * n)), pl.BlockSpec(memory_space=pltpu.VMEM)),
        input_output_aliases={i: 2 + i for i in range(2 * n)},
        compiler_params=pltpu.CompilerParams(
            has_side_effects=pltpu.SideEffectType.DATAFLOW_SIDE_EFFECTING),
    )(*srcs, *lands, after)
    handle = (modes, outs[0], outs[1], list(outs[2:2 + n]), list(outs[2 + n:2 + 2 * n]))
    return handle, outs[-1][0, 0]


def _exchange_wait(handle, after, name):
    modes, send_sems, recv_sems, srcs, lands = handle
    n = len(modes)

    def body(*refs):
        for cp in _peer_copies(refs[:n], refs[n:2 * n], modes, refs[2 * n], refs[2 * n + 1]):
            cp.wait_send()
            cp.wait_recv()

    hbm = pl.BlockSpec(memory_space=pltpu.HBM)
    sem = pl.BlockSpec(memory_space=pltpu.SEMAPHORE)
    outs = pl.pallas_call(
        body, name=name,
        out_shape=tuple(pltpu.HBM(a.shape, a.dtype) for a in srcs + lands),
        in_specs=[hbm] * (2 * n) + [sem, sem, pl.BlockSpec(memory_space=pl.ANY)],
        out_specs=tuple([hbm] * (2 * n)),
        input_output_aliases={i: i for i in range(2 * n)},
        compiler_params=pltpu.CompilerParams(
            has_side_effects=pltpu.SideEffectType.DATAFLOW_SIDE_EFFECTING),
    )(*srcs, *lands, send_sems, recv_sems, after)
    return list(outs[n:])


def _with_own(slots, own, me):
    idx = lax.broadcasted_iota(jnp.int32, (N_DEV,) + (1,) * own.ndim, 0)
    return jnp.where(idx == me, own[None], slots)


def _matmul(pairs, mode, out_dtype, name, n_dim=None, tm=512, tn=1024, tk=1024, n_outer=False):
    dims = {"nn": NN, "nt": NT, "tn": TN}[mode]
    pairs = [tuple(pr) + (0, 0) * (len(pr) == 2) for pr in pairs]
    a0, b0 = pairs[0][:2]
    m_dim = a0.shape[1] if mode == "tn" else a0.shape[0]
    if n_dim is None:
        n_dim = b0.shape[0] if mode == "nt" else b0.shape[1]
    tm = min(tm, m_dim)
    tn = min(tn, n_dim)
    segs = []
    off = 0
    for a, _, k0, n0 in pairs:
        k_dim = a.shape[0] if mode == "tn" else a.shape[1]
        t = min(tk, k_dim)
        segs.append((off, k_dim // t, t, k0 // t, n0 // tn))
        off += k_dim // t
    nk = off
    n_pairs = len(pairs)

    def ij(g0, g1):
        return (g1, g0) if n_outer else (g0, g1)

    in_specs = []
    for (o, cnt, t, kb, nb) in segs:
        def kc(kk, o=o, cnt=cnt):
            return jnp.clip(kk - o, 0, cnt - 1)
        if mode == "tn":
            in_specs.append(pl.BlockSpec((t, tm), lambda g0, g1, kk, kc=kc: (kc(kk), ij(g0, g1)[0])))
        else:
            in_specs.append(pl.BlockSpec((tm, t), lambda g0, g1, kk, kc=kc: (ij(g0, g1)[0], kc(kk))))
        if mode == "nt":
            in_specs.append(pl.BlockSpec((tn, t), lambda g0, g1, kk, kc=kc, kb=kb, nb=nb:
                                         (nb + ij(g0, g1)[1], kb + kc(kk))))
        else:
            in_specs.append(pl.BlockSpec((t, tn), lambda g0, g1, kk, kc=kc, kb=kb, nb=nb:
                                         (kb + kc(kk), nb + ij(g0, g1)[1])))

    one_shot = all(sg[1] == 1 for sg in segs)

    def body_sum(*refs):
        total = _dot(refs[0][...], refs[1][...], dims)
        for idx in range(1, n_pairs):
            total = total + _dot(refs[2 * idx][...], refs[2 * idx + 1][...], dims)
        refs[2 * n_pairs][...] = total.astype(out_dtype)

    def body(*refs):
        out_ref = refs[2 * n_pairs]
        acc = refs[2 * n_pairs + 1]
        kk = pl.program_id(2)

        @pl.when(kk == 0)
        def _():
            acc[...] = jnp.zeros_like(acc)

        for idx, (o, cnt) in enumerate(sg[:2] for sg in segs):
            @pl.when((kk >= o) & (kk < o + cnt))
            def _(idx=idx):
                acc[...] += _dot(refs[2 * idx][...], refs[2 * idx + 1][...], dims)

        @pl.when(kk == nk - 1)
        def _():
            out_ref[...] = acc[...].astype(out_dtype)

    flat = [t for pr in pairs for t in pr[:2]]
    tiles = (m_dim // tm, n_dim // tn)
    return _call(body_sum if one_shot else body, name=name,
                 grid=ij(*tiles) + (1 if one_shot else nk,), in_specs=in_specs,
                 out_specs=pl.BlockSpec((tm, tn), lambda g0, g1, kk: ij(g0, g1)),
                 out_shape=jax.ShapeDtypeStruct((m_dim, n_dim), out_dtype),
                 scratch_shapes=[] if one_shot else [pltpu.VMEM((tm, tn), F32)],
                 semantics=("parallel", "parallel", "arbitrary"))(*flat)


def _row_tile(t):
    return min(512, t)


def _rms_fwd(h, g, name):
    t, d = h.shape
    tt = _row_tile(t)

    def body(h_ref, g_ref, o_ref):
        hv = h_ref[...]
        r = lax.rsqrt(jnp.mean(hv * hv, axis=-1, keepdims=True) + EPS)
        o_ref[...] = (hv * r * g_ref[...]).astype(BF16)

    row = pl.BlockSpec((tt, d), lambda i: (i, 0))
    vec = pl.BlockSpec((1, d), lambda i: (0, 0))
    return _call(body, name=name, grid=(t // tt,), in_specs=[row, vec], out_specs=row,
                 out_shape=jax.ShapeDtypeStruct((t, d), BF16), semantics=("parallel",))(h, g)


def _post_fwd(h, mix, g, name):
    t, d = h.shape
    tt = _row_tile(t)

    def body(h_ref, m_ref, g_ref, o_ref, ob_ref):
        mv = m_ref[...]
        r = lax.rsqrt(jnp.mean(mv * mv, axis=-1, keepdims=True) + EPS)
        h1 = h_ref[...] + mv * r * g_ref[...]
        o_ref[...] = h1
        ob_ref[...] = h1.astype(BF16)

    row = pl.BlockSpec((tt, d), lambda i: (i, 0))
    vec = pl.BlockSpec((1, d), lambda i: (0, 0))
    return _call(body, name=name, grid=(t // tt,), in_specs=[row, row, vec], out_specs=[row, row],
                 out_shape=[jax.ShapeDtypeStruct((t, d), F32), jax.ShapeDtypeStruct((t, d), BF16)],
                 semantics=("parallel",))(h, mix, g)


def _ple_fwd(h1, gpre, e, name):
    t, d = h1.shape
    tt = _row_tile(t)

    def body(h_ref, g_ref, e_ref, o_ref):
        o_ref[...] = h_ref[...] + _sigmoid(g_ref[...]) * e_ref[...]

    row = pl.BlockSpec((tt, d), lambda i: (i, 0))
    return _call(body, name=name, grid=(t // tt,), in_specs=[row, row, row], out_specs=row,
                 out_shape=jax.ShapeDtypeStruct((t, d), F32), semantics=("parallel",))(h1, gpre, e)


def _loss_bwd(y, target, name):
    t, d = y.shape
    tt = _row_tile(t)

    def body(y_ref, t_ref, dy_ref, s_ref):
        @pl.when(pl.program_id(0) == 0)
        def _():
            s_ref[...] = jnp.zeros_like(s_ref)
        diff = y_ref[...] - t_ref[...]
        dy_ref[...] = diff * (1.0 / d)
        s_ref[...] += jnp.sum(diff * diff, axis=0, keepdims=True)

    row = pl.BlockSpec((tt, d), lambda i: (i, 0))
    vec = pl.BlockSpec((1, d), lambda i: (0, 0))
    return _call(body, name=name, grid=(t // tt,), in_specs=[row, row], out_specs=[row, vec],
                 out_shape=[jax.ShapeDtypeStruct((t, d), F32), jax.ShapeDtypeStruct((1, d), F32)],
                 semantics=("arbitrary",))(y, target)


def _ple_bwd(dh2, gpre, e, name):
    t, d = dh2.shape
    tt = _row_tile(t)

    def body(d_ref, g_ref, e_ref, de_ref, dp_ref):
        gate = _sigmoid(g_ref[...])
        dv = d_ref[...]
        de_ref[...] = (dv * gate).astype(BF16)
        dp_ref[...] = (dv * e_ref[...] * gate * (1.0 - gate)).astype(BF16)

    row = pl.BlockSpec((tt, d), lambda i: (i, 0))
    return _call(body, name=name, grid=(t // tt,), in_specs=[row, row, row], out_specs=[row, row],
                 out_shape=[jax.ShapeDtypeStruct((t, d), BF16)] * 2,
                 semantics=("parallel",))(dh2, gpre, e)


def _matmul_rows(pairs, rows_in, vec_in, epilogue, row_dtypes, name, tm):
    n_pairs = len(pairs)
    m_dim = pairs[0][0].shape[0]
    n_dim = pairs[0][1].shape[0]
    tm = min(tm, m_dim)
    in_specs = []
    for a, _, k0 in pairs:
        k_dim = a.shape[1]
        in_specs.append(pl.BlockSpec((tm, k_dim), lambda i: (i, 0)))
        in_specs.append(pl.BlockSpec((n_dim, k_dim), lambda i, kb=k0 // k_dim: (0, kb)))
    row = pl.BlockSpec((tm, n_dim), lambda i: (i, 0))
    vec = pl.BlockSpec((1, n_dim), lambda i: (0, 0))
    n_rows = len(rows_in)

    def body(*refs):
        ops = refs[:2 * n_pairs]
        row_refs = refs[2 * n_pairs:2 * n_pairs + n_rows]
        vec_ref = refs[2 * n_pairs + n_rows]
        outs = refs[2 * n_pairs + n_rows + 1:]
        total = _dot(ops[0][...], ops[1][...], NT)
        for idx in range(1, n_pairs):
            total = total + _dot(ops[2 * idx][...], ops[2 * idx + 1][...], NT)
        results, partial = epilogue(total, [r[...] for r in row_refs], vec_ref[...])
        for out_ref, val in zip(outs[:-1], results):
            out_ref[...] = val.astype(out_ref.dtype)

        @pl.when(pl.program_id(0) == 0)
        def _():
            outs[-1][...] = jnp.zeros_like(outs[-1])
        outs[-1][...] += partial

    flat = [t_ for a, b_, _ in pairs for t_ in (a, b_)]
    return _call(body, name=name, grid=(m_dim // tm,),
                 in_specs=in_specs + [row] * n_rows + [vec],
                 out_specs=[row] * len(row_dtypes) + [vec],
                 out_shape=[jax.ShapeDtypeStruct((m_dim, n_dim), dt) for dt in row_dtypes]
                 + [jax.ShapeDtypeStruct((1, n_dim), F32)],
                 semantics=("arbitrary",))(*flat, *rows_in, vec_in)


def _post_bwd_epilogue(t1, rows, g):
    dh2, mv = rows
    dh1 = dh2 + t1
    r = lax.rsqrt(jnp.mean(mv * mv, axis=-1, keepdims=True) + EPS)
    w = dh1 * g
    dot = jnp.mean(w * mv, axis=-1, keepdims=True)
    dmix = r * w - mv * (r * r * r) * dot
    return (dh1, dmix), jnp.sum(dh1 * mv * r, axis=0, keepdims=True)


def _pre_bwd_epilogue(dhn, rows, g):
    hv, dh1 = rows
    r = lax.rsqrt(jnp.mean(hv * hv, axis=-1, keepdims=True) + EPS)
    w = dhn * g
    dot = jnp.mean(w * hv, axis=-1, keepdims=True)
    return (dh1 + r * w - hv * (r * r * r) * dot,), jnp.sum(dhn * hv * r, axis=0, keepdims=True)


def _split3(v):
    hi = v.astype(BF16)
    r1 = v - hi.astype(F32)
    mid = r1.astype(BF16)
    lo = (r1 - mid.astype(F32)).astype(BF16)
    return hi, mid, lo


def _gates_fwd(fl, bf, name):
    b, s, _ = fl.shape

    def body(f_ref, b_ref, c_ref):
        xv = f_ref[0] + b_ref[...]
        lf = jnp.minimum(xv, 0.0) - jnp.log(1.0 + jnp.exp(-jnp.abs(xv)))
        dst = lax.broadcasted_iota(jnp.int32, (s, s), 0)
        src = lax.broadcasted_iota(jnp.int32, (s, s), 1)
        lower = (src <= dst).astype(BF16)
        acc = jnp.zeros((s, LANES), F32)
        for part in _split3(lf):
            acc = acc + _dot(lower, part, NN)
        c_ref[0] = acc

    blk = pl.BlockSpec((1, s, LANES), lambda i: (i, 0, 0))
    return _call(body, name=name, grid=(b,),
                 in_specs=[blk, pl.BlockSpec((1, LANES), lambda i: (0, 0))],
                 out_specs=blk, out_shape=jax.ShapeDtypeStruct((b, s, LANES), F32),
                 semantics=("parallel",))(fl, bf)


def _gates_bwd(dc, fl, bf, heads, name):
    b, s, _ = fl.shape

    def body(d_ref, f_ref, b_ref, o_ref, db_ref):
        xv = f_ref[0] + b_ref[...]
        dst = lax.broadcasted_iota(jnp.int32, (s, s), 0)
        src = lax.broadcasted_iota(jnp.int32, (s, s), 1)
        later = (src >= dst).astype(BF16)
        dlf = jnp.zeros((s, LANES), F32)
        for part in _split3(d_ref[0]):
            dlf = dlf + _dot(later, part, NN)
        lane = lax.broadcasted_iota(jnp.int32, (s, LANES), 1)
        dfl = jnp.where(lane < heads, dlf * _sigmoid(-xv), 0.0)
        o_ref[0] = dfl.astype(BF16)
        db_ref[0] = jnp.sum(dfl, axis=0, keepdims=True)

    blk = pl.BlockSpec((1, s, LANES), lambda i: (i, 0, 0))
    return _call(body, name=name, grid=(b,),
                 in_specs=[blk, blk, pl.BlockSpec((1, LANES), lambda i: (0, 0))],
                 out_specs=[blk, pl.BlockSpec((1, 1, LANES), lambda i: (i, 0, 0))],
                 out_shape=[jax.ShapeDtypeStruct((b, s, LANES), BF16),
                            jax.ShapeDtypeStruct((b, 1, LANES), F32)],
                 semantics=("parallel",))(dc, fl, bf)


LANE_CQ = 64
LANE_CK = 67
LANE_LSE = 70
LANE_D = 64
N_PARTS = 3


def _attn_tiles(s):
    return min(512, s), min(256, s)


def _lanes_in(lane, first):
    return (lane >= first) & (lane < first + N_PARTS)


def _attn_prep_fwd(pa, c, name):
    b, s, a4 = pa.shape
    pairs = a4 // (4 * LANES)
    scale = 1.0 / math.sqrt(HEAD_DIM)

    def body(q_ref, k_ref, v_ref, c_ref, qa_ref, ka_ref, kat_ref, va_ref, vt_ref):
        hp = pl.program_id(1)
        cv = c_ref[0]
        vv = v_ref[0]
        lane = lax.broadcasted_iota(jnp.int32, (s, LANES), 1)
        r128 = lax.broadcasted_iota(jnp.int32, (LANES, LANES), 0)
        c128 = lax.broadcasted_iota(jnp.int32, (LANES, LANES), 1)
        ident = (r128 == c128).astype(BF16)
        for j in range(2):
            head = 2 * hp + j
            move128 = (r128 == c128 + HEAD_DIM * j) & (c128 < HEAD_DIM)
            cparts = _split3(jnp.sum(jnp.where(lane == head, cv, 0.0), axis=1, keepdims=True))
            qa = _dot(q_ref[0], jnp.where(move128, scale, 0.0).astype(BF16), NN)
            ka = _dot(k_ref[0], move128.astype(BF16), NN)
            for i in range(N_PARTS):
                qa = jnp.where(lane == LANE_CQ + i, cparts[i].astype(F32), qa)
                ka = jnp.where(lane == LANE_CK + i, -cparts[i].astype(F32), ka)
            qa = jnp.where(_lanes_in(lane, LANE_CK), 1.0, qa)
            ka = jnp.where(_lanes_in(lane, LANE_CQ) | _lanes_in(lane, LANE_LSE), 1.0, ka)
            va = _dot(vv, move128.astype(BF16), NN) + jnp.where(_lanes_in(lane, LANE_D), 1.0, 0.0)
            kab = ka.astype(BF16)
            qa_ref[0, 0, j] = qa.astype(BF16)
            ka_ref[0, 0, j] = kab
            kat_ref[0, 0, j] = _dot(ident, kab, NT).astype(BF16)
            va_ref[0, 0, j] = va.astype(BF16)
        vt_ref[0, 0] = _dot(ident, vv, NT).astype(BF16)

    col_blk = lambda cidx: pl.BlockSpec((1, s, LANES), lambda bi, hp: (bi, 0, cidx * pairs + hp))
    tok = pl.BlockSpec((1, 1, 2, s, LANES), lambda bi, hp: (bi, hp, 0, 0, 0))
    tok_t = pl.BlockSpec((1, 1, 2, LANES, s), lambda bi, hp: (bi, hp, 0, 0, 0))
    tok_shape = jax.ShapeDtypeStruct((b, pairs, 2, s, LANES), BF16)
    return _call(
        body, name=name, grid=(b, pairs),
        in_specs=[col_blk(0), col_blk(1), col_blk(2),
                  pl.BlockSpec((1, s, LANES), lambda bi, hp: (bi, 0, 0))],
        out_specs=[tok, tok, tok_t, tok,
                   pl.BlockSpec((1, 1, LANES, s), lambda bi, hp: (bi, hp, 0, 0))],
        out_shape=[tok_shape, tok_shape, jax.ShapeDtypeStruct((b, pairs, 2, LANES, s), BF16),
                   tok_shape, jax.ShapeDtypeStruct((b, pairs, LANES, s), BF16)],
        semantics=("parallel", "parallel"))(pa, pa, pa, c)


def _attn_fwd(qa, ka, vt, pa, name):
    b, pairs, _, s, _ = qa.shape
    a = pairs * LANES
    tq = min(1024, s)
    nq = s // tq

    def body(q_ref, k_ref, vt_ref, z_ref, o_ref, g_ref, lse_ref):
        key_i = lax.broadcasted_iota(jnp.int32, (tq, tq), 0)
        qry_i = lax.broadcasted_iota(jnp.int32, (tq, tq), 1)

        def query_block(c):
            past = tq * c
            heads_out = []
            for j in range(2):
                qv = q_ref[0, 0, j]
                vrows = slice(HEAD_DIM * j, HEAD_DIM * (j + 1))
                sd = _dot(k_ref[0, 0, j, past:past + tq, :], qv, NT)
                sd = jnp.where(key_i <= qry_i, sd, NEG_INF)
                m = jnp.max(sd, axis=0, keepdims=True)
                if c > 0:
                    sp = _dot(k_ref[0, 0, j, 0:past, :], qv, NT)
                    m = jnp.maximum(m, jnp.max(sp, axis=0, keepdims=True))
                pd = jnp.exp(sd - m)
                l = jnp.sum(pd, axis=0, keepdims=True)
                acc = _dot(vt_ref[0, 0, vrows, past:past + tq], pd.astype(BF16), NN)
                if c > 0:
                    pp = jnp.exp(sp - m)
                    l = l + jnp.sum(pp, axis=0, keepdims=True)
                    acc = acc + _dot(vt_ref[0, 0, vrows, 0:past], pp.astype(BF16), NN)
                heads_out.append(acc / l)
                lse_ref[0, 0, j:j + 1, :] = m + jnp.log(l)
            ov = jnp.transpose(jnp.concatenate(heads_out, axis=0))
            o_ref[0] = ov.astype(BF16)
            zv = z_ref[0].astype(F32)
            g_ref[0] = (ov * zv * _sigmoid(zv)).astype(BF16)

        for c in range(nq):
            pl.when(pl.program_id(2) == c)(functools.partial(query_block, c))

    return _call(
        body, name=name, grid=(b, pairs, s // tq),
        in_specs=[pl.BlockSpec((1, 1, 2, tq, LANES), lambda bi, hp, qi: (bi, hp, 0, qi, 0)),
                  pl.BlockSpec((1, 1, 2, s, LANES), lambda bi, hp, qi: (bi, hp, 0, 0, 0)),
                  pl.BlockSpec((1, 1, LANES, s), lambda bi, hp, qi: (bi, hp, 0, 0)),
                  pl.BlockSpec((1, tq, LANES), lambda bi, hp, qi: (bi, qi, 3 * pairs + hp))],
        out_specs=[pl.BlockSpec((1, tq, LANES), lambda bi, hp, qi: (bi, qi, hp)),
                   pl.BlockSpec((1, tq, LANES), lambda bi, hp, qi: (bi, qi, hp)),
                   pl.BlockSpec((1, 1, 2, tq), lambda bi, hp, qi: (bi, hp, 0, qi))],
        out_shape=[jax.ShapeDtypeStruct((b, s, a), BF16), jax.ShapeDtypeStruct((b, s, a), BF16),
                   jax.ShapeDtypeStruct((b, pairs, 2, s), F32)],
        semantics=("parallel", "parallel", "arbitrary"))(qa, ka, vt, pa)


def _attn_prep_bwd(dcat, pa, o, lse, qa, name):
    b, pairs, _, s, _ = qa.shape
    a = pairs * LANES
    sub = 16

    def body(da_ref, z_ref, o_ref, lse_ref, qa_ref, qab_ref, doa_ref, dz_ref):
        zv = z_ref[0].astype(F32)
        dav = da_ref[0].astype(F32)
        ov = o_ref[0].astype(F32)
        sg = _sigmoid(zv)
        dov = dav * zv * sg
        dz_ref[0] = (dav * ov * sg * (1.0 + zv * (1.0 - sg))).astype(BF16)
        prod = dov * ov
        dob = dov.astype(BF16)
        lane = lax.broadcasted_iota(jnp.int32, (s, LANES), 1)
        r128 = lax.broadcasted_iota(jnp.int32, (LANES, LANES), 0)
        c128 = lax.broadcasted_iota(jnp.int32, (LANES, LANES), 1)
        prow = lax.broadcasted_iota(jnp.int32, (sub, s), 0)
        srow = lax.broadcasted_iota(jnp.int32, (sub, LANES), 0)
        scol = lax.broadcasted_iota(jnp.int32, (sub, LANES), 1)
        place = ((scol == srow + LANE_LSE) & (srow < N_PARTS)).astype(BF16)
        for j in range(2):
            in_head = (lane >= HEAD_DIM * j) & (lane < HEAD_DIM * (j + 1))
            dparts = _split3(jnp.sum(jnp.where(in_head, prod, 0.0), axis=1, keepdims=True))
            move128 = ((r128 == c128 + HEAD_DIM * j) & (c128 < HEAD_DIM)).astype(BF16)
            doa = _dot(dob, move128, NN)
            for i in range(N_PARTS):
                doa = jnp.where(lane == LANE_D + i, -dparts[i].astype(F32), doa)
            doa_ref[0, 0, j] = doa.astype(BF16)
            lparts = _split3(lse_ref[0, 0, j:j + 1, :])
            pmat = jnp.zeros((sub, s), BF16)
            for i in range(N_PARTS):
                pmat = jnp.where(prow == i, lparts[i], pmat)
            lcol = _dot(pmat, place, TN)
            qab_ref[0, 0, j] = (qa_ref[0, 0, j].astype(F32) - lcol).astype(BF16)

    tok = pl.BlockSpec((1, 1, 2, s, LANES), lambda bi, hp: (bi, hp, 0, 0, 0))
    tok_shape = jax.ShapeDtypeStruct((b, pairs, 2, s, LANES), BF16)
    pair_blk = pl.BlockSpec((1, s, LANES), lambda bi, hp: (bi, 0, hp))
    return _call(
        body, name=name, grid=(b, pairs),
        in_specs=[pair_blk,
                  pl.BlockSpec((1, s, LANES), lambda bi, hp: (bi, 0, 3 * pairs + hp)),
                  pair_blk,
                  pl.BlockSpec((1, 1, 2, s), lambda bi, hp: (bi, hp, 0, 0)),
                  tok],
        out_specs=[tok, tok, pair_blk],
        out_shape=[tok_shape, tok_shape, jax.ShapeDtypeStruct((b, s, a), BF16)],
        semantics=("parallel", "parallel"))(dcat, pa, o, lse, qa)


def _attn_bwd(ka, kat, va, qab, doa, name):
    b, pairs, _, s, _ = ka.shape
    a = pairs * LANES
    tq, tk = _attn_tiles(s)
    ratio = tq // tk
    nq, nk = s // tq, s // tk
    scale = 1.0 / math.sqrt(HEAD_DIM)

    def body(k_ref, kt_ref, v_ref, q_ref, do_ref, dq_ref, dk_ref, dv_ref, dc_ref,
             dqt_acc, dk_s, dv_s):
        key_i = lax.broadcasted_iota(jnp.int32, (tk, tq), 0)
        qry_i = lax.broadcasted_iota(jnp.int32, (tk, tq), 1)
        lane = lax.broadcasted_iota(jnp.int32, (tq, LANES), 1)
        low = lane < HEAD_DIM

        def key_block(kj):
            krows = slice(kj * tk, (kj + 1) * tk)
            q0 = (kj // ratio) * tq
            spans = [(slice(q0, q0 + tq), kj * tk - q0)]
            if q0 + tq < s:
                spans.append((slice(q0 + tq, s), None))
            for j in range(2):
                kb = k_ref[0, 0, j, krows, :]
                vb = v_ref[0, 0, j, krows, :]
                ktb = kt_ref[0, 0, j, :, krows]
                dk = dv = None
                for qrows, diag in spans:
                    qb = q_ref[0, 0, j, qrows, :]
                    dob = do_ref[0, 0, j, qrows, :]
                    pt = jnp.exp(_dot(kb, qb, NT))
                    if diag is not None:
                        pt = jnp.where(key_i + diag <= qry_i, pt, 0.0)
                    dsb = (pt * _dot(vb, dob, NT)).astype(BF16)
                    dv_part = _dot(pt.astype(BF16), dob, NN)
                    dk_part = _dot(dsb, qb, NN)
                    dv = dv_part if dv is None else dv + dv_part
                    dk = dk_part if dk is None else dk + dk_part
                    dq_part = _dot(ktb, dsb, NN)
                    if kj == 0:
                        dqt_acc[j, :, qrows] = dq_part
                    else:
                        dqt_acc[j, :, qrows] += dq_part
                dk_s[j, krows, :] = dk
                dv_s[j, krows, :] = dv

        for kj in range(nk):
            key_block(kj)

        def finish(i, _):
            rows = pl.ds(pl.multiple_of(i * tq, tq), tq)
            dq = [jnp.transpose(dqt_acc[j, :, rows]) for j in range(2)]
            dk = [dk_s[j, rows, :] for j in range(2)]
            dv = [dv_s[j, rows, :] for j in range(2)]
            dcol = [dq[j][:, LANE_CQ:LANE_CQ + 1] - dk[j][:, LANE_CK:LANE_CK + 1] for j in range(2)]
            dq = [dq[j] * scale for j in range(2)]
            for out_ref, val in ((dq_ref, dq), (dk_ref, dk), (dv_ref, dv)):
                merged = jnp.where(low, val[0], pltpu.roll(val[1], HEAD_DIM, 1))
                out_ref[0, rows, :] = merged.astype(BF16)
            dc_ref[0, 0, rows, :] = jnp.where(lane == 0, dcol[0], jnp.where(lane == 1, dcol[1], 0.0))
            return 0

        lax.fori_loop(0, nq, finish, 0)

    tok = pl.BlockSpec((1, 1, 2, s, LANES), lambda bi, hp: (bi, hp, 0, 0, 0))
    tok_t = pl.BlockSpec((1, 1, 2, LANES, s), lambda bi, hp: (bi, hp, 0, 0, 0))
    pair_blk = pl.BlockSpec((1, s, LANES), lambda bi, hp: (bi, 0, hp))
    pair_shape = jax.ShapeDtypeStruct((b, s, a), BF16)
    return _call(
        body, name=name, grid=(b, pairs),
        in_specs=[tok, tok_t, tok, tok, tok],
        out_specs=[pair_blk, pair_blk, pair_blk,
                   pl.BlockSpec((1, 1, s, LANES), lambda bi, hp: (bi, hp, 0, 0))],
        out_shape=[pair_shape, pair_shape, pair_shape,
                   jax.ShapeDtypeStruct((b, pairs, s, LANES), F32)],
        scratch_shapes=[pltpu.VMEM((2, LANES, s), F32), pltpu.VMEM((2, s, LANES), F32),
                        pltpu.VMEM((2, s, LANES), F32)],
        semantics=("parallel", "parallel"))(ka, kat, va, qab, doa)


def _pool_tile(s):
    return min(256, s)


def _band(tb, window, shift):
    tgt = lax.broadcasted_iota(jnp.int32, (tb, tb), 0)
    src = lax.broadcasted_iota(jnp.int32, (tb, tb), 1) + shift
    return ((src <= tgt) & (src > tgt - window)).astype(BF16)


def _band_t(tb, window, shift):
    src = lax.broadcasted_iota(jnp.int32, (tb, tb), 0)
    tgt = lax.broadcasted_iota(jnp.int32, (tb, tb), 1) + shift
    return ((src <= tgt) & (src > tgt - window)).astype(BF16)


def _pool_fwd(pp, w_pool, scale, name):
    b, s, pw2 = pp.shape
    pw = pw2 // 2
    pg = pw // N_POOL_GROUPS
    tb = _pool_tile(s)
    nb = s // tb

    def body(u_ref, z_ref, w_ref, s_ref, o_ref):
        window = 2 << pl.program_id(1)
        band0 = _band(tb, window, 0)
        band1 = _band(tb, window, -tb)
        pos = lax.broadcasted_iota(jnp.int32, (tb, pg), 0)

        def block(i, _):
            rows = pl.ds(pl.multiple_of(i * tb, tb), tb)
            prev = pl.ds(pl.multiple_of(jnp.maximum(i - 1, 0) * tb, tb), tb)
            ub = u_ref[0, rows, :]
            up = u_ref[0, prev, :]
            up = jnp.where(i > 0, up, jnp.zeros_like(up))
            count = jnp.minimum(pos + i * tb + 1, window).astype(F32)
            pooled = (_dot(band0, ub, NN) + _dot(band1, up, NN)) / count - ub.astype(F32)
            mixed = _dot(pooled.astype(BF16), w_ref[0], NN) * s_ref[...]
            zv = z_ref[0, rows, :].astype(F32)
            o_ref[0, rows, :] = (mixed * zv * _sigmoid(zv)).astype(BF16)
            return 0

        lax.fori_loop(0, nb, block, 0)

    return _call(
        body, name=name, grid=(b, N_POOL_GROUPS),
        in_specs=[pl.BlockSpec((1, s, pg), lambda bi, g: (bi, 0, g)),
                  pl.BlockSpec((1, s, pg), lambda bi, g: (bi, 0, N_POOL_GROUPS + g)),
                  pl.BlockSpec((1, pg, pg), lambda bi, g: (g, 0, 0)),
                  pl.BlockSpec((1, pg), lambda bi, g: (0, g))],
        out_specs=pl.BlockSpec((1, s, pg), lambda bi, g: (bi, 0, g)),
        out_shape=jax.ShapeDtypeStruct((b, s, pw), BF16),
        semantics=("parallel", "parallel"))(pp, pp, w_pool, scale)


def _pool_bwd(pp, dcat, w_pool, scale, first_block, name):
    b, s, pw2 = pp.shape
    pw = pw2 // 2
    pg = pw // N_POOL_GROUPS
    tb = _pool_tile(s)
    nb = s // tb

    def body(u_ref, z_ref, d_ref, w_ref, s_ref, du_ref, dz_ref, dw_ref, ds_ref, dpool_s):
        @pl.when(pl.program_id(1) == 0)
        def _():
            dw_ref[...] = jnp.zeros_like(dw_ref)
            ds_ref[...] = jnp.zeros_like(ds_ref)

        window = 2 << pl.program_id(0)
        band0 = _band(tb, window, 0)
        band1 = _band(tb, window, -tb)
        band0_t = _band_t(tb, window, 0)
        band1_t = _band_t(tb, window, tb)
        pos = lax.broadcasted_iota(jnp.int32, (tb, pg), 0)

        def first(i, _):
            rows = pl.ds(pl.multiple_of(i * tb, tb), tb)
            prev = pl.ds(pl.multiple_of(jnp.maximum(i - 1, 0) * tb, tb), tb)
            ub = u_ref[0, rows, :]
            up = u_ref[0, prev, :]
            up = jnp.where(i > 0, up, jnp.zeros_like(up))
            count = jnp.minimum(pos + i * tb + 1, window).astype(F32)
            pooled = ((_dot(band0, ub, NN) + _dot(band1, up, NN)) / count
                      - ub.astype(F32)).astype(BF16)
            mixed = _dot(pooled, w_ref[0], NN)
            pm = mixed * s_ref[...]
            zv = z_ref[0, rows, :].astype(F32)
            sg = _sigmoid(zv)
            dpl = d_ref[0, rows, :].astype(F32)
            dpm = dpl * zv * sg
            dz_ref[0, rows, :] = (dpl * pm * sg * (1.0 + zv * (1.0 - sg))).astype(BF16)
            ds_ref[...] += jnp.sum(dpm * mixed, axis=0, keepdims=True)
            dmixed = (dpm * s_ref[...]).astype(BF16)
            dw_ref[0] += _dot(pooled, dmixed, TN)
            dpool_s[rows, :] = _dot(dmixed, w_ref[0], NT)
            return 0

        lax.fori_loop(0, nb, first, 0)

        def second(i, _):
            rows = pl.ds(pl.multiple_of(i * tb, tb), tb)
            nxt_i = jnp.minimum(i + 1, nb - 1)
            nxt = pl.ds(pl.multiple_of(nxt_i * tb, tb), tb)
            count = jnp.minimum(pos + i * tb + 1, window).astype(F32)
            count_n = jnp.minimum(pos + nxt_i * tb + 1, window).astype(F32)
            dpb = dpool_s[rows, :]
            cur = (dpb / count).astype(BF16)
            nx = dpool_s[nxt, :] / count_n
            nx = jnp.where(i < nb - 1, nx, 0.0).astype(BF16)
            du = _dot(band0_t, cur, NN) + _dot(band1_t, nx, NN) - dpb
            du_ref[0, rows, :] = du.astype(BF16)
            return 0

        lax.fori_loop(0, nb, second, 0)

    return _call(
        body, name=name, grid=(N_POOL_GROUPS, b),
        in_specs=[pl.BlockSpec((1, s, pg), lambda g, bi: (bi, 0, g)),
                  pl.BlockSpec((1, s, pg), lambda g, bi: (bi, 0, N_POOL_GROUPS + g)),
                  pl.BlockSpec((1, s, pg), lambda g, bi: (bi, 0, first_block + g)),
                  pl.BlockSpec((1, pg, pg), lambda g, bi: (g, 0, 0)),
                  pl.BlockSpec((1, pg), lambda g, bi: (0, g))],
        out_specs=[pl.BlockSpec((1, s, pg), lambda g, bi: (bi, 0, g)),
                   pl.BlockSpec((1, s, pg), lambda g, bi: (bi, 0, g)),
                   pl.BlockSpec((1, pg, pg), lambda g, bi: (g, 0, 0)),
                   pl.BlockSpec((1, pg), lambda g, bi: (0, g))],
        out_shape=[jax.ShapeDtypeStruct((b, s, pw), BF16), jax.ShapeDtypeStruct((b, s, pw), BF16),
                   jax.ShapeDtypeStruct((N_POOL_GROUPS, pg, pg), F32),
                   jax.ShapeDtypeStruct((1, pw), F32)],
        scratch_shapes=[pltpu.VMEM((s, pg), F32)],
        semantics=("parallel", "arbitrary"))(pp, pp, dcat, w_pool, scale)


def _adamw(recvs, sent, me, w, m, v, name):
    depth = len(recvs)
    r, c = w.shape[1:]
    tr = min(128, r)
    nb = r // tr
    c1 = 1.0 - ADAM_B1 ** ADAM_STEP
    c2 = 1.0 - ADAM_B2 ** ADAM_STEP
    slotted = sent[0].ndim == 3

    def body(me_ref, *refs):
        recv_refs, own_refs = refs[:depth], refs[depth:2 * depth]
        w_ref, m_ref, v_ref, g_ref, d_ref, nm_ref, nv_ref = refs[2 * depth:]
        me = me_ref[0]
        for layer in range(depth):
            @pl.when(pl.program_id(0) == layer)
            def _(layer=layer):
                own = (own_refs[layer][0] if slotted else own_refs[layer][...]).astype(F32)
                g = jnp.where(me == 0, own, recv_refs[layer][0].astype(F32))
                for sl in range(1, N_DEV):
                    g = g + jnp.where(me == sl, own, recv_refs[layer][sl].astype(F32))
                mn = ADAM_B1 * m_ref[0] + (1.0 - ADAM_B1) * g
                vn = ADAM_B2 * v_ref[0] + (1.0 - ADAM_B2) * (g * g)
                m_hat = mn / c1
                v_hat = vn / c2
                g_ref[0] = g
                d_ref[0] = -ADAM_LR * (m_hat / (jnp.sqrt(v_hat) + ADAM_EPS) + ADAM_WD * w_ref[0])
                nm_ref[0] = mn
                nv_ref[0] = vn

    def blk(layer):
        return lambda l, i: jnp.clip(i + (l - layer) * nb, 0, nb - 1)

    in_specs = [pl.BlockSpec((N_DEV, tr, c), lambda l, i, me_ref, f=blk(layer): (0, f(l, i), 0))
                for layer in range(depth)]
    if slotted:
        in_specs += [pl.BlockSpec((1, tr, c),
                                  lambda l, i, me_ref, f=blk(layer): (me_ref[0], f(l, i), 0))
                     for layer in range(depth)]
    else:
        in_specs += [pl.BlockSpec((tr, c), lambda l, i, me_ref, f=blk(layer): (f(l, i), 0))
                     for layer in range(depth)]
    row = pl.BlockSpec((1, tr, c), lambda l, i, me_ref: (l, i, 0))
    return pl.pallas_call(
        body, name=name, out_shape=[jax.ShapeDtypeStruct((depth, r, c), F32)] * 4,
        grid_spec=pltpu.PrefetchScalarGridSpec(
            num_scalar_prefetch=1, grid=(depth, nb), in_specs=in_specs + [row, row, row],
            out_specs=[row] * 4),
        compiler_params=pltpu.CompilerParams(dimension_semantics=("arbitrary", "arbitrary"),
                                             vmem_limit_bytes=VMEM_LIMIT_BYTES),
    )(me, *recvs, *sent, w, m, v)


def _pack_w_in(gathered, a, heads, pw):
    d = gathered.shape[1]
    w_full = jnp.transpose(gathered, (1, 0, 2)).reshape(d, -1)
    wf = jnp.pad(w_full[:, 4 * a:4 * a + heads], ((0, 0), (0, LANES - heads)))
    return w_full, w_full[:, 4 * a + heads:], wf


def _unpack_dw_in(parts, heads):
    dq, dk, dv, dz, dwf, du, dzp = parts
    d = dq.shape[0]
    full = jnp.concatenate([dq, dk, dv, dz, dwf[:, :heads], du, dzp], axis=1)
    return jnp.transpose(full.reshape(d, N_DEV, -1), (1, 0, 2))


def kernel(x, p, norm_pre, norm_post, w_in, b_f, w_pool, pool_scale, w_out, w_pg, w_pe, loss_target, m_norm_pre, m_norm_post, m_w_in, m_b_f, m_w_pool, m_pool_scale, m_w_out, m_w_pg, m_w_pe, v_norm_pre, v_norm_post, v_w_in, v_b_f, v_w_pool, v_pool_scale, v_w_out, v_w_pg, v_w_pe):
    depth = w_in.shape[0]
    b, s, d = x.shape
    t = b * s
    heads = b_f.shape[1]
    a = heads * HEAD_DIM
    pairs = a // LANES
    pw = pool_scale.shape[1]
    pg = pw // N_POOL_GROUPS
    ple = p.shape[-1]
    mix_w = a + pw

    me = 4 * lax.axis_index("x") + 2 * lax.axis_index("y") + lax.axis_index("c")
    shard = {
        "w_in": [w_in[i].astype(BF16) for i in range(depth)],
        "w_pool": [w_pool[i].reshape(N_POOL_GROUPS * (pg // N_DEV), pg).astype(BF16)
                   for i in range(depth)],
        "w_out": [w_out[i].astype(BF16) for i in range(depth)],
        "w_pg": [w_pg[i].astype(BF16) for i in range(depth)],
        "w_pe": [w_pe[i].astype(BF16) for i in range(depth)],
    }
    names = list(shard)
    rest = names[1:]

    def unpack_rest(lands, layer, which):
        g = {nm: _with_own(ld, shard[nm][layer], me) for nm, ld in zip(which, lands)}
        g_pool = g["w_pool"].reshape(N_DEV, N_POOL_GROUPS, pg // N_DEV, pg)
        return dict(wpool=jnp.transpose(g_pool, (1, 0, 2, 3)).reshape(N_POOL_GROUPS, pg, pg),
                    wout=g["w_out"].reshape(mix_w, d), wpg=g["w_pg"].reshape(d, d),
                    wpe=jnp.transpose(g["w_pe"], (1, 0, 2)).reshape(ple, d))

    g_in0 = _gather_two_level(shard["w_in"][0], "gather_w_in0")
    rest0, tok_rest0 = _exchange_start([(shard[nm][0], False) for nm in rest], g_in0,
                                       "gather_rest0_start")
    later, tok = [], tok_rest0
    for i in range(1, depth):
        hdl, tk_i = _exchange_start([(shard[nm][i], False) for nm in names], g_in0,
                                    "gather_layer%d_start" % i)
        later.append(hdl)
        tok = tok + tk_i

    h = x.reshape(t, d)
    saved = []
    layers = []
    for i in range(depth):
        sv = dict(h=h)
        g_pre = norm_pre[i:i + 1]
        g_post = norm_post[i:i + 1]
        bf = jnp.pad(b_f[i:i + 1], ((0, 0), (0, LANES - heads)))
        scale = pool_scale[i:i + 1]
        if i == 0:
            lw = dict(zip(("wa", "wp", "wf"), _pack_w_in(g_in0, a, heads, pw)))
            g_pre = g_pre + tok
        else:
            lands = _exchange_wait(later[i - 1], h, "gather_layer%d_wait" % i)
            g_in = _with_own(lands[0], shard["w_in"][i], me)
            lw = dict(zip(("wa", "wp", "wf"), _pack_w_in(g_in, a, heads, pw)))
            lw.update(unpack_rest(lands[1:], i, rest))
        hn = _rms_fwd(h, g_pre, "rms_pre")
        pa = _matmul([(hn, lw["wa"])], "nn", BF16, "proj_attn", n_dim=4 * a, tn=2048,
                     n_outer=True).reshape(b, s, 4 * a)
        pp = _matmul([(hn, lw["wp"])], "nn", BF16, "proj_pool", tn=2048,
                     n_outer=True).reshape(b, s, 2 * pw)
        fl = _matmul([(hn, lw["wf"])], "nn", F32, "proj_gate").reshape(b, s, LANES)
        c = _gates_fwd(fl, bf, "gates_fwd")
        qa, ka, kat, va, vt = _attn_prep_fwd(pa, c, "attn_prep_fwd")
        o, ga, lse = _attn_fwd(qa, ka, vt, pa, "attn_fwd")
        if i == 0:
            lw.update(unpack_rest(_exchange_wait(rest0, lse, "gather_rest0_wait"), 0, rest))
        layers.append(lw)
        gp = _pool_fwd(pp, lw["wpool"], scale, "pool_fwd")
        ga2 = ga.reshape(t, a)
        gp2 = gp.reshape(t, pw)
        mix = _matmul([(ga2, lw["wout"], 0, 0), (gp2, lw["wout"], a, 0)], "nn", F32, "mix_out")
        h1, h1b = _post_fwd(h, mix, g_post, "post_fwd")
        pb = p[i].reshape(t, ple).astype(BF16)
        gpre = _matmul([(h1b, lw["wpg"])], "nn", F32, "ple_gate")
        e = _matmul([(pb, lw["wpe"])], "nn", F32, "ple_embed")
        h = _ple_fwd(h1, gpre, e, "ple_fwd")
        sv.update(hn=hn, pa=pa, pp=pp, fl=fl, bf=bf, qa=qa, ka=ka, kat=kat, va=va, o=o, lse=lse, ga=ga2,
                  gp=gp2, mix=mix,
                  h1b=h1b, pb=pb, gpre=gpre, e=e, g_pre=g_pre, g_post=g_post, scale=scale)
        saved.append(sv)

    dh, sq = _loss_bwd(h, loss_target.reshape(t, d), "loss")
    loss = lax.psum(0.5 * jnp.sum(sq) / d, MESH_AXES)

    big = {nm: [None] * depth for nm in names}
    small = {nm: [None] * depth for nm in ("norm_pre", "norm_post", "b_f", "pool_scale")}
    grad_handles = [None] * depth
    rest_handles = [None] * depth
    for i in reversed(range(depth)):
        lw, sv = layers[i], saved[i]
        de, dpre = _ple_bwd(dh, sv["gpre"], sv["e"], "ple_bwd")
        dwpe = _matmul([(sv["pb"], de)], "tn", BF16, "dw_pe", tm=1024)
        dwpg = _matmul([(sv["h1b"], dpre)], "tn", BF16, "dw_pg", tm=1024)
        dh1, dmix, dg_post = _matmul_rows(
            [(dpre, lw["wpg"], 0)], [dh, sv["mix"]], sv["g_post"], _post_bwd_epilogue,
            (F32, BF16), "d_h1_post_bwd", tm=512)
        dwout = jnp.concatenate(
            [_matmul([(sv["ga"], dmix)], "tn", BF16, "dw_out_attn", tm=1024),
             _matmul([(sv["gp"], dmix)], "tn", BF16, "dw_out_pool", tm=1024)], axis=0)
        dcat = _matmul([(dmix, lw["wout"])], "nt", BF16, "d_cat", tn=2048).reshape(b, s, mix_w)
        du, dzp, dwpool, dscale = _pool_bwd(sv["pp"], dcat, lw["wpool"], sv["scale"], a // pg,
                                            "pool_bwd")
        big["w_pool"][i] = jnp.transpose(
            dwpool.astype(BF16).reshape(N_POOL_GROUPS, N_DEV, pg // N_DEV, pg), (1, 0, 2, 3)
        ).reshape(N_DEV, N_POOL_GROUPS * (pg // N_DEV), pg)
        big["w_out"][i] = dwout.reshape(N_DEV, mix_w // N_DEV, d)
        big["w_pg"][i] = dwpg.reshape(N_DEV, d // N_DEV, d)
        big["w_pe"][i] = jnp.transpose(dwpe.reshape(ple, N_DEV, d // N_DEV), (1, 0, 2))
        rest_handles[i], tok = _exchange_start([(big[nm][i], True) for nm in rest], du,
                                               "grads_rest%d_start" % i)
        qab, doa, dz = _attn_prep_bwd(dcat, sv["pa"], sv["o"], sv["lse"] + tok, sv["qa"],
                                      "attn_prep_bwd")
        dq, dk, dv, dcp = _attn_bwd(sv["ka"], sv["kat"], sv["va"], qab, doa, "attn_bwd")
        dc = jnp.transpose(dcp[..., :2], (0, 2, 1, 3)).reshape(b, s, heads)
        dc = jnp.pad(dc, ((0, 0), (0, 0), (0, LANES - heads)))
        dfl, dbf = _gates_bwd(dc, sv["fl"], sv["bf"], heads, "gates_bwd")
        dproj = [g_.reshape(t, -1) for g_ in (dq, dk, dv, dz, dfl, du, dzp)]
        dw_parts = [_matmul([(sv["hn"], g_)], "tn", BF16, "dw_in_%d" % n_, tm=1024)
                    for n_, g_ in enumerate(dproj)]

        big["w_in"][i] = _unpack_dw_in(dw_parts, heads)
        grad_handles[i], tok = _exchange_start([(big["w_in"][i], True)], dw_parts[-1],
                                               "grads_w_in%d_start" % i)

        dq2, dk2, dv2, dz2, dfl2, du2, dzp2 = dproj
        dh, dg_pre = _matmul_rows(
            [(dq2, lw["wa"], 0), (dk2, lw["wa"], a), (dv2, lw["wa"], 2 * a), (dz2, lw["wa"], 3 * a),
             (du2, lw["wp"], 0), (dzp2, lw["wp"], pw), (dfl2, lw["wf"] + tok.astype(BF16), 0)],
            [sv["h"], dh1], sv["g_pre"] + tok, _pre_bwd_epilogue, (F32,), "d_hn_pre_bwd", tm=256)
        small["norm_pre"][i] = dg_pre
        small["norm_post"][i] = dg_post
        small["b_f"][i] = jnp.sum(dbf, axis=0)
        small["pool_scale"][i] = dscale
    grad_x = dh.reshape(b, s, d)

    width = max(d, pw)
    small_names = ("norm_pre", "norm_post", "pool_scale", "b_f")

    def small_rows(get):
        rows = []
        for nm in small_names:
            for i in range(depth):
                v_ = get(nm, i)
                rows.append(jnp.pad(v_, ((0, 0), (0, width - v_.shape[1]))))
        return jnp.concatenate(rows, axis=0)

    small_g = small_rows(lambda nm, i: small[nm][i])
    (small_recv,) = _exchange([(small_g, False)], "exchange_small")
    me1 = jnp.reshape(me, (1,)).astype(jnp.int32)

    weights = dict(norm_pre=norm_pre, norm_post=norm_post, w_in=w_in, b_f=b_f, w_pool=w_pool,
                   pool_scale=pool_scale, w_out=w_out, w_pg=w_pg, w_pe=w_pe)
    mom1 = dict(norm_pre=m_norm_pre, norm_post=m_norm_post, w_in=m_w_in, b_f=m_b_f, w_pool=m_w_pool,
                pool_scale=m_pool_scale, w_out=m_w_out, w_pg=m_w_pg, w_pe=m_w_pe)
    mom2 = dict(norm_pre=v_norm_pre, norm_post=v_norm_post, w_in=v_w_in, b_f=v_b_f, w_pool=v_w_pool,
                pool_scale=v_pool_scale, w_out=v_w_out, w_pg=v_w_pg, w_pe=v_w_pe)

    results = {}

    def update(nm, recvs):
        shp = weights[nm].shape
        sent = [big[nm][i] for i in range(depth)]
        flat = lambda arr: arr.reshape((depth,) + sent[0].shape[1:])
        outs = _adamw(recvs, sent, me1, flat(weights[nm]), flat(mom1[nm]), flat(mom2[nm]),
                      "adamw_" + nm)
        results[nm] = [o_.reshape(shp) for o_ in outs]
        return outs[0]

    got_rest = [_exchange_wait(rest_handles[i], dh, "grads_rest%d_wait" % i) for i in range(depth)]
    for j, nm in enumerate(rest):
        last = update(nm, [got_rest[i][j] for i in range(depth)])

    small_w = small_rows(lambda nm, i: weights[nm][i:i + 1])[None]
    small_m = small_rows(lambda nm, i: mom1[nm][i:i + 1])[None]
    small_v = small_rows(lambda nm, i: mom2[nm][i:i + 1])[None]
    outs = _adamw([small_recv], [small_g], me1, small_w, small_m, small_v, "adamw_small")
    for j, nm in enumerate(small_names):
        cols = weights[nm].shape[1]
        results[nm] = [o_[0, j * depth:(j + 1) * depth, :cols] for o_ in outs]

    got_w_in = [_exchange_wait(grad_handles[i], last + outs[0][0, 0, 0], "grads_w_in%d_wait" % i)[0]
                for i in range(depth)]
    update("w_in", got_w_in)

    order = ("norm_pre", "norm_post", "w_in", "b_f", "w_pool", "pool_scale", "w_out", "w_pg", "w_pe")
    return (loss, grad_x, *[results[nm][0] for nm in order], *[results[nm][1] for nm in order],
            *[results[nm][2] for nm in order], *[results[nm][3] for nm in order])
```

```python
import functools
import math

import jax
import jax.numpy as jnp
from jax import lax
from jax.experimental import pallas as pl
from jax.experimental.pallas import tpu as pltpu

N_DEV = 8
MESH_AXES = ("x", "y", "c")
HEAD_DIM = 64
LANES = 128
N_POOL_GROUPS = 4
EPS = 1e-6
ADAM_LR = 0.001
ADAM_B1 = 0.9
ADAM_B2 = 0.999
ADAM_EPS = 1e-08
ADAM_WD = 0.01
ADAM_STEP = 10
VMEM_LIMIT_BYTES = 56 * 1024 * 1024
F32 = jnp.float32
BF16 = jnp.bfloat16
NEG_INF = float("-inf")


def _call(body, *, name, grid, in_specs, out_specs, out_shape, scratch_shapes=(), semantics=None):
    return pl.pallas_call(
        body, name=name, grid=grid, in_specs=in_specs, out_specs=out_specs, out_shape=out_shape,
        scratch_shapes=list(scratch_shapes),
        compiler_params=pltpu.CompilerParams(dimension_semantics=semantics,
                                             vmem_limit_bytes=VMEM_LIMIT_BYTES))


def _sigmoid(z):
    return 1.0 / (1.0 + jnp.exp(-z))


def _dot(a, b, dims):
    return lax.dot_general(a, b, (dims, ((), ())), preferred_element_type=F32)


NN = ((1,), (0,))
NT = ((1,), (1,))
TN = ((0,), (0,))


def _exchange(items, name):
    n = len(items)
    modes = [s for _, s in items]
    out_shapes = []
    for a, s in items:
        shp = a.shape[1:] if s else a.shape
        out_shapes.append(jax.ShapeDtypeStruct((N_DEV,) + tuple(shp), a.dtype))

    def body(*refs):
        ins = refs[:n]
        outs = refs[n:2 * n]
        send_sems, recv_sems, local_sems = refs[2 * n:]
        x, y, c = (lax.axis_index(ax) for ax in MESH_AXES)
        me = 4 * x + 2 * y + c
        started = []
        for i in range(n):
            mine = ins[i].at[me] if modes[i] else ins[i]
            loc = pltpu.make_async_copy(mine, outs[i].at[me], local_sems.at[i])
            loc.start()
            started.append(loc)
        remote = []
        for k in range(1, N_DEV):
            px = x ^ ((k >> 2) & 1)
            py = y ^ ((k >> 1) & 1)
            pc = c ^ (k & 1)
            peer = me ^ k
            for i in range(n):
                src = ins[i].at[peer] if modes[i] else ins[i]
                cp = pltpu.make_async_remote_copy(
                    src_ref=src, dst_ref=outs[i].at[me],
                    send_sem=send_sems.at[i, k - 1], recv_sem=recv_sems.at[i, k - 1],
                    device_id=(px, py, pc), device_id_type=pl.DeviceIdType.MESH)
                cp.start()
                remote.append(cp)
        for cp in remote:
            cp.wait()
        for loc in started:
            loc.wait()

    hbm = pl.BlockSpec(memory_space=pltpu.HBM)
    return pl.pallas_call(
        body, name=name, out_shape=out_shapes,
        in_specs=[hbm] * n, out_specs=[hbm] * n,
        scratch_shapes=[pltpu.SemaphoreType.DMA((n, N_DEV - 1)),
                        pltpu.SemaphoreType.DMA((n, N_DEV - 1)),
                        pltpu.SemaphoreType.DMA((n,))],
    )(*[a for a, _ in items])


def _gather_two_level(shard, name):
    def body(x_ref, out_ref, send_sems, recv_sems, local_sem):
        x, y, c = (lax.axis_index(ax) for ax in MESH_AXES)
        sibling = (x, y, 1 - c)
        chips = [(1 - x, y), (x, 1 - y), (1 - x, 1 - y)]

        def slot(px, py, pc):
            return out_ref.at[4 * px + 2 * py + pc]

        def copy(k, block, to, src=None):
            return pltpu.make_async_remote_copy(
                src_ref=slot(*block) if src is None else src, dst_ref=slot(*block),
                send_sem=send_sems.at[k], recv_sem=recv_sems.at[k],
                device_id=to, device_id_type=pl.DeviceIdType.MESH)

        mine = pltpu.make_async_copy(x_ref, slot(x, y, c), local_sem)
        mine.start()
        first = [copy(0, (x, y, c), sibling, src=x_ref)]
        first += [copy(1 + j, (x, y, c), (*chip, c), src=x_ref) for j, chip in enumerate(chips)]
        for cp in first:
            cp.start()
        passed = [copy(4 + j, (*chip, c), sibling) for j, chip in enumerate(chips)]
        for j, chip in enumerate(chips):
            copy(1 + j, (*chip, c), (x, y, c)).wait_recv()
            passed[j].start()
        copy(0, (x, y, 1 - c), (x, y, c)).wait_recv()
        for j, chip in enumerate(chips):
            copy(4 + j, (*chip, 1 - c), (x, y, c)).wait_recv()
        for cp in first + passed:
            cp.wait_send()
        mine.wait()

    hbm = pl.BlockSpec(memory_space=pltpu.HBM)
    return pl.pallas_call(
        body, name=name, out_shape=jax.ShapeDtypeStruct((N_DEV,) + shard.shape, shard.dtype),
        in_specs=[hbm], out_specs=hbm,
        scratch_shapes=[pltpu.SemaphoreType.DMA((N_DEV - 1,)), pltpu.SemaphoreType.DMA((N_DEV - 1,)),
                        pltpu.SemaphoreType.DMA],
    )(shard)


def _peer_copies(srcs, lands, modes, send_sems, recv_sems):
    x, y, c = (lax.axis_index(ax) for ax in MESH_AXES)
    me = 4 * x + 2 * y + c
    copies = []
    for k in range(1, N_DEV):
        peer_id = (x ^ ((k >> 2) & 1), y ^ ((k >> 1) & 1), c ^ (k & 1))
        for i, scatter in enumerate(modes):
            src = srcs[i].at[me ^ k] if scatter else srcs[i]
            pair = i * (N_DEV - 1) + k - 1
            copies.append(pltpu.make_async_remote_copy(
                src_ref=src, dst_ref=lands[i].at[me],
                send_sem=send_sems.at[pair], recv_sem=recv_sems.at[pair],
                device_id=peer_id, device_id_type=pl.DeviceIdType.MESH))
    return copies


def _exchange_start(items, after, name):
    n = len(items)
    modes = [s for _, s in items]
    srcs = [pltpu.with_memory_space_constraint(a, pltpu.HBM) for a, _ in items]
    lands = []
    for a, s in items:
        shp = (N_DEV,) + tuple(a.shape[1:] if s else a.shape)
        lands.append(pltpu.with_memory_space_constraint(lax.empty(shp, a.dtype), pltpu.HBM))

    def body(*refs):
        send_sems, recv_sems = refs[2 * n + 1], refs[2 * n + 2]
        token = refs[-1]
        for cp in _peer_copies(refs[:n], refs[n:2 * n], modes, send_sems, recv_sems):
            cp.start()
        token[...] = jnp.zeros_like(token)

    hbm = pl.BlockSpec(memory_space=pltpu.HBM)
    sem = pl.BlockSpec(memory_space=pltpu.SEMAPHORE)
    outs = pl.pallas_call(
        body, name=name,
        out_shape=(pltpu.SemaphoreType.DMA((n * (N_DEV - 1),)),
                   pltpu.SemaphoreType.DMA((n * (N_DEV - 1),)),
                   *[pltpu.HBM(a.shape, a.dtype) for a in srcs + lands],
                   jax.ShapeDtypeStruct((8, LANES), F32)),
        in_specs=[hbm] * (2 * n) + [pl.BlockSpec(memory_space=pl.ANY)],
        out_specs=(sem, sem, *([hbm] * (2 * n)), pl.BlockSpec(memory_space=pltpu.VMEM)),
        input_output_aliases={i: 2 + i for i in range(2 * n)},
        compiler_params=pltpu.CompilerParams(
            has_side_effects=pltpu.SideEffectType.DATAFLOW_SIDE_EFFECTING),
    )(*srcs, *lands, after)
    handle = (modes, outs[0], outs[1], list(outs[2:2 + n]), list(outs[2 + n:2 + 2 * n]))
    return handle, outs[-1][0, 0]


def _exchange_wait(handle, after, name):
    modes, send_sems, recv_sems, srcs, lands = handle
    n = len(modes)

    def body(*refs):
        for cp in _peer_copies(refs[:n], refs[n:2 * n], modes, refs[2 * n], refs[2 * n + 1]):
            cp.wait_send()
            cp.wait_recv()

    hbm = pl.BlockSpec(memory_space=pltpu.HBM)
    sem = pl.BlockSpec(memory_space=pltpu.SEMAPHORE)
    outs = pl.pallas_call(
        body, name=name,
        out_shape=tuple(pltpu.HBM(a.shape, a.dtype) for a in srcs + lands),
        in_specs=[hbm] * (2 * n) + [sem, sem, pl.BlockSpec(memory_space=pl.ANY)],
        out_specs=tuple([hbm] * (2 * n)),
        input_output_aliases={i: i for i in range(2 * n)},
        compiler_params=pltpu.CompilerParams(
            has_side_effects=pltpu.SideEffectType.DATAFLOW_SIDE_EFFECTING),
    )(*srcs, *lands, send_sems, recv_sems, after)
    return list(outs[n:])


def _with_own(slots, own, me):
    idx = lax.broadcasted_iota(jnp.int32, (N_DEV,) + (1,) * own.ndim, 0)
    return jnp.where(idx == me, own[None], slots)


def _matmul(pairs, mode, out_dtype, name, n_dim=None, tm=512, tn=1024, tk=1024, n_outer=False):
    dims = {"nn": NN, "nt": NT, "tn": TN}[mode]
    pairs = [tuple(pr) + (0, 0) * (len(pr) == 2) for pr in pairs]
    a0, b0 = pairs[0][:2]
    m_dim = a0.shape[1] if mode == "tn" else a0.shape[0]
    if n_dim is None:
        n_dim = b0.shape[0] if mode == "nt" else b0.shape[1]
    tm = min(tm, m_dim)
    tn = min(tn, n_dim)
    segs = []
    off = 0
    for a, _, k0, n0 in pairs:
        k_dim = a.shape[0] if mode == "tn" else a.shape[1]
        t = min(tk, k_dim)
        segs.append((off, k_dim // t, t, k0 // t, n0 // tn))
        off += k_dim // t
    nk = off
    n_pairs = len(pairs)

    def ij(g0, g1):
        return (g1, g0) if n_outer else (g0, g1)

    in_specs = []
    for (o, cnt, t, kb, nb) in segs:
        def kc(kk, o=o, cnt=cnt):
            return jnp.clip(kk - o, 0, cnt - 1)
        if mode == "tn":
            in_specs.append(pl.BlockSpec((t, tm), lambda g0, g1, kk, kc=kc: (kc(kk), ij(g0, g1)[0])))
        else:
            in_specs.append(pl.BlockSpec((tm, t), lambda g0, g1, kk, kc=kc: (ij(g0, g1)[0], kc(kk))))
        if mode == "nt":
            in_specs.append(pl.BlockSpec((tn, t), lambda g0, g1, kk, kc=kc, kb=kb, nb=nb:
                                         (nb + ij(g0, g1)[1], kb + kc(kk))))
        else:
            in_specs.append(pl.BlockSpec((t, tn), lambda g0, g1, kk, kc=kc, kb=kb, nb=nb:
                                         (kb + kc(kk), nb + ij(g0, g1)[1])))

    one_shot = all(sg[1] == 1 for sg in segs)

    def body_sum(*refs):
        total = _dot(refs[0][...], refs[1][...], dims)
        for idx in range(1, n_pairs):
            total = total + _dot(refs[2 * idx][...], refs[2 * idx + 1][...], dims)
        refs[2 * n_pairs][...] = total.astype(out_dtype)

    def body(*refs):
        out_ref = refs[2 * n_pairs]
        acc = refs[2 * n_pairs + 1]
        kk = pl.program_id(2)

        @pl.when(kk == 0)
        def _():
            acc[...] = jnp.zeros_like(acc)

        for idx, (o, cnt) in enumerate(sg[:2] for sg in segs):
            @pl.when((kk >= o) & (kk < o + cnt))
            def _(idx=idx):
                acc[...] += _dot(refs[2 * idx][...], refs[2 * idx + 1][...], dims)

        @pl.when(kk == nk - 1)
        def _():
            out_ref[...] = acc[...].astype(out_dtype)

    flat = [t for pr in pairs for t in pr[:2]]
    tiles = (m_dim // tm, n_dim // tn)
    return _call(body_sum if one_shot else body, name=name,
                 grid=ij(*tiles) + (1 if one_shot else nk,), in_specs=in_specs,
                 out_specs=pl.BlockSpec((tm, tn), lambda g0, g1, kk: ij(g0, g1)),
                 out_shape=jax.ShapeDtypeStruct((m_dim, n_dim), out_dtype),
                 scratch_shapes=[] if one_shot else [pltpu.VMEM((tm, tn), F32)],
                 semantics=("parallel", "parallel", "arbitrary"))(*flat)


def _row_tile(t):
    return min(512, t)


def _rms_fwd(h, g, name):
    t, d = h.shape
    tt = _row_tile(t)

    def body(h_ref, g_ref, o_ref):
        hv = h_ref[...]
        r = lax.rsqrt(jnp.mean(hv * hv, axis=-1, keepdims=True) + EPS)
        o_ref[...] = (hv * r * g_ref[...]).astype(BF16)

    row = pl.BlockSpec((tt, d), lambda i: (i, 0))
    vec = pl.BlockSpec((1, d), lambda i: (0, 0))
    return _call(body, name=name, grid=(t // tt,), in_specs=[row, vec], out_specs=row,
                 out_shape=jax.ShapeDtypeStruct((t, d), BF16), semantics=("parallel",))(h, g)


def _post_fwd(h, mix, g, name):
    t, d = h.shape
    tt = _row_tile(t)

    def body(h_ref, m_ref, g_ref, o_ref, ob_ref):
        mv = m_ref[...]
        r = lax.rsqrt(jnp.mean(mv * mv, axis=-1, keepdims=True) + EPS)
        h1 = h_ref[...] + mv * r * g_ref[...]
        o_ref[...] = h1
        ob_ref[...] = h1.astype(BF16)

    row = pl.BlockSpec((tt, d), lambda i: (i, 0))
    vec = pl.BlockSpec((1, d), lambda i: (0, 0))
    return _call(body, name=name, grid=(t // tt,), in_specs=[row, row, vec], out_specs=[row, row],
                 out_shape=[jax.ShapeDtypeStruct((t, d), F32), jax.ShapeDtypeStruct((t, d), BF16)],
                 semantics=("parallel",))(h, mix, g)


def _ple_fwd(h1, gpre, e, name):
    t, d = h1.shape
    tt = _row_tile(t)

    def body(h_ref, g_ref, e_ref, o_ref):
        o_ref[...] = h_ref[...] + _sigmoid(g_ref[...]) * e_ref[...]

    row = pl.BlockSpec((tt, d), lambda i: (i, 0))
    return _call(body, name=name, grid=(t // tt,), in_specs=[row, row, row], out_specs=row,
                 out_shape=jax.ShapeDtypeStruct((t, d), F32), semantics=("parallel",))(h1, gpre, e)


def _ple_loss(h1, gpre, e, target, name):
    t, d = h1.shape
    tt = _row_tile(t)

    def body(h_ref, g_ref, e_ref, t_ref, dy_ref, s_ref):
        @pl.when(pl.program_id(0) == 0)
        def _():
            s_ref[...] = jnp.zeros_like(s_ref)
        diff = h_ref[...] + _sigmoid(g_ref[...]) * e_ref[...] - t_ref[...]
        dy_ref[...] = diff * (1.0 / d)
        s_ref[...] += jnp.sum(diff * diff, axis=0, keepdims=True)

    row = pl.BlockSpec((tt, d), lambda i: (i, 0))
    vec = pl.BlockSpec((1, d), lambda i: (0, 0))
    return _call(body, name=name, grid=(t // tt,), in_specs=[row] * 4, out_specs=[row, vec],
                 out_shape=[jax.ShapeDtypeStruct((t, d), F32), jax.ShapeDtypeStruct((1, d), F32)],
                 semantics=("arbitrary",))(h1, gpre, e, target)


def _ple_bwd(dh2, gpre, e, name):
    t, d = dh2.shape
    tt = _row_tile(t)

    def body(d_ref, g_ref, e_ref, de_ref, dp_ref):
        gate = _sigmoid(g_ref[...])
        dv = d_ref[...]
        de_ref[...] = (dv * gate).astype(BF16)
        dp_ref[...] = (dv * e_ref[...] * gate * (1.0 - gate)).astype(BF16)

    row = pl.BlockSpec((tt, d), lambda i: (i, 0))
    return _call(body, name=name, grid=(t // tt,), in_specs=[row, row, row], out_specs=[row, row],
                 out_shape=[jax.ShapeDtypeStruct((t, d), BF16)] * 2,
                 semantics=("parallel",))(dh2, gpre, e)


def _matmul_rows(pairs, rows_in, vec_in, epilogue, row_dtypes, name, tm):
    n_pairs = len(pairs)
    m_dim = pairs[0][0].shape[0]
    n_dim = pairs[0][1].shape[0]
    tm = min(tm, m_dim)
    in_specs = []
    for a, _, k0 in pairs:
        k_dim = a.shape[1]
        in_specs.append(pl.BlockSpec((tm, k_dim), lambda i: (i, 0)))
        in_specs.append(pl.BlockSpec((n_dim, k_dim), lambda i, kb=k0 // k_dim: (0, kb)))
    row = pl.BlockSpec((tm, n_dim), lambda i: (i, 0))
    vec = pl.BlockSpec((1, n_dim), lambda i: (0, 0))
    n_rows = len(rows_in)

    def body(*refs):
        ops = refs[:2 * n_pairs]
        row_refs = refs[2 * n_pairs:2 * n_pairs + n_rows]
        vec_ref = refs[2 * n_pairs + n_rows]
        outs = refs[2 * n_pairs + n_rows + 1:]
        total = _dot(ops[0][...], ops[1][...], NT)
        for idx in range(1, n_pairs):
            total = total + _dot(ops[2 * idx][...], ops[2 * idx + 1][...], NT)
        results, partial = epilogue(total, [r[...] for r in row_refs], vec_ref[...])
        for out_ref, val in zip(outs[:-1], results):
            out_ref[...] = val.astype(out_ref.dtype)

        @pl.when(pl.program_id(0) == 0)
        def _():
            outs[-1][...] = jnp.zeros_like(outs[-1])
        outs[-1][...] += partial

    flat = [t_ for a, b_, _ in pairs for t_ in (a, b_)]
    return _call(body, name=name, grid=(m_dim // tm,),
                 in_specs=in_specs + [row] * n_rows + [vec],
                 out_specs=[row] * len(row_dtypes) + [vec],
                 out_shape=[jax.ShapeDtypeStruct((m_dim, n_dim), dt) for dt in row_dtypes]
                 + [jax.ShapeDtypeStruct((1, n_dim), F32)],
                 semantics=("arbitrary",))(*flat, *rows_in, vec_in)


def _post_bwd_epilogue(t1, rows, g):
    dh2, mv = rows
    dh1 = dh2 + t1
    r = lax.rsqrt(jnp.mean(mv * mv, axis=-1, keepdims=True) + EPS)
    w = dh1 * g
    dot = jnp.mean(w * mv, axis=-1, keepdims=True)
    dmix = r * w - mv * (r * r * r) * dot
    return (dh1, dmix), jnp.sum(dh1 * mv * r, axis=0, keepdims=True)


def _pre_bwd_epilogue(dhn, rows, g):
    hv, dh1 = rows
    r = lax.rsqrt(jnp.mean(hv * hv, axis=-1, keepdims=True) + EPS)
    w = dhn * g
    dot = jnp.mean(w * hv, axis=-1, keepdims=True)
    return (dh1 + r * w - hv * (r * r * r) * dot,), jnp.sum(dhn * hv * r, axis=0, keepdims=True)


def _split3(v):
    hi = v.astype(BF16)
    r1 = v - hi.astype(F32)
    mid = r1.astype(BF16)
    lo = (r1 - mid.astype(F32)).astype(BF16)
    return hi, mid, lo


def _scan_tile(s):
    return min(256, s)


def _gates_fwd(fl, bf, name):
    b, s, _ = fl.shape

    tb = _scan_tile(s)

    def body(f_ref, b_ref, c_ref):
        dst = lax.broadcasted_iota(jnp.int32, (tb, tb), 0)
        src = lax.broadcasted_iota(jnp.int32, (tb, tb), 1)
        lower = (src <= dst).astype(BF16)
        carry = jnp.zeros((1, LANES), F32)
        for blk_i in range(s // tb):
            rows = slice(blk_i * tb, (blk_i + 1) * tb)
            xv = f_ref[0, rows, :] + b_ref[...]
            lf = jnp.minimum(xv, 0.0) - jnp.log(1.0 + jnp.exp(-jnp.abs(xv)))
            acc = carry
            for part in _split3(lf):
                acc = acc + _dot(lower, part, NN)
            c_ref[0, rows, :] = acc
            carry = acc[tb - 1:tb, :]

    blk = pl.BlockSpec((1, s, LANES), lambda i: (i, 0, 0))
    return _call(body, name=name, grid=(b,),
                 in_specs=[blk, pl.BlockSpec((1, LANES), lambda i: (0, 0))],
                 out_specs=blk, out_shape=jax.ShapeDtypeStruct((b, s, LANES), F32),
                 semantics=("parallel",))(fl, bf)


def _gates_bwd(dc, fl, bf, heads, name):
    b, s, _ = fl.shape

    tb = _scan_tile(s)

    def body(d_ref, f_ref, b_ref, o_ref, db_ref):
        dst = lax.broadcasted_iota(jnp.int32, (tb, tb), 0)
        src = lax.broadcasted_iota(jnp.int32, (tb, tb), 1)
        later = (src >= dst).astype(BF16)
        lane = lax.broadcasted_iota(jnp.int32, (tb, LANES), 1)
        carry = jnp.zeros((1, LANES), F32)
        db = jnp.zeros((1, LANES), F32)
        for blk_i in reversed(range(s // tb)):
            rows = slice(blk_i * tb, (blk_i + 1) * tb)
            dlf = carry
            for part in _split3(d_ref[0, rows, :]):
                dlf = dlf + _dot(later, part, NN)
            carry = dlf[0:1, :]
            xv = f_ref[0, rows, :] + b_ref[...]
            dfl = jnp.where(lane < heads, dlf * _sigmoid(-xv), 0.0)
            o_ref[0, rows, :] = dfl.astype(BF16)
            db = db + jnp.sum(dfl, axis=0, keepdims=True)
        db_ref[0] = db

    blk = pl.BlockSpec((1, s, LANES), lambda i: (i, 0, 0))
    return _call(body, name=name, grid=(b,),
                 in_specs=[blk, blk, pl.BlockSpec((1, LANES), lambda i: (0, 0))],
                 out_specs=[blk, pl.BlockSpec((1, 1, LANES), lambda i: (i, 0, 0))],
                 out_shape=[jax.ShapeDtypeStruct((b, s, LANES), BF16),
                            jax.ShapeDtypeStruct((b, 1, LANES), F32)],
                 semantics=("parallel",))(dc, fl, bf)


LANE_CQ = 64
LANE_CK = 67
LANE_LSE = 70
LANE_D = 64
N_PARTS = 3


def _attn_tiles(s):
    return min(512, s), min(256, s)


def _lanes_in(lane, first):
    return (lane >= first) & (lane < first + N_PARTS)


def _attn_prep_fwd(pa, c, name):
    b, s, a4 = pa.shape
    pairs = a4 // (4 * LANES)
    scale = 1.0 / math.sqrt(HEAD_DIM)

    def body(q_ref, k_ref, v_ref, c_ref, qa_ref, ka_ref, kat_ref, va_ref, vt_ref):
        hp = pl.program_id(1)
        cv = c_ref[0]
        vv = v_ref[0]
        lane = lax.broadcasted_iota(jnp.int32, (s, LANES), 1)
        r128 = lax.broadcasted_iota(jnp.int32, (LANES, LANES), 0)
        c128 = lax.broadcasted_iota(jnp.int32, (LANES, LANES), 1)
        ident = (r128 == c128).astype(BF16)
        for j in range(2):
            head = 2 * hp + j
            move128 = (r128 == c128 + HEAD_DIM * j) & (c128 < HEAD_DIM)
            cparts = _split3(jnp.sum(jnp.where(lane == head, cv, 0.0), axis=1, keepdims=True))
            qa = _dot(q_ref[0], jnp.where(move128, scale, 0.0).astype(BF16), NN)
            ka = _dot(k_ref[0], move128.astype(BF16), NN)
            for i in range(N_PARTS):
                qa = jnp.where(lane == LANE_CQ + i, cparts[i].astype(F32), qa)
                ka = jnp.where(lane == LANE_CK + i, -cparts[i].astype(F32), ka)
            qa = jnp.where(_lanes_in(lane, LANE_CK), 1.0, qa)
            ka = jnp.where(_lanes_in(lane, LANE_CQ) | _lanes_in(lane, LANE_LSE), 1.0, ka)
            va = _dot(vv, move128.astype(BF16), NN) + jnp.where(_lanes_in(lane, LANE_D), 1.0, 0.0)
            kab = ka.astype(BF16)
            qa_ref[0, 0, j] = qa.astype(BF16)
            ka_ref[0, 0, j] = kab
            kat_ref[0, 0, j] = _dot(ident, kab, NT).astype(BF16)
            va_ref[0, 0, j] = va.astype(BF16)
        vt_ref[0, 0] = _dot(ident, vv, NT).astype(BF16)

    col_blk = lambda cidx: pl.BlockSpec((1, s, LANES), lambda bi, hp: (bi, 0, cidx * pairs + hp))
    tok = pl.BlockSpec((1, 1, 2, s, LANES), lambda bi, hp: (bi, hp, 0, 0, 0))
    tok_t = pl.BlockSpec((1, 1, 2, LANES, s), lambda bi, hp: (bi, hp, 0, 0, 0))
    tok_shape = jax.ShapeDtypeStruct((b, pairs, 2, s, LANES), BF16)
    return _call(
        body, name=name, grid=(b, pairs),
        in_specs=[col_blk(0), col_blk(1), col_blk(2),
                  pl.BlockSpec((1, s, LANES), lambda bi, hp: (bi, 0, 0))],
        out_specs=[tok, tok, tok_t, tok,
                   pl.BlockSpec((1, 1, LANES, s), lambda bi, hp: (bi, hp, 0, 0))],
        out_shape=[tok_shape, tok_shape, jax.ShapeDtypeStruct((b, pairs, 2, LANES, s), BF16),
                   tok_shape, jax.ShapeDtypeStruct((b, pairs, LANES, s), BF16)],
        semantics=("parallel", "parallel"))(pa, pa, pa, c)


def _attn_fwd(qa, ka, vt, pa, name):
    b, pairs, _, s, _ = qa.shape
    a = pairs * LANES
    tq = min(1024, s)
    nq = s // tq

    def body(q_ref, k_ref, vt_ref, z_ref, o_ref, g_ref, lse_ref):
        key_i = lax.broadcasted_iota(jnp.int32, (tq, tq), 0)
        qry_i = lax.broadcasted_iota(jnp.int32, (tq, tq), 1)

        def query_block(c):
            past = tq * c
            heads_out = []
            for j in range(2):
                qv = q_ref[0, 0, j]
                vrows = slice(HEAD_DIM * j, HEAD_DIM * (j + 1))
                sd = _dot(k_ref[0, 0, j, past:past + tq, :], qv, NT)
                sd = jnp.where(key_i <= qry_i, sd, NEG_INF)
                m = jnp.max(sd, axis=0, keepdims=True)
                if c > 0:
                    sp = _dot(k_ref[0, 0, j, 0:past, :], qv, NT)
                    m = jnp.maximum(m, jnp.max(sp, axis=0, keepdims=True))
                pd = jnp.exp(sd - m)
                l = jnp.sum(pd, axis=0, keepdims=True)
                acc = _dot(vt_ref[0, 0, vrows, past:past + tq], pd.astype(BF16), NN)
                if c > 0:
                    pp = jnp.exp(sp - m)
                    l = l + jnp.sum(pp, axis=0, keepdims=True)
                    acc = acc + _dot(vt_ref[0, 0, vrows, 0:past], pp.astype(BF16), NN)
                heads_out.append(acc / l)
                lse_ref[0, 0, j:j + 1, :] = m + jnp.log(l)
            ov = jnp.transpose(jnp.concatenate(heads_out, axis=0))
            o_ref[0] = ov.astype(BF16)
            zv = z_ref[0].astype(F32)
            g_ref[0] = (ov * zv * _sigmoid(zv)).astype(BF16)

        for c in range(nq):
            pl.when(pl.program_id(2) == c)(functools.partial(query_block, c))

    return _call(
        body, name=name, grid=(b, pairs, s // tq),
        in_specs=[pl.BlockSpec((1, 1, 2, tq, LANES), lambda bi, hp, qi: (bi, hp, 0, qi, 0)),
                  pl.BlockSpec((1, 1, 2, s, LANES), lambda bi, hp, qi: (bi, hp, 0, 0, 0)),
                  pl.BlockSpec((1, 1, LANES, s), lambda bi, hp, qi: (bi, hp, 0, 0)),
                  pl.BlockSpec((1, tq, LANES), lambda bi, hp, qi: (bi, qi, 3 * pairs + hp))],
        out_specs=[pl.BlockSpec((1, tq, LANES), lambda bi, hp, qi: (bi, qi, hp)),
                   pl.BlockSpec((1, tq, LANES), lambda bi, hp, qi: (bi, qi, hp)),
                   pl.BlockSpec((1, 1, 2, tq), lambda bi, hp, qi: (bi, hp, 0, qi))],
        out_shape=[jax.ShapeDtypeStruct((b, s, a), BF16), jax.ShapeDtypeStruct((b, s, a), BF16),
                   jax.ShapeDtypeStruct((b, pairs, 2, s), F32)],
        semantics=("parallel", "parallel", "arbitrary"))(qa, ka, vt, pa)


def _attn_prep_bwd(dcat, pa, o, lse, qa, name):
    b, pairs, _, s, _ = qa.shape
    a = pairs * LANES
    sub = 16

    def body(da_ref, z_ref, o_ref, lse_ref, qa_ref, qab_ref, doa_ref, dz_ref):
        zv = z_ref[0].astype(F32)
        dav = da_ref[0].astype(F32)
        ov = o_ref[0].astype(F32)
        sg = _sigmoid(zv)
        dov = dav * zv * sg
        dz_ref[0] = (dav * ov * sg * (1.0 + zv * (1.0 - sg))).astype(BF16)
        prod = dov * ov
        dob = dov.astype(BF16)
        lane = lax.broadcasted_iota(jnp.int32, (s, LANES), 1)
        r128 = lax.broadcasted_iota(jnp.int32, (LANES, LANES), 0)
        c128 = lax.broadcasted_iota(jnp.int32, (LANES, LANES), 1)
        prow = lax.broadcasted_iota(jnp.int32, (sub, s), 0)
        srow = lax.broadcasted_iota(jnp.int32, (sub, LANES), 0)
        scol = lax.broadcasted_iota(jnp.int32, (sub, LANES), 1)
        place = ((scol == srow + LANE_LSE) & (srow < N_PARTS)).astype(BF16)
        for j in range(2):
            in_head = (lane >= HEAD_DIM * j) & (lane < HEAD_DIM * (j + 1))
            dparts = _split3(jnp.sum(jnp.where(in_head, prod, 0.0), axis=1, keepdims=True))
            move128 = ((r128 == c128 + HEAD_DIM * j) & (c128 < HEAD_DIM)).astype(BF16)
            doa = _dot(dob, move128, NN)
            for i in range(N_PARTS):
                doa = jnp.where(lane == LANE_D + i, -dparts[i].astype(F32), doa)
            doa_ref[0, 0, j] = doa.astype(BF16)
            lparts = _split3(lse_ref[0, 0, j:j + 1, :])
            pmat = jnp.zeros((sub, s), BF16)
            for i in range(N_PARTS):
                pmat = jnp.where(prow == i, lparts[i], pmat)
            lcol = _dot(pmat, place, TN)
            qab_ref[0, 0, j] = (qa_ref[0, 0, j].astype(F32) - lcol).astype(BF16)

    tok = pl.BlockSpec((1, 1, 2, s, LANES), lambda bi, hp: (bi, hp, 0, 0, 0))
    tok_shape = jax.ShapeDtypeStruct((b, pairs, 2, s, LANES), BF16)
    pair_blk = pl.BlockSpec((1, s, LANES), lambda bi, hp: (bi, 0, hp))
    return _call(
        body, name=name, grid=(b, pairs),
        in_specs=[pair_blk,
                  pl.BlockSpec((1, s, LANES), lambda bi, hp: (bi, 0, 3 * pairs + hp)),
                  pair_blk,
                  pl.BlockSpec((1, 1, 2, s), lambda bi, hp: (bi, hp, 0, 0)),
                  tok],
        out_specs=[tok, tok, pair_blk],
        out_shape=[tok_shape, tok_shape, jax.ShapeDtypeStruct((b, s, a), BF16)],
        semantics=("parallel", "parallel"))(dcat, pa, o, lse, qa)


def _attn_bwd(ka, kat, va, qab, doa, name):
    b, pairs, _, s, _ = ka.shape
    a = pairs * LANES
    tq, tk = _attn_tiles(s)
    ratio = tq // tk
    nq, nk = s // tq, s // tk
    scale = 1.0 / math.sqrt(HEAD_DIM)

    def body(k_ref, kt_ref, v_ref, q_ref, do_ref, dq_ref, dk_ref, dv_ref, dc_ref,
             dqt_acc, dk_s, dv_s):
        key_i = lax.broadcasted_iota(jnp.int32, (tk, tq), 0)
        qry_i = lax.broadcasted_iota(jnp.int32, (tk, tq), 1)
        lane = lax.broadcasted_iota(jnp.int32, (tq, LANES), 1)
        low = lane < HEAD_DIM

        def key_block(kj):
            krows = slice(kj * tk, (kj + 1) * tk)
            q0 = (kj // ratio) * tq
            spans = [(slice(q0, q0 + tq), kj * tk - q0)]
            if q0 + tq < s:
                spans.append((slice(q0 + tq, s), None))
            for j in range(2):
                kb = k_ref[0, 0, j, krows, :]
                vb = v_ref[0, 0, j, krows, :]
                ktb = kt_ref[0, 0, j, :, krows]
                dk = dv = None
                for qrows, diag in spans:
                    qb = q_ref[0, 0, j, qrows, :]
                    dob = do_ref[0, 0, j, qrows, :]
                    pt = jnp.exp(_dot(kb, qb, NT))
                    if diag is not None:
                        pt = jnp.where(key_i + diag <= qry_i, pt, 0.0)
                    dsb = (pt * _dot(vb, dob, NT)).astype(BF16)
                    dv_part = _dot(pt.astype(BF16), dob, NN)
                    dk_part = _dot(dsb, qb, NN)
                    dv = dv_part if dv is None else dv + dv_part
                    dk = dk_part if dk is None else dk + dk_part
                    dq_part = _dot(ktb, dsb, NN)
                    if kj == 0:
                        dqt_acc[j, :, qrows] = dq_part
                    else:
                        dqt_acc[j, :, qrows] += dq_part
                dk_s[j, krows, :] = dk
                dv_s[j, krows, :] = dv

        for kj in range(nk):
            key_block(kj)

        def finish(i, _):
            rows = pl.ds(pl.multiple_of(i * tq, tq), tq)
            dq = [jnp.transpose(dqt_acc[j, :, rows]) for j in range(2)]
            dk = [dk_s[j, rows, :] for j in range(2)]
            dv = [dv_s[j, rows, :] for j in range(2)]
            dcol = [dq[j][:, LANE_CQ:LANE_CQ + 1] - dk[j][:, LANE_CK:LANE_CK + 1] for j in range(2)]
            dq = [dq[j] * scale for j in range(2)]
            for out_ref, val in ((dq_ref, dq), (dk_ref, dk), (dv_ref, dv)):
                merged = jnp.where(low, val[0], pltpu.roll(val[1], HEAD_DIM, 1))
                out_ref[0, rows, :] = merged.astype(BF16)
            dc_ref[0, 0, rows, :] = jnp.where(lane == 0, dcol[0], jnp.where(lane == 1, dcol[1], 0.0))
            return 0

        lax.fori_loop(0, nq, finish, 0)

    tok = pl.BlockSpec((1, 1, 2, s, LANES), lambda bi, hp: (bi, hp, 0, 0, 0))
    tok_t = pl.BlockSpec((1, 1, 2, LANES, s), lambda bi, hp: (bi, hp, 0, 0, 0))
    pair_blk = pl.BlockSpec((1, s, LANES), lambda bi, hp: (bi, 0, hp))
    pair_shape = jax.ShapeDtypeStruct((b, s, a), BF16)
    return _call(
        body, name=name, grid=(b, pairs),
        in_specs=[tok, tok_t, tok, tok, tok],
        out_specs=[pair_blk, pair_blk, pair_blk,
                   pl.BlockSpec((1, 1, s, LANES), lambda bi, hp: (bi, hp, 0, 0))],
        out_shape=[pair_shape, pair_shape, pair_shape,
                   jax.ShapeDtypeStruct((b, pairs, s, LANES), F32)],
        scratch_shapes=[pltpu.VMEM((2, LANES, s), F32), pltpu.VMEM((2, s, LANES), F32),
                        pltpu.VMEM((2, s, LANES), F32)],
        semantics=("parallel", "parallel"))(ka, kat, va, qab, doa)


def _pool_tile(s):
    return min(256, s)


def _band(tb, window, shift):
    tgt = lax.broadcasted_iota(jnp.int32, (tb, tb), 0)
    src = lax.broadcasted_iota(jnp.int32, (tb, tb), 1) + shift
    return ((src <= tgt) & (src > tgt - window)).astype(BF16)


def _band_t(tb, window, shift):
    src = lax.broadcasted_iota(jnp.int32, (tb, tb), 0)
    tgt = lax.broadcasted_iota(jnp.int32, (tb, tb), 1) + shift
    return ((src <= tgt) & (src > tgt - window)).astype(BF16)


def _pool_fwd(pp, w_pool, scale, name):
    b, s, pw2 = pp.shape
    pw = pw2 // 2
    pg = pw // N_POOL_GROUPS
    tb = _pool_tile(s)
    nb = s // tb

    def body(u_ref, z_ref, w_ref, s_ref, o_ref):
        window = 2 << pl.program_id(1)
        band0 = _band(tb, window, 0)
        band1 = _band(tb, window, -tb)
        pos = lax.broadcasted_iota(jnp.int32, (tb, pg), 0)

        def block(i, _):
            rows = pl.ds(pl.multiple_of(i * tb, tb), tb)
            prev = pl.ds(pl.multiple_of(jnp.maximum(i - 1, 0) * tb, tb), tb)
            ub = u_ref[0, rows, :]
            up = u_ref[0, prev, :]
            up = jnp.where(i > 0, up, jnp.zeros_like(up))
            count = jnp.minimum(pos + i * tb + 1, window).astype(F32)
            pooled = (_dot(band0, ub, NN) + _dot(band1, up, NN)) / count - ub.astype(F32)
            mixed = _dot(pooled.astype(BF16), w_ref[0], NN) * s_ref[...]
            zv = z_ref[0, rows, :].astype(F32)
            o_ref[0, rows, :] = (mixed * zv * _sigmoid(zv)).astype(BF16)
            return 0

        lax.fori_loop(0, nb, block, 0)

    return _call(
        body, name=name, grid=(b, N_POOL_GROUPS),
        in_specs=[pl.BlockSpec((1, s, pg), lambda bi, g: (bi, 0, g)),
                  pl.BlockSpec((1, s, pg), lambda bi, g: (bi, 0, N_POOL_GROUPS + g)),
                  pl.BlockSpec((1, pg, pg), lambda bi, g: (g, 0, 0)),
                  pl.BlockSpec((1, pg), lambda bi, g: (0, g))],
        out_specs=pl.BlockSpec((1, s, pg), lambda bi, g: (bi, 0, g)),
        out_shape=jax.ShapeDtypeStruct((b, s, pw), BF16),
        semantics=("parallel", "parallel"))(pp, pp, w_pool, scale)


def _pool_bwd(pp, dcat, w_pool, scale, first_block, name):
    b, s, pw2 = pp.shape
    pw = pw2 // 2
    pg = pw // N_POOL_GROUPS
    tb = _pool_tile(s)
    nb = s // tb

    def body(u_ref, z_ref, d_ref, w_ref, s_ref, du_ref, dz_ref, dw_ref, ds_ref, dpool_s):
        @pl.when(pl.program_id(1) == 0)
        def _():
            dw_ref[...] = jnp.zeros_like(dw_ref)
            ds_ref[...] = jnp.zeros_like(ds_ref)

        window = 2 << pl.program_id(0)
        band0 = _band(tb, window, 0)
        band1 = _band(tb, window, -tb)
        band0_t = _band_t(tb, window, 0)
        band1_t = _band_t(tb, window, tb)
        pos = lax.broadcasted_iota(jnp.int32, (tb, pg), 0)

        def first(i, _):
            rows = pl.ds(pl.multiple_of(i * tb, tb), tb)
            prev = pl.ds(pl.multiple_of(jnp.maximum(i - 1, 0) * tb, tb), tb)
            ub = u_ref[0, rows, :]
            up = u_ref[0, prev, :]
            up = jnp.where(i > 0, up, jnp.zeros_like(up))
            count = jnp.minimum(pos + i * tb + 1, window).astype(F32)
            pooled = ((_dot(band0, ub, NN) + _dot(band1, up, NN)) / count
                      - ub.astype(F32)).astype(BF16)
            mixed = _dot(pooled, w_ref[0], NN)
            pm = mixed * s_ref[...]
            zv = z_ref[0, rows, :].astype(F32)
            sg = _sigmoid(zv)
            dpl = d_ref[0, rows, :].astype(F32)
            dpm = dpl * zv * sg
            dz_ref[0, rows, :] = (dpl * pm * sg * (1.0 + zv * (1.0 - sg))).astype(BF16)
            ds_ref[...] += jnp.sum(dpm * mixed, axis=0, keepdims=True)
            dmixed = (dpm * s_ref[...]).astype(BF16)
            dw_ref[0] += _dot(pooled, dmixed, TN)
            dpool_s[rows, :] = _dot(dmixed, w_ref[0], NT)
            return 0

        lax.fori_loop(0, nb, first, 0)

        def second(i, _):
            rows = pl.ds(pl.multiple_of(i * tb, tb), tb)
            nxt_i = jnp.minimum(i + 1, nb - 1)
            nxt = pl.ds(pl.multiple_of(nxt_i * tb, tb), tb)
            count = jnp.minimum(pos + i * tb + 1, window).astype(F32)
            count_n = jnp.minimum(pos + nxt_i * tb + 1, window).astype(F32)
            dpb = dpool_s[rows, :]
            cur = (dpb / count).astype(BF16)
            nx = dpool_s[nxt, :] / count_n
            nx = jnp.where(i < nb - 1, nx, 0.0).astype(BF16)
            du = _dot(band0_t, cur, NN) + _dot(band1_t, nx, NN) - dpb
            du_ref[0, rows, :] = du.astype(BF16)
            return 0

        lax.fori_loop(0, nb, second, 0)

    return _call(
        body, name=name, grid=(N_POOL_GROUPS, b),
        in_specs=[pl.BlockSpec((1, s, pg), lambda g, bi: (bi, 0, g)),
                  pl.BlockSpec((1, s, pg), lambda g, bi: (bi, 0, N_POOL_GROUPS + g)),
                  pl.BlockSpec((1, s, pg), lambda g, bi: (bi, 0, first_block + g)),
                  pl.BlockSpec((1, pg, pg), lambda g, bi: (g, 0, 0)),
                  pl.BlockSpec((1, pg), lambda g, bi: (0, g))],
        out_specs=[pl.BlockSpec((1, s, pg), lambda g, bi: (bi, 0, g)),
                   pl.BlockSpec((1, s, pg), lambda g, bi: (bi, 0, g)),
                   pl.BlockSpec((1, pg, pg), lambda g, bi: (g, 0, 0)),
                   pl.BlockSpec((1, pg), lambda g, bi: (0, g))],
        out_shape=[jax.ShapeDtypeStruct((b, s, pw), BF16), jax.ShapeDtypeStruct((b, s, pw), BF16),
                   jax.ShapeDtypeStruct((N_POOL_GROUPS, pg, pg), F32),
                   jax.ShapeDtypeStruct((1, pw), F32)],
        scratch_shapes=[pltpu.VMEM((s, pg), F32)],
        semantics=("parallel", "arbitrary"))(pp, pp, dcat, w_pool, scale)


def _adamw(recvs, sent, me, w, m, v, name):
    depth = len(recvs)
    r, c = w.shape[1:]
    tr = min(128, r)
    nb = r // tr
    c1 = 1.0 - ADAM_B1 ** ADAM_STEP
    c2 = 1.0 - ADAM_B2 ** ADAM_STEP
    slotted = sent[0].ndim == 3

    def body(me_ref, *refs):
        recv_refs, own_refs = refs[:depth], refs[depth:2 * depth]
        w_ref, m_ref, v_ref, g_ref, d_ref, nm_ref, nv_ref = refs[2 * depth:]
        me = me_ref[0]
        for layer in range(depth):
            @pl.when(pl.program_id(0) == layer)
            def _(layer=layer):
                own = (own_refs[layer][0] if slotted else own_refs[layer][...]).astype(F32)
                g = jnp.where(me == 0, own, recv_refs[layer][0].astype(F32))
                for sl in range(1, N_DEV):
                    g = g + jnp.where(me == sl, own, recv_refs[layer][sl].astype(F32))
                mn = ADAM_B1 * m_ref[0] + (1.0 - ADAM_B1) * g
                vn = ADAM_B2 * v_ref[0] + (1.0 - ADAM_B2) * (g * g)
                m_hat = mn / c1
                v_hat = vn / c2
                g_ref[0] = g
                d_ref[0] = -ADAM_LR * (m_hat / (jnp.sqrt(v_hat) + ADAM_EPS) + ADAM_WD * w_ref[0])
                nm_ref[0] = mn
                nv_ref[0] = vn

    def blk(layer):
        return lambda l, i: jnp.clip(i + (l - layer) * nb, 0, nb - 1)

    in_specs = [pl.BlockSpec((N_DEV, tr, c), lambda l, i, me_ref, f=blk(layer): (0, f(l, i), 0))
                for layer in range(depth)]
    if slotted:
        in_specs += [pl.BlockSpec((1, tr, c),
                                  lambda l, i, me_ref, f=blk(layer): (me_ref[0], f(l, i), 0))
                     for layer in range(depth)]
    else:
        in_specs += [pl.BlockSpec((tr, c), lambda l, i, me_ref, f=blk(layer): (f(l, i), 0))
                     for layer in range(depth)]
    row = pl.BlockSpec((1, tr, c), lambda l, i, me_ref: (l, i, 0))
    return pl.pallas_call(
        body, name=name, out_shape=[jax.ShapeDtypeStruct((depth, r, c), F32)] * 4,
        grid_spec=pltpu.PrefetchScalarGridSpec(
            num_scalar_prefetch=1, grid=(depth, nb), in_specs=in_specs + [row, row, row],
            out_specs=[row] * 4),
        compiler_params=pltpu.CompilerParams(dimension_semantics=("arbitrary", "arbitrary"),
                                             vmem_limit_bytes=VMEM_LIMIT_BYTES),
    )(me, *recvs, *sent, w, m, v)


def _pack_w_in(gathered, a, heads, pw):
    d = gathered.shape[1]
    w_full = jnp.transpose(gathered, (1, 0, 2)).reshape(d, -1)
    wf = jnp.pad(w_full[:, 4 * a:4 * a + heads], ((0, 0), (0, LANES - heads)))
    return w_full, w_full[:, 4 * a + heads:], wf


def _unpack_dw_in(parts, heads):
    dq, dk, dv, dz, dwf, du, dzp = parts
    d = dq.shape[0]
    full = jnp.concatenate([dq, dk, dv, dz, dwf[:, :heads], du, dzp], axis=1)
    return jnp.transpose(full.reshape(d, N_DEV, -1), (1, 0, 2))


def kernel(x, p, norm_pre, norm_post, w_in, b_f, w_pool, pool_scale, w_out, w_pg, w_pe, loss_target, m_norm_pre, m_norm_post, m_w_in, m_b_f, m_w_pool, m_pool_scale, m_w_out, m_w_pg, m_w_pe, v_norm_pre, v_norm_post, v_w_in, v_b_f, v_w_pool, v_pool_scale, v_w_out, v_w_pg, v_w_pe):
    depth = w_in.shape[0]
    b, s, d = x.shape
    t = b * s
    heads = b_f.shape[1]
    a = heads * HEAD_DIM
    pairs = a // LANES
    pw = pool_scale.shape[1]
    pg = pw // N_POOL_GROUPS
    ple = p.shape[-1]
    mix_w = a + pw

    me = 4 * lax.axis_index("x") + 2 * lax.axis_index("y") + lax.axis_index("c")
    shard = {
        "w_in": [w_in[i].astype(BF16) for i in range(depth)],
        "w_pool": [w_pool[i].reshape(N_POOL_GROUPS * (pg // N_DEV), pg).astype(BF16)
                   for i in range(depth)],
        "w_out": [w_out[i].astype(BF16) for i in range(depth)],
        "w_pg": [w_pg[i].astype(BF16) for i in range(depth)],
        "w_pe": [w_pe[i].astype(BF16) for i in range(depth)],
    }
    names = list(shard)
    rest = names[1:]

    def unpack_rest(lands, layer, which):
        g = {nm: _with_own(ld, shard[nm][layer], me) for nm, ld in zip(which, lands)}
        g_pool = g["w_pool"].reshape(N_DEV, N_POOL_GROUPS, pg // N_DEV, pg)
        return dict(wpool=jnp.transpose(g_pool, (1, 0, 2, 3)).reshape(N_POOL_GROUPS, pg, pg),
                    wout=g["w_out"].reshape(mix_w, d), wpg=g["w_pg"].reshape(d, d),
                    wpe=jnp.transpose(g["w_pe"], (1, 0, 2)).reshape(ple, d))

    g_in0 = _gather_two_level(shard["w_in"][0], "gather_w_in0")
    rest0, tok_rest0 = _exchange_start([(shard[nm][0], False) for nm in rest], g_in0,
                                       "gather_rest0_start")
    later, tok = [], tok_rest0
    for i in range(1, depth):
        hdl, tk_i = _exchange_start([(shard[nm][i], False) for nm in names], g_in0,
                                    "gather_layer%d_start" % i)
        later.append(hdl)
        tok = tok + tk_i

    h = x.reshape(t, d)
    saved = []
    layers = []
    for i in range(depth):
        sv = dict(h=h)
        g_pre = norm_pre[i:i + 1]
        g_post = norm_post[i:i + 1]
        bf = jnp.pad(b_f[i:i + 1], ((0, 0), (0, LANES - heads)))
        scale = pool_scale[i:i + 1]
        if i == 0:
            lw = dict(zip(("wa", "wp", "wf"), _pack_w_in(g_in0, a, heads, pw)))
            g_pre = g_pre + tok
        else:
            lands = _exchange_wait(later[i - 1], h, "gather_layer%d_wait" % i)
            g_in = _with_own(lands[0], shard["w_in"][i], me)
            lw = dict(zip(("wa", "wp", "wf"), _pack_w_in(g_in, a, heads, pw)))
            lw.update(unpack_rest(lands[1:], i, rest))
        hn = _rms_fwd(h, g_pre, "rms_pre")
        pa = _matmul([(hn, lw["wa"])], "nn", BF16, "proj_attn", n_dim=4 * a, tn=2048,
                     n_outer=True).reshape(b, s, 4 * a)
        pp = _matmul([(hn, lw["wp"])], "nn", BF16, "proj_pool", tn=2048,
                     n_outer=True).reshape(b, s, 2 * pw)
        fl = _matmul([(hn, lw["wf"])], "nn", F32, "proj_gate").reshape(b, s, LANES)
        c = _gates_fwd(fl, bf, "gates_fwd")
        qa, ka, kat, va, vt = _attn_prep_fwd(pa, c, "attn_prep_fwd")
        o, ga, lse = _attn_fwd(qa, ka, vt, pa, "attn_fwd")
        if i == 0:
            lw.update(unpack_rest(_exchange_wait(rest0, lse, "gather_rest0_wait"), 0, rest))
        layers.append(lw)
        gp = _pool_fwd(pp, lw["wpool"], scale, "pool_fwd")
        ga2 = ga.reshape(t, a)
        gp2 = gp.reshape(t, pw)
        mix = _matmul([(ga2, lw["wout"], 0, 0), (gp2, lw["wout"], a, 0)], "nn", F32, "mix_out")
        h1, h1b = _post_fwd(h, mix, g_post, "post_fwd")
        pb = p[i].reshape(t, ple).astype(BF16)
        gpre = _matmul([(h1b, lw["wpg"])], "nn", F32, "ple_gate")
        e = _matmul([(pb, lw["wpe"])], "nn", F32, "ple_embed")
        if i < depth - 1:
            h = _ple_fwd(h1, gpre, e, "ple_fwd")
        sv.update(hn=hn, pa=pa, pp=pp, fl=fl, bf=bf, qa=qa, ka=ka, kat=kat, va=va, o=o, lse=lse, ga=ga2,
                  gp=gp2, mix=mix,
                  h1b=h1b, pb=pb, gpre=gpre, e=e, g_pre=g_pre, g_post=g_post, scale=scale)
        saved.append(sv)

    dh, sq = _ple_loss(h1, gpre, e, loss_target.reshape(t, d), "ple_loss")
    loss = lax.psum(0.5 * jnp.sum(sq) / d, MESH_AXES)

    big = {nm: [None] * depth for nm in names}
    small = {nm: [None] * depth for nm in ("norm_pre", "norm_post", "b_f", "pool_scale")}
    grad_handles = [None] * depth
    rest_handles = [None] * depth
    for i in reversed(range(depth)):
        lw, sv = layers[i], saved[i]
        de, dpre = _ple_bwd(dh, sv["gpre"], sv["e"], "ple_bwd")
        dwpe = _matmul([(sv["pb"], de)], "tn", BF16, "dw_pe", tm=1024)
        dwpg = _matmul([(sv["h1b"], dpre)], "tn", BF16, "dw_pg", tm=1024)
        dh1, dmix, dg_post = _matmul_rows(
            [(dpre, lw["wpg"], 0)], [dh, sv["mix"]], sv["g_post"], _post_bwd_epilogue,
            (F32, BF16), "d_h1_post_bwd", tm=512)
        dwout = jnp.concatenate(
            [_matmul([(sv["ga"], dmix)], "tn", BF16, "dw_out_attn", tm=1024),
             _matmul([(sv["gp"], dmix)], "tn", BF16, "dw_out_pool", tm=1024)], axis=0)
        dcat = _matmul([(dmix, lw["wout"])], "nt", BF16, "d_cat", tn=2048).reshape(b, s, mix_w)
        du, dzp, dwpool, dscale = _pool_bwd(sv["pp"], dcat, lw["wpool"], sv["scale"], a // pg,
                                            "pool_bwd")
        big["w_pool"][i] = jnp.transpose(
            dwpool.astype(BF16).reshape(N_POOL_GROUPS, N_DEV, pg // N_DEV, pg), (1, 0, 2, 3)
        ).reshape(N_DEV, N_POOL_GROUPS * (pg // N_DEV), pg)
        big["w_out"][i] = dwout.reshape(N_DEV, mix_w // N_DEV, d)
        big["w_pg"][i] = dwpg.reshape(N_DEV, d // N_DEV, d)
        big["w_pe"][i] = jnp.transpose(dwpe.reshape(ple, N_DEV, d // N_DEV), (1, 0, 2))
        rest_handles[i], tok = _exchange_start([(big[nm][i], True) for nm in rest], du,
                                               "grads_rest%d_start" % i)
        qab, doa, dz = _attn_prep_bwd(dcat, sv["pa"], sv["o"], sv["lse"] + tok, sv["qa"],
                                      "attn_prep_bwd")
        dq, dk, dv, dcp = _attn_bwd(sv["ka"], sv["kat"], sv["va"], qab, doa, "attn_bwd")
        dc = jnp.transpose(dcp[..., :2], (0, 2, 1, 3)).reshape(b, s, heads)
        dc = jnp.pad(dc, ((0, 0), (0, 0), (0, LANES - heads)))
        dfl, dbf = _gates_bwd(dc, sv["fl"], sv["bf"], heads, "gates_bwd")
        dproj = [g_.reshape(t, -1) for g_ in (dq, dk, dv, dz, dfl, du, dzp)]
        dw_parts = [_matmul([(sv["hn"], g_)], "tn", BF16, "dw_in_%d" % n_, tm=1024)
                    for n_, g_ in enumerate(dproj)]

        big["w_in"][i] = _unpack_dw_in(dw_parts, heads)
        grad_handles[i], tok = _exchange_start([(big["w_in"][i], True)], dw_parts[-1],
                                               "grads_w_in%d_start" % i)

        dq2, dk2, dv2, dz2, dfl2, du2, dzp2 = dproj
        dh, dg_pre = _matmul_rows(
            [(dq2, lw["wa"], 0), (dk2, lw["wa"], a), (dv2, lw["wa"], 2 * a), (dz2, lw["wa"], 3 * a),
             (du2, lw["wp"], 0), (dzp2, lw["wp"], pw), (dfl2, lw["wf"] + tok.astype(BF16), 0)],
            [sv["h"], dh1], sv["g_pre"] + tok, _pre_bwd_epilogue, (F32,), "d_hn_pre_bwd", tm=256)
        small["norm_pre"][i] = dg_pre
        small["norm_post"][i] = dg_post
        small["b_f"][i] = jnp.sum(dbf, axis=0)
        small["pool_scale"][i] = dscale
    grad_x = dh.reshape(b, s, d)

    width = max(d, pw)
    small_names = ("norm_pre", "norm_post", "pool_scale", "b_f")

    def small_rows(get):
        rows = []
        for nm in small_names:
            for i in range(depth):
                v_ = get(nm, i)
                rows.append(jnp.pad(v_, ((0, 0), (0, width - v_.shape[1]))))
        return jnp.concatenate(rows, axis=0)

    small_g = small_rows(lambda nm, i: small[nm][i])
    (small_recv,) = _exchange([(small_g, False)], "exchange_small")
    me1 = jnp.reshape(me, (1,)).astype(jnp.int32)

    weights = dict(norm_pre=norm_pre, norm_post=norm_post, w_in=w_in, b_f=b_f, w_pool=w_pool,
                   pool_scale=pool_scale, w_out=w_out, w_pg=w_pg, w_pe=w_pe)
    mom1 = dict(norm_pre=m_norm_pre, norm_post=m_norm_post, w_in=m_w_in, b_f=m_b_f, w_pool=m_w_pool,
                pool_scale=m_pool_scale, w_out=m_w_out, w_pg=m_w_pg, w_pe=m_w_pe)
    mom2 = dict(norm_pre=v_norm_pre, norm_post=v_norm_post, w_in=v_w_in, b_f=v_b_f, w_pool=v_w_pool,
                pool_scale=v_pool_scale, w_out=v_w_out, w_pg=v_w_pg, w_pe=v_w_pe)

    results = {}

    def update(nm, recvs):
        shp = weights[nm].shape
        sent = [big[nm][i] for i in range(depth)]
        flat = lambda arr: arr.reshape((depth,) + sent[0].shape[1:])
        outs = _adamw(recvs, sent, me1, flat(weights[nm]), flat(mom1[nm]), flat(mom2[nm]),
                      "adamw_" + nm)
        results[nm] = [o_.reshape(shp) for o_ in outs]
        return outs[0]

    got_rest = [_exchange_wait(rest_handles[i], dh, "grads_rest%d_wait" % i) for i in range(depth)]
    for j, nm in enumerate(rest):
        last = update(nm, [got_rest[i][j] for i in range(depth)])

    small_w = small_rows(lambda nm, i: weights[nm][i:i + 1])[None]
    small_m = small_rows(lambda nm, i: mom1[nm][i:i + 1])[None]
    small_v = small_rows(lambda nm, i: mom2[nm][i:i + 1])[None]
    outs = _adamw([small_recv], [small_g], me1, small_w, small_m, small_v, "adamw_small")
    for j, nm in enumerate(small_names):
        cols = weights[nm].shape[1]
        results[nm] = [o_[0, j * depth:(j + 1) * depth, :cols] for o_ in outs]

    got_w_in = [_exchange_wait(grad_handles[i], last + outs[0][0, 0, 0], "grads_w_in%d_wait" % i)[0]
                for i in range(depth)]
    update("w_in", got_w_in)

    order = ("norm_pre", "norm_post", "w_in", "b_f", "w_pool", "pool_scale", "w_out", "w_pg", "w_pe")
    return (loss, grad_x, *[results[nm][0] for nm in order], *[results[nm][1] for nm in order],
            *[results[nm][2] for nm in order], *[results[nm][3] for nm in order])
```

```python
import functools
import math

import jax
import jax.numpy as jnp
from jax import lax
from jax.experimental import pallas as pl
from jax.experimental.pallas import tpu as pltpu

N_DEV = 8
MESH_AXES = ("x", "y", "c")
HEAD_DIM = 64
LANES = 128
N_POOL_GROUPS = 4
EPS = 1e-6
ADAM_LR = 0.001
ADAM_B1 = 0.9
ADAM_B2 = 0.999
ADAM_EPS = 1e-08
ADAM_WD = 0.01
ADAM_STEP = 10
VMEM_LIMIT_BYTES = 56 * 1024 * 1024
F32 = jnp.float32
BF16 = jnp.bfloat16
NEG_INF = float("-inf")


def _call(body, *, name, grid, in_specs, out_specs, out_shape, scratch_shapes=(), semantics=None):
    return pl.pallas_call(
        body, name=name, grid=grid, in_specs=in_specs, out_specs=out_specs, out_shape=out_shape,
        scratch_shapes=list(scratch_shapes),
        compiler_params=pltpu.CompilerParams(dimension_semantics=semantics,
                                             vmem_limit_bytes=VMEM_LIMIT_BYTES))


def _sigmoid(z):
    return 1.0 / (1.0 + jnp.exp(-z))


def _dot(a, b, dims):
    return lax.dot_general(a, b, (dims, ((), ())), preferred_element_type=F32)


NN = ((1,), (0,))
NT = ((1,), (1,))
TN = ((0,), (0,))


def _exchange(items, name):
    n = len(items)
    modes = [s for _, s in items]
    out_shapes = []
    for a, s in items:
        shp = a.shape[1:] if s else a.shape
        out_shapes.append(jax.ShapeDtypeStruct((N_DEV,) + tuple(shp), a.dtype))

    def body(*refs):
        ins = refs[:n]
        outs = refs[n:2 * n]
        send_sems, recv_sems, local_sems = refs[2 * n:]
        x, y, c = (lax.axis_index(ax) for ax in MESH_AXES)
        me = 4 * x + 2 * y + c
        started = []
        for i in range(n):
            mine = ins[i].at[me] if modes[i] else ins[i]
            loc = pltpu.make_async_copy(mine, outs[i].at[me], local_sems.at[i])
            loc.start()
            started.append(loc)
        remote = []
        for k in range(1, N_DEV):
            px = x ^ ((k >> 2) & 1)
            py = y ^ ((k >> 1) & 1)
            pc = c ^ (k & 1)
            peer = me ^ k
            for i in range(n):
                src = ins[i].at[peer] if modes[i] else ins[i]
                cp = pltpu.make_async_remote_copy(
                    src_ref=src, dst_ref=outs[i].at[me],
                    send_sem=send_sems.at[i, k - 1], recv_sem=recv_sems.at[i, k - 1],
                    device_id=(px, py, pc), device_id_type=pl.DeviceIdType.MESH)
                cp.start()
                remote.append(cp)
        for cp in remote:
            cp.wait()
        for loc in started:
            loc.wait()

    hbm = pl.BlockSpec(memory_space=pltpu.HBM)
    return pl.pallas_call(
        body, name=name, out_shape=out_shapes,
        in_specs=[hbm] * n, out_specs=[hbm] * n,
        scratch_shapes=[pltpu.SemaphoreType.DMA((n, N_DEV - 1)),
                        pltpu.SemaphoreType.DMA((n, N_DEV - 1)),
                        pltpu.SemaphoreType.DMA((n,))],
    )(*[a for a, _ in items])


def _gather_two_level(shard, name):
    def body(x_ref, out_ref, send_sems, recv_sems, local_sem):
        x, y, c = (lax.axis_index(ax) for ax in MESH_AXES)
        sibling = (x, y, 1 - c)
        chips = [(1 - x, y), (x, 1 - y), (1 - x, 1 - y)]

        def slot(px, py, pc):
            return out_ref.at[4 * px + 2 * py + pc]

        def copy(k, block, to, src=None):
            return pltpu.make_async_remote_copy(
                src_ref=slot(*block) if src is None else src, dst_ref=slot(*block),
                send_sem=send_sems.at[k], recv_sem=recv_sems.at[k],
                device_id=to, device_id_type=pl.DeviceIdType.MESH)

        mine = pltpu.make_async_copy(x_ref, slot(x, y, c), local_sem)
        mine.start()
        first = [copy(0, (x, y, c), sibling, src=x_ref)]
        first += [copy(1 + j, (x, y, c), (*chip, c), src=x_ref) for j, chip in enumerate(chips)]
        for cp in first:
            cp.start()
        passed = [copy(4 + j, (*chip, c), sibling) for j, chip in enumerate(chips)]
        for j, chip in enumerate(chips):
            copy(1 + j, (*chip, c), (x, y, c)).wait_recv()
            passed[j].start()
        copy(0, (x, y, 1 - c), (x, y, c)).wait_recv()
        for j, chip in enumerate(chips):
            copy(4 + j, (*chip, 1 - c), (x, y, c)).wait_recv()
        for cp in first + passed:
            cp.wait_send()
        mine.wait()

    hbm = pl.BlockSpec(memory_space=pltpu.HBM)
    return pl.pallas_call(
        body, name=name, out_shape=jax.ShapeDtypeStruct((N_DEV,) + shard.shape, shard.dtype),
        in_specs=[hbm], out_specs=hbm,
        scratch_shapes=[pltpu.SemaphoreType.DMA((N_DEV - 1,)), pltpu.SemaphoreType.DMA((N_DEV - 1,)),
                        pltpu.SemaphoreType.DMA],
    )(shard)


def _peer_copies(srcs, lands, modes, send_sems, recv_sems):
    x, y, c = (lax.axis_index(ax) for ax in MESH_AXES)
    me = 4 * x + 2 * y + c
    copies = []
    for k in range(1, N_DEV):
        peer_id = (x ^ ((k >> 2) & 1), y ^ ((k >> 1) & 1), c ^ (k & 1))
        for i, scatter in enumerate(modes):
            src = srcs[i].at[me ^ k] if scatter else srcs[i]
            pair = i * (N_DEV - 1) + k - 1
            copies.append(pltpu.make_async_remote_copy(
                src_ref=src, dst_ref=lands[i].at[me],
                send_sem=send_sems.at[pair], recv_sem=recv_sems.at[pair],
                device_id=peer_id, device_id_type=pl.DeviceIdType.MESH))
    return copies


def _exchange_start(items, after, name):
    n = len(items)
    modes = [s for _, s in items]
    srcs = [pltpu.with_memory_space_constraint(a, pltpu.HBM) for a, _ in items]
    lands = []
    for a, s in items:
        shp = (N_DEV,) + tuple(a.shape[1:] if s else a.shape)
        lands.append(pltpu.with_memory_space_constraint(lax.empty(shp, a.dtype), pltpu.HBM))

    def body(*refs):
        send_sems, recv_sems = refs[2 * n + 1], refs[2 * n + 2]
        token = refs[-1]
        for cp in _peer_copies(refs[:n], refs[n:2 * n], modes, send_sems, recv_sems):
            cp.start()
        token[...] = jnp.zeros_like(token)

    hbm = pl.BlockSpec(memory_space=pltpu.HBM)
    sem = pl.BlockSpec(memory_space=pltpu.SEMAPHORE)
    outs = pl.pallas_call(
        body, name=name,
        out_shape=(pltpu.SemaphoreType.DMA((n * (N_DEV - 1),)),
                   pltpu.SemaphoreType.DMA((n * (N_DEV - 1),)),
                   *[pltpu.HBM(a.shape, a.dtype) for a in srcs + lands],
                   jax.ShapeDtypeStruct((8, LANES), F32)),
        in_specs=[hbm] * (2 * n) + [pl.BlockSpec(memory_space=pl.ANY)],
        out_specs=(sem, sem, *([hbm] * (2 * n)), pl.BlockSpec(memory_space=pltpu.VMEM)),
        input_output_aliases={i: 2 + i for i in range(2 * n)},
        compiler_params=pltpu.CompilerParams(
            has_side_effects=pltpu.SideEffectType.DATAFLOW_SIDE_EFFECTING),
    )(*srcs, *lands, after)
    handle = (modes, outs[0], outs[1], list(outs[2:2 + n]), list(outs[2 + n:2 + 2 * n]))
    return handle, outs[-1][0, 0]


def _exchange_wait(handle, after, name):
    modes, send_sems, recv_sems, srcs, lands = handle
    n = len(modes)

    def body(*refs):
        for cp in _peer_copies(refs[:n], refs[n:2 * n], modes, refs[2 * n], refs[2 * n + 1]):
            cp.wait_send()
            cp.wait_recv()

    hbm = pl.BlockSpec(memory_space=pltpu.HBM)
    sem = pl.BlockSpec(memory_space=pltpu.SEMAPHORE)
    outs = pl.pallas_call(
        body, name=name,
        out_shape=tuple(pltpu.HBM(a.shape, a.dtype) for a in srcs + lands),
        in_specs=[hbm] * (2 * n) + [sem, sem, pl.BlockSpec(memory_space=pl.ANY)],
        out_specs=tuple([hbm] * (2 * n)),
        input_output_aliases={i: i for i in range(2 * n)},
        compiler_params=pltpu.CompilerParams(
            has_side_effects=pltpu.SideEffectType.DATAFLOW_SIDE_EFFECTING),
    )(*srcs, *lands, send_sems, recv_sems, after)
    return list(outs[n:])


def _with_own(slots, own, me):
    idx = lax.broadcasted_iota(jnp.int32, (N_DEV,) + (1,) * own.ndim, 0)
    return jnp.where(idx == me, own[None], slots)


def _matmul(pairs, mode, out_dtype, name, n_dim=None, tm=512, tn=1024, tk=1024, n_outer=False):
    dims = {"nn": NN, "nt": NT, "tn": TN}[mode]
    pairs = [tuple(pr) + (0, 0) * (len(pr) == 2) for pr in pairs]
    a0, b0 = pairs[0][:2]
    m_dim = a0.shape[1] if mode == "tn" else a0.shape[0]
    if n_dim is None:
        n_dim = b0.shape[0] if mode == "nt" else b0.shape[1]
    tm = min(tm, m_dim)
    tn = min(tn, n_dim)
    segs = []
    off = 0
    for a, _, k0, n0 in pairs:
        k_dim = a.shape[0] if mode == "tn" else a.shape[1]
        t = min(tk, k_dim)
        segs.append((off, k_dim // t, t, k0 // t, n0 // tn))
        off += k_dim // t
    nk = off
    n_pairs = len(pairs)

    def ij(g0, g1):
        return (g1, g0) if n_outer else (g0, g1)

    in_specs = []
    for (o, cnt, t, kb, nb) in segs:
        def kc(kk, o=o, cnt=cnt):
            return jnp.clip(kk - o, 0, cnt - 1)
        if mode == "tn":
            in_specs.append(pl.BlockSpec((t, tm), lambda g0, g1, kk, kc=kc: (kc(kk), ij(g0, g1)[0])))
        else:
            in_specs.append(pl.BlockSpec((tm, t), lambda g0, g1, kk, kc=kc: (ij(g0, g1)[0], kc(kk))))
        if mode == "nt":
            in_specs.append(pl.BlockSpec((tn, t), lambda g0, g1, kk, kc=kc, kb=kb, nb=nb:
                                         (nb + ij(g0, g1)[1], kb + kc(kk))))
        else:
            in_specs.append(pl.BlockSpec((t, tn), lambda g0, g1, kk, kc=kc, kb=kb, nb=nb:
                                         (kb + kc(kk), nb + ij(g0, g1)[1])))

    one_shot = all(sg[1] == 1 for sg in segs)

    def body_sum(*refs):
        total = _dot(refs[0][...], refs[1][...], dims)
        for idx in range(1, n_pairs):
            total = total + _dot(refs[2 * idx][...], refs[2 * idx + 1][...], dims)
        refs[2 * n_pairs][...] = total.astype(out_dtype)

    def body(*refs):
        out_ref = refs[2 * n_pairs]
        acc = refs[2 * n_pairs + 1]
        kk = pl.program_id(2)

        @pl.when(kk == 0)
        def _():
            acc[...] = jnp.zeros_like(acc)

        for idx, (o, cnt) in enumerate(sg[:2] for sg in segs):
            @pl.when((kk >= o) & (kk < o + cnt))
            def _(idx=idx):
                acc[...] += _dot(refs[2 * idx][...], refs[2 * idx + 1][...], dims)

        @pl.when(kk == nk - 1)
        def _():
            out_ref[...] = acc[...].astype(out_dtype)

    flat = [t for pr in pairs for t in pr[:2]]
    tiles = (m_dim // tm, n_dim // tn)
    return _call(body_sum if one_shot else body, name=name,
                 grid=ij(*tiles) + (1 if one_shot else nk,), in_specs=in_specs,
                 out_specs=pl.BlockSpec((tm, tn), lambda g0, g1, kk: ij(g0, g1)),
                 out_shape=jax.ShapeDtypeStruct((m_dim, n_dim), out_dtype),
                 scratch_shapes=[] if one_shot else [pltpu.VMEM((tm, tn), F32)],
                 semantics=("parallel", "parallel", "arbitrary"))(*flat)


def _row_tile(t):
    return min(512, t)


def _rms_fwd(h, g, name):
    t, d = h.shape
    tt = _row_tile(t)

    def body(h_ref, g_ref, o_ref):
        hv = h_ref[...]
        r = lax.rsqrt(jnp.mean(hv * hv, axis=-1, keepdims=True) + EPS)
        o_ref[...] = (hv * r * g_ref[...]).astype(BF16)

    row = pl.BlockSpec((tt, d), lambda i: (i, 0))
    vec = pl.BlockSpec((1, d), lambda i: (0, 0))
    return _call(body, name=name, grid=(t // tt,), in_specs=[row, vec], out_specs=row,
                 out_shape=jax.ShapeDtypeStruct((t, d), BF16), semantics=("parallel",))(h, g)


def _post_fwd(h, mix, g, name):
    t, d = h.shape
    tt = _row_tile(t)

    def body(h_ref, m_ref, g_ref, o_ref, ob_ref):
        mv = m_ref[...]
        r = lax.rsqrt(jnp.mean(mv * mv, axis=-1, keepdims=True) + EPS)
        h1 = h_ref[...] + mv * r * g_ref[...]
        o_ref[...] = h1
        ob_ref[...] = h1.astype(BF16)

    row = pl.BlockSpec((tt, d), lambda i: (i, 0))
    vec = pl.BlockSpec((1, d), lambda i: (0, 0))
    return _call(body, name=name, grid=(t // tt,), in_specs=[row, row, vec], out_specs=[row, row],
                 out_shape=[jax.ShapeDtypeStruct((t, d), F32), jax.ShapeDtypeStruct((t, d), BF16)],
                 semantics=("parallel",))(h, mix, g)


def _ple_fwd(h1, gpre, e, name):
    t, d = h1.shape
    tt = _row_tile(t)

    def body(h_ref, g_ref, e_ref, o_ref):
        o_ref[...] = h_ref[...] + _sigmoid(g_ref[...]) * e_ref[...]

    row = pl.BlockSpec((tt, d), lambda i: (i, 0))
    return _call(body, name=name, grid=(t // tt,), in_specs=[row, row, row], out_specs=row,
                 out_shape=jax.ShapeDtypeStruct((t, d), F32), semantics=("parallel",))(h1, gpre, e)


def _ple_loss(h1, gpre, e, target, name):
    t, d = h1.shape
    tt = _row_tile(t)

    def body(h_ref, g_ref, e_ref, t_ref, dy_ref, s_ref):
        @pl.when(pl.program_id(0) == 0)
        def _():
            s_ref[...] = jnp.zeros_like(s_ref)
        diff = h_ref[...] + _sigmoid(g_ref[...]) * e_ref[...] - t_ref[...]
        dy_ref[...] = diff * (1.0 / d)
        s_ref[...] += jnp.sum(diff * diff, axis=0, keepdims=True)

    row = pl.BlockSpec((tt, d), lambda i: (i, 0))
    vec = pl.BlockSpec((1, d), lambda i: (0, 0))
    return _call(body, name=name, grid=(t // tt,), in_specs=[row] * 4, out_specs=[row, vec],
                 out_shape=[jax.ShapeDtypeStruct((t, d), F32), jax.ShapeDtypeStruct((1, d), F32)],
                 semantics=("arbitrary",))(h1, gpre, e, target)


def _ple_bwd(dh2, gpre, e, name):
    t, d = dh2.shape
    tt = _row_tile(t)

    def body(d_ref, g_ref, e_ref, de_ref, dp_ref):
        gate = _sigmoid(g_ref[...])
        dv = d_ref[...]
        de_ref[...] = (dv * gate).astype(BF16)
        dp_ref[...] = (dv * e_ref[...] * gate * (1.0 - gate)).astype(BF16)

    row = pl.BlockSpec((tt, d), lambda i: (i, 0))
    return _call(body, name=name, grid=(t // tt,), in_specs=[row, row, row], out_specs=[row, row],
                 out_shape=[jax.ShapeDtypeStruct((t, d), BF16)] * 2,
                 semantics=("parallel",))(dh2, gpre, e)


def _matmul_rows(pairs, rows_in, vec_in, epilogue, row_dtypes, name, tm):
    n_pairs = len(pairs)
    m_dim = pairs[0][0].shape[0]
    n_dim = pairs[0][1].shape[0]
    tm = min(tm, m_dim)
    in_specs = []
    for a, _, k0 in pairs:
        k_dim = a.shape[1]
        in_specs.append(pl.BlockSpec((tm, k_dim), lambda i: (i, 0)))
        in_specs.append(pl.BlockSpec((n_dim, k_dim), lambda i, kb=k0 // k_dim: (0, kb)))
    row = pl.BlockSpec((tm, n_dim), lambda i: (i, 0))
    vec = pl.BlockSpec((1, n_dim), lambda i: (0, 0))
    n_rows = len(rows_in)

    def body(*refs):
        ops = refs[:2 * n_pairs]
        row_refs = refs[2 * n_pairs:2 * n_pairs + n_rows]
        vec_ref = refs[2 * n_pairs + n_rows]
        outs = refs[2 * n_pairs + n_rows + 1:]
        total = _dot(ops[0][...], ops[1][...], NT)
        for idx in range(1, n_pairs):
            total = total + _dot(ops[2 * idx][...], ops[2 * idx + 1][...], NT)
        results, partial = epilogue(total, [r[...] for r in row_refs], vec_ref[...])
        for out_ref, val in zip(outs[:-1], results):
            out_ref[...] = val.astype(out_ref.dtype)

        @pl.when(pl.program_id(0) == 0)
        def _():
            outs[-1][...] = jnp.zeros_like(outs[-1])
        outs[-1][...] += partial

    flat = [t_ for a, b_, _ in pairs for t_ in (a, b_)]
    return _call(body, name=name, grid=(m_dim // tm,),
                 in_specs=in_specs + [row] * n_rows + [vec],
                 out_specs=[row] * len(row_dtypes) + [vec],
                 out_shape=[jax.ShapeDtypeStruct((m_dim, n_dim), dt) for dt in row_dtypes]
                 + [jax.ShapeDtypeStruct((1, n_dim), F32)],
                 semantics=("arbitrary",))(*flat, *rows_in, vec_in)


def _post_bwd_epilogue(t1, rows, g):
    dh2, mv = rows
    dh1 = dh2 + t1
    r = lax.rsqrt(jnp.mean(mv * mv, axis=-1, keepdims=True) + EPS)
    w = dh1 * g
    dot = jnp.mean(w * mv, axis=-1, keepdims=True)
    dmix = r * w - mv * (r * r * r) * dot
    return (dh1, dmix), jnp.sum(dh1 * mv * r, axis=0, keepdims=True)


def _pre_bwd_epilogue(dhn, rows, g):
    hv, dh1 = rows
    r = lax.rsqrt(jnp.mean(hv * hv, axis=-1, keepdims=True) + EPS)
    w = dhn * g
    dot = jnp.mean(w * hv, axis=-1, keepdims=True)
    return (dh1 + r * w - hv * (r * r * r) * dot,), jnp.sum(dhn * hv * r, axis=0, keepdims=True)


def _split3(v):
    hi = v.astype(BF16)
    r1 = v - hi.astype(F32)
    mid = r1.astype(BF16)
    lo = (r1 - mid.astype(F32)).astype(BF16)
    return hi, mid, lo


def _scan_tile(s):
    return min(256, s)


def _gates_fwd(fl, bf, name):
    b, s, _ = fl.shape

    tb = _scan_tile(s)

    def body(f_ref, b_ref, c_ref):
        dst = lax.broadcasted_iota(jnp.int32, (tb, tb), 0)
        src = lax.broadcasted_iota(jnp.int32, (tb, tb), 1)
        lower = (src <= dst).astype(BF16)
        carry = jnp.zeros((1, LANES), F32)
        for blk_i in range(s // tb):
            rows = slice(blk_i * tb, (blk_i + 1) * tb)
            xv = f_ref[0, rows, :] + b_ref[...]
            lf = jnp.minimum(xv, 0.0) - jnp.log(1.0 + jnp.exp(-jnp.abs(xv)))
            acc = carry
            for part in _split3(lf):
                acc = acc + _dot(lower, part, NN)
            c_ref[0, rows, :] = acc
            carry = acc[tb - 1:tb, :]

    blk = pl.BlockSpec((1, s, LANES), lambda i: (i, 0, 0))
    return _call(body, name=name, grid=(b,),
                 in_specs=[blk, pl.BlockSpec((1, LANES), lambda i: (0, 0))],
                 out_specs=blk, out_shape=jax.ShapeDtypeStruct((b, s, LANES), F32),
                 semantics=("parallel",))(fl, bf)


def _gates_bwd(dc, fl, bf, heads, name):
    b, s, _ = fl.shape

    tb = _scan_tile(s)

    def body(d_ref, f_ref, b_ref, o_ref, db_ref):
        dst = lax.broadcasted_iota(jnp.int32, (tb, tb), 0)
        src = lax.broadcasted_iota(jnp.int32, (tb, tb), 1)
        later = (src >= dst).astype(BF16)
        lane = lax.broadcasted_iota(jnp.int32, (tb, LANES), 1)
        carry = jnp.zeros((1, LANES), F32)
        db = jnp.zeros((1, LANES), F32)
        for blk_i in reversed(range(s // tb)):
            rows = slice(blk_i * tb, (blk_i + 1) * tb)
            dlf = carry
            for part in _split3(d_ref[0, rows, :]):
                dlf = dlf + _dot(later, part, NN)
            carry = dlf[0:1, :]
            xv = f_ref[0, rows, :] + b_ref[...]
            dfl = jnp.where(lane < heads, dlf * _sigmoid(-xv), 0.0)
            o_ref[0, rows, :] = dfl.astype(BF16)
            db = db + jnp.sum(dfl, axis=0, keepdims=True)
        db_ref[0] = db

    blk = pl.BlockSpec((1, s, LANES), lambda i: (i, 0, 0))
    return _call(body, name=name, grid=(b,),
                 in_specs=[blk, blk, pl.BlockSpec((1, LANES), lambda i: (0, 0))],
                 out_specs=[blk, pl.BlockSpec((1, 1, LANES), lambda i: (i, 0, 0))],
                 out_shape=[jax.ShapeDtypeStruct((b, s, LANES), BF16),
                            jax.ShapeDtypeStruct((b, 1, LANES), F32)],
                 semantics=("parallel",))(dc, fl, bf)


LANE_CQ = 64
LANE_CK = 67
LANE_LSE = 70
LANE_D = 64
N_PARTS = 3


def _attn_tiles(s):
    return min(512, s), min(256, s)


def _lanes_in(lane, first):
    return (lane >= first) & (lane < first + N_PARTS)


def _attn_prep_fwd(pa, c, name):
    b, s, a4 = pa.shape
    pairs = a4 // (4 * LANES)
    scale = 1.0 / math.sqrt(HEAD_DIM)

    def body(q_ref, k_ref, v_ref, c_ref, qa_ref, ka_ref, kat_ref, va_ref, vt_ref):
        hp = pl.program_id(1)
        cv = c_ref[0]
        vv = v_ref[0]
        lane = lax.broadcasted_iota(jnp.int32, (s, LANES), 1)
        r128 = lax.broadcasted_iota(jnp.int32, (LANES, LANES), 0)
        c128 = lax.broadcasted_iota(jnp.int32, (LANES, LANES), 1)
        ident = (r128 == c128).astype(BF16)
        for j in range(2):
            head = 2 * hp + j
            move128 = (r128 == c128 + HEAD_DIM * j) & (c128 < HEAD_DIM)
            cparts = _split3(jnp.sum(jnp.where(lane == head, cv, 0.0), axis=1, keepdims=True))
            qa = _dot(q_ref[0], jnp.where(move128, scale, 0.0).astype(BF16), NN)
            ka = _dot(k_ref[0], move128.astype(BF16), NN)
            for i in range(N_PARTS):
                qa = jnp.where(lane == LANE_CQ + i, cparts[i].astype(F32), qa)
                ka = jnp.where(lane == LANE_CK + i, -cparts[i].astype(F32), ka)
            qa = jnp.where(_lanes_in(lane, LANE_CK), 1.0, qa)
            ka = jnp.where(_lanes_in(lane, LANE_CQ) | _lanes_in(lane, LANE_LSE), 1.0, ka)
            va = _dot(vv, move128.astype(BF16), NN) + jnp.where(_lanes_in(lane, LANE_D), 1.0, 0.0)
            kab = ka.astype(BF16)
            qa_ref[0, 0, j] = qa.astype(BF16)
            ka_ref[0, 0, j] = kab
            kat_ref[0, 0, j] = _dot(ident, kab, NT).astype(BF16)
            va_ref[0, 0, j] = va.astype(BF16)
        vt_ref[0, 0] = _dot(ident, vv, NT).astype(BF16)

    col_blk = lambda cidx: pl.BlockSpec((1, s, LANES), lambda bi, hp: (bi, 0, cidx * pairs + hp))
    tok = pl.BlockSpec((1, 1, 2, s, LANES), lambda bi, hp: (bi, hp, 0, 0, 0))
    tok_t = pl.BlockSpec((1, 1, 2, LANES, s), lambda bi, hp: (bi, hp, 0, 0, 0))
    tok_shape = jax.ShapeDtypeStruct((b, pairs, 2, s, LANES), BF16)
    return _call(
        body, name=name, grid=(b, pairs),
        in_specs=[col_blk(0), col_blk(1), col_blk(2),
                  pl.BlockSpec((1, s, LANES), lambda bi, hp: (bi, 0, 0))],
        out_specs=[tok, tok, tok_t, tok,
                   pl.BlockSpec((1, 1, LANES, s), lambda bi, hp: (bi, hp, 0, 0))],
        out_shape=[tok_shape, tok_shape, jax.ShapeDtypeStruct((b, pairs, 2, LANES, s), BF16),
                   tok_shape, jax.ShapeDtypeStruct((b, pairs, LANES, s), BF16)],
        semantics=("parallel", "parallel"))(pa, pa, pa, c)


def _attn_fwd(qa, ka, vt, pa, name):
    b, pairs, _, s, _ = qa.shape
    a = pairs * LANES
    tq = min(1024, s)
    nq = s // tq

    def body(q_ref, k_ref, vt_ref, z_ref, o_ref, g_ref, lse_ref):
        key_i = lax.broadcasted_iota(jnp.int32, (tq, tq), 0)
        qry_i = lax.broadcasted_iota(jnp.int32, (tq, tq), 1)

        def query_block(c):
            past = tq * c
            heads_out = []
            for j in range(2):
                qv = q_ref[0, 0, j]
                vrows = slice(HEAD_DIM * j, HEAD_DIM * (j + 1))
                sd = _dot(k_ref[0, 0, j, past:past + tq, :], qv, NT)
                sd = jnp.where(key_i <= qry_i, sd, NEG_INF)
                m = jnp.max(sd, axis=0, keepdims=True)
                if c > 0:
                    sp = _dot(k_ref[0, 0, j, 0:past, :], qv, NT)
                    m = jnp.maximum(m, jnp.max(sp, axis=0, keepdims=True))
                pd = jnp.exp(sd - m)
                l = jnp.sum(pd, axis=0, keepdims=True)
                acc = _dot(vt_ref[0, 0, vrows, past:past + tq], pd.astype(BF16), NN)
                if c > 0:
                    pp = jnp.exp(sp - m)
                    l = l + jnp.sum(pp, axis=0, keepdims=True)
                    acc = acc + _dot(vt_ref[0, 0, vrows, 0:past], pp.astype(BF16), NN)
                heads_out.append(acc / l)
                lse_ref[0, 0, j:j + 1, :] = m + jnp.log(l)
            ov = jnp.transpose(jnp.concatenate(heads_out, axis=0))
            o_ref[0] = ov.astype(BF16)
            zv = z_ref[0].astype(F32)
            g_ref[0] = (ov * zv * _sigmoid(zv)).astype(BF16)

        for c in range(nq):
            pl.when(pl.program_id(2) == c)(functools.partial(query_block, c))

    return _call(
        body, name=name, grid=(b, pairs, s // tq),
        in_specs=[pl.BlockSpec((1, 1, 2, tq, LANES), lambda bi, hp, qi: (bi, hp, 0, qi, 0)),
                  pl.BlockSpec((1, 1, 2, s, LANES), lambda bi, hp, qi: (bi, hp, 0, 0, 0)),
                  pl.BlockSpec((1, 1, LANES, s), lambda bi, hp, qi: (bi, hp, 0, 0)),
                  pl.BlockSpec((1, tq, LANES), lambda bi, hp, qi: (bi, qi, 3 * pairs + hp))],
        out_specs=[pl.BlockSpec((1, tq, LANES), lambda bi, hp, qi: (bi, qi, hp)),
                   pl.BlockSpec((1, tq, LANES), lambda bi, hp, qi: (bi, qi, hp)),
                   pl.BlockSpec((1, 1, 2, tq), lambda bi, hp, qi: (bi, hp, 0, qi))],
        out_shape=[jax.ShapeDtypeStruct((b, s, a), BF16), jax.ShapeDtypeStruct((b, s, a), BF16),
                   jax.ShapeDtypeStruct((b, pairs, 2, s), F32)],
        semantics=("parallel", "parallel", "arbitrary"))(qa, ka, vt, pa)


def _attn_prep_bwd(dcat, pa, o, lse, qa, name):
    b, pairs, _, s, _ = qa.shape
    a = pairs * LANES
    sub = 16

    def body(da_ref, z_ref, o_ref, lse_ref, qa_ref, qab_ref, doa_ref, dz_ref):
        zv = z_ref[0].astype(F32)
        dav = da_ref[0].astype(F32)
        ov = o_ref[0].astype(F32)
        sg = _sigmoid(zv)
        dov = dav * zv * sg
        dz_ref[0] = (dav * ov * sg * (1.0 + zv * (1.0 - sg))).astype(BF16)
        prod = dov * ov
        dob = dov.astype(BF16)
        lane = lax.broadcasted_iota(jnp.int32, (s, LANES), 1)
        r128 = lax.broadcasted_iota(jnp.int32, (LANES, LANES), 0)
        c128 = lax.broadcasted_iota(jnp.int32, (LANES, LANES), 1)
        prow = lax.broadcasted_iota(jnp.int32, (sub, s), 0)
        srow = lax.broadcasted_iota(jnp.int32, (sub, LANES), 0)
        scol = lax.broadcasted_iota(jnp.int32, (sub, LANES), 1)
        place = ((scol == srow + LANE_LSE) & (srow < N_PARTS)).astype(BF16)
        for j in range(2):
            in_head = (lane >= HEAD_DIM * j) & (lane < HEAD_DIM * (j + 1))
            dparts = _split3(jnp.sum(jnp.where(in_head, prod, 0.0), axis=1, keepdims=True))
            move128 = ((r128 == c128 + HEAD_DIM * j) & (c128 < HEAD_DIM)).astype(BF16)
            doa = _dot(dob, move128, NN)
            for i in range(N_PARTS):
                doa = jnp.where(lane == LANE_D + i, -dparts[i].astype(F32), doa)
            doa_ref[0, 0, j] = doa.astype(BF16)
            lparts = _split3(lse_ref[0, 0, j:j + 1, :])
            pmat = jnp.zeros((sub, s), BF16)
            for i in range(N_PARTS):
                pmat = jnp.where(prow == i, lparts[i], pmat)
            lcol = _dot(pmat, place, TN)
            qab_ref[0, 0, j] = (qa_ref[0, 0, j].astype(F32) - lcol).astype(BF16)

    tok = pl.BlockSpec((1, 1, 2, s, LANES), lambda bi, hp: (bi, hp, 0, 0, 0))
    tok_shape = jax.ShapeDtypeStruct((b, pairs, 2, s, LANES), BF16)
    pair_blk = pl.BlockSpec((1, s, LANES), lambda bi, hp: (bi, 0, hp))
    return _call(
        body, name=name, grid=(b, pairs),
        in_specs=[pair_blk,
                  pl.BlockSpec((1, s, LANES), lambda bi, hp: (bi, 0, 3 * pairs + hp)),
                  pair_blk,
                  pl.BlockSpec((1, 1, 2, s), lambda bi, hp: (bi, hp, 0, 0)),
                  tok],
        out_specs=[tok, tok, pair_blk],
        out_shape=[tok_shape, tok_shape, jax.ShapeDtypeStruct((b, s, a), BF16)],
        semantics=("parallel", "parallel"))(dcat, pa, o, lse, qa)


def _attn_bwd(ka, kat, va, qab, doa, name):
    b, pairs, _, s, _ = ka.shape
    a = pairs * LANES
    tq, tk = _attn_tiles(s)
    ratio = tq // tk
    nq, nk = s // tq, s // tk
    scale = 1.0 / math.sqrt(HEAD_DIM)

    def body(k_ref, kt_ref, v_ref, q_ref, do_ref, dq_ref, dk_ref, dv_ref, dc_ref,
             dqt_acc, dk_s, dv_s):
        key_i = lax.broadcasted_iota(jnp.int32, (tk, tq), 0)
        qry_i = lax.broadcasted_iota(jnp.int32, (tk, tq), 1)
        lane = lax.broadcasted_iota(jnp.int32, (tq, LANES), 1)
        low = lane < HEAD_DIM

        def key_block(kj):
            krows = slice(kj * tk, (kj + 1) * tk)
            q0 = (kj // ratio) * tq
            spans = [(slice(q0, q0 + tq), kj * tk - q0)]
            if q0 + tq < s:
                spans.append((slice(q0 + tq, s), None))
            for j in range(2):
                kb = k_ref[0, 0, j, krows, :]
                vb = v_ref[0, 0, j, krows, :]
                ktb = kt_ref[0, 0, j, :, krows]
                dk = dv = None
                for qrows, diag in spans:
                    qb = q_ref[0, 0, j, qrows, :]
                    dob = do_ref[0, 0, j, qrows, :]
                    pt = jnp.exp(_dot(kb, qb, NT))
                    if diag is not None:
                        pt = jnp.where(key_i + diag <= qry_i, pt, 0.0)
                    dsb = (pt * _dot(vb, dob, NT)).astype(BF16)
                    dv_part = _dot(pt.astype(BF16), dob, NN)
                    dk_part = _dot(dsb, qb, NN)
                    dv = dv_part if dv is None else dv + dv_part
                    dk = dk_part if dk is None else dk + dk_part
                    dq_part = _dot(ktb, dsb, NN)
                    if kj == 0:
                        dqt_acc[j, :, qrows] = dq_part
                    else:
                        dqt_acc[j, :, qrows] += dq_part
                dk_s[j, krows, :] = dk
                dv_s[j, krows, :] = dv

        for kj in range(nk):
            key_block(kj)

        def finish(i, _):
            rows = pl.ds(pl.multiple_of(i * tq, tq), tq)
            dq = [jnp.transpose(dqt_acc[j, :, rows]) for j in range(2)]
            dk = [dk_s[j, rows, :] for j in range(2)]
            dv = [dv_s[j, rows, :] for j in range(2)]
            dcol = [dq[j][:, LANE_CQ:LANE_CQ + 1] - dk[j][:, LANE_CK:LANE_CK + 1] for j in range(2)]
            dq = [dq[j] * scale for j in range(2)]
            for out_ref, val in ((dq_ref, dq), (dk_ref, dk), (dv_ref, dv)):
                merged = jnp.where(low, val[0], pltpu.roll(val[1], HEAD_DIM, 1))
                out_ref[0, rows, :] = merged.astype(BF16)
            hp = pl.program_id(1)
            prev = jnp.where(hp == 0, 0.0, dc_ref[0, rows, :])
            dc_ref[0, rows, :] = jnp.where(lane == 2 * hp, dcol[0],
                                           jnp.where(lane == 2 * hp + 1, dcol[1], prev))
            return 0

        lax.fori_loop(0, nq, finish, 0)

    tok = pl.BlockSpec((1, 1, 2, s, LANES), lambda bi, hp: (bi, hp, 0, 0, 0))
    tok_t = pl.BlockSpec((1, 1, 2, LANES, s), lambda bi, hp: (bi, hp, 0, 0, 0))
    pair_blk = pl.BlockSpec((1, s, LANES), lambda bi, hp: (bi, 0, hp))
    pair_shape = jax.ShapeDtypeStruct((b, s, a), BF16)
    return _call(
        body, name=name, grid=(b, pairs),
        in_specs=[tok, tok_t, tok, tok, tok],
        out_specs=[pair_blk, pair_blk, pair_blk,
                   pl.BlockSpec((1, s, LANES), lambda bi, hp: (bi, 0, 0))],
        out_shape=[pair_shape, pair_shape, pair_shape,
                   jax.ShapeDtypeStruct((b, s, LANES), F32)],
        scratch_shapes=[pltpu.VMEM((2, LANES, s), F32), pltpu.VMEM((2, s, LANES), F32),
                        pltpu.VMEM((2, s, LANES), F32)],
        semantics=("parallel", "arbitrary"))(ka, kat, va, qab, doa)


def _pool_tile(s):
    return min(256, s)


def _band(tb, window, shift):
    tgt = lax.broadcasted_iota(jnp.int32, (tb, tb), 0)
    src = lax.broadcasted_iota(jnp.int32, (tb, tb), 1) + shift
    return ((src <= tgt) & (src > tgt - window)).astype(BF16)


def _band_t(tb, window, shift):
    src = lax.broadcasted_iota(jnp.int32, (tb, tb), 0)
    tgt = lax.broadcasted_iota(jnp.int32, (tb, tb), 1) + shift
    return ((src <= tgt) & (src > tgt - window)).astype(BF16)


def _pool_fwd(pp, w_pool, scale, name):
    b, s, pw2 = pp.shape
    pw = pw2 // 2
    pg = pw // N_POOL_GROUPS
    tb = _pool_tile(s)
    nb = s // tb

    def body(u_ref, z_ref, w_ref, s_ref, o_ref):
        window = 2 << pl.program_id(1)
        band0 = _band(tb, window, 0)
        band1 = _band(tb, window, -tb)
        pos = lax.broadcasted_iota(jnp.int32, (tb, pg), 0)

        def block(i, _):
            rows = pl.ds(pl.multiple_of(i * tb, tb), tb)
            prev = pl.ds(pl.multiple_of(jnp.maximum(i - 1, 0) * tb, tb), tb)
            ub = u_ref[0, rows, :]
            up = u_ref[0, prev, :]
            up = jnp.where(i > 0, up, jnp.zeros_like(up))
            count = jnp.minimum(pos + i * tb + 1, window).astype(F32)
            pooled = (_dot(band0, ub, NN) + _dot(band1, up, NN)) / count - ub.astype(F32)
            mixed = _dot(pooled.astype(BF16), w_ref[0], NN) * s_ref[...]
            zv = z_ref[0, rows, :].astype(F32)
            o_ref[0, rows, :] = (mixed * zv * _sigmoid(zv)).astype(BF16)
            return 0

        lax.fori_loop(0, nb, block, 0)

    return _call(
        body, name=name, grid=(b, N_POOL_GROUPS),
        in_specs=[pl.BlockSpec((1, s, pg), lambda bi, g: (bi, 0, g)),
                  pl.BlockSpec((1, s, pg), lambda bi, g: (bi, 0, N_POOL_GROUPS + g)),
                  pl.BlockSpec((1, pg, pg), lambda bi, g: (g, 0, 0)),
                  pl.BlockSpec((1, pg), lambda bi, g: (0, g))],
        out_specs=pl.BlockSpec((1, s, pg), lambda bi, g: (bi, 0, g)),
        out_shape=jax.ShapeDtypeStruct((b, s, pw), BF16),
        semantics=("parallel", "parallel"))(pp, pp, w_pool, scale)


def _pool_bwd(pp, dcat, w_pool, scale, first_block, name):
    b, s, pw2 = pp.shape
    pw = pw2 // 2
    pg = pw // N_POOL_GROUPS
    tb = _pool_tile(s)
    nb = s // tb

    def body(u_ref, z_ref, d_ref, w_ref, s_ref, du_ref, dz_ref, dw_ref, ds_ref, dpool_s):
        @pl.when(pl.program_id(1) == 0)
        def _():
            dw_ref[...] = jnp.zeros_like(dw_ref)
            ds_ref[...] = jnp.zeros_like(ds_ref)

        window = 2 << pl.program_id(0)
        band0 = _band(tb, window, 0)
        band1 = _band(tb, window, -tb)
        band0_t = _band_t(tb, window, 0)
        band1_t = _band_t(tb, window, tb)
        pos = lax.broadcasted_iota(jnp.int32, (tb, pg), 0)

        def first(i, _):
            rows = pl.ds(pl.multiple_of(i * tb, tb), tb)
            prev = pl.ds(pl.multiple_of(jnp.maximum(i - 1, 0) * tb, tb), tb)
            ub = u_ref[0, rows, :]
            up = u_ref[0, prev, :]
            up = jnp.where(i > 0, up, jnp.zeros_like(up))
            count = jnp.minimum(pos + i * tb + 1, window).astype(F32)
            pooled = ((_dot(band0, ub, NN) + _dot(band1, up, NN)) / count
                      - ub.astype(F32)).astype(BF16)
            mixed = _dot(pooled, w_ref[0], NN)
            pm = mixed * s_ref[...]
            zv = z_ref[0, rows, :].astype(F32)
            sg = _sigmoid(zv)
            dpl = d_ref[0, rows, :].astype(F32)
            dpm = dpl * zv * sg
            dz_ref[0, rows, :] = (dpl * pm * sg * (1.0 + zv * (1.0 - sg))).astype(BF16)
            ds_ref[...] += jnp.sum(dpm * mixed, axis=0, keepdims=True)
            dmixed = (dpm * s_ref[...]).astype(BF16)
            dw_ref[0] += _dot(pooled, dmixed, TN)
            dpool_s[rows, :] = _dot(dmixed, w_ref[0], NT)
            return 0

        lax.fori_loop(0, nb, first, 0)

        def second(i, _):
            rows = pl.ds(pl.multiple_of(i * tb, tb), tb)
            nxt_i = jnp.minimum(i + 1, nb - 1)
            nxt = pl.ds(pl.multiple_of(nxt_i * tb, tb), tb)
            count = jnp.minimum(pos + i * tb + 1, window).astype(F32)
            count_n = jnp.minimum(pos + nxt_i * tb + 1, window).astype(F32)
            dpb = dpool_s[rows, :]
            cur = (dpb / count).astype(BF16)
            nx = dpool_s[nxt, :] / count_n
            nx = jnp.where(i < nb - 1, nx, 0.0).astype(BF16)
            du = _dot(band0_t, cur, NN) + _dot(band1_t, nx, NN) - dpb
            du_ref[0, rows, :] = du.astype(BF16)
            return 0

        lax.fori_loop(0, nb, second, 0)

    return _call(
        body, name=name, grid=(N_POOL_GROUPS, b),
        in_specs=[pl.BlockSpec((1, s, pg), lambda g, bi: (bi, 0, g)),
                  pl.BlockSpec((1, s, pg), lambda g, bi: (bi, 0, N_POOL_GROUPS + g)),
                  pl.BlockSpec((1, s, pg), lambda g, bi: (bi, 0, first_block + g)),
                  pl.BlockSpec((1, pg, pg), lambda g, bi: (g, 0, 0)),
                  pl.BlockSpec((1, pg), lambda g, bi: (0, g))],
        out_specs=[pl.BlockSpec((1, s, pg), lambda g, bi: (bi, 0, g)),
                   pl.BlockSpec((1, s, pg), lambda g, bi: (bi, 0, g)),
                   pl.BlockSpec((1, pg, pg), lambda g, bi: (g, 0, 0)),
                   pl.BlockSpec((1, pg), lambda g, bi: (0, g))],
        out_shape=[jax.ShapeDtypeStruct((b, s, pw), BF16), jax.ShapeDtypeStruct((b, s, pw), BF16),
                   jax.ShapeDtypeStruct((N_POOL_GROUPS, pg, pg), F32),
                   jax.ShapeDtypeStruct((1, pw), F32)],
        scratch_shapes=[pltpu.VMEM((s, pg), F32)],
        semantics=("parallel", "arbitrary"))(pp, pp, dcat, w_pool, scale)


def _adamw(recvs, sent, me, w, m, v, name):
    depth = len(recvs)
    r, c = w.shape[1:]
    tr = min(128, r)
    nb = r // tr
    c1 = 1.0 - ADAM_B1 ** ADAM_STEP
    c2 = 1.0 - ADAM_B2 ** ADAM_STEP
    slotted = sent[0].ndim == 3

    def body(me_ref, *refs):
        recv_refs, own_refs = refs[:depth], refs[depth:2 * depth]
        w_ref, m_ref, v_ref, g_ref, d_ref, nm_ref, nv_ref = refs[2 * depth:]
        me = me_ref[0]
        for layer in range(depth):
            @pl.when(pl.program_id(0) == layer)
            def _(layer=layer):
                own = (own_refs[layer][0] if slotted else own_refs[layer][...]).astype(F32)
                g = jnp.where(me == 0, own, recv_refs[layer][0].astype(F32))
                for sl in range(1, N_DEV):
                    g = g + jnp.where(me == sl, own, recv_refs[layer][sl].astype(F32))
                mn = ADAM_B1 * m_ref[0] + (1.0 - ADAM_B1) * g
                vn = ADAM_B2 * v_ref[0] + (1.0 - ADAM_B2) * (g * g)
                m_hat = mn / c1
                v_hat = vn / c2
                g_ref[0] = g
                d_ref[0] = -ADAM_LR * (m_hat / (jnp.sqrt(v_hat) + ADAM_EPS) + ADAM_WD * w_ref[0])
                nm_ref[0] = mn
                nv_ref[0] = vn

    def blk(layer):
        return lambda l, i: jnp.clip(i + (l - layer) * nb, 0, nb - 1)

    in_specs = [pl.BlockSpec((N_DEV, tr, c), lambda l, i, me_ref, f=blk(layer): (0, f(l, i), 0))
                for layer in range(depth)]
    if slotted:
        in_specs += [pl.BlockSpec((1, tr, c),
                                  lambda l, i, me_ref, f=blk(layer): (me_ref[0], f(l, i), 0))
                     for layer in range(depth)]
    else:
        in_specs += [pl.BlockSpec((tr, c), lambda l, i, me_ref, f=blk(layer): (f(l, i), 0))
                     for layer in range(depth)]
    row = pl.BlockSpec((1, tr, c), lambda l, i, me_ref: (l, i, 0))
    return pl.pallas_call(
        body, name=name, out_shape=[jax.ShapeDtypeStruct((depth, r, c), F32)] * 4,
        grid_spec=pltpu.PrefetchScalarGridSpec(
            num_scalar_prefetch=1, grid=(depth, nb), in_specs=in_specs + [row, row, row],
            out_specs=[row] * 4),
        compiler_params=pltpu.CompilerParams(dimension_semantics=("arbitrary", "arbitrary"),
                                             vmem_limit_bytes=VMEM_LIMIT_BYTES),
    )(me, *recvs, *sent, w, m, v)


def _pack_w_in(gathered, a, heads, pw):
    d = gathered.shape[1]
    w_full = jnp.transpose(gathered, (1, 0, 2)).reshape(d, -1)
    wf = jnp.pad(w_full[:, 4 * a:4 * a + heads], ((0, 0), (0, LANES - heads)))
    return w_full, w_full[:, 4 * a + heads:], wf


def _unpack_dw_in(parts, heads):
    dq, dk, dv, dz, dwf, du, dzp = parts
    d = dq.shape[0]
    full = jnp.concatenate([dq, dk, dv, dz, dwf[:, :heads], du, dzp], axis=1)
    return jnp.transpose(full.reshape(d, N_DEV, -1), (1, 0, 2))


def kernel(x, p, norm_pre, norm_post, w_in, b_f, w_pool, pool_scale, w_out, w_pg, w_pe, loss_target, m_norm_pre, m_norm_post, m_w_in, m_b_f, m_w_pool, m_pool_scale, m_w_out, m_w_pg, m_w_pe, v_norm_pre, v_norm_post, v_w_in, v_b_f, v_w_pool, v_pool_scale, v_w_out, v_w_pg, v_w_pe):
    depth = w_in.shape[0]
    b, s, d = x.shape
    t = b * s
    heads = b_f.shape[1]
    a = heads * HEAD_DIM
    pairs = a // LANES
    pw = pool_scale.shape[1]
    pg = pw // N_POOL_GROUPS
    ple = p.shape[-1]
    mix_w = a + pw

    me = 4 * lax.axis_index("x") + 2 * lax.axis_index("y") + lax.axis_index("c")
    shard = {
        "w_in": [w_in[i].astype(BF16) for i in range(depth)],
        "w_pool": [w_pool[i].reshape(N_POOL_GROUPS * (pg // N_DEV), pg).astype(BF16)
                   for i in range(depth)],
        "w_out": [w_out[i].astype(BF16) for i in range(depth)],
        "w_pg": [w_pg[i].astype(BF16) for i in range(depth)],
        "w_pe": [w_pe[i].astype(BF16) for i in range(depth)],
    }
    names = list(shard)
    rest = names[1:]

    def unpack_rest(lands, layer, which):
        g = {nm: _with_own(ld, shard[nm][layer], me) for nm, ld in zip(which, lands)}
        g_pool = g["w_pool"].reshape(N_DEV, N_POOL_GROUPS, pg // N_DEV, pg)
        return dict(wpool=jnp.transpose(g_pool, (1, 0, 2, 3)).reshape(N_POOL_GROUPS, pg, pg),
                    wout=g["w_out"].reshape(mix_w, d), wpg=g["w_pg"].reshape(d, d),
                    wpe=jnp.transpose(g["w_pe"], (1, 0, 2)).reshape(ple, d))

    g_in0 = _gather_two_level(shard["w_in"][0], "gather_w_in0")
    rest0, tok_rest0 = _exchange_start([(shard[nm][0], False) for nm in rest], g_in0,
                                       "gather_rest0_start")
    later, tok = [], tok_rest0
    for i in range(1, depth):
        hdl, tk_i = _exchange_start([(shard[nm][i], False) for nm in names], g_in0,
                                    "gather_layer%d_start" % i)
        later.append(hdl)
        tok = tok + tk_i

    h = x.reshape(t, d)
    saved = []
    layers = []
    for i in range(depth):
        sv = dict(h=h)
        g_pre = norm_pre[i:i + 1]
        g_post = norm_post[i:i + 1]
        bf = jnp.pad(b_f[i:i + 1], ((0, 0), (0, LANES - heads)))
        scale = pool_scale[i:i + 1]
        if i == 0:
            lw = dict(zip(("wa", "wp", "wf"), _pack_w_in(g_in0, a, heads, pw)))
            g_pre = g_pre + tok
        else:
            lands = _exchange_wait(later[i - 1], h, "gather_layer%d_wait" % i)
            g_in = _with_own(lands[0], shard["w_in"][i], me)
            lw = dict(zip(("wa", "wp", "wf"), _pack_w_in(g_in, a, heads, pw)))
            lw.update(unpack_rest(lands[1:], i, rest))
        hn = _rms_fwd(h, g_pre, "rms_pre")
        pa = _matmul([(hn, lw["wa"])], "nn", BF16, "proj_attn", n_dim=4 * a, tn=2048,
                     n_outer=True).reshape(b, s, 4 * a)
        pp = _matmul([(hn, lw["wp"])], "nn", BF16, "proj_pool", tn=2048,
                     n_outer=True).reshape(b, s, 2 * pw)
        fl = _matmul([(hn, lw["wf"])], "nn", F32, "proj_gate").reshape(b, s, LANES)
        c = _gates_fwd(fl, bf, "gates_fwd")
        qa, ka, kat, va, vt = _attn_prep_fwd(pa, c, "attn_prep_fwd")
        o, ga, lse = _attn_fwd(qa, ka, vt, pa, "attn_fwd")
        if i == 0:
            lw.update(unpack_rest(_exchange_wait(rest0, lse, "gather_rest0_wait"), 0, rest))
        layers.append(lw)
        gp = _pool_fwd(pp, lw["wpool"], scale, "pool_fwd")
        ga2 = ga.reshape(t, a)
        gp2 = gp.reshape(t, pw)
        mix = _matmul([(ga2, lw["wout"], 0, 0), (gp2, lw["wout"], a, 0)], "nn", F32, "mix_out")
        h1, h1b = _post_fwd(h, mix, g_post, "post_fwd")
        pb = p[i].reshape(t, ple).astype(BF16)
        gpre = _matmul([(h1b, lw["wpg"])], "nn", F32, "ple_gate")
        e = _matmul([(pb, lw["wpe"])], "nn", F32, "ple_embed")
        if i < depth - 1:
            h = _ple_fwd(h1, gpre, e, "ple_fwd")
        sv.update(hn=hn, pa=pa, pp=pp, fl=fl, bf=bf, qa=qa, ka=ka, kat=kat, va=va, o=o, lse=lse, ga=ga2,
                  gp=gp2, mix=mix,
                  h1b=h1b, pb=pb, gpre=gpre, e=e, g_pre=g_pre, g_post=g_post, scale=scale)
        saved.append(sv)

    dh, sq = _ple_loss(h1, gpre, e, loss_target.reshape(t, d), "ple_loss")
    loss = lax.psum(0.5 * jnp.sum(sq) / d, MESH_AXES)

    big = {nm: [None] * depth for nm in names}
    small = {nm: [None] * depth for nm in ("norm_pre", "norm_post", "b_f", "pool_scale")}
    grad_handles = [None] * depth
    rest_handles = [None] * depth
    for i in reversed(range(depth)):
        lw, sv = layers[i], saved[i]
        de, dpre = _ple_bwd(dh, sv["gpre"], sv["e"], "ple_bwd")
        dwpe = _matmul([(sv["pb"], de)], "tn", BF16, "dw_pe", tm=1024)
        dwpg = _matmul([(sv["h1b"], dpre)], "tn", BF16, "dw_pg", tm=1024)
        dh1, dmix, dg_post = _matmul_rows(
            [(dpre, lw["wpg"], 0)], [dh, sv["mix"]], sv["g_post"], _post_bwd_epilogue,
            (F32, BF16), "d_h1_post_bwd", tm=512)
        dwout = jnp.concatenate(
            [_matmul([(sv["ga"], dmix)], "tn", BF16, "dw_out_attn", tm=1024),
             _matmul([(sv["gp"], dmix)], "tn", BF16, "dw_out_pool", tm=1024)], axis=0)
        dcat = _matmul([(dmix, lw["wout"])], "nt", BF16, "d_cat", tn=2048).reshape(b, s, mix_w)
        du, dzp, dwpool, dscale = _pool_bwd(sv["pp"], dcat, lw["wpool"], sv["scale"], a // pg,
                                            "pool_bwd")
        big["w_pool"][i] = jnp.transpose(
            dwpool.astype(BF16).reshape(N_POOL_GROUPS, N_DEV, pg // N_DEV, pg), (1, 0, 2, 3)
        ).reshape(N_DEV, N_POOL_GROUPS * (pg // N_DEV), pg)
        big["w_out"][i] = dwout.reshape(N_DEV, mix_w // N_DEV, d)
        big["w_pg"][i] = dwpg.reshape(N_DEV, d // N_DEV, d)
        big["w_pe"][i] = jnp.transpose(dwpe.reshape(ple, N_DEV, d // N_DEV), (1, 0, 2))
        rest_handles[i], tok = _exchange_start([(big[nm][i], True) for nm in rest], du,
                                               "grads_rest%d_start" % i)
        qab, doa, dz = _attn_prep_bwd(dcat, sv["pa"], sv["o"], sv["lse"] + tok, sv["qa"],
                                      "attn_prep_bwd")
        dq, dk, dv, dc = _attn_bwd(sv["ka"], sv["kat"], sv["va"], qab, doa, "attn_bwd")
        dfl, dbf = _gates_bwd(dc, sv["fl"], sv["bf"], heads, "gates_bwd")
        dproj = [g_.reshape(t, -1) for g_ in (dq, dk, dv, dz, dfl, du, dzp)]
        dw_parts = [_matmul([(sv["hn"], g_)], "tn", BF16, "dw_in_%d" % n_, tm=1024)
                    for n_, g_ in enumerate(dproj)]

        big["w_in"][i] = _unpack_dw_in(dw_parts, heads)
        grad_handles[i], tok = _exchange_start([(big["w_in"][i], True)], dw_parts[-1],
                                               "grads_w_in%d_start" % i)

        dq2, dk2, dv2, dz2, dfl2, du2, dzp2 = dproj
        dh, dg_pre = _matmul_rows(
            [(dq2, lw["wa"], 0), (dk2, lw["wa"], a), (dv2, lw["wa"], 2 * a), (dz2, lw["wa"], 3 * a),
             (du2, lw["wp"], 0), (dzp2, lw["wp"], pw), (dfl2, lw["wf"] + tok.astype(BF16), 0)],
            [sv["h"], dh1], sv["g_pre"] + tok, _pre_bwd_epilogue, (F32,), "d_hn_pre_bwd", tm=256)
        small["norm_pre"][i] = dg_pre
        small["norm_post"][i] = dg_post
        small["b_f"][i] = jnp.sum(dbf, axis=0)
        small["pool_scale"][i] = dscale
    grad_x = dh.reshape(b, s, d)

    width = max(d, pw)
    small_names = ("norm_pre", "norm_post", "pool_scale", "b_f")

    def small_rows(get):
        rows = []
        for nm in small_names:
            for i in range(depth):
                v_ = get(nm, i)
                rows.append(jnp.pad(v_, ((0, 0), (0, width - v_.shape[1]))))
        return jnp.concatenate(rows, axis=0)

    small_g = small_rows(lambda nm, i: small[nm][i])
    (small_recv,) = _exchange([(small_g, False)], "exchange_small")
    me1 = jnp.reshape(me, (1,)).astype(jnp.int32)

    weights = dict(norm_pre=norm_pre, norm_post=norm_post, w_in=w_in, b_f=b_f, w_pool=w_pool,
                   pool_scale=pool_scale, w_out=w_out, w_pg=w_pg, w_pe=w_pe)
    mom1 = dict(norm_pre=m_norm_pre, norm_post=m_norm_post, w_in=m_w_in, b_f=m_b_f, w_pool=m_w_pool,
                pool_scale=m_pool_scale, w_out=m_w_out, w_pg=m_w_pg, w_pe=m_w_pe)
    mom2 = dict(norm_pre=v_norm_pre, norm_post=v_norm_post, w_in=v_w_in, b_f=v_b_f, w_pool=v_w_pool,
                pool_scale=v_pool_scale, w_out=v_w_out, w_pg=v_w_pg, w_pe=v_w_pe)

    results = {}

    def update(nm, recvs):
        shp = weights[nm].shape
        sent = [big[nm][i] for i in range(depth)]
        flat = lambda arr: arr.reshape((depth,) + sent[0].shape[1:])
        outs = _adamw(recvs, sent, me1, flat(weights[nm]), flat(mom1[nm]), flat(mom2[nm]),
                      "adamw_" + nm)
        results[nm] = [o_.reshape(shp) for o_ in outs]
        return outs[0]

    got_rest = [_exchange_wait(rest_handles[i], dh, "grads_rest%d_wait" % i) for i in range(depth)]
    for j, nm in enumerate(rest):
        last = update(nm, [got_rest[i][j] for i in range(depth)])

    small_w = small_rows(lambda nm, i: weights[nm][i:i + 1])[None]
    small_m = small_rows(lambda nm, i: mom1[nm][i:i + 1])[None]
    small_v = small_rows(lambda nm, i: mom2[nm][i:i + 1])[None]
    outs = _adamw([small_recv], [small_g], me1, small_w, small_m, small_v, "adamw_small")
    for j, nm in enumerate(small_names):
        cols = weights[nm].shape[1]
        results[nm] = [o_[0, j * depth:(j + 1) * depth, :cols] for o_ in outs]

    got_w_in = [_exchange_wait(grad_handles[i], last + outs[0][0, 0, 0], "grads_w_in%d_wait" % i)[0]
                for i in range(depth)]
    update("w_in", got_w_in)

    order = ("norm_pre", "norm_post", "w_in", "b_f", "w_pool", "pool_scale", "w_out", "w_pg", "w_pe")
    return (loss, grad_x, *[results[nm][0] for nm in order], *[results[nm][1] for nm in order],
            *[results[nm][2] for nm in order], *[results[nm][3] for nm in order])
```

```python
import functools
import math

import jax
import jax.numpy as jnp
from jax import lax
from jax.experimental import pallas as pl
from jax.experimental.pallas import tpu as pltpu

N_DEV = 8
MESH_AXES = ("x", "y", "c")
HEAD_DIM = 64
LANES = 128
N_POOL_GROUPS = 4
EPS = 1e-6
ADAM_LR = 0.001
ADAM_B1 = 0.9
ADAM_B2 = 0.999
ADAM_EPS = 1e-08
ADAM_WD = 0.01
ADAM_STEP = 10
VMEM_LIMIT_BYTES = 56 * 1024 * 1024
F32 = jnp.float32
BF16 = jnp.bfloat16
NEG_INF = float("-inf")


def _call(body, *, name, grid, in_specs, out_specs, out_shape, scratch_shapes=(), semantics=None):
    return pl.pallas_call(
        body, name=name, grid=grid, in_specs=in_specs, out_specs=out_specs, out_shape=out_shape,
        scratch_shapes=list(scratch_shapes),
        compiler_params=pltpu.CompilerParams(dimension_semantics=semantics,
                                             vmem_limit_bytes=VMEM_LIMIT_BYTES))


def _sigmoid(z):
    return 1.0 / (1.0 + jnp.exp(-z))


def _dot(a, b, dims):
    return lax.dot_general(a, b, (dims, ((), ())), preferred_element_type=F32)


NN = ((1,), (0,))
NT = ((1,), (1,))
TN = ((0,), (0,))


def _exchange(items, name):
    n = len(items)
    modes = [s for _, s in items]
    out_shapes = []
    for a, s in items:
        shp = a.shape[1:] if s else a.shape
        out_shapes.append(jax.ShapeDtypeStruct((N_DEV,) + tuple(shp), a.dtype))

    def body(*refs):
        ins = refs[:n]
        outs = refs[n:2 * n]
        send_sems, recv_sems, local_sems = refs[2 * n:]
        x, y, c = (lax.axis_index(ax) for ax in MESH_AXES)
        me = 4 * x + 2 * y + c
        started = []
        for i in range(n):
            mine = ins[i].at[me] if modes[i] else ins[i]
            loc = pltpu.make_async_copy(mine, outs[i].at[me], local_sems.at[i])
            loc.start()
            started.append(loc)
        remote = []
        for k in range(1, N_DEV):
            px = x ^ ((k >> 2) & 1)
            py = y ^ ((k >> 1) & 1)
            pc = c ^ (k & 1)
            peer = me ^ k
            for i in range(n):
                src = ins[i].at[peer] if modes[i] else ins[i]
                cp = pltpu.make_async_remote_copy(
                    src_ref=src, dst_ref=outs[i].at[me],
                    send_sem=send_sems.at[i, k - 1], recv_sem=recv_sems.at[i, k - 1],
                    device_id=(px, py, pc), device_id_type=pl.DeviceIdType.MESH)
                cp.start()
                remote.append(cp)
        for cp in remote:
            cp.wait()
        for loc in started:
            loc.wait()

    hbm = pl.BlockSpec(memory_space=pltpu.HBM)
    return pl.pallas_call(
        body, name=name, out_shape=out_shapes,
        in_specs=[hbm] * n, out_specs=[hbm] * n,
        scratch_shapes=[pltpu.SemaphoreType.DMA((n, N_DEV - 1)),
                        pltpu.SemaphoreType.DMA((n, N_DEV - 1)),
                        pltpu.SemaphoreType.DMA((n,))],
    )(*[a for a, _ in items])


def _gather_two_level(shard, name):
    def body(x_ref, out_ref, send_sems, recv_sems, local_sem):
        x, y, c = (lax.axis_index(ax) for ax in MESH_AXES)
        sibling = (x, y, 1 - c)
        chips = [(1 - x, y), (x, 1 - y), (1 - x, 1 - y)]

        def slot(px, py, pc):
            return out_ref.at[4 * px + 2 * py + pc]

        def copy(k, block, to, src=None):
            return pltpu.make_async_remote_copy(
                src_ref=slot(*block) if src is None else src, dst_ref=slot(*block),
                send_sem=send_sems.at[k], recv_sem=recv_sems.at[k],
                device_id=to, device_id_type=pl.DeviceIdType.MESH)

        mine = pltpu.make_async_copy(x_ref, slot(x, y, c), local_sem)
        mine.start()
        first = [copy(0, (x, y, c), sibling, src=x_ref)]
        first += [copy(1 + j, (x, y, c), (*chip, c), src=x_ref) for j, chip in enumerate(chips)]
        for cp in first:
            cp.start()
        passed = [copy(4 + j, (*chip, c), sibling) for j, chip in enumerate(chips)]
        for j, chip in enumerate(chips):
            copy(1 + j, (*chip, c), (x, y, c)).wait_recv()
            passed[j].start()
        copy(0, (x, y, 1 - c), (x, y, c)).wait_recv()
        for j, chip in enumerate(chips):
            copy(4 + j, (*chip, 1 - c), (x, y, c)).wait_recv()
        for cp in first + passed:
            cp.wait_send()
        mine.wait()

    hbm = pl.BlockSpec(memory_space=pltpu.HBM)
    return pl.pallas_call(
        body, name=name, out_shape=jax.ShapeDtypeStruct((N_DEV,) + shard.shape, shard.dtype),
        in_specs=[hbm], out_specs=hbm,
        scratch_shapes=[pltpu.SemaphoreType.DMA((N_DEV - 1,)), pltpu.SemaphoreType.DMA((N_DEV - 1,)),
                        pltpu.SemaphoreType.DMA],
    )(shard)


def _peer_copies(srcs, lands, modes, send_sems, recv_sems):
    x, y, c = (lax.axis_index(ax) for ax in MESH_AXES)
    me = 4 * x + 2 * y + c
    copies = []
    for k in range(1, N_DEV):
        peer_id = (x ^ ((k >> 2) & 1), y ^ ((k >> 1) & 1), c ^ (k & 1))
        for i, scatter in enumerate(modes):
            src = srcs[i].at[me ^ k] if scatter else srcs[i]
            pair = i * (N_DEV - 1) + k - 1
            copies.append(pltpu.make_async_remote_copy(
                src_ref=src, dst_ref=lands[i].at[me],
                send_sem=send_sems.at[pair], recv_sem=recv_sems.at[pair],
                device_id=peer_id, device_id_type=pl.DeviceIdType.MESH))
    return copies


def _exchange_start(items, after, name):
    n = len(items)
    modes = [s for _, s in items]
    srcs = [pltpu.with_memory_space_constraint(a, pltpu.HBM) for a, _ in items]
    lands = []
    for a, s in items:
        shp = (N_DEV,) + tuple(a.shape[1:] if s else a.shape)
        lands.append(pltpu.with_memory_space_constraint(lax.empty(shp, a.dtype), pltpu.HBM))

    def body(*refs):
        send_sems, recv_sems = refs[2 * n + 1], refs[2 * n + 2]
        token = refs[-1]
        for cp in _peer_copies(refs[:n], refs[n:2 * n], modes, send_sems, recv_sems):
            cp.start()
        token[...] = jnp.zeros_like(token)

    hbm = pl.BlockSpec(memory_space=pltpu.HBM)
    sem = pl.BlockSpec(memory_space=pltpu.SEMAPHORE)
    outs = pl.pallas_call(
        body, name=name,
        out_shape=(pltpu.SemaphoreType.DMA((n * (N_DEV - 1),)),
                   pltpu.SemaphoreType.DMA((n * (N_DEV - 1),)),
                   *[pltpu.HBM(a.shape, a.dtype) for a in srcs + lands],
                   jax.ShapeDtypeStruct((8, LANES), F32)),
        in_specs=[hbm] * (2 * n) + [pl.BlockSpec(memory_space=pl.ANY)],
        out_specs=(sem, sem, *([hbm] * (2 * n)), pl.BlockSpec(memory_space=pltpu.VMEM)),
        input_output_aliases={i: 2 + i for i in range(2 * n)},
        compiler_params=pltpu.CompilerParams(
            has_side_effects=pltpu.SideEffectType.DATAFLOW_SIDE_EFFECTING),
    )(*srcs, *lands, after)
    handle = (modes, outs[0], outs[1], list(outs[2:2 + n]), list(outs[2 + n:2 + 2 * n]))
    return handle, outs[-1][0, 0]


def _exchange_wait(handle, after, name):
    modes, send_sems, recv_sems, srcs, lands = handle
    n = len(modes)

    def body(*refs):
        for cp in _peer_copies(refs[:n], refs[n:2 * n], modes, refs[2 * n], refs[2 * n + 1]):
            cp.wait_send()
            cp.wait_recv()

    hbm = pl.BlockSpec(memory_space=pltpu.HBM)
    sem = pl.BlockSpec(memory_space=pltpu.SEMAPHORE)
    outs = pl.pallas_call(
        body, name=name,
        out_shape=tuple(pltpu.HBM(a.shape, a.dtype) for a in srcs + lands),
        in_specs=[hbm] * (2 * n) + [sem, sem, pl.BlockSpec(memory_space=pl.ANY)],
        out_specs=tuple([hbm] * (2 * n)),
        input_output_aliases={i: i for i in range(2 * n)},
        compiler_params=pltpu.CompilerParams(
            has_side_effects=pltpu.SideEffectType.DATAFLOW_SIDE_EFFECTING),
    )(*srcs, *lands, send_sems, recv_sems, after)
    return list(outs[n:])


def _with_own(slots, own, me):
    idx = lax.broadcasted_iota(jnp.int32, (N_DEV,) + (1,) * own.ndim, 0)
    return jnp.where(idx == me, own[None], slots)


def _matmul(pairs, mode, out_dtype, name, n_dim=None, tm=512, tn=1024, tk=1024, n_outer=False):
    dims = {"nn": NN, "nt": NT, "tn": TN}[mode]
    pairs = [tuple(pr) + (0, 0) * (len(pr) == 2) for pr in pairs]
    a0, b0 = pairs[0][:2]
    m_dim = a0.shape[1] if mode == "tn" else a0.shape[0]
    if n_dim is None:
        n_dim = b0.shape[0] if mode == "nt" else b0.shape[1]
    tm = min(tm, m_dim)
    tn = min(tn, n_dim)
    segs = []
    off = 0
    for a, _, k0, n0 in pairs:
        k_dim = a.shape[0] if mode == "tn" else a.shape[1]
        t = min(tk, k_dim)
        segs.append((off, k_dim // t, t, k0 // t, n0 // tn))
        off += k_dim // t
    nk = off
    n_pairs = len(pairs)

    def ij(g0, g1):
        return (g1, g0) if n_outer else (g0, g1)

    in_specs = []
    for (o, cnt, t, kb, nb) in segs:
        def kc(kk, o=o, cnt=cnt):
            return jnp.clip(kk - o, 0, cnt - 1)
        if mode == "tn":
            in_specs.append(pl.BlockSpec((t, tm), lambda g0, g1, kk, kc=kc: (kc(kk), ij(g0, g1)[0])))
        else:
            in_specs.append(pl.BlockSpec((tm, t), lambda g0, g1, kk, kc=kc: (ij(g0, g1)[0], kc(kk))))
        if mode == "nt":
            in_specs.append(pl.BlockSpec((tn, t), lambda g0, g1, kk, kc=kc, kb=kb, nb=nb:
                                         (nb + ij(g0, g1)[1], kb + kc(kk))))
        else:
            in_specs.append(pl.BlockSpec((t, tn), lambda g0, g1, kk, kc=kc, kb=kb, nb=nb:
                                         (kb + kc(kk), nb + ij(g0, g1)[1])))

    one_shot = all(sg[1] == 1 for sg in segs)

    def body_sum(*refs):
        total = _dot(refs[0][...], refs[1][...], dims)
        for idx in range(1, n_pairs):
            total = total + _dot(refs[2 * idx][...], refs[2 * idx + 1][...], dims)
        refs[2 * n_pairs][...] = total.astype(out_dtype)

    def body(*refs):
        out_ref = refs[2 * n_pairs]
        acc = refs[2 * n_pairs + 1]
        kk = pl.program_id(2)

        @pl.when(kk == 0)
        def _():
            acc[...] = jnp.zeros_like(acc)

        for idx, (o, cnt) in enumerate(sg[:2] for sg in segs):
            @pl.when((kk >= o) & (kk < o + cnt))
            def _(idx=idx):
                acc[...] += _dot(refs[2 * idx][...], refs[2 * idx + 1][...], dims)

        @pl.when(kk == nk - 1)
        def _():
            out_ref[...] = acc[...].astype(out_dtype)

    flat = [t for pr in pairs for t in pr[:2]]
    tiles = (m_dim // tm, n_dim // tn)
    return _call(body_sum if one_shot else body, name=name,
                 grid=ij(*tiles) + (1 if one_shot else nk,), in_specs=in_specs,
                 out_specs=pl.BlockSpec((tm, tn), lambda g0, g1, kk: ij(g0, g1)),
                 out_shape=jax.ShapeDtypeStruct((m_dim, n_dim), out_dtype),
                 scratch_shapes=[] if one_shot else [pltpu.VMEM((tm, tn), F32)],
                 semantics=("parallel", "parallel", "arbitrary"))(*flat)


def _row_tile(t):
    return min(512, t)


def _rms_fwd(h, g, name):
    t, d = h.shape
    tt = _row_tile(t)

    def body(h_ref, g_ref, o_ref):
        hv = h_ref[...]
        r = lax.rsqrt(jnp.mean(hv * hv, axis=-1, keepdims=True) + EPS)
        o_ref[...] = (hv * r * g_ref[...]).astype(BF16)

    row = pl.BlockSpec((tt, d), lambda i: (i, 0))
    vec = pl.BlockSpec((1, d), lambda i: (0, 0))
    return _call(body, name=name, grid=(t // tt,), in_specs=[row, vec], out_specs=row,
                 out_shape=jax.ShapeDtypeStruct((t, d), BF16), semantics=("parallel",))(h, g)


def _post_fwd(h, mix, g, name):
    t, d = h.shape
    tt = _row_tile(t)

    def body(h_ref, m_ref, g_ref, o_ref, ob_ref):
        mv = m_ref[...]
        r = lax.rsqrt(jnp.mean(mv * mv, axis=-1, keepdims=True) + EPS)
        h1 = h_ref[...] + mv * r * g_ref[...]
        o_ref[...] = h1
        ob_ref[...] = h1.astype(BF16)

    row = pl.BlockSpec((tt, d), lambda i: (i, 0))
    vec = pl.BlockSpec((1, d), lambda i: (0, 0))
    return _call(body, name=name, grid=(t // tt,), in_specs=[row, row, vec], out_specs=[row, row],
                 out_shape=[jax.ShapeDtypeStruct((t, d), F32), jax.ShapeDtypeStruct((t, d), BF16)],
                 semantics=("parallel",))(h, mix, g)


def _ple_fwd(h1, h1b, wpg, pb, wpe, target, name):
    t, d = h1.shape
    tt = _row_tile(t)
    last = target is not None

    def body(h_ref, hb_ref, wg_ref, p_ref, we_ref, *rest):
        gpre = _dot(hb_ref[...], wg_ref[...], NN)
        e = _dot(p_ref[...], we_ref[...], NN)
        y = h_ref[...] + _sigmoid(gpre) * e
        if last:
            t_ref, g_ref, e_ref, dy_ref, s_ref = rest

            @pl.when(pl.program_id(0) == 0)
            def _():
                s_ref[...] = jnp.zeros_like(s_ref)
            diff = y - t_ref[...]
            dy_ref[...] = diff * (1.0 / d)
            s_ref[...] += jnp.sum(diff * diff, axis=0, keepdims=True)
        else:
            g_ref, e_ref, y_ref = rest
            y_ref[...] = y
        g_ref[...] = gpre
        e_ref[...] = e

    row = pl.BlockSpec((tt, d), lambda i: (i, 0))
    whole = lambda arr: pl.BlockSpec(arr.shape, lambda i: (0, 0))
    rows_f32 = jax.ShapeDtypeStruct((t, d), F32)
    in_specs = [row, row, whole(wpg), pl.BlockSpec((tt, pb.shape[1]), lambda i: (i, 0)), whole(wpe)]
    operands = [h1, h1b, wpg, pb, wpe]
    out_specs, out_shape = [row, row, row], [rows_f32, rows_f32, rows_f32]
    if last:
        in_specs.append(row)
        operands.append(target)
        out_specs.append(pl.BlockSpec((1, d), lambda i: (0, 0)))
        out_shape.append(jax.ShapeDtypeStruct((1, d), F32))
    return _call(body, name=name, grid=(t // tt,), in_specs=in_specs, out_specs=out_specs,
                 out_shape=out_shape, semantics=("arbitrary",))(*operands)


def _ple_bwd(dh2, gpre, e, name):
    t, d = dh2.shape
    tt = _row_tile(t)

    def body(d_ref, g_ref, e_ref, de_ref, dp_ref):
        gate = _sigmoid(g_ref[...])
        dv = d_ref[...]
        de_ref[...] = (dv * gate).astype(BF16)
        dp_ref[...] = (dv * e_ref[...] * gate * (1.0 - gate)).astype(BF16)

    row = pl.BlockSpec((tt, d), lambda i: (i, 0))
    return _call(body, name=name, grid=(t // tt,), in_specs=[row, row, row], out_specs=[row, row],
                 out_shape=[jax.ShapeDtypeStruct((t, d), BF16)] * 2,
                 semantics=("parallel",))(dh2, gpre, e)


def _matmul_rows(pairs, rows_in, vec_in, epilogue, row_dtypes, name, tm):
    n_pairs = len(pairs)
    m_dim = pairs[0][0].shape[0]
    n_dim = pairs[0][1].shape[0]
    tm = min(tm, m_dim)
    in_specs = []
    for a, _, k0 in pairs:
        k_dim = a.shape[1]
        in_specs.append(pl.BlockSpec((tm, k_dim), lambda i: (i, 0)))
        in_specs.append(pl.BlockSpec((n_dim, k_dim), lambda i, kb=k0 // k_dim: (0, kb)))
    row = pl.BlockSpec((tm, n_dim), lambda i: (i, 0))
    vec = pl.BlockSpec((1, n_dim), lambda i: (0, 0))
    n_rows = len(rows_in)

    def body(*refs):
        ops = refs[:2 * n_pairs]
        row_refs = refs[2 * n_pairs:2 * n_pairs + n_rows]
        vec_ref = refs[2 * n_pairs + n_rows]
        outs = refs[2 * n_pairs + n_rows + 1:]
        total = _dot(ops[0][...], ops[1][...], NT)
        for idx in range(1, n_pairs):
            total = total + _dot(ops[2 * idx][...], ops[2 * idx + 1][...], NT)
        results, partial = epilogue(total, [r[...] for r in row_refs], vec_ref[...])
        for out_ref, val in zip(outs[:-1], results):
            out_ref[...] = val.astype(out_ref.dtype)

        @pl.when(pl.program_id(0) == 0)
        def _():
            outs[-1][...] = jnp.zeros_like(outs[-1])
        outs[-1][...] += partial

    flat = [t_ for a, b_, _ in pairs for t_ in (a, b_)]
    return _call(body, name=name, grid=(m_dim // tm,),
                 in_specs=in_specs + [row] * n_rows + [vec],
                 out_specs=[row] * len(row_dtypes) + [vec],
                 out_shape=[jax.ShapeDtypeStruct((m_dim, n_dim), dt) for dt in row_dtypes]
                 + [jax.ShapeDtypeStruct((1, n_dim), F32)],
                 semantics=("arbitrary",))(*flat, *rows_in, vec_in)


def _post_bwd_epilogue(t1, rows, g):
    dh2, mv = rows
    dh1 = dh2 + t1
    r = lax.rsqrt(jnp.mean(mv * mv, axis=-1, keepdims=True) + EPS)
    w = dh1 * g
    dot = jnp.mean(w * mv, axis=-1, keepdims=True)
    dmix = r * w - mv * (r * r * r) * dot
    return (dh1, dmix), jnp.sum(dh1 * mv * r, axis=0, keepdims=True)


def _pre_bwd_epilogue(dhn, rows, g):
    hv, dh1 = rows
    r = lax.rsqrt(jnp.mean(hv * hv, axis=-1, keepdims=True) + EPS)
    w = dhn * g
    dot = jnp.mean(w * hv, axis=-1, keepdims=True)
    return (dh1 + r * w - hv * (r * r * r) * dot,), jnp.sum(dhn * hv * r, axis=0, keepdims=True)


def _split3(v):
    hi = v.astype(BF16)
    r1 = v - hi.astype(F32)
    mid = r1.astype(BF16)
    lo = (r1 - mid.astype(F32)).astype(BF16)
    return hi, mid, lo


def _scan_tile(s):
    return min(256, s)


def _gates_fwd(fl, bf, name):
    b, s, _ = fl.shape

    tb = _scan_tile(s)

    def body(f_ref, b_ref, c_ref):
        dst = lax.broadcasted_iota(jnp.int32, (tb, tb), 0)
        src = lax.broadcasted_iota(jnp.int32, (tb, tb), 1)
        lower = (src <= dst).astype(BF16)
        carry = jnp.zeros((1, LANES), F32)
        for blk_i in range(s // tb):
            rows = slice(blk_i * tb, (blk_i + 1) * tb)
            xv = f_ref[0, rows, :] + b_ref[...]
            lf = jnp.minimum(xv, 0.0) - jnp.log(1.0 + jnp.exp(-jnp.abs(xv)))
            acc = carry
            for part in _split3(lf):
                acc = acc + _dot(lower, part, NN)
            c_ref[0, rows, :] = acc
            carry = acc[tb - 1:tb, :]

    blk = pl.BlockSpec((1, s, LANES), lambda i: (i, 0, 0))
    return _call(body, name=name, grid=(b,),
                 in_specs=[blk, pl.BlockSpec((1, LANES), lambda i: (0, 0))],
                 out_specs=blk, out_shape=jax.ShapeDtypeStruct((b, s, LANES), F32),
                 semantics=("parallel",))(fl, bf)


def _gates_bwd(dc, fl, bf, heads, name):
    b, s, _ = fl.shape

    tb = _scan_tile(s)

    def body(d_ref, f_ref, b_ref, o_ref, db_ref):
        dst = lax.broadcasted_iota(jnp.int32, (tb, tb), 0)
        src = lax.broadcasted_iota(jnp.int32, (tb, tb), 1)
        later = (src >= dst).astype(BF16)
        lane = lax.broadcasted_iota(jnp.int32, (tb, LANES), 1)
        carry = jnp.zeros((1, LANES), F32)
        db = jnp.zeros((1, LANES), F32)
        for blk_i in reversed(range(s // tb)):
            rows = slice(blk_i * tb, (blk_i + 1) * tb)
            dlf = carry
            for part in _split3(d_ref[0, rows, :]):
                dlf = dlf + _dot(later, part, NN)
            carry = dlf[0:1, :]
            xv = f_ref[0, rows, :] + b_ref[...]
            dfl = jnp.where(lane < heads, dlf * _sigmoid(-xv), 0.0)
            o_ref[0, rows, :] = dfl.astype(BF16)
            db = db + jnp.sum(dfl, axis=0, keepdims=True)
        db_ref[0] = db

    blk = pl.BlockSpec((1, s, LANES), lambda i: (i, 0, 0))
    return _call(body, name=name, grid=(b,),
                 in_specs=[blk, blk, pl.BlockSpec((1, LANES), lambda i: (0, 0))],
                 out_specs=[blk, pl.BlockSpec((1, 1, LANES), lambda i: (i, 0, 0))],
                 out_shape=[jax.ShapeDtypeStruct((b, s, LANES), BF16),
                            jax.ShapeDtypeStruct((b, 1, LANES), F32)],
                 semantics=("parallel",))(dc, fl, bf)


LANE_CQ = 64
LANE_CK = 67
LANE_LSE = 70
LANE_D = 64
N_PARTS = 3


def _attn_tiles(s):
    return min(512, s), min(256, s)


def _lanes_in(lane, first):
    return (lane >= first) & (lane < first + N_PARTS)


def _attn_prep_fwd(pa, c, name):
    b, s, a4 = pa.shape
    pairs = a4 // (4 * LANES)
    scale = 1.0 / math.sqrt(HEAD_DIM)

    def body(q_ref, k_ref, v_ref, c_ref, qa_ref, ka_ref, kat_ref, va_ref, vt_ref):
        hp = pl.program_id(1)
        cv = c_ref[0]
        vv = v_ref[0]
        lane = lax.broadcasted_iota(jnp.int32, (s, LANES), 1)
        r128 = lax.broadcasted_iota(jnp.int32, (LANES, LANES), 0)
        c128 = lax.broadcasted_iota(jnp.int32, (LANES, LANES), 1)
        ident = (r128 == c128).astype(BF16)
        for j in range(2):
            head = 2 * hp + j
            move128 = (r128 == c128 + HEAD_DIM * j) & (c128 < HEAD_DIM)
            cparts = _split3(jnp.sum(jnp.where(lane == head, cv, 0.0), axis=1, keepdims=True))
            qa = _dot(q_ref[0], jnp.where(move128, scale, 0.0).astype(BF16), NN)
            ka = _dot(k_ref[0], move128.astype(BF16), NN)
            for i in range(N_PARTS):
                qa = jnp.where(lane == LANE_CQ + i, cparts[i].astype(F32), qa)
                ka = jnp.where(lane == LANE_CK + i, -cparts[i].astype(F32), ka)
            qa = jnp.where(_lanes_in(lane, LANE_CK), 1.0, qa)
            ka = jnp.where(_lanes_in(lane, LANE_CQ) | _lanes_in(lane, LANE_LSE), 1.0, ka)
            va = _dot(vv, move128.astype(BF16), NN) + jnp.where(_lanes_in(lane, LANE_D), 1.0, 0.0)
            kab = ka.astype(BF16)
            qa_ref[0, 0, j] = qa.astype(BF16)
            ka_ref[0, 0, j] = kab
            kat_ref[0, 0, j] = _dot(ident, kab, NT).astype(BF16)
            va_ref[0, 0, j] = va.astype(BF16)
        vt_ref[0, 0] = _dot(ident, vv, NT).astype(BF16)

    col_blk = lambda cidx: pl.BlockSpec((1, s, LANES), lambda bi, hp: (bi, 0, cidx * pairs + hp))
    tok = pl.BlockSpec((1, 1, 2, s, LANES), lambda bi, hp: (bi, hp, 0, 0, 0))
    tok_t = pl.BlockSpec((1, 1, 2, LANES, s), lambda bi, hp: (bi, hp, 0, 0, 0))
    tok_shape = jax.ShapeDtypeStruct((b, pairs, 2, s, LANES), BF16)
    return _call(
        body, name=name, grid=(b, pairs),
        in_specs=[col_blk(0), col_blk(1), col_blk(2),
                  pl.BlockSpec((1, s, LANES), lambda bi, hp: (bi, 0, 0))],
        out_specs=[tok, tok, tok_t, tok,
                   pl.BlockSpec((1, 1, LANES, s), lambda bi, hp: (bi, hp, 0, 0))],
        out_shape=[tok_shape, tok_shape, jax.ShapeDtypeStruct((b, pairs, 2, LANES, s), BF16),
                   tok_shape, jax.ShapeDtypeStruct((b, pairs, LANES, s), BF16)],
        semantics=("parallel", "parallel"))(pa, pa, pa, c)


def _attn_fwd(qa, ka, vt, pa, name):
    b, pairs, _, s, _ = qa.shape
    a = pairs * LANES
    tq = min(1024, s)
    nq = s // tq

    def body(q_ref, k_ref, vt_ref, z_ref, o_ref, g_ref, lse_ref):
        key_i = lax.broadcasted_iota(jnp.int32, (tq, tq), 0)
        qry_i = lax.broadcasted_iota(jnp.int32, (tq, tq), 1)

        def query_block(c):
            past = tq * c
            heads_out = []
            for j in range(2):
                qv = q_ref[0, 0, j]
                vrows = slice(HEAD_DIM * j, HEAD_DIM * (j + 1))
                sd = _dot(k_ref[0, 0, j, past:past + tq, :], qv, NT)
                sd = jnp.where(key_i <= qry_i, sd, NEG_INF)
                m = jnp.max(sd, axis=0, keepdims=True)
                if c > 0:
                    sp = _dot(k_ref[0, 0, j, 0:past, :], qv, NT)
                    m = jnp.maximum(m, jnp.max(sp, axis=0, keepdims=True))
                pd = jnp.exp(sd - m)
                l = jnp.sum(pd, axis=0, keepdims=True)
                acc = _dot(vt_ref[0, 0, vrows, past:past + tq], pd.astype(BF16), NN)
                if c > 0:
                    pp = jnp.exp(sp - m)
                    l = l + jnp.sum(pp, axis=0, keepdims=True)
                    acc = acc + _dot(vt_ref[0, 0, vrows, 0:past], pp.astype(BF16), NN)
                heads_out.append(acc / l)
                lse_ref[0, 0, j:j + 1, :] = m + jnp.log(l)
            ov = jnp.transpose(jnp.concatenate(heads_out, axis=0))
            o_ref[0] = ov.astype(BF16)
            zv = z_ref[0].astype(F32)
            g_ref[0] = (ov * zv * _sigmoid(zv)).astype(BF16)

        for c in range(nq):
            pl.when(pl.program_id(2) == c)(functools.partial(query_block, c))

    return _call(
        body, name=name, grid=(b, pairs, s // tq),
        in_specs=[pl.BlockSpec((1, 1, 2, tq, LANES), lambda bi, hp, qi: (bi, hp, 0, qi, 0)),
                  pl.BlockSpec((1, 1, 2, s, LANES), lambda bi, hp, qi: (bi, hp, 0, 0, 0)),
                  pl.BlockSpec((1, 1, LANES, s), lambda bi, hp, qi: (bi, hp, 0, 0)),
                  pl.BlockSpec((1, tq, LANES), lambda bi, hp, qi: (bi, qi, 3 * pairs + hp))],
        out_specs=[pl.BlockSpec((1, tq, LANES), lambda bi, hp, qi: (bi, qi, hp)),
                   pl.BlockSpec((1, tq, LANES), lambda bi, hp, qi: (bi, qi, hp)),
                   pl.BlockSpec((1, 1, 2, tq), lambda bi, hp, qi: (bi, hp, 0, qi))],
        out_shape=[jax.ShapeDtypeStruct((b, s, a), BF16), jax.ShapeDtypeStruct((b, s, a), BF16),
                   jax.ShapeDtypeStruct((b, pairs, 2, s), F32)],
        semantics=("parallel", "parallel", "arbitrary"))(qa, ka, vt, pa)


def _attn_prep_bwd(dcat, pa, o, lse, qa, name):
    b, pairs, _, s, _ = qa.shape
    a = pairs * LANES
    sub = 16

    def body(da_ref, z_ref, o_ref, lse_ref, qa_ref, qab_ref, doa_ref, dz_ref):
        zv = z_ref[0].astype(F32)
        dav = da_ref[0].astype(F32)
        ov = o_ref[0].astype(F32)
        sg = _sigmoid(zv)
        dov = dav * zv * sg
        dz_ref[0] = (dav * ov * sg * (1.0 + zv * (1.0 - sg))).astype(BF16)
        prod = dov * ov
        dob = dov.astype(BF16)
        lane = lax.broadcasted_iota(jnp.int32, (s, LANES), 1)
        r128 = lax.broadcasted_iota(jnp.int32, (LANES, LANES), 0)
        c128 = lax.broadcasted_iota(jnp.int32, (LANES, LANES), 1)
        prow = lax.broadcasted_iota(jnp.int32, (sub, s), 0)
        srow = lax.broadcasted_iota(jnp.int32, (sub, LANES), 0)
        scol = lax.broadcasted_iota(jnp.int32, (sub, LANES), 1)
        place = ((scol == srow + LANE_LSE) & (srow < N_PARTS)).astype(BF16)
        for j in range(2):
            in_head = (lane >= HEAD_DIM * j) & (lane < HEAD_DIM * (j + 1))
            dparts = _split3(jnp.sum(jnp.where(in_head, prod, 0.0), axis=1, keepdims=True))
            move128 = ((r128 == c128 + HEAD_DIM * j) & (c128 < HEAD_DIM)).astype(BF16)
            doa = _dot(dob, move128, NN)
            for i in range(N_PARTS):
                doa = jnp.where(lane == LANE_D + i, -dparts[i].astype(F32), doa)
            doa_ref[0, 0, j] = doa.astype(BF16)
            lparts = _split3(lse_ref[0, 0, j:j + 1, :])
            pmat = jnp.zeros((sub, s), BF16)
            for i in range(N_PARTS):
                pmat = jnp.where(prow == i, lparts[i], pmat)
            lcol = _dot(pmat, place, TN)
            qab_ref[0, 0, j] = (qa_ref[0, 0, j].astype(F32) - lcol).astype(BF16)

    tok = pl.BlockSpec((1, 1, 2, s, LANES), lambda bi, hp: (bi, hp, 0, 0, 0))
    tok_shape = jax.ShapeDtypeStruct((b, pairs, 2, s, LANES), BF16)
    pair_blk = pl.BlockSpec((1, s, LANES), lambda bi, hp: (bi, 0, hp))
    return _call(
        body, name=name, grid=(b, pairs),
        in_specs=[pair_blk,
                  pl.BlockSpec((1, s, LANES), lambda bi, hp: (bi, 0, 3 * pairs + hp)),
                  pair_blk,
                  pl.BlockSpec((1, 1, 2, s), lambda bi, hp: (bi, hp, 0, 0)),
                  tok],
        out_specs=[tok, tok, pair_blk],
        out_shape=[tok_shape, tok_shape, jax.ShapeDtypeStruct((b, s, a), BF16)],
        semantics=("parallel", "parallel"))(dcat, pa, o, lse, qa)


def _attn_bwd(ka, kat, va, qab, doa, name):
    b, pairs, _, s, _ = ka.shape
    a = pairs * LANES
    tq, tk = _attn_tiles(s)
    ratio = tq // tk
    nq, nk = s // tq, s // tk
    scale = 1.0 / math.sqrt(HEAD_DIM)

    def body(k_ref, kt_ref, v_ref, q_ref, do_ref, dq_ref, dk_ref, dv_ref, dc_ref,
             dqt_acc, dk_s, dv_s):
        key_i = lax.broadcasted_iota(jnp.int32, (tk, tq), 0)
        qry_i = lax.broadcasted_iota(jnp.int32, (tk, tq), 1)
        lane = lax.broadcasted_iota(jnp.int32, (tq, LANES), 1)
        low = lane < HEAD_DIM

        def key_block(kj):
            krows = slice(kj * tk, (kj + 1) * tk)
            q0 = (kj // ratio) * tq
            spans = [(slice(q0, q0 + tq), kj * tk - q0)]
            if q0 + tq < s:
                spans.append((slice(q0 + tq, s), None))
            for j in range(2):
                kb = k_ref[0, 0, j, krows, :]
                vb = v_ref[0, 0, j, krows, :]
                ktb = kt_ref[0, 0, j, :, krows]
                dk = dv = None
                for qrows, diag in spans:
                    qb = q_ref[0, 0, j, qrows, :]
                    dob = do_ref[0, 0, j, qrows, :]
                    pt = jnp.exp(_dot(kb, qb, NT))
                    if diag is not None:
                        pt = jnp.where(key_i + diag <= qry_i, pt, 0.0)
                    dsb = (pt * _dot(vb, dob, NT)).astype(BF16)
                    dv_part = _dot(pt.astype(BF16), dob, NN)
                    dk_part = _dot(dsb, qb, NN)
                    dv = dv_part if dv is None else dv + dv_part
                    dk = dk_part if dk is None else dk + dk_part
                    dq_part = _dot(ktb, dsb, NN)
                    if kj == 0:
                        dqt_acc[j, :, qrows] = dq_part
                    else:
                        dqt_acc[j, :, qrows] += dq_part
                dk_s[j, krows, :] = dk
                dv_s[j, krows, :] = dv

        for kj in range(nk):
            key_block(kj)

        def finish(i, _):
            rows = pl.ds(pl.multiple_of(i * tq, tq), tq)
            dq = [jnp.transpose(dqt_acc[j, :, rows]) for j in range(2)]
            dk = [dk_s[j, rows, :] for j in range(2)]
            dv = [dv_s[j, rows, :] for j in range(2)]
            dcol = [dq[j][:, LANE_CQ:LANE_CQ + 1] - dk[j][:, LANE_CK:LANE_CK + 1] for j in range(2)]
            dq = [dq[j] * scale for j in range(2)]
            for out_ref, val in ((dq_ref, dq), (dk_ref, dk), (dv_ref, dv)):
                merged = jnp.where(low, val[0], pltpu.roll(val[1], HEAD_DIM, 1))
                out_ref[0, rows, :] = merged.astype(BF16)
            hp = pl.program_id(1)
            prev = jnp.where(hp == 0, 0.0, dc_ref[0, rows, :])
            dc_ref[0, rows, :] = jnp.where(lane == 2 * hp, dcol[0],
                                           jnp.where(lane == 2 * hp + 1, dcol[1], prev))
            return 0

        lax.fori_loop(0, nq, finish, 0)

    tok = pl.BlockSpec((1, 1, 2, s, LANES), lambda bi, hp: (bi, hp, 0, 0, 0))
    tok_t = pl.BlockSpec((1, 1, 2, LANES, s), lambda bi, hp: (bi, hp, 0, 0, 0))
    pair_blk = pl.BlockSpec((1, s, LANES), lambda bi, hp: (bi, 0, hp))
    pair_shape = jax.ShapeDtypeStruct((b, s, a), BF16)
    return _call(
        body, name=name, grid=(b, pairs),
        in_specs=[tok, tok_t, tok, tok, tok],
        out_specs=[pair_blk, pair_blk, pair_blk,
                   pl.BlockSpec((1, s, LANES), lambda bi, hp: (bi, 0, 0))],
        out_shape=[pair_shape, pair_shape, pair_shape,
                   jax.ShapeDtypeStruct((b, s, LANES), F32)],
        scratch_shapes=[pltpu.VMEM((2, LANES, s), F32), pltpu.VMEM((2, s, LANES), F32),
                        pltpu.VMEM((2, s, LANES), F32)],
        semantics=("parallel", "arbitrary"))(ka, kat, va, qab, doa)


def _pool_tile(s):
    return min(256, s)


def _band(tb, window, shift):
    tgt = lax.broadcasted_iota(jnp.int32, (tb, tb), 0)
    src = lax.broadcasted_iota(jnp.int32, (tb, tb), 1) + shift
    return ((src <= tgt) & (src > tgt - window)).astype(BF16)


def _band_t(tb, window, shift):
    src = lax.broadcasted_iota(jnp.int32, (tb, tb), 0)
    tgt = lax.broadcasted_iota(jnp.int32, (tb, tb), 1) + shift
    return ((src <= tgt) & (src > tgt - window)).astype(BF16)


def _pool_fwd(pp, w_pool, scale, name):
    b, s, pw2 = pp.shape
    pw = pw2 // 2
    pg = pw // N_POOL_GROUPS
    tb = _pool_tile(s)
    nb = s // tb

    def body(u_ref, z_ref, w_ref, s_ref, o_ref):
        window = 2 << pl.program_id(1)
        band0 = _band(tb, window, 0)
        band1 = _band(tb, window, -tb)
        pos = lax.broadcasted_iota(jnp.int32, (tb, pg), 0)

        def block(i, _):
            rows = pl.ds(pl.multiple_of(i * tb, tb), tb)
            prev = pl.ds(pl.multiple_of(jnp.maximum(i - 1, 0) * tb, tb), tb)
            ub = u_ref[0, rows, :]
            up = u_ref[0, prev, :]
            up = jnp.where(i > 0, up, jnp.zeros_like(up))
            count = jnp.minimum(pos + i * tb + 1, window).astype(F32)
            pooled = (_dot(band0, ub, NN) + _dot(band1, up, NN)) / count - ub.astype(F32)
            mixed = _dot(pooled.astype(BF16), w_ref[0], NN) * s_ref[...]
            zv = z_ref[0, rows, :].astype(F32)
            o_ref[0, rows, :] = (mixed * zv * _sigmoid(zv)).astype(BF16)
            return 0

        lax.fori_loop(0, nb, block, 0)

    return _call(
        body, name=name, grid=(b, N_POOL_GROUPS),
        in_specs=[pl.BlockSpec((1, s, pg), lambda bi, g: (bi, 0, g)),
                  pl.BlockSpec((1, s, pg), lambda bi, g: (bi, 0, N_POOL_GROUPS + g)),
                  pl.BlockSpec((1, pg, pg), lambda bi, g: (g, 0, 0)),
                  pl.BlockSpec((1, pg), lambda bi, g: (0, g))],
        out_specs=pl.BlockSpec((1, s, pg), lambda bi, g: (bi, 0, g)),
        out_shape=jax.ShapeDtypeStruct((b, s, pw), BF16),
        semantics=("parallel", "parallel"))(pp, pp, w_pool, scale)


def _pool_bwd(pp, dcat, w_pool, scale, first_block, name):
    b, s, pw2 = pp.shape
    pw = pw2 // 2
    pg = pw // N_POOL_GROUPS
    tb = _pool_tile(s)
    nb = s // tb

    def body(u_ref, z_ref, d_ref, w_ref, s_ref, du_ref, dz_ref, dw_ref, ds_ref, dpool_s):
        @pl.when(pl.program_id(1) == 0)
        def _():
            dw_ref[...] = jnp.zeros_like(dw_ref)
            ds_ref[...] = jnp.zeros_like(ds_ref)

        window = 2 << pl.program_id(0)
        band0 = _band(tb, window, 0)
        band1 = _band(tb, window, -tb)
        band0_t = _band_t(tb, window, 0)
        band1_t = _band_t(tb, window, tb)
        pos = lax.broadcasted_iota(jnp.int32, (tb, pg), 0)

        def first(i, _):
            rows = pl.ds(pl.multiple_of(i * tb, tb), tb)
            prev = pl.ds(pl.multiple_of(jnp.maximum(i - 1, 0) * tb, tb), tb)
            ub = u_ref[0, rows, :]
            up = u_ref[0, prev, :]
            up = jnp.where(i > 0, up, jnp.zeros_like(up))
            count = jnp.minimum(pos + i * tb + 1, window).astype(F32)
            pooled = ((_dot(band0, ub, NN) + _dot(band1, up, NN)) / count
                      - ub.astype(F32)).astype(BF16)
            mixed = _dot(pooled, w_ref[0], NN)
            pm = mixed * s_ref[...]
            zv = z_ref[0, rows, :].astype(F32)
            sg = _sigmoid(zv)
            dpl = d_ref[0, rows, :].astype(F32)
            dpm = dpl * zv * sg
            dz_ref[0, rows, :] = (dpl * pm * sg * (1.0 + zv * (1.0 - sg))).astype(BF16)
            ds_ref[...] += jnp.sum(dpm * mixed, axis=0, keepdims=True)
            dmixed = (dpm * s_ref[...]).astype(BF16)
            dw_ref[0] += _dot(pooled, dmixed, TN)
            dpool_s[rows, :] = _dot(dmixed, w_ref[0], NT)
            return 0

        lax.fori_loop(0, nb, first, 0)

        def second(i, _):
            rows = pl.ds(pl.multiple_of(i * tb, tb), tb)
            nxt_i = jnp.minimum(i + 1, nb - 1)
            nxt = pl.ds(pl.multiple_of(nxt_i * tb, tb), tb)
            count = jnp.minimum(pos + i * tb + 1, window).astype(F32)
            count_n = jnp.minimum(pos + nxt_i * tb + 1, window).astype(F32)
            dpb = dpool_s[rows, :]
            cur = (dpb / count).astype(BF16)
            nx = dpool_s[nxt, :] / count_n
            nx = jnp.where(i < nb - 1, nx, 0.0).astype(BF16)
            du = _dot(band0_t, cur, NN) + _dot(band1_t, nx, NN) - dpb
            du_ref[0, rows, :] = du.astype(BF16)
            return 0

        lax.fori_loop(0, nb, second, 0)

    return _call(
        body, name=name, grid=(N_POOL_GROUPS, b),
        in_specs=[pl.BlockSpec((1, s, pg), lambda g, bi: (bi, 0, g)),
                  pl.BlockSpec((1, s, pg), lambda g, bi: (bi, 0, N_POOL_GROUPS + g)),
                  pl.BlockSpec((1, s, pg), lambda g, bi: (bi, 0, first_block + g)),
                  pl.BlockSpec((1, pg, pg), lambda g, bi: (g, 0, 0)),
                  pl.BlockSpec((1, pg), lambda g, bi: (0, g))],
        out_specs=[pl.BlockSpec((1, s, pg), lambda g, bi: (bi, 0, g)),
                   pl.BlockSpec((1, s, pg), lambda g, bi: (bi, 0, g)),
                   pl.BlockSpec((1, pg, pg), lambda g, bi: (g, 0, 0)),
                   pl.BlockSpec((1, pg), lambda g, bi: (0, g))],
        out_shape=[jax.ShapeDtypeStruct((b, s, pw), BF16), jax.ShapeDtypeStruct((b, s, pw), BF16),
                   jax.ShapeDtypeStruct((N_POOL_GROUPS, pg, pg), F32),
                   jax.ShapeDtypeStruct((1, pw), F32)],
        scratch_shapes=[pltpu.VMEM((s, pg), F32)],
        semantics=("parallel", "arbitrary"))(pp, pp, dcat, w_pool, scale)


def _adamw(recvs, sent, me, w, m, v, name):
    depth = len(recvs)
    r, c = w.shape[1:]
    tr = min(128, r)
    nb = r // tr
    c1 = 1.0 - ADAM_B1 ** ADAM_STEP
    c2 = 1.0 - ADAM_B2 ** ADAM_STEP
    slotted = sent[0].ndim == 3

    def body(me_ref, *refs):
        recv_refs, own_refs = refs[:depth], refs[depth:2 * depth]
        w_ref, m_ref, v_ref, g_ref, d_ref, nm_ref, nv_ref = refs[2 * depth:]
        me = me_ref[0]
        for layer in range(depth):
            @pl.when(pl.program_id(0) == layer)
            def _(layer=layer):
                own = (own_refs[layer][0] if slotted else own_refs[layer][...]).astype(F32)
                g = jnp.where(me == 0, own, recv_refs[layer][0].astype(F32))
                for sl in range(1, N_DEV):
                    g = g + jnp.where(me == sl, own, recv_refs[layer][sl].astype(F32))
                mn = ADAM_B1 * m_ref[0] + (1.0 - ADAM_B1) * g
                vn = ADAM_B2 * v_ref[0] + (1.0 - ADAM_B2) * (g * g)
                m_hat = mn / c1
                v_hat = vn / c2
                g_ref[0] = g
                d_ref[0] = -ADAM_LR * (m_hat / (jnp.sqrt(v_hat) + ADAM_EPS) + ADAM_WD * w_ref[0])
                nm_ref[0] = mn
                nv_ref[0] = vn

    def blk(layer):
        return lambda l, i: jnp.clip(i + (l - layer) * nb, 0, nb - 1)

    in_specs = [pl.BlockSpec((N_DEV, tr, c), lambda l, i, me_ref, f=blk(layer): (0, f(l, i), 0))
                for layer in range(depth)]
    if slotted:
        in_specs += [pl.BlockSpec((1, tr, c),
                                  lambda l, i, me_ref, f=blk(layer): (me_ref[0], f(l, i), 0))
                     for layer in range(depth)]
    else:
        in_specs += [pl.BlockSpec((tr, c), lambda l, i, me_ref, f=blk(layer): (f(l, i), 0))
                     for layer in range(depth)]
    row = pl.BlockSpec((1, tr, c), lambda l, i, me_ref: (l, i, 0))
    return pl.pallas_call(
        body, name=name, out_shape=[jax.ShapeDtypeStruct((depth, r, c), F32)] * 4,
        grid_spec=pltpu.PrefetchScalarGridSpec(
            num_scalar_prefetch=1, grid=(depth, nb), in_specs=in_specs + [row, row, row],
            out_specs=[row] * 4),
        compiler_params=pltpu.CompilerParams(dimension_semantics=("arbitrary", "arbitrary"),
                                             vmem_limit_bytes=VMEM_LIMIT_BYTES),
    )(me, *recvs, *sent, w, m, v)


def _pack_w_in(gathered, a, heads, pw):
    d = gathered.shape[1]
    w_full = jnp.transpose(gathered, (1, 0, 2)).reshape(d, -1)
    wf = jnp.pad(w_full[:, 4 * a:4 * a + heads], ((0, 0), (0, LANES - heads)))
    return w_full, w_full[:, 4 * a + heads:], wf


def _unpack_dw_in(parts, heads):
    dq, dk, dv, dz, dwf, du, dzp = parts
    d = dq.shape[0]
    full = jnp.concatenate([dq, dk, dv, dz, dwf[:, :heads], du, dzp], axis=1)
    return jnp.transpose(full.reshape(d, N_DEV, -1), (1, 0, 2))


def kernel(x, p, norm_pre, norm_post, w_in, b_f, w_pool, pool_scale, w_out, w_pg, w_pe, loss_target, m_norm_pre, m_norm_post, m_w_in, m_b_f, m_w_pool, m_pool_scale, m_w_out, m_w_pg, m_w_pe, v_norm_pre, v_norm_post, v_w_in, v_b_f, v_w_pool, v_pool_scale, v_w_out, v_w_pg, v_w_pe):
    depth = w_in.shape[0]
    b, s, d = x.shape
    t = b * s
    heads = b_f.shape[1]
    a = heads * HEAD_DIM
    pairs = a // LANES
    pw = pool_scale.shape[1]
    pg = pw // N_POOL_GROUPS
    ple = p.shape[-1]
    mix_w = a + pw

    me = 4 * lax.axis_index("x") + 2 * lax.axis_index("y") + lax.axis_index("c")
    shard = {
        "w_in": [w_in[i].astype(BF16) for i in range(depth)],
        "w_pool": [w_pool[i].reshape(N_POOL_GROUPS * (pg // N_DEV), pg).astype(BF16)
                   for i in range(depth)],
        "w_out": [w_out[i].astype(BF16) for i in range(depth)],
        "w_pg": [w_pg[i].astype(BF16) for i in range(depth)],
        "w_pe": [w_pe[i].astype(BF16) for i in range(depth)],
    }
    names = list(shard)
    rest = names[1:]

    def unpack_rest(lands, layer, which):
        g = {nm: _with_own(ld, shard[nm][layer], me) for nm, ld in zip(which, lands)}
        g_pool = g["w_pool"].reshape(N_DEV, N_POOL_GROUPS, pg // N_DEV, pg)
        return dict(wpool=jnp.transpose(g_pool, (1, 0, 2, 3)).reshape(N_POOL_GROUPS, pg, pg),
                    wout=g["w_out"].reshape(mix_w, d), wpg=g["w_pg"].reshape(d, d),
                    wpe=jnp.transpose(g["w_pe"], (1, 0, 2)).reshape(ple, d))

    g_in0 = _gather_two_level(shard["w_in"][0], "gather_w_in0")
    rest0, tok_rest0 = _exchange_start([(shard[nm][0], False) for nm in rest], g_in0,
                                       "gather_rest0_start")
    later, tok = [], tok_rest0
    for i in range(1, depth):
        hdl, tk_i = _exchange_start([(shard[nm][i], False) for nm in names], g_in0,
                                    "gather_layer%d_start" % i)
        later.append(hdl)
        tok = tok + tk_i

    h = x.reshape(t, d)
    saved = []
    layers = []
    for i in range(depth):
        sv = dict(h=h)
        g_pre = norm_pre[i:i + 1]
        g_post = norm_post[i:i + 1]
        bf = jnp.pad(b_f[i:i + 1], ((0, 0), (0, LANES - heads)))
        scale = pool_scale[i:i + 1]
        if i == 0:
            lw = dict(zip(("wa", "wp", "wf"), _pack_w_in(g_in0, a, heads, pw)))
            g_pre = g_pre + tok
        else:
            lands = _exchange_wait(later[i - 1], h, "gather_layer%d_wait" % i)
            g_in = _with_own(lands[0], shard["w_in"][i], me)
            lw = dict(zip(("wa", "wp", "wf"), _pack_w_in(g_in, a, heads, pw)))
            lw.update(unpack_rest(lands[1:], i, rest))
        hn = _rms_fwd(h, g_pre, "rms_pre")
        pa = _matmul([(hn, lw["wa"])], "nn", BF16, "proj_attn", n_dim=4 * a, tn=2048,
                     n_outer=True).reshape(b, s, 4 * a)
        pp = _matmul([(hn, lw["wp"])], "nn", BF16, "proj_pool", tn=2048,
                     n_outer=True).reshape(b, s, 2 * pw)
        fl = _matmul([(hn, lw["wf"])], "nn", F32, "proj_gate").reshape(b, s, LANES)
        c = _gates_fwd(fl, bf, "gates_fwd")
        qa, ka, kat, va, vt = _attn_prep_fwd(pa, c, "attn_prep_fwd")
        o, ga, lse = _attn_fwd(qa, ka, vt, pa, "attn_fwd")
        if i == 0:
            lw.update(unpack_rest(_exchange_wait(rest0, lse, "gather_rest0_wait"), 0, rest))
        layers.append(lw)
        gp = _pool_fwd(pp, lw["wpool"], scale, "pool_fwd")
        ga2 = ga.reshape(t, a)
        gp2 = gp.reshape(t, pw)
        mix = _matmul([(ga2, lw["wout"], 0, 0), (gp2, lw["wout"], a, 0)], "nn", F32, "mix_out")
        h1, h1b = _post_fwd(h, mix, g_post, "post_fwd")
        pb = p[i].reshape(t, ple).astype(BF16)
        if i < depth - 1:
            gpre, e, h = _ple_fwd(h1, h1b, lw["wpg"], pb, lw["wpe"], None, "ple_fwd")
        else:
            gpre, e, dh, sq = _ple_fwd(h1, h1b, lw["wpg"], pb, lw["wpe"],
                                       loss_target.reshape(t, d), "ple_loss")
        sv.update(hn=hn, pa=pa, pp=pp, fl=fl, bf=bf, qa=qa, ka=ka, kat=kat, va=va, o=o, lse=lse, ga=ga2,
                  gp=gp2, mix=mix,
                  h1b=h1b, pb=pb, gpre=gpre, e=e, g_pre=g_pre, g_post=g_post, scale=scale)
        saved.append(sv)

    loss = lax.psum(0.5 * jnp.sum(sq) / d, MESH_AXES)

    big = {nm: [None] * depth for nm in names}
    small = {nm: [None] * depth for nm in ("norm_pre", "norm_post", "b_f", "pool_scale")}
    grad_handles = [None] * depth
    rest_handles = [None] * depth
    for i in reversed(range(depth)):
        lw, sv = layers[i], saved[i]
        de, dpre = _ple_bwd(dh, sv["gpre"], sv["e"], "ple_bwd")
        dwpe = _matmul([(sv["pb"], de)], "tn", BF16, "dw_pe", tm=1024)
        dwpg = _matmul([(sv["h1b"], dpre)], "tn", BF16, "dw_pg", tm=1024)
        dh1, dmix, dg_post = _matmul_rows(
            [(dpre, lw["wpg"], 0)], [dh, sv["mix"]], sv["g_post"], _post_bwd_epilogue,
            (F32, BF16), "d_h1_post_bwd", tm=512)
        dwout = jnp.concatenate(
            [_matmul([(sv["ga"], dmix)], "tn", BF16, "dw_out_attn", tm=1024),
             _matmul([(sv["gp"], dmix)], "tn", BF16, "dw_out_pool", tm=1024)], axis=0)
        dcat = _matmul([(dmix, lw["wout"])], "nt", BF16, "d_cat", tn=2048).reshape(b, s, mix_w)
        du, dzp, dwpool, dscale = _pool_bwd(sv["pp"], dcat, lw["wpool"], sv["scale"], a // pg,
                                            "pool_bwd")
        big["w_pool"][i] = jnp.transpose(
            dwpool.astype(BF16).reshape(N_POOL_GROUPS, N_DEV, pg // N_DEV, pg), (1, 0, 2, 3)
        ).reshape(N_DEV, N_POOL_GROUPS * (pg // N_DEV), pg)
        big["w_out"][i] = dwout.reshape(N_DEV, mix_w // N_DEV, d)
        big["w_pg"][i] = dwpg.reshape(N_DEV, d // N_DEV, d)
        big["w_pe"][i] = jnp.transpose(dwpe.reshape(ple, N_DEV, d // N_DEV), (1, 0, 2))
        rest_handles[i], tok = _exchange_start([(big[nm][i], True) for nm in rest], du,
                                               "grads_rest%d_start" % i)
        qab, doa, dz = _attn_prep_bwd(dcat, sv["pa"], sv["o"], sv["lse"] + tok, sv["qa"],
                                      "attn_prep_bwd")
        dq, dk, dv, dc = _attn_bwd(sv["ka"], sv["kat"], sv["va"], qab, doa, "attn_bwd")
        dfl, dbf = _gates_bwd(dc, sv["fl"], sv["bf"], heads, "gates_bwd")
        dproj = [g_.reshape(t, -1) for g_ in (dq, dk, dv, dz, dfl, du, dzp)]
        dw_parts = [_matmul([(sv["hn"], g_)], "tn", BF16, "dw_in_%d" % n_, tm=1024)
                    for n_, g_ in enumerate(dproj)]

        big["w_in"][i] = _unpack_dw_in(dw_parts, heads)
        grad_handles[i], tok = _exchange_start([(big["w_in"][i], True)], dw_parts[-1],
                                               "grads_w_in%d_start" % i)

        dq2, dk2, dv2, dz2, dfl2, du2, dzp2 = dproj
        dh, dg_pre = _matmul_rows(
            [(dq2, lw["wa"], 0), (dk2, lw["wa"], a), (dv2, lw["wa"], 2 * a), (dz2, lw["wa"], 3 * a),
             (du2, lw["wp"], 0), (dzp2, lw["wp"], pw), (dfl2, lw["wf"] + tok.astype(BF16), 0)],
            [sv["h"], dh1], sv["g_pre"] + tok, _pre_bwd_epilogue, (F32,), "d_hn_pre_bwd", tm=256)
        small["norm_pre"][i] = dg_pre
        small["norm_post"][i] = dg_post
        small["b_f"][i] = jnp.sum(dbf, axis=0)
        small["pool_scale"][i] = dscale
    grad_x = dh.reshape(b, s, d)

    width = max(d, pw)
    small_names = ("norm_pre", "norm_post", "pool_scale", "b_f")

    def small_rows(get):
        rows = []
        for nm in small_names:
            for i in range(depth):
                v_ = get(nm, i)
                rows.append(jnp.pad(v_, ((0, 0), (0, width - v_.shape[1]))))
        return jnp.concatenate(rows, axis=0)

    small_g = small_rows(lambda nm, i: small[nm][i])
    (small_recv,) = _exchange([(small_g, False)], "exchange_small")
    me1 = jnp.reshape(me, (1,)).astype(jnp.int32)

    weights = dict(norm_pre=norm_pre, norm_post=norm_post, w_in=w_in, b_f=b_f, w_pool=w_pool,
                   pool_scale=pool_scale, w_out=w_out, w_pg=w_pg, w_pe=w_pe)
    mom1 = dict(norm_pre=m_norm_pre, norm_post=m_norm_post, w_in=m_w_in, b_f=m_b_f, w_pool=m_w_pool,
                pool_scale=m_pool_scale, w_out=m_w_out, w_pg=m_w_pg, w_pe=m_w_pe)
    mom2 = dict(norm_pre=v_norm_pre, norm_post=v_norm_post, w_in=v_w_in, b_f=v_b_f, w_pool=v_w_pool,
                pool_scale=v_pool_scale, w_out=v_w_out, w_pg=v_w_pg, w_pe=v_w_pe)

    results = {}

    def update(nm, recvs):
        shp = weights[nm].shape
        sent = [big[nm][i] for i in range(depth)]
        flat = lambda arr: arr.reshape((depth,) + sent[0].shape[1:])
        outs = _adamw(recvs, sent, me1, flat(weights[nm]), flat(mom1[nm]), flat(mom2[nm]),
                      "adamw_" + nm)
        results[nm] = [o_.reshape(shp) for o_ in outs]
        return outs[0]

    got_rest = [_exchange_wait(rest_handles[i], dh, "grads_rest%d_wait" % i) for i in range(depth)]
    for j, nm in enumerate(rest):
        last = update(nm, [got_rest[i][j] for i in range(depth)])

    small_w = small_rows(lambda nm, i: weights[nm][i:i + 1])[None]
    small_m = small_rows(lambda nm, i: mom1[nm][i:i + 1])[None]
    small_v = small_rows(lambda nm, i: mom2[nm][i:i + 1])[None]
    outs = _adamw([small_recv], [small_g], me1, small_w, small_m, small_v, "adamw_small")
    for j, nm in enumerate(small_names):
        cols = weights[nm].shape[1]
        results[nm] = [o_[0, j * depth:(j + 1) * depth, :cols] for o_ in outs]

    got_w_in = [_exchange_wait(grad_handles[i], last + outs[0][0, 0, 0], "grads_w_in%d_wait" % i)[0]
                for i in range(depth)]
    update("w_in", got_w_in)

    order = ("norm_pre", "norm_post", "w_in", "b_f", "w_pool", "pool_scale", "w_out", "w_pg", "w_pe")
    return (loss, grad_x, *[results[nm][0] for nm in order], *[results[nm][1] for nm in order],
            *[results[nm][2] for nm in order], *[results[nm][3] for nm in order])
```

```python
import functools
import math

import jax
import jax.numpy as jnp
from jax import lax
from jax.experimental import pallas as pl
from jax.experimental.pallas import tpu as pltpu

N_DEV = 8
MESH_AXES = ("x", "y", "c")
HEAD_DIM = 64
LANES = 128
N_POOL_GROUPS = 4
EPS = 1e-6
ADAM_LR = 0.001
ADAM_B1 = 0.9
ADAM_B2 = 0.999
ADAM_EPS = 1e-08
ADAM_WD = 0.01
ADAM_STEP = 10
VMEM_LIMIT_BYTES = 56 * 1024 * 1024
F32 = jnp.float32
BF16 = jnp.bfloat16
NEG_INF = float("-inf")


def _call(body, *, name, grid, in_specs, out_specs, out_shape, scratch_shapes=(), semantics=None):
    return pl.pallas_call(
        body, name=name, grid=grid, in_specs=in_specs, out_specs=out_specs, out_shape=out_shape,
        scratch_shapes=list(scratch_shapes),
        compiler_params=pltpu.CompilerParams(dimension_semantics=semantics,
                                             vmem_limit_bytes=VMEM_LIMIT_BYTES))


def _sigmoid(z):
    return 1.0 / (1.0 + jnp.exp(-z))


def _dot(a, b, dims):
    return lax.dot_general(a, b, (dims, ((), ())), preferred_element_type=F32)


NN = ((1,), (0,))
NT = ((1,), (1,))
TN = ((0,), (0,))


def _exchange(items, name):
    n = len(items)
    modes = [s for _, s in items]
    out_shapes = []
    for a, s in items:
        shp = a.shape[1:] if s else a.shape
        out_shapes.append(jax.ShapeDtypeStruct((N_DEV,) + tuple(shp), a.dtype))

    def body(*refs):
        ins = refs[:n]
        outs = refs[n:2 * n]
        send_sems, recv_sems, local_sems = refs[2 * n:]
        x, y, c = (lax.axis_index(ax) for ax in MESH_AXES)
        me = 4 * x + 2 * y + c
        started = []
        for i in range(n):
            mine = ins[i].at[me] if modes[i] else ins[i]
            loc = pltpu.make_async_copy(mine, outs[i].at[me], local_sems.at[i])
            loc.start()
            started.append(loc)
        remote = []
        for k in range(1, N_DEV):
            px = x ^ ((k >> 2) & 1)
            py = y ^ ((k >> 1) & 1)
            pc = c ^ (k & 1)
            peer = me ^ k
            for i in range(n):
                src = ins[i].at[peer] if modes[i] else ins[i]
                cp = pltpu.make_async_remote_copy(
                    src_ref=src, dst_ref=outs[i].at[me],
                    send_sem=send_sems.at[i, k - 1], recv_sem=recv_sems.at[i, k - 1],
                    device_id=(px, py, pc), device_id_type=pl.DeviceIdType.MESH)
                cp.start()
                remote.append(cp)
        for cp in remote:
            cp.wait()
        for loc in started:
            loc.wait()

    hbm = pl.BlockSpec(memory_space=pltpu.HBM)
    return pl.pallas_call(
        body, name=name, out_shape=out_shapes,
        in_specs=[hbm] * n, out_specs=[hbm] * n,
        scratch_shapes=[pltpu.SemaphoreType.DMA((n, N_DEV - 1)),
                        pltpu.SemaphoreType.DMA((n, N_DEV - 1)),
                        pltpu.SemaphoreType.DMA((n,))],
    )(*[a for a, _ in items])


def _gather_two_level(shard, name):
    def body(x_ref, out_ref, send_sems, recv_sems, local_sem):
        x, y, c = (lax.axis_index(ax) for ax in MESH_AXES)
        sibling = (x, y, 1 - c)
        chips = [(1 - x, y), (x, 1 - y), (1 - x, 1 - y)]

        def slot(px, py, pc):
            return out_ref.at[4 * px + 2 * py + pc]

        def copy(k, block, to, src=None):
            return pltpu.make_async_remote_copy(
                src_ref=slot(*block) if src is None else src, dst_ref=slot(*block),
                send_sem=send_sems.at[k], recv_sem=recv_sems.at[k],
                device_id=to, device_id_type=pl.DeviceIdType.MESH)

        mine = pltpu.make_async_copy(x_ref, slot(x, y, c), local_sem)
        mine.start()
        first = [copy(0, (x, y, c), sibling, src=x_ref)]
        first += [copy(1 + j, (x, y, c), (*chip, c), src=x_ref) for j, chip in enumerate(chips)]
        for cp in first:
            cp.start()
        passed = [copy(4 + j, (*chip, c), sibling) for j, chip in enumerate(chips)]
        for j, chip in enumerate(chips):
            copy(1 + j, (*chip, c), (x, y, c)).wait_recv()
            passed[j].start()
        copy(0, (x, y, 1 - c), (x, y, c)).wait_recv()
        for j, chip in enumerate(chips):
            copy(4 + j, (*chip, 1 - c), (x, y, c)).wait_recv()
        for cp in first + passed:
            cp.wait_send()
        mine.wait()

    hbm = pl.BlockSpec(memory_space=pltpu.HBM)
    return pl.pallas_call(
        body, name=name, out_shape=jax.ShapeDtypeStruct((N_DEV,) + shard.shape, shard.dtype),
        in_specs=[hbm], out_specs=hbm,
        scratch_shapes=[pltpu.SemaphoreType.DMA((N_DEV - 1,)), pltpu.SemaphoreType.DMA((N_DEV - 1,)),
                        pltpu.SemaphoreType.DMA],
    )(shard)


def _peer_copies(srcs, lands, modes, send_sems, recv_sems):
    x, y, c = (lax.axis_index(ax) for ax in MESH_AXES)
    me = 4 * x + 2 * y + c
    copies = []
    for k in range(1, N_DEV):
        peer_id = (x ^ ((k >> 2) & 1), y ^ ((k >> 1) & 1), c ^ (k & 1))
        for i, scatter in enumerate(modes):
            src = srcs[i].at[me ^ k] if scatter else srcs[i]
            pair = i * (N_DEV - 1) + k - 1
            copies.append(pltpu.make_async_remote_copy(
                src_ref=src, dst_ref=lands[i].at[me],
                send_sem=send_sems.at[pair], recv_sem=recv_sems.at[pair],
                device_id=peer_id, device_id_type=pl.DeviceIdType.MESH))
    return copies


def _exchange_start(items, after, name):
    n = len(items)
    modes = [s for _, s in items]
    srcs = [pltpu.with_memory_space_constraint(a, pltpu.HBM) for a, _ in items]
    lands = []
    for a, s in items:
        shp = (N_DEV,) + tuple(a.shape[1:] if s else a.shape)
        lands.append(pltpu.with_memory_space_constraint(lax.empty(shp, a.dtype), pltpu.HBM))

    def body(*refs):
        send_sems, recv_sems = refs[2 * n + 1], refs[2 * n + 2]
        token = refs[-1]
        for cp in _peer_copies(refs[:n], refs[n:2 * n], modes, send_sems, recv_sems):
            cp.start()
        token[...] = jnp.zeros_like(token)

    hbm = pl.BlockSpec(memory_space=pltpu.HBM)
    sem = pl.BlockSpec(memory_space=pltpu.SEMAPHORE)
    outs = pl.pallas_call(
        body, name=name,
        out_shape=(pltpu.SemaphoreType.DMA((n * (N_DEV - 1),)),
                   pltpu.SemaphoreType.DMA((n * (N_DEV - 1),)),
                   *[pltpu.HBM(a.shape, a.dtype) for a in srcs + lands],
                   jax.ShapeDtypeStruct((8, LANES), F32)),
        in_specs=[hbm] * (2 * n) + [pl.BlockSpec(memory_space=pl.ANY)],
        out_specs=(sem, sem, *([hbm] * (2 * n)), pl.BlockSpec(memory_space=pltpu.VMEM)),
        input_output_aliases={i: 2 + i for i in range(2 * n)},
        compiler_params=pltpu.CompilerParams(
            has_side_effects=pltpu.SideEffectType.DATAFLOW_SIDE_EFFECTING),
    )(*srcs, *lands, after)
    handle = (modes, outs[0], outs[1], list(outs[2:2 + n]), list(outs[2 + n:2 + 2 * n]))
    return handle, outs[-1][0, 0]


def _exchange_wait(handle, after, name):
    modes, send_sems, recv_sems, srcs, lands = handle
    n = len(modes)

    def body(*refs):
        for cp in _peer_copies(refs[:n], refs[n:2 * n], modes, refs[2 * n], refs[2 * n + 1]):
            cp.wait_send()
            cp.wait_recv()

    hbm = pl.BlockSpec(memory_space=pltpu.HBM)
    sem = pl.BlockSpec(memory_space=pltpu.SEMAPHORE)
    outs = pl.pallas_call(
        body, name=name,
        out_shape=tuple(pltpu.HBM(a.shape, a.dtype) for a in srcs + lands),
        in_specs=[hbm] * (2 * n) + [sem, sem, pl.BlockSpec(memory_space=pl.ANY)],
        out_specs=tuple([hbm] * (2 * n)),
        input_output_aliases={i: i for i in range(2 * n)},
        compiler_params=pltpu.CompilerParams(
            has_side_effects=pltpu.SideEffectType.DATAFLOW_SIDE_EFFECTING),
    )(*srcs, *lands, send_sems, recv_sems, after)
    return list(outs[n:])


def _with_own(slots, own, me):
    idx = lax.broadcasted_iota(jnp.int32, (N_DEV,) + (1,) * own.ndim, 0)
    return jnp.where(idx == me, own[None], slots)


def _matmul(pairs, mode, out_dtype, name, n_dim=None, tm=512, tn=1024, tk=1024, n_outer=False):
    dims = {"nn": NN, "nt": NT, "tn": TN}[mode]
    pairs = [tuple(pr) + (0, 0) * (len(pr) == 2) for pr in pairs]
    a0, b0 = pairs[0][:2]
    m_dim = a0.shape[1] if mode == "tn" else a0.shape[0]
    if n_dim is None:
        n_dim = b0.shape[0] if mode == "nt" else b0.shape[1]
    tm = min(tm, m_dim)
    tn = min(tn, n_dim)
    segs = []
    off = 0
    for a, _, k0, n0 in pairs:
        k_dim = a.shape[0] if mode == "tn" else a.shape[1]
        t = min(tk, k_dim)
        segs.append((off, k_dim // t, t, k0 // t, n0 // tn))
        off += k_dim // t
    nk = off
    n_pairs = len(pairs)

    def ij(g0, g1):
        return (g1, g0) if n_outer else (g0, g1)

    in_specs = []
    for (o, cnt, t, kb, nb) in segs:
        def kc(kk, o=o, cnt=cnt):
            return jnp.clip(kk - o, 0, cnt - 1)
        if mode == "tn":
            in_specs.append(pl.BlockSpec((t, tm), lambda g0, g1, kk, kc=kc: (kc(kk), ij(g0, g1)[0])))
        else:
            in_specs.append(pl.BlockSpec((tm, t), lambda g0, g1, kk, kc=kc: (ij(g0, g1)[0], kc(kk))))
        if mode == "nt":
            in_specs.append(pl.BlockSpec((tn, t), lambda g0, g1, kk, kc=kc, kb=kb, nb=nb:
                                         (nb + ij(g0, g1)[1], kb + kc(kk))))
        else:
            in_specs.append(pl.BlockSpec((t, tn), lambda g0, g1, kk, kc=kc, kb=kb, nb=nb:
                                         (kb + kc(kk), nb + ij(g0, g1)[1])))

    one_shot = all(sg[1] == 1 for sg in segs)

    def body_sum(*refs):
        total = _dot(refs[0][...], refs[1][...], dims)
        for idx in range(1, n_pairs):
            total = total + _dot(refs[2 * idx][...], refs[2 * idx + 1][...], dims)
        refs[2 * n_pairs][...] = total.astype(out_dtype)

    def body(*refs):
        out_ref = refs[2 * n_pairs]
        acc = refs[2 * n_pairs + 1]
        kk = pl.program_id(2)

        @pl.when(kk == 0)
        def _():
            acc[...] = jnp.zeros_like(acc)

        for idx, (o, cnt) in enumerate(sg[:2] for sg in segs):
            @pl.when((kk >= o) & (kk < o + cnt))
            def _(idx=idx):
                acc[...] += _dot(refs[2 * idx][...], refs[2 * idx + 1][...], dims)

        @pl.when(kk == nk - 1)
        def _():
            out_ref[...] = acc[...].astype(out_dtype)

    flat = [t for pr in pairs for t in pr[:2]]
    tiles = (m_dim // tm, n_dim // tn)
    return _call(body_sum if one_shot else body, name=name,
                 grid=ij(*tiles) + (1 if one_shot else nk,), in_specs=in_specs,
                 out_specs=pl.BlockSpec((tm, tn), lambda g0, g1, kk: ij(g0, g1)),
                 out_shape=jax.ShapeDtypeStruct((m_dim, n_dim), out_dtype),
                 scratch_shapes=[] if one_shot else [pltpu.VMEM((tm, tn), F32)],
                 semantics=("parallel", "parallel", "arbitrary"))(*flat)


def _row_tile(t):
    return min(512, t)


def _rms_fwd(h, g, name):
    t, d = h.shape
    tt = _row_tile(t)

    def body(h_ref, g_ref, o_ref):
        hv = h_ref[...]
        r = lax.rsqrt(jnp.mean(hv * hv, axis=-1, keepdims=True) + EPS)
        o_ref[...] = (hv * r * g_ref[...]).astype(BF16)

    row = pl.BlockSpec((tt, d), lambda i: (i, 0))
    vec = pl.BlockSpec((1, d), lambda i: (0, 0))
    return _call(body, name=name, grid=(t // tt,), in_specs=[row, vec], out_specs=row,
                 out_shape=jax.ShapeDtypeStruct((t, d), BF16), semantics=("parallel",))(h, g)


def _post_fwd(h, mix, g, name):
    t, d = h.shape
    tt = _row_tile(t)

    def body(h_ref, m_ref, g_ref, o_ref, ob_ref):
        mv = m_ref[...]
        r = lax.rsqrt(jnp.mean(mv * mv, axis=-1, keepdims=True) + EPS)
        h1 = h_ref[...] + mv * r * g_ref[...]
        o_ref[...] = h1
        ob_ref[...] = h1.astype(BF16)

    row = pl.BlockSpec((tt, d), lambda i: (i, 0))
    vec = pl.BlockSpec((1, d), lambda i: (0, 0))
    return _call(body, name=name, grid=(t // tt,), in_specs=[row, row, vec], out_specs=[row, row],
                 out_shape=[jax.ShapeDtypeStruct((t, d), F32), jax.ShapeDtypeStruct((t, d), BF16)],
                 semantics=("parallel",))(h, mix, g)


def _ple_fwd(h1, h1b, wpg, pb, wpe, target, name):
    t, d = h1.shape
    tt = _row_tile(t)
    last = target is not None

    def body(h_ref, hb_ref, wg_ref, p_ref, we_ref, *rest):
        gpre = _dot(hb_ref[...], wg_ref[...], NN)
        e = _dot(p_ref[...], we_ref[...], NN)
        y = h_ref[...] + _sigmoid(gpre) * e
        if last:
            t_ref, g_ref, e_ref, dy_ref, s_ref = rest

            @pl.when(pl.program_id(0) == 0)
            def _():
                s_ref[...] = jnp.zeros_like(s_ref)
            diff = y - t_ref[...]
            dy_ref[...] = diff * (1.0 / d)
            s_ref[...] += jnp.sum(diff * diff, axis=0, keepdims=True)
        else:
            g_ref, e_ref, y_ref = rest
            y_ref[...] = y
        g_ref[...] = gpre
        e_ref[...] = e

    row = pl.BlockSpec((tt, d), lambda i: (i, 0))
    whole = lambda arr: pl.BlockSpec(arr.shape, lambda i: (0, 0))
    rows_f32 = jax.ShapeDtypeStruct((t, d), F32)
    in_specs = [row, row, whole(wpg), pl.BlockSpec((tt, pb.shape[1]), lambda i: (i, 0)), whole(wpe)]
    operands = [h1, h1b, wpg, pb, wpe]
    out_specs, out_shape = [row, row, row], [rows_f32, rows_f32, rows_f32]
    if last:
        in_specs.append(row)
        operands.append(target)
        out_specs.append(pl.BlockSpec((1, d), lambda i: (0, 0)))
        out_shape.append(jax.ShapeDtypeStruct((1, d), F32))
    return _call(body, name=name, grid=(t // tt,), in_specs=in_specs, out_specs=out_specs,
                 out_shape=out_shape, semantics=("arbitrary",))(*operands)


def _ple_bwd(dh2, gpre, e, name):
    t, d = dh2.shape
    tt = _row_tile(t)

    def body(d_ref, g_ref, e_ref, de_ref, dp_ref):
        gate = _sigmoid(g_ref[...])
        dv = d_ref[...]
        de_ref[...] = (dv * gate).astype(BF16)
        dp_ref[...] = (dv * e_ref[...] * gate * (1.0 - gate)).astype(BF16)

    row = pl.BlockSpec((tt, d), lambda i: (i, 0))
    return _call(body, name=name, grid=(t // tt,), in_specs=[row, row, row], out_specs=[row, row],
                 out_shape=[jax.ShapeDtypeStruct((t, d), BF16)] * 2,
                 semantics=("parallel",))(dh2, gpre, e)


def _weight_grads(x, grads, name, tk=512):
    n = len(grads)
    t, d = x.shape
    c = grads[0].shape[1]
    tk = min(tk, t)
    nk = t // tk

    def body(x_ref, *refs):
        g_refs, out_ref, acc = refs[:n], refs[n], refs[n + 1]
        seg, kk = pl.program_id(0), pl.program_id(1)

        @pl.when(kk == 0)
        def _():
            acc[...] = jnp.zeros_like(acc)

        for idx in range(n):
            @pl.when(seg == idx)
            def _(idx=idx):
                acc[...] += _dot(x_ref[...], g_refs[idx][...], TN)

        @pl.when(kk == nk - 1)
        def _():
            out_ref[0] = acc[...].astype(BF16)

    def tile_of(idx):
        return lambda seg, kk: (jnp.where(seg == idx, kk, jnp.where(seg < idx, 0, nk - 1)), 0)

    return _call(body, name=name, grid=(n, nk),
                 in_specs=[pl.BlockSpec((tk, d), lambda seg, kk: (kk, 0))]
                 + [pl.BlockSpec((tk, c), tile_of(idx)) for idx in range(n)],
                 out_specs=pl.BlockSpec((1, d, c), lambda seg, kk: (seg, 0, 0)),
                 out_shape=jax.ShapeDtypeStruct((n, d, c), BF16),
                 scratch_shapes=[pltpu.VMEM((d, c), F32)],
                 semantics=("arbitrary", "arbitrary"))(x, *grads)


def _matmul_rows(pairs, rows_in, vec_in, epilogue, row_dtypes, name, tm):
    n_pairs = len(pairs)
    m_dim = pairs[0][0].shape[0]
    n_dim = pairs[0][1].shape[0]
    tm = min(tm, m_dim)
    in_specs = []
    for a, _, k0 in pairs:
        k_dim = a.shape[1]
        in_specs.append(pl.BlockSpec((tm, k_dim), lambda i: (i, 0)))
        in_specs.append(pl.BlockSpec((n_dim, k_dim), lambda i, kb=k0 // k_dim: (0, kb)))
    row = pl.BlockSpec((tm, n_dim), lambda i: (i, 0))
    vec = pl.BlockSpec((1, n_dim), lambda i: (0, 0))
    n_rows = len(rows_in)

    def body(*refs):
        ops = refs[:2 * n_pairs]
        row_refs = refs[2 * n_pairs:2 * n_pairs + n_rows]
        vec_ref = refs[2 * n_pairs + n_rows]
        outs = refs[2 * n_pairs + n_rows + 1:]
        total = _dot(ops[0][...], ops[1][...], NT)
        for idx in range(1, n_pairs):
            total = total + _dot(ops[2 * idx][...], ops[2 * idx + 1][...], NT)
        results, partial = epilogue(total, [r[...] for r in row_refs], vec_ref[...])
        for out_ref, val in zip(outs[:-1], results):
            out_ref[...] = val.astype(out_ref.dtype)

        @pl.when(pl.program_id(0) == 0)
        def _():
            outs[-1][...] = jnp.zeros_like(outs[-1])
        outs[-1][...] += partial

    flat = [t_ for a, b_, _ in pairs for t_ in (a, b_)]
    return _call(body, name=name, grid=(m_dim // tm,),
                 in_specs=in_specs + [row] * n_rows + [vec],
                 out_specs=[row] * len(row_dtypes) + [vec],
                 out_shape=[jax.ShapeDtypeStruct((m_dim, n_dim), dt) for dt in row_dtypes]
                 + [jax.ShapeDtypeStruct((1, n_dim), F32)],
                 semantics=("arbitrary",))(*flat, *rows_in, vec_in)


def _post_bwd_epilogue(t1, rows, g):
    dh2, mv = rows
    dh1 = dh2 + t1
    r = lax.rsqrt(jnp.mean(mv * mv, axis=-1, keepdims=True) + EPS)
    w = dh1 * g
    dot = jnp.mean(w * mv, axis=-1, keepdims=True)
    dmix = r * w - mv * (r * r * r) * dot
    return (dh1, dmix), jnp.sum(dh1 * mv * r, axis=0, keepdims=True)


def _pre_bwd_epilogue(dhn, rows, g):
    hv, dh1 = rows
    r = lax.rsqrt(jnp.mean(hv * hv, axis=-1, keepdims=True) + EPS)
    w = dhn * g
    dot = jnp.mean(w * hv, axis=-1, keepdims=True)
    return (dh1 + r * w - hv * (r * r * r) * dot,), jnp.sum(dhn * hv * r, axis=0, keepdims=True)


def _split3(v):
    hi = v.astype(BF16)
    r1 = v - hi.astype(F32)
    mid = r1.astype(BF16)
    lo = (r1 - mid.astype(F32)).astype(BF16)
    return hi, mid, lo


def _scan_tile(s):
    return min(256, s)


def _gates_fwd(fl, bf, name):
    b, s, _ = fl.shape

    tb = _scan_tile(s)

    def body(f_ref, b_ref, c_ref):
        dst = lax.broadcasted_iota(jnp.int32, (tb, tb), 0)
        src = lax.broadcasted_iota(jnp.int32, (tb, tb), 1)
        lower = (src <= dst).astype(BF16)
        carry = jnp.zeros((1, LANES), F32)
        for blk_i in range(s // tb):
            rows = slice(blk_i * tb, (blk_i + 1) * tb)
            xv = f_ref[0, rows, :] + b_ref[...]
            lf = jnp.minimum(xv, 0.0) - jnp.log(1.0 + jnp.exp(-jnp.abs(xv)))
            acc = carry
            for part in _split3(lf):
                acc = acc + _dot(lower, part, NN)
            c_ref[0, rows, :] = acc
            carry = acc[tb - 1:tb, :]

    blk = pl.BlockSpec((1, s, LANES), lambda i: (i, 0, 0))
    return _call(body, name=name, grid=(b,),
                 in_specs=[blk, pl.BlockSpec((1, LANES), lambda i: (0, 0))],
                 out_specs=blk, out_shape=jax.ShapeDtypeStruct((b, s, LANES), F32),
                 semantics=("parallel",))(fl, bf)


def _gates_bwd(dc, fl, bf, heads, name):
    b, s, _ = fl.shape

    tb = _scan_tile(s)

    def body(d_ref, f_ref, b_ref, o_ref, db_ref):
        dst = lax.broadcasted_iota(jnp.int32, (tb, tb), 0)
        src = lax.broadcasted_iota(jnp.int32, (tb, tb), 1)
        later = (src >= dst).astype(BF16)
        lane = lax.broadcasted_iota(jnp.int32, (tb, LANES), 1)
        carry = jnp.zeros((1, LANES), F32)
        db = jnp.zeros((1, LANES), F32)
        for blk_i in reversed(range(s // tb)):
            rows = slice(blk_i * tb, (blk_i + 1) * tb)
            dlf = carry
            for part in _split3(d_ref[0, rows, :]):
                dlf = dlf + _dot(later, part, NN)
            carry = dlf[0:1, :]
            xv = f_ref[0, rows, :] + b_ref[...]
            dfl = jnp.where(lane < heads, dlf * _sigmoid(-xv), 0.0)
            o_ref[0, rows, :] = dfl.astype(BF16)
            db = db + jnp.sum(dfl, axis=0, keepdims=True)
        db_ref[0] = db

    blk = pl.BlockSpec((1, s, LANES), lambda i: (i, 0, 0))
    return _call(body, name=name, grid=(b,),
                 in_specs=[blk, blk, pl.BlockSpec((1, LANES), lambda i: (0, 0))],
                 out_specs=[blk, pl.BlockSpec((1, 1, LANES), lambda i: (i, 0, 0))],
                 out_shape=[jax.ShapeDtypeStruct((b, s, LANES), BF16),
                            jax.ShapeDtypeStruct((b, 1, LANES), F32)],
                 semantics=("parallel",))(dc, fl, bf)


LANE_CQ = 64
LANE_CK = 67
LANE_LSE = 70
LANE_D = 64
N_PARTS = 3


def _attn_tiles(s):
    return min(512, s), min(256, s)


def _lanes_in(lane, first):
    return (lane >= first) & (lane < first + N_PARTS)


def _attn_prep_fwd(pa, c, name):
    b, s, a4 = pa.shape
    pairs = a4 // (4 * LANES)
    scale = 1.0 / math.sqrt(HEAD_DIM)

    def body(q_ref, k_ref, v_ref, c_ref, qa_ref, ka_ref, kat_ref, va_ref, vt_ref):
        hp = pl.program_id(1)
        cv = c_ref[0]
        vv = v_ref[0]
        lane = lax.broadcasted_iota(jnp.int32, (s, LANES), 1)
        r128 = lax.broadcasted_iota(jnp.int32, (LANES, LANES), 0)
        c128 = lax.broadcasted_iota(jnp.int32, (LANES, LANES), 1)
        ident = (r128 == c128).astype(BF16)
        for j in range(2):
            head = 2 * hp + j
            move128 = (r128 == c128 + HEAD_DIM * j) & (c128 < HEAD_DIM)
            cparts = _split3(jnp.sum(jnp.where(lane == head, cv, 0.0), axis=1, keepdims=True))
            qa = _dot(q_ref[0], jnp.where(move128, scale, 0.0).astype(BF16), NN)
            ka = _dot(k_ref[0], move128.astype(BF16), NN)
            for i in range(N_PARTS):
                qa = jnp.where(lane == LANE_CQ + i, cparts[i].astype(F32), qa)
                ka = jnp.where(lane == LANE_CK + i, -cparts[i].astype(F32), ka)
            qa = jnp.where(_lanes_in(lane, LANE_CK), 1.0, qa)
            ka = jnp.where(_lanes_in(lane, LANE_CQ) | _lanes_in(lane, LANE_LSE), 1.0, ka)
            va = _dot(vv, move128.astype(BF16), NN) + jnp.where(_lanes_in(lane, LANE_D), 1.0, 0.0)
            kab = ka.astype(BF16)
            qa_ref[0, 0, j] = qa.astype(BF16)
            ka_ref[0, 0, j] = kab
            kat_ref[0, 0, j] = _dot(ident, kab, NT).astype(BF16)
            va_ref[0, 0, j] = va.astype(BF16)
        vt_ref[0, 0] = _dot(ident, vv, NT).astype(BF16)

    col_blk = lambda cidx: pl.BlockSpec((1, s, LANES), lambda bi, hp: (bi, 0, cidx * pairs + hp))
    tok = pl.BlockSpec((1, 1, 2, s, LANES), lambda bi, hp: (bi, hp, 0, 0, 0))
    tok_t = pl.BlockSpec((1, 1, 2, LANES, s), lambda bi, hp: (bi, hp, 0, 0, 0))
    tok_shape = jax.ShapeDtypeStruct((b, pairs, 2, s, LANES), BF16)
    return _call(
        body, name=name, grid=(b, pairs),
        in_specs=[col_blk(0), col_blk(1), col_blk(2),
                  pl.BlockSpec((1, s, LANES), lambda bi, hp: (bi, 0, 0))],
        out_specs=[tok, tok, tok_t, tok,
                   pl.BlockSpec((1, 1, LANES, s), lambda bi, hp: (bi, hp, 0, 0))],
        out_shape=[tok_shape, tok_shape, jax.ShapeDtypeStruct((b, pairs, 2, LANES, s), BF16),
                   tok_shape, jax.ShapeDtypeStruct((b, pairs, LANES, s), BF16)],
        semantics=("parallel", "parallel"))(pa, pa, pa, c)


def _attn_fwd(qa, ka, vt, pa, name):
    b, pairs, _, s, _ = qa.shape
    a = pairs * LANES
    tq = min(1024, s)
    nq = s // tq

    def body(q_ref, k_ref, vt_ref, z_ref, o_ref, g_ref, lse_ref):
        key_i = lax.broadcasted_iota(jnp.int32, (tq, tq), 0)
        qry_i = lax.broadcasted_iota(jnp.int32, (tq, tq), 1)

        def query_block(c):
            past = tq * c
            heads_out = []
            for j in range(2):
                qv = q_ref[0, 0, j]
                vrows = slice(HEAD_DIM * j, HEAD_DIM * (j + 1))
                sd = _dot(k_ref[0, 0, j, past:past + tq, :], qv, NT)
                sd = jnp.where(key_i <= qry_i, sd, NEG_INF)
                m = jnp.max(sd, axis=0, keepdims=True)
                if c > 0:
                    sp = _dot(k_ref[0, 0, j, 0:past, :], qv, NT)
                    m = jnp.maximum(m, jnp.max(sp, axis=0, keepdims=True))
                pd = jnp.exp(sd - m)
                l = jnp.sum(pd, axis=0, keepdims=True)
                acc = _dot(vt_ref[0, 0, vrows, past:past + tq], pd.astype(BF16), NN)
                if c > 0:
                    pp = jnp.exp(sp - m)
                    l = l + jnp.sum(pp, axis=0, keepdims=True)
                    acc = acc + _dot(vt_ref[0, 0, vrows, 0:past], pp.astype(BF16), NN)
                heads_out.append(acc / l)
                lse_ref[0, 0, j:j + 1, :] = m + jnp.log(l)
            ov = jnp.transpose(jnp.concatenate(heads_out, axis=0))
            o_ref[0] = ov.astype(BF16)
            zv = z_ref[0].astype(F32)
            g_ref[0] = (ov * zv * _sigmoid(zv)).astype(BF16)

        for c in range(nq):
            pl.when(pl.program_id(2) == c)(functools.partial(query_block, c))

    return _call(
        body, name=name, grid=(b, pairs, s // tq),
        in_specs=[pl.BlockSpec((1, 1, 2, tq, LANES), lambda bi, hp, qi: (bi, hp, 0, qi, 0)),
                  pl.BlockSpec((1, 1, 2, s, LANES), lambda bi, hp, qi: (bi, hp, 0, 0, 0)),
                  pl.BlockSpec((1, 1, LANES, s), lambda bi, hp, qi: (bi, hp, 0, 0)),
                  pl.BlockSpec((1, tq, LANES), lambda bi, hp, qi: (bi, qi, 3 * pairs + hp))],
        out_specs=[pl.BlockSpec((1, tq, LANES), lambda bi, hp, qi: (bi, qi, hp)),
                   pl.BlockSpec((1, tq, LANES), lambda bi, hp, qi: (bi, qi, hp)),
                   pl.BlockSpec((1, 1, 2, tq), lambda bi, hp, qi: (bi, hp, 0, qi))],
        out_shape=[jax.ShapeDtypeStruct((b, s, a), BF16), jax.ShapeDtypeStruct((b, s, a), BF16),
                   jax.ShapeDtypeStruct((b, pairs, 2, s), F32)],
        semantics=("parallel", "parallel", "arbitrary"))(qa, ka, vt, pa)


def _attn_prep_bwd(dcat, pa, o, lse, qa, name):
    b, pairs, _, s, _ = qa.shape
    a = pairs * LANES
    sub = 16

    def body(da_ref, z_ref, o_ref, lse_ref, qa_ref, qab_ref, doa_ref, dz_ref):
        zv = z_ref[0].astype(F32)
        dav = da_ref[0].astype(F32)
        ov = o_ref[0].astype(F32)
        sg = _sigmoid(zv)
        dov = dav * zv * sg
        dz_ref[0] = (dav * ov * sg * (1.0 + zv * (1.0 - sg))).astype(BF16)
        prod = dov * ov
        dob = dov.astype(BF16)
        lane = lax.broadcasted_iota(jnp.int32, (s, LANES), 1)
        r128 = lax.broadcasted_iota(jnp.int32, (LANES, LANES), 0)
        c128 = lax.broadcasted_iota(jnp.int32, (LANES, LANES), 1)
        prow = lax.broadcasted_iota(jnp.int32, (sub, s), 0)
        srow = lax.broadcasted_iota(jnp.int32, (sub, LANES), 0)
        scol = lax.broadcasted_iota(jnp.int32, (sub, LANES), 1)
        place = ((scol == srow + LANE_LSE) & (srow < N_PARTS)).astype(BF16)
        for j in range(2):
            in_head = (lane >= HEAD_DIM * j) & (lane < HEAD_DIM * (j + 1))
            dparts = _split3(jnp.sum(jnp.where(in_head, prod, 0.0), axis=1, keepdims=True))
            move128 = ((r128 == c128 + HEAD_DIM * j) & (c128 < HEAD_DIM)).astype(BF16)
            doa = _dot(dob, move128, NN)
            for i in range(N_PARTS):
                doa = jnp.where(lane == LANE_D + i, -dparts[i].astype(F32), doa)
            doa_ref[0, 0, j] = doa.astype(BF16)
            lparts = _split3(lse_ref[0, 0, j:j + 1, :])
            pmat = jnp.zeros((sub, s), BF16)
            for i in range(N_PARTS):
                pmat = jnp.where(prow == i, lparts[i], pmat)
            lcol = _dot(pmat, place, TN)
            qab_ref[0, 0, j] = (qa_ref[0, 0, j].astype(F32) - lcol).astype(BF16)

    tok = pl.BlockSpec((1, 1, 2, s, LANES), lambda bi, hp: (bi, hp, 0, 0, 0))
    tok_shape = jax.ShapeDtypeStruct((b, pairs, 2, s, LANES), BF16)
    pair_blk = pl.BlockSpec((1, s, LANES), lambda bi, hp: (bi, 0, hp))
    return _call(
        body, name=name, grid=(b, pairs),
        in_specs=[pair_blk,
                  pl.BlockSpec((1, s, LANES), lambda bi, hp: (bi, 0, 3 * pairs + hp)),
                  pair_blk,
                  pl.BlockSpec((1, 1, 2, s), lambda bi, hp: (bi, hp, 0, 0)),
                  tok],
        out_specs=[tok, tok, pair_blk],
        out_shape=[tok_shape, tok_shape, jax.ShapeDtypeStruct((b, s, a), BF16)],
        semantics=("parallel", "parallel"))(dcat, pa, o, lse, qa)


def _attn_bwd(ka, kat, va, qab, doa, name):
    b, pairs, _, s, _ = ka.shape
    a = pairs * LANES
    tq, tk = _attn_tiles(s)
    ratio = tq // tk
    nq, nk = s // tq, s // tk
    scale = 1.0 / math.sqrt(HEAD_DIM)

    def body(k_ref, kt_ref, v_ref, q_ref, do_ref, dq_ref, dk_ref, dv_ref, dc_ref,
             dqt_acc, dk_s, dv_s):
        key_i = lax.broadcasted_iota(jnp.int32, (tk, tq), 0)
        qry_i = lax.broadcasted_iota(jnp.int32, (tk, tq), 1)
        lane = lax.broadcasted_iota(jnp.int32, (tq, LANES), 1)
        low = lane < HEAD_DIM

        def key_block(kj):
            krows = slice(kj * tk, (kj + 1) * tk)
            q0 = (kj // ratio) * tq
            spans = [(slice(q0, q0 + tq), kj * tk - q0)]
            if q0 + tq < s:
                spans.append((slice(q0 + tq, s), None))
            for j in range(2):
                kb = k_ref[0, 0, j, krows, :]
                vb = v_ref[0, 0, j, krows, :]
                ktb = kt_ref[0, 0, j, :, krows]
                dk = dv = None
                for qrows, diag in spans:
                    qb = q_ref[0, 0, j, qrows, :]
                    dob = do_ref[0, 0, j, qrows, :]
                    pt = jnp.exp(_dot(kb, qb, NT))
                    if diag is not None:
                        pt = jnp.where(key_i + diag <= qry_i, pt, 0.0)
                    dsb = (pt * _dot(vb, dob, NT)).astype(BF16)
                    dv_part = _dot(pt.astype(BF16), dob, NN)
                    dk_part = _dot(dsb, qb, NN)
                    dv = dv_part if dv is None else dv + dv_part
                    dk = dk_part if dk is None else dk + dk_part
                    dq_part = _dot(ktb, dsb, NN)
                    if kj == 0:
                        dqt_acc[j, :, qrows] = dq_part
                    else:
                        dqt_acc[j, :, qrows] += dq_part
                dk_s[j, krows, :] = dk
                dv_s[j, krows, :] = dv

        for kj in range(nk):
            key_block(kj)

        def finish(i, _):
            rows = pl.ds(pl.multiple_of(i * tq, tq), tq)
            dq = [jnp.transpose(dqt_acc[j, :, rows]) for j in range(2)]
            dk = [dk_s[j, rows, :] for j in range(2)]
            dv = [dv_s[j, rows, :] for j in range(2)]
            dcol = [dq[j][:, LANE_CQ:LANE_CQ + 1] - dk[j][:, LANE_CK:LANE_CK + 1] for j in range(2)]
            dq = [dq[j] * scale for j in range(2)]
            for out_ref, val in ((dq_ref, dq), (dk_ref, dk), (dv_ref, dv)):
                merged = jnp.where(low, val[0], pltpu.roll(val[1], HEAD_DIM, 1))
                out_ref[0, rows, :] = merged.astype(BF16)
            hp = pl.program_id(1)
            prev = jnp.where(hp == 0, 0.0, dc_ref[0, rows, :])
            dc_ref[0, rows, :] = jnp.where(lane == 2 * hp, dcol[0],
                                           jnp.where(lane == 2 * hp + 1, dcol[1], prev))
            return 0

        lax.fori_loop(0, nq, finish, 0)

    tok = pl.BlockSpec((1, 1, 2, s, LANES), lambda bi, hp: (bi, hp, 0, 0, 0))
    tok_t = pl.BlockSpec((1, 1, 2, LANES, s), lambda bi, hp: (bi, hp, 0, 0, 0))
    pair_blk = pl.BlockSpec((1, s, LANES), lambda bi, hp: (bi, 0, hp))
    pair_shape = jax.ShapeDtypeStruct((b, s, a), BF16)
    return _call(
        body, name=name, grid=(b, pairs),
        in_specs=[tok, tok_t, tok, tok, tok],
        out_specs=[pair_blk, pair_blk, pair_blk,
                   pl.BlockSpec((1, s, LANES), lambda bi, hp: (bi, 0, 0))],
        out_shape=[pair_shape, pair_shape, pair_shape,
                   jax.ShapeDtypeStruct((b, s, LANES), F32)],
        scratch_shapes=[pltpu.VMEM((2, LANES, s), F32), pltpu.VMEM((2, s, LANES), F32),
                        pltpu.VMEM((2, s, LANES), F32)],
        semantics=("parallel", "arbitrary"))(ka, kat, va, qab, doa)


def _pool_tile(s):
    return min(256, s)


def _band(tb, window, shift):
    tgt = lax.broadcasted_iota(jnp.int32, (tb, tb), 0)
    src = lax.broadcasted_iota(jnp.int32, (tb, tb), 1) + shift
    return ((src <= tgt) & (src > tgt - window)).astype(BF16)


def _band_t(tb, window, shift):
    src = lax.broadcasted_iota(jnp.int32, (tb, tb), 0)
    tgt = lax.broadcasted_iota(jnp.int32, (tb, tb), 1) + shift
    return ((src <= tgt) & (src > tgt - window)).astype(BF16)


def _pool_fwd(pp, w_pool, scale, name):
    b, s, pw2 = pp.shape
    pw = pw2 // 2
    pg = pw // N_POOL_GROUPS
    tb = _pool_tile(s)
    nb = s // tb

    def body(u_ref, z_ref, w_ref, s_ref, o_ref):
        window = 2 << pl.program_id(1)
        band0 = _band(tb, window, 0)
        band1 = _band(tb, window, -tb)
        pos = lax.broadcasted_iota(jnp.int32, (tb, pg), 0)

        def block(i, _):
            rows = pl.ds(pl.multiple_of(i * tb, tb), tb)
            prev = pl.ds(pl.multiple_of(jnp.maximum(i - 1, 0) * tb, tb), tb)
            ub = u_ref[0, rows, :]
            up = u_ref[0, prev, :]
            up = jnp.where(i > 0, up, jnp.zeros_like(up))
            count = jnp.minimum(pos + i * tb + 1, window).astype(F32)
            pooled = (_dot(band0, ub, NN) + _dot(band1, up, NN)) / count - ub.astype(F32)
            mixed = _dot(pooled.astype(BF16), w_ref[0], NN) * s_ref[...]
            zv = z_ref[0, rows, :].astype(F32)
            o_ref[0, rows, :] = (mixed * zv * _sigmoid(zv)).astype(BF16)
            return 0

        lax.fori_loop(0, nb, block, 0)

    return _call(
        body, name=name, grid=(b, N_POOL_GROUPS),
        in_specs=[pl.BlockSpec((1, s, pg), lambda bi, g: (bi, 0, g)),
                  pl.BlockSpec((1, s, pg), lambda bi, g: (bi, 0, N_POOL_GROUPS + g)),
                  pl.BlockSpec((1, pg, pg), lambda bi, g: (g, 0, 0)),
                  pl.BlockSpec((1, pg), lambda bi, g: (0, g))],
        out_specs=pl.BlockSpec((1, s, pg), lambda bi, g: (bi, 0, g)),
        out_shape=jax.ShapeDtypeStruct((b, s, pw), BF16),
        semantics=("parallel", "parallel"))(pp, pp, w_pool, scale)


def _pool_bwd(pp, dcat, w_pool, scale, first_block, name):
    b, s, pw2 = pp.shape
    pw = pw2 // 2
    pg = pw // N_POOL_GROUPS
    tb = _pool_tile(s)
    nb = s // tb

    def body(u_ref, z_ref, d_ref, w_ref, s_ref, du_ref, dz_ref, dw_ref, ds_ref, dpool_s):
        @pl.when(pl.program_id(1) == 0)
        def _():
            dw_ref[...] = jnp.zeros_like(dw_ref)
            ds_ref[...] = jnp.zeros_like(ds_ref)

        window = 2 << pl.program_id(0)
        band0 = _band(tb, window, 0)
        band1 = _band(tb, window, -tb)
        band0_t = _band_t(tb, window, 0)
        band1_t = _band_t(tb, window, tb)
        pos = lax.broadcasted_iota(jnp.int32, (tb, pg), 0)

        def first(i, _):
            rows = pl.ds(pl.multiple_of(i * tb, tb), tb)
            prev = pl.ds(pl.multiple_of(jnp.maximum(i - 1, 0) * tb, tb), tb)
            ub = u_ref[0, rows, :]
            up = u_ref[0, prev, :]
            up = jnp.where(i > 0, up, jnp.zeros_like(up))
            count = jnp.minimum(pos + i * tb + 1, window).astype(F32)
            pooled = ((_dot(band0, ub, NN) + _dot(band1, up, NN)) / count
                      - ub.astype(F32)).astype(BF16)
            mixed = _dot(pooled, w_ref[0], NN)
            pm = mixed * s_ref[...]
            zv = z_ref[0, rows, :].astype(F32)
            sg = _sigmoid(zv)
            dpl = d_ref[0, rows, :].astype(F32)
            dpm = dpl * zv * sg
            dz_ref[0, rows, :] = (dpl * pm * sg * (1.0 + zv * (1.0 - sg))).astype(BF16)
            ds_ref[...] += jnp.sum(dpm * mixed, axis=0, keepdims=True)
            dmixed = (dpm * s_ref[...]).astype(BF16)
            dw_ref[0] += _dot(pooled, dmixed, TN)
            dpool_s[rows, :] = _dot(dmixed, w_ref[0], NT)
            return 0

        lax.fori_loop(0, nb, first, 0)

        def second(i, _):
            rows = pl.ds(pl.multiple_of(i * tb, tb), tb)
            nxt_i = jnp.minimum(i + 1, nb - 1)
            nxt = pl.ds(pl.multiple_of(nxt_i * tb, tb), tb)
            count = jnp.minimum(pos + i * tb + 1, window).astype(F32)
            count_n = jnp.minimum(pos + nxt_i * tb + 1, window).astype(F32)
            dpb = dpool_s[rows, :]
            cur = (dpb / count).astype(BF16)
            nx = dpool_s[nxt, :] / count_n
            nx = jnp.where(i < nb - 1, nx, 0.0).astype(BF16)
            du = _dot(band0_t, cur, NN) + _dot(band1_t, nx, NN) - dpb
            du_ref[0, rows, :] = du.astype(BF16)
            return 0

        lax.fori_loop(0, nb, second, 0)

    return _call(
        body, name=name, grid=(N_POOL_GROUPS, b),
        in_specs=[pl.BlockSpec((1, s, pg), lambda g, bi: (bi, 0, g)),
                  pl.BlockSpec((1, s, pg), lambda g, bi: (bi, 0, N_POOL_GROUPS + g)),
                  pl.BlockSpec((1, s, pg), lambda g, bi: (bi, 0, first_block + g)),
                  pl.BlockSpec((1, pg, pg), lambda g, bi: (g, 0, 0)),
                  pl.BlockSpec((1, pg), lambda g, bi: (0, g))],
        out_specs=[pl.BlockSpec((1, s, pg), lambda g, bi: (bi, 0, g)),
                   pl.BlockSpec((1, s, pg), lambda g, bi: (bi, 0, g)),
                   pl.BlockSpec((1, pg, pg), lambda g, bi: (g, 0, 0)),
                   pl.BlockSpec((1, pg), lambda g, bi: (0, g))],
        out_shape=[jax.ShapeDtypeStruct((b, s, pw), BF16), jax.ShapeDtypeStruct((b, s, pw), BF16),
                   jax.ShapeDtypeStruct((N_POOL_GROUPS, pg, pg), F32),
                   jax.ShapeDtypeStruct((1, pw), F32)],
        scratch_shapes=[pltpu.VMEM((s, pg), F32)],
        semantics=("parallel", "arbitrary"))(pp, pp, dcat, w_pool, scale)


def _adamw(recvs, sent, me, w, m, v, name):
    depth = len(recvs)
    r, c = w.shape[1:]
    tr = min(128, r)
    nb = r // tr
    c1 = 1.0 - ADAM_B1 ** ADAM_STEP
    c2 = 1.0 - ADAM_B2 ** ADAM_STEP
    slotted = sent[0].ndim == 3

    def body(me_ref, *refs):
        recv_refs, own_refs = refs[:depth], refs[depth:2 * depth]
        w_ref, m_ref, v_ref, g_ref, d_ref, nm_ref, nv_ref = refs[2 * depth:]
        me = me_ref[0]
        for layer in range(depth):
            @pl.when(pl.program_id(0) == layer)
            def _(layer=layer):
                own = (own_refs[layer][0] if slotted else own_refs[layer][...]).astype(F32)
                g = jnp.where(me == 0, own, recv_refs[layer][0].astype(F32))
                for sl in range(1, N_DEV):
                    g = g + jnp.where(me == sl, own, recv_refs[layer][sl].astype(F32))
                mn = ADAM_B1 * m_ref[0] + (1.0 - ADAM_B1) * g
                vn = ADAM_B2 * v_ref[0] + (1.0 - ADAM_B2) * (g * g)
                m_hat = mn / c1
                v_hat = vn / c2
                g_ref[0] = g
                d_ref[0] = -ADAM_LR * (m_hat / (jnp.sqrt(v_hat) + ADAM_EPS) + ADAM_WD * w_ref[0])
                nm_ref[0] = mn
                nv_ref[0] = vn

    def blk(layer):
        return lambda l, i: jnp.clip(i + (l - layer) * nb, 0, nb - 1)

    in_specs = [pl.BlockSpec((N_DEV, tr, c), lambda l, i, me_ref, f=blk(layer): (0, f(l, i), 0))
                for layer in range(depth)]
    if slotted:
        in_specs += [pl.BlockSpec((1, tr, c),
                                  lambda l, i, me_ref, f=blk(layer): (me_ref[0], f(l, i), 0))
                     for layer in range(depth)]
    else:
        in_specs += [pl.BlockSpec((tr, c), lambda l, i, me_ref, f=blk(layer): (f(l, i), 0))
                     for layer in range(depth)]
    row = pl.BlockSpec((1, tr, c), lambda l, i, me_ref: (l, i, 0))
    return pl.pallas_call(
        body, name=name, out_shape=[jax.ShapeDtypeStruct((depth, r, c), F32)] * 4,
        grid_spec=pltpu.PrefetchScalarGridSpec(
            num_scalar_prefetch=1, grid=(depth, nb), in_specs=in_specs + [row, row, row],
            out_specs=[row] * 4),
        compiler_params=pltpu.CompilerParams(dimension_semantics=("arbitrary", "arbitrary"),
                                             vmem_limit_bytes=VMEM_LIMIT_BYTES),
    )(me, *recvs, *sent, w, m, v)


def _pack_w_in(gathered, a, heads, pw):
    d = gathered.shape[1]
    w_full = jnp.transpose(gathered, (1, 0, 2)).reshape(d, -1)
    wf = jnp.pad(w_full[:, 4 * a:4 * a + heads], ((0, 0), (0, LANES - heads)))
    return w_full, w_full[:, 4 * a + heads:], wf


def _unpack_dw_in(parts, heads):
    dq, dk, dv, dz, dwf, du, dzp = parts
    d = dq.shape[0]
    full = jnp.concatenate([dq, dk, dv, dz, dwf[:, :heads], du, dzp], axis=1)
    return jnp.transpose(full.reshape(d, N_DEV, -1), (1, 0, 2))


def kernel(x, p, norm_pre, norm_post, w_in, b_f, w_pool, pool_scale, w_out, w_pg, w_pe, loss_target, m_norm_pre, m_norm_post, m_w_in, m_b_f, m_w_pool, m_pool_scale, m_w_out, m_w_pg, m_w_pe, v_norm_pre, v_norm_post, v_w_in, v_b_f, v_w_pool, v_pool_scale, v_w_out, v_w_pg, v_w_pe):
    depth = w_in.shape[0]
    b, s, d = x.shape
    t = b * s
    heads = b_f.shape[1]
    a = heads * HEAD_DIM
    pairs = a // LANES
    pw = pool_scale.shape[1]
    pg = pw // N_POOL_GROUPS
    ple = p.shape[-1]
    mix_w = a + pw

    me = 4 * lax.axis_index("x") + 2 * lax.axis_index("y") + lax.axis_index("c")
    shard = {
        "w_in": [w_in[i].astype(BF16) for i in range(depth)],
        "w_pool": [w_pool[i].reshape(N_POOL_GROUPS * (pg // N_DEV), pg).astype(BF16)
                   for i in range(depth)],
        "w_out": [w_out[i].astype(BF16) for i in range(depth)],
        "w_pg": [w_pg[i].astype(BF16) for i in range(depth)],
        "w_pe": [w_pe[i].astype(BF16) for i in range(depth)],
    }
    names = list(shard)
    rest = names[1:]

    def unpack_rest(lands, layer, which):
        g = {nm: _with_own(ld, shard[nm][layer], me) for nm, ld in zip(which, lands)}
        g_pool = g["w_pool"].reshape(N_DEV, N_POOL_GROUPS, pg // N_DEV, pg)
        return dict(wpool=jnp.transpose(g_pool, (1, 0, 2, 3)).reshape(N_POOL_GROUPS, pg, pg),
                    wout=g["w_out"].reshape(mix_w, d), wpg=g["w_pg"].reshape(d, d),
                    wpe=jnp.transpose(g["w_pe"], (1, 0, 2)).reshape(ple, d))

    g_in0 = _gather_two_level(shard["w_in"][0], "gather_w_in0")
    rest0, tok_rest0 = _exchange_start([(shard[nm][0], False) for nm in rest], g_in0,
                                       "gather_rest0_start")
    later, tok = [], tok_rest0
    for i in range(1, depth):
        hdl, tk_i = _exchange_start([(shard[nm][i], False) for nm in names], g_in0,
                                    "gather_layer%d_start" % i)
        later.append(hdl)
        tok = tok + tk_i

    h = x.reshape(t, d)
    saved = []
    layers = []
    for i in range(depth):
        sv = dict(h=h)
        g_pre = norm_pre[i:i + 1]
        g_post = norm_post[i:i + 1]
        bf = jnp.pad(b_f[i:i + 1], ((0, 0), (0, LANES - heads)))
        scale = pool_scale[i:i + 1]
        if i == 0:
            lw = dict(zip(("wa", "wp", "wf"), _pack_w_in(g_in0, a, heads, pw)))
            g_pre = g_pre + tok
        else:
            lands = _exchange_wait(later[i - 1], h, "gather_layer%d_wait" % i)
            g_in = _with_own(lands[0], shard["w_in"][i], me)
            lw = dict(zip(("wa", "wp", "wf"), _pack_w_in(g_in, a, heads, pw)))
            lw.update(unpack_rest(lands[1:], i, rest))
        hn = _rms_fwd(h, g_pre, "rms_pre")
        pa = _matmul([(hn, lw["wa"])], "nn", BF16, "proj_attn", n_dim=4 * a, tn=2048,
                     n_outer=True).reshape(b, s, 4 * a)
        pp = _matmul([(hn, lw["wp"])], "nn", BF16, "proj_pool", tn=2048,
                     n_outer=True).reshape(b, s, 2 * pw)
        fl = _matmul([(hn, lw["wf"])], "nn", F32, "proj_gate").reshape(b, s, LANES)
        c = _gates_fwd(fl, bf, "gates_fwd")
        qa, ka, kat, va, vt = _attn_prep_fwd(pa, c, "attn_prep_fwd")
        o, ga, lse = _attn_fwd(qa, ka, vt, pa, "attn_fwd")
        if i == 0:
            lw.update(unpack_rest(_exchange_wait(rest0, lse, "gather_rest0_wait"), 0, rest))
        layers.append(lw)
        gp = _pool_fwd(pp, lw["wpool"], scale, "pool_fwd")
        ga2 = ga.reshape(t, a)
        gp2 = gp.reshape(t, pw)
        mix = _matmul([(ga2, lw["wout"], 0, 0), (gp2, lw["wout"], a, 0)], "nn", F32, "mix_out")
        h1, h1b = _post_fwd(h, mix, g_post, "post_fwd")
        pb = p[i].reshape(t, ple).astype(BF16)
        if i < depth - 1:
            gpre, e, h = _ple_fwd(h1, h1b, lw["wpg"], pb, lw["wpe"], None, "ple_fwd")
        else:
            gpre, e, dh, sq = _ple_fwd(h1, h1b, lw["wpg"], pb, lw["wpe"],
                                       loss_target.reshape(t, d), "ple_loss")
        sv.update(hn=hn, pa=pa, pp=pp, fl=fl, bf=bf, qa=qa, ka=ka, kat=kat, va=va, o=o, lse=lse, ga=ga2,
                  gp=gp2, mix=mix,
                  h1b=h1b, pb=pb, gpre=gpre, e=e, g_pre=g_pre, g_post=g_post, scale=scale)
        saved.append(sv)

    loss = lax.psum(0.5 * jnp.sum(sq) / d, MESH_AXES)

    big = {nm: [None] * depth for nm in names}
    small = {nm: [None] * depth for nm in ("norm_pre", "norm_post", "b_f", "pool_scale")}
    grad_handles = [None] * depth
    rest_handles = [None] * depth
    for i in reversed(range(depth)):
        lw, sv = layers[i], saved[i]
        de, dpre = _ple_bwd(dh, sv["gpre"], sv["e"], "ple_bwd")
        dwpe = _matmul([(sv["pb"], de)], "tn", BF16, "dw_pe", tm=1024)
        dwpg = _matmul([(sv["h1b"], dpre)], "tn", BF16, "dw_pg", tm=1024)
        dh1, dmix, dg_post = _matmul_rows(
            [(dpre, lw["wpg"], 0)], [dh, sv["mix"]], sv["g_post"], _post_bwd_epilogue,
            (F32, BF16), "d_h1_post_bwd", tm=512)
        dwout = jnp.concatenate(
            [_matmul([(sv["ga"], dmix)], "tn", BF16, "dw_out_attn", tm=1024),
             _matmul([(sv["gp"], dmix)], "tn", BF16, "dw_out_pool", tm=1024)], axis=0)
        dcat = _matmul([(dmix, lw["wout"])], "nt", BF16, "d_cat", tn=2048).reshape(b, s, mix_w)
        du, dzp, dwpool, dscale = _pool_bwd(sv["pp"], dcat, lw["wpool"], sv["scale"], a // pg,
                                            "pool_bwd")
        big["w_pool"][i] = jnp.transpose(
            dwpool.astype(BF16).reshape(N_POOL_GROUPS, N_DEV, pg // N_DEV, pg), (1, 0, 2, 3)
        ).reshape(N_DEV, N_POOL_GROUPS * (pg // N_DEV), pg)
        big["w_out"][i] = dwout.reshape(N_DEV, mix_w // N_DEV, d)
        big["w_pg"][i] = dwpg.reshape(N_DEV, d // N_DEV, d)
        big["w_pe"][i] = jnp.transpose(dwpe.reshape(ple, N_DEV, d // N_DEV), (1, 0, 2))
        rest_handles[i], tok = _exchange_start([(big[nm][i], True) for nm in rest], du,
                                               "grads_rest%d_start" % i)
        qab, doa, dz = _attn_prep_bwd(dcat, sv["pa"], sv["o"], sv["lse"] + tok, sv["qa"],
                                      "attn_prep_bwd")
        dq, dk, dv, dc = _attn_bwd(sv["ka"], sv["kat"], sv["va"], qab, doa, "attn_bwd")
        dfl, dbf = _gates_bwd(dc, sv["fl"], sv["bf"], heads, "gates_bwd")
        dproj = [g_.reshape(t, -1) for g_ in (dq, dk, dv, dz, dfl, du, dzp)]
        assert a == pw, "the six wide column blocks of w_in share one kernel call"
        dw_wide = _weight_grads(sv["hn"], dproj[:4] + dproj[5:], "dw_in")
        dw_gate = _matmul([(sv["hn"], dproj[4])], "tn", BF16, "dw_in_gate", tm=1024)
        dw_parts = [dw_wide[0], dw_wide[1], dw_wide[2], dw_wide[3], dw_gate, dw_wide[4], dw_wide[5]]

        big["w_in"][i] = _unpack_dw_in(dw_parts, heads)
        grad_handles[i], tok = _exchange_start([(big["w_in"][i], True)], dw_parts[-1],
                                               "grads_w_in%d_start" % i)

        dq2, dk2, dv2, dz2, dfl2, du2, dzp2 = dproj
        dh, dg_pre = _matmul_rows(
            [(dq2, lw["wa"], 0), (dk2, lw["wa"], a), (dv2, lw["wa"], 2 * a), (dz2, lw["wa"], 3 * a),
             (du2, lw["wp"], 0), (dzp2, lw["wp"], pw), (dfl2, lw["wf"] + tok.astype(BF16), 0)],
            [sv["h"], dh1], sv["g_pre"] + tok, _pre_bwd_epilogue, (F32,), "d_hn_pre_bwd", tm=256)
        small["norm_pre"][i] = dg_pre
        small["norm_post"][i] = dg_post
        small["b_f"][i] = jnp.sum(dbf, axis=0)
        small["pool_scale"][i] = dscale
    grad_x = dh.reshape(b, s, d)

    width = max(d, pw)
    small_names = ("norm_pre", "norm_post", "pool_scale", "b_f")

    def small_rows(get):
        rows = []
        for nm in small_names:
            for i in range(depth):
                v_ = get(nm, i)
                rows.append(jnp.pad(v_, ((0, 0), (0, width - v_.shape[1]))))
        return jnp.concatenate(rows, axis=0)

    small_g = small_rows(lambda nm, i: small[nm][i])
    (small_recv,) = _exchange([(small_g, False)], "exchange_small")
    me1 = jnp.reshape(me, (1,)).astype(jnp.int32)

    weights = dict(norm_pre=norm_pre, norm_post=norm_post, w_in=w_in, b_f=b_f, w_pool=w_pool,
                   pool_scale=pool_scale, w_out=w_out, w_pg=w_pg, w_pe=w_pe)
    mom1 = dict(norm_pre=m_norm_pre, norm_post=m_norm_post, w_in=m_w_in, b_f=m_b_f, w_pool=m_w_pool,
                pool_scale=m_pool_scale, w_out=m_w_out, w_pg=m_w_pg, w_pe=m_w_pe)
    mom2 = dict(norm_pre=v_norm_pre, norm_post=v_norm_post, w_in=v_w_in, b_f=v_b_f, w_pool=v_w_pool,
                pool_scale=v_pool_scale, w_out=v_w_out, w_pg=v_w_pg, w_pe=v_w_pe)

    results = {}

    def update(nm, recvs):
        shp = weights[nm].shape
        sent = [big[nm][i] for i in range(depth)]
        flat = lambda arr: arr.reshape((depth,) + sent[0].shape[1:])
        outs = _adamw(recvs, sent, me1, flat(weights[nm]), flat(mom1[nm]), flat(mom2[nm]),
                      "adamw_" + nm)
        results[nm] = [o_.reshape(shp) for o_ in outs]
        return outs[0]

    got_rest = [_exchange_wait(rest_handles[i], dh, "grads_rest%d_wait" % i) for i in range(depth)]
    for j, nm in enumerate(rest):
        last = update(nm, [got_rest[i][j] for i in range(depth)])

    small_w = small_rows(lambda nm, i: weights[nm][i:i + 1])[None]
    small_m = small_rows(lambda nm, i: mom1[nm][i:i + 1])[None]
    small_v = small_rows(lambda nm, i: mom2[nm][i:i + 1])[None]
    outs = _adamw([small_recv], [small_g], me1, small_w, small_m, small_v, "adamw_small")
    for j, nm in enumerate(small_names):
        cols = weights[nm].shape[1]
        results[nm] = [o_[0, j * depth:(j + 1) * depth, :cols] for o_ in outs]

    got_w_in = [_exchange_wait(grad_handles[i], last + outs[0][0, 0, 0], "grads_w_in%d_wait" % i)[0]
                for i in range(depth)]
    update("w_in", got_w_in)

    order = ("norm_pre", "norm_post", "w_in", "b_f", "w_pool", "pool_scale", "w_out", "w_pg", "w_pe")
    return (loss, grad_x, *[results[nm][0] for nm in order], *[results[nm][1] for nm in order],
            *[results[nm][2] for nm in order], *[results[nm][3] for nm in order])
```

```python
import functools
import math

import jax
import jax.numpy as jnp
from jax import lax
from jax.experimental import pallas as pl
from jax.experimental.pallas import tpu as pltpu

N_DEV = 8
MESH_AXES = ("x", "y", "c")
HEAD_DIM = 64
LANES = 128
N_POOL_GROUPS = 4
EPS = 1e-6
ADAM_LR = 0.001
ADAM_B1 = 0.9
ADAM_B2 = 0.999
ADAM_EPS = 1e-08
ADAM_WD = 0.01
ADAM_STEP = 10
VMEM_LIMIT_BYTES = 56 * 1024 * 1024
F32 = jnp.float32
BF16 = jnp.bfloat16
NEG_INF = float("-inf")


def _call(body, *, name, grid, in_specs, out_specs, out_shape, scratch_shapes=(), semantics=None):
    return pl.pallas_call(
        body, name=name, grid=grid, in_specs=in_specs, out_specs=out_specs, out_shape=out_shape,
        scratch_shapes=list(scratch_shapes),
        compiler_params=pltpu.CompilerParams(dimension_semantics=semantics,
                                             vmem_limit_bytes=VMEM_LIMIT_BYTES))


def _sigmoid(z):
    return 1.0 / (1.0 + jnp.exp(-z))


def _dot(a, b, dims):
    return lax.dot_general(a, b, (dims, ((), ())), preferred_element_type=F32)


NN = ((1,), (0,))
NT = ((1,), (1,))
TN = ((0,), (0,))


def _exchange(items, name):
    n = len(items)
    modes = [s for _, s in items]
    out_shapes = []
    for a, s in items:
        shp = a.shape[1:] if s else a.shape
        out_shapes.append(jax.ShapeDtypeStruct((N_DEV,) + tuple(shp), a.dtype))

    def body(*refs):
        ins = refs[:n]
        outs = refs[n:2 * n]
        send_sems, recv_sems, local_sems = refs[2 * n:]
        x, y, c = (lax.axis_index(ax) for ax in MESH_AXES)
        me = 4 * x + 2 * y + c
        started = []
        for i in range(n):
            mine = ins[i].at[me] if modes[i] else ins[i]
            loc = pltpu.make_async_copy(mine, outs[i].at[me], local_sems.at[i])
            loc.start()
            started.append(loc)
        remote = []
        for k in range(1, N_DEV):
            px = x ^ ((k >> 2) & 1)
            py = y ^ ((k >> 1) & 1)
            pc = c ^ (k & 1)
            peer = me ^ k
            for i in range(n):
                src = ins[i].at[peer] if modes[i] else ins[i]
                cp = pltpu.make_async_remote_copy(
                    src_ref=src, dst_ref=outs[i].at[me],
                    send_sem=send_sems.at[i, k - 1], recv_sem=recv_sems.at[i, k - 1],
                    device_id=(px, py, pc), device_id_type=pl.DeviceIdType.MESH)
                cp.start()
                remote.append(cp)
        for cp in remote:
            cp.wait()
        for loc in started:
            loc.wait()

    hbm = pl.BlockSpec(memory_space=pltpu.HBM)
    return pl.pallas_call(
        body, name=name, out_shape=out_shapes,
        in_specs=[hbm] * n, out_specs=[hbm] * n,
        scratch_shapes=[pltpu.SemaphoreType.DMA((n, N_DEV - 1)),
                        pltpu.SemaphoreType.DMA((n, N_DEV - 1)),
                        pltpu.SemaphoreType.DMA((n,))],
    )(*[a for a, _ in items])


def _gather_two_level(shard, name):
    def body(x_ref, out_ref, send_sems, recv_sems, local_sem):
        x, y, c = (lax.axis_index(ax) for ax in MESH_AXES)
        sibling = (x, y, 1 - c)
        chips = [(1 - x, y), (x, 1 - y), (1 - x, 1 - y)]

        def slot(px, py, pc):
            return out_ref.at[4 * px + 2 * py + pc]

        def copy(k, block, to, src=None):
            return pltpu.make_async_remote_copy(
                src_ref=slot(*block) if src is None else src, dst_ref=slot(*block),
                send_sem=send_sems.at[k], recv_sem=recv_sems.at[k],
                device_id=to, device_id_type=pl.DeviceIdType.MESH)

        mine = pltpu.make_async_copy(x_ref, slot(x, y, c), local_sem)
        mine.start()
        first = [copy(0, (x, y, c), sibling, src=x_ref)]
        first += [copy(1 + j, (x, y, c), (*chip, c), src=x_ref) for j, chip in enumerate(chips)]
        for cp in first:
            cp.start()
        passed = [copy(4 + j, (*chip, c), sibling) for j, chip in enumerate(chips)]
        for j, chip in enumerate(chips):
            copy(1 + j, (*chip, c), (x, y, c)).wait_recv()
            passed[j].start()
        copy(0, (x, y, 1 - c), (x, y, c)).wait_recv()
        for j, chip in enumerate(chips):
            copy(4 + j, (*chip, 1 - c), (x, y, c)).wait_recv()
        for cp in first + passed:
            cp.wait_send()
        mine.wait()

    hbm = pl.BlockSpec(memory_space=pltpu.HBM)
    return pl.pallas_call(
        body, name=name, out_shape=jax.ShapeDtypeStruct((N_DEV,) + shard.shape, shard.dtype),
        in_specs=[hbm], out_specs=hbm,
        scratch_shapes=[pltpu.SemaphoreType.DMA((N_DEV - 1,)), pltpu.SemaphoreType.DMA((N_DEV - 1,)),
                        pltpu.SemaphoreType.DMA],
    )(shard)


def _peer_copies(srcs, lands, modes, send_sems, recv_sems):
    x, y, c = (lax.axis_index(ax) for ax in MESH_AXES)
    me = 4 * x + 2 * y + c
    copies = []
    for k in range(1, N_DEV):
        peer_id = (x ^ ((k >> 2) & 1), y ^ ((k >> 1) & 1), c ^ (k & 1))
        for i, scatter in enumerate(modes):
            src = srcs[i].at[me ^ k] if scatter else srcs[i]
            pair = i * (N_DEV - 1) + k - 1
            copies.append(pltpu.make_async_remote_copy(
                src_ref=src, dst_ref=lands[i].at[me],
                send_sem=send_sems.at[pair], recv_sem=recv_sems.at[pair],
                device_id=peer_id, device_id_type=pl.DeviceIdType.MESH))
    return copies


def _exchange_start(items, after, name):
    n = len(items)
    modes = [s for _, s in items]
    srcs = [pltpu.with_memory_space_constraint(a, pltpu.HBM) for a, _ in items]
    lands = []
    for a, s in items:
        shp = (N_DEV,) + tuple(a.shape[1:] if s else a.shape)
        lands.append(pltpu.with_memory_space_constraint(lax.empty(shp, a.dtype), pltpu.HBM))

    def body(*refs):
        send_sems, recv_sems = refs[2 * n + 1], refs[2 * n + 2]
        token = refs[-1]
        for cp in _peer_copies(refs[:n], refs[n:2 * n], modes, send_sems, recv_sems):
            cp.start()
        token[...] = jnp.zeros_like(token)

    hbm = pl.BlockSpec(memory_space=pltpu.HBM)
    sem = pl.BlockSpec(memory_space=pltpu.SEMAPHORE)
    outs = pl.pallas_call(
        body, name=name,
        out_shape=(pltpu.SemaphoreType.DMA((n * (N_DEV - 1),)),
                   pltpu.SemaphoreType.DMA((n * (N_DEV - 1),)),
                   *[pltpu.HBM(a.shape, a.dtype) for a in srcs + lands],
                   jax.ShapeDtypeStruct((8, LANES), F32)),
        in_specs=[hbm] * (2 * n) + [pl.BlockSpec(memory_space=pl.ANY)],
        out_specs=(sem, sem, *([hbm] * (2 * n)), pl.BlockSpec(memory_space=pltpu.VMEM)),
        input_output_aliases={i: 2 + i for i in range(2 * n)},
        compiler_params=pltpu.CompilerParams(
            has_side_effects=pltpu.SideEffectType.DATAFLOW_SIDE_EFFECTING),
    )(*srcs, *lands, after)
    handle = (modes, outs[0], outs[1], list(outs[2:2 + n]), list(outs[2 + n:2 + 2 * n]))
    return handle, outs[-1][0, 0]


def _exchange_wait(handle, after, name):
    modes, send_sems, recv_sems, srcs, lands = handle
    n = len(modes)

    def body(*refs):
        for cp in _peer_copies(refs[:n], refs[n:2 * n], modes, refs[2 * n], refs[2 * n + 1]):
            cp.wait_send()
            cp.wait_recv()

    hbm = pl.BlockSpec(memory_space=pltpu.HBM)
    sem = pl.BlockSpec(memory_space=pltpu.SEMAPHORE)
    outs = pl.pallas_call(
        body, name=name,
        out_shape=tuple(pltpu.HBM(a.shape, a.dtype) for a in srcs + lands),
        in_specs=[hbm] * (2 * n) + [sem, sem, pl.BlockSpec(memory_space=pl.ANY)],
        out_specs=tuple([hbm] * (2 * n)),
        input_output_aliases={i: i for i in range(2 * n)},
        compiler_params=pltpu.CompilerParams(
            has_side_effects=pltpu.SideEffectType.DATAFLOW_SIDE_EFFECTING),
    )(*srcs, *lands, send_sems, recv_sems, after)
    return list(outs[n:])


def _with_own(slots, own, me):
    idx = lax.broadcasted_iota(jnp.int32, (N_DEV,) + (1,) * own.ndim, 0)
    return jnp.where(idx == me, own[None], slots)


def _matmul(pairs, mode, out_dtype, name, n_dim=None, tm=512, tn=1024, tk=1024, n_outer=False):
    dims = {"nn": NN, "nt": NT, "tn": TN}[mode]
    pairs = [tuple(pr) + (0, 0) * (len(pr) == 2) for pr in pairs]
    a0, b0 = pairs[0][:2]
    m_dim = a0.shape[1] if mode == "tn" else a0.shape[0]
    if n_dim is None:
        n_dim = b0.shape[0] if mode == "nt" else b0.shape[1]
    tm = min(tm, m_dim)
    tn = min(tn, n_dim)
    segs = []
    off = 0
    for a, _, k0, n0 in pairs:
        k_dim = a.shape[0] if mode == "tn" else a.shape[1]
        t = min(tk, k_dim)
        segs.append((off, k_dim // t, t, k0 // t, n0 // tn))
        off += k_dim // t
    nk = off
    n_pairs = len(pairs)

    def ij(g0, g1):
        return (g1, g0) if n_outer else (g0, g1)

    in_specs = []
    for (o, cnt, t, kb, nb) in segs:
        def kc(kk, o=o, cnt=cnt):
            return jnp.clip(kk - o, 0, cnt - 1)
        if mode == "tn":
            in_specs.append(pl.BlockSpec((t, tm), lambda g0, g1, kk, kc=kc: (kc(kk), ij(g0, g1)[0])))
        else:
            in_specs.append(pl.BlockSpec((tm, t), lambda g0, g1, kk, kc=kc: (ij(g0, g1)[0], kc(kk))))
        if mode == "nt":
            in_specs.append(pl.BlockSpec((tn, t), lambda g0, g1, kk, kc=kc, kb=kb, nb=nb:
                                         (nb + ij(g0, g1)[1], kb + kc(kk))))
        else:
            in_specs.append(pl.BlockSpec((t, tn), lambda g0, g1, kk, kc=kc, kb=kb, nb=nb:
                                         (kb + kc(kk), nb + ij(g0, g1)[1])))

    one_shot = all(sg[1] == 1 for sg in segs)

    def body_sum(*refs):
        total = _dot(refs[0][...], refs[1][...], dims)
        for idx in range(1, n_pairs):
            total = total + _dot(refs[2 * idx][...], refs[2 * idx + 1][...], dims)
        refs[2 * n_pairs][...] = total.astype(out_dtype)

    def body(*refs):
        out_ref = refs[2 * n_pairs]
        acc = refs[2 * n_pairs + 1]
        kk = pl.program_id(2)

        @pl.when(kk == 0)
        def _():
            acc[...] = jnp.zeros_like(acc)

        for idx, (o, cnt) in enumerate(sg[:2] for sg in segs):
            @pl.when((kk >= o) & (kk < o + cnt))
            def _(idx=idx):
                acc[...] += _dot(refs[2 * idx][...], refs[2 * idx + 1][...], dims)

        @pl.when(kk == nk - 1)
        def _():
            out_ref[...] = acc[...].astype(out_dtype)

    flat = [t for pr in pairs for t in pr[:2]]
    tiles = (m_dim // tm, n_dim // tn)
    return _call(body_sum if one_shot else body, name=name,
                 grid=ij(*tiles) + (1 if one_shot else nk,), in_specs=in_specs,
                 out_specs=pl.BlockSpec((tm, tn), lambda g0, g1, kk: ij(g0, g1)),
                 out_shape=jax.ShapeDtypeStruct((m_dim, n_dim), out_dtype),
                 scratch_shapes=[] if one_shot else [pltpu.VMEM((tm, tn), F32)],
                 semantics=("parallel", "parallel", "arbitrary"))(*flat)


def _row_tile(t):
    return min(512, t)


def _rms_fwd(h, g, name):
    t, d = h.shape
    tt = _row_tile(t)

    def body(h_ref, g_ref, o_ref):
        hv = h_ref[...]
        r = lax.rsqrt(jnp.mean(hv * hv, axis=-1, keepdims=True) + EPS)
        o_ref[...] = (hv * r * g_ref[...]).astype(BF16)

    row = pl.BlockSpec((tt, d), lambda i: (i, 0))
    vec = pl.BlockSpec((1, d), lambda i: (0, 0))
    return _call(body, name=name, grid=(t // tt,), in_specs=[row, vec], out_specs=row,
                 out_shape=jax.ShapeDtypeStruct((t, d), BF16), semantics=("parallel",))(h, g)


def _post_fwd(h, mix, g, name):
    t, d = h.shape
    tt = _row_tile(t)

    def body(h_ref, m_ref, g_ref, o_ref, ob_ref):
        mv = m_ref[...]
        r = lax.rsqrt(jnp.mean(mv * mv, axis=-1, keepdims=True) + EPS)
        h1 = h_ref[...] + mv * r * g_ref[...]
        o_ref[...] = h1
        ob_ref[...] = h1.astype(BF16)

    row = pl.BlockSpec((tt, d), lambda i: (i, 0))
    vec = pl.BlockSpec((1, d), lambda i: (0, 0))
    return _call(body, name=name, grid=(t // tt,), in_specs=[row, row, vec], out_specs=[row, row],
                 out_shape=[jax.ShapeDtypeStruct((t, d), F32), jax.ShapeDtypeStruct((t, d), BF16)],
                 semantics=("parallel",))(h, mix, g)


def _ple_fwd(h1, h1b, wpg, pb, wpe, target, name):
    t, d = h1.shape
    tt = _row_tile(t)
    last = target is not None

    def body(h_ref, hb_ref, wg_ref, p_ref, we_ref, *rest):
        gpre = _dot(hb_ref[...], wg_ref[...], NN)
        e = _dot(p_ref[...], we_ref[...], NN)
        y = h_ref[...] + _sigmoid(gpre) * e
        if last:
            t_ref, g_ref, e_ref, dy_ref, s_ref = rest

            @pl.when(pl.program_id(0) == 0)
            def _():
                s_ref[...] = jnp.zeros_like(s_ref)
            diff = y - t_ref[...]
            dy_ref[...] = diff * (1.0 / d)
            s_ref[...] += jnp.sum(diff * diff, axis=0, keepdims=True)
        else:
            g_ref, e_ref, y_ref = rest
            y_ref[...] = y
        g_ref[...] = gpre
        e_ref[...] = e

    row = pl.BlockSpec((tt, d), lambda i: (i, 0))
    whole = lambda arr: pl.BlockSpec(arr.shape, lambda i: (0, 0))
    rows_f32 = jax.ShapeDtypeStruct((t, d), F32)
    in_specs = [row, row, whole(wpg), pl.BlockSpec((tt, pb.shape[1]), lambda i: (i, 0)), whole(wpe)]
    operands = [h1, h1b, wpg, pb, wpe]
    out_specs, out_shape = [row, row, row], [rows_f32, rows_f32, rows_f32]
    if last:
        in_specs.append(row)
        operands.append(target)
        out_specs.append(pl.BlockSpec((1, d), lambda i: (0, 0)))
        out_shape.append(jax.ShapeDtypeStruct((1, d), F32))
    return _call(body, name=name, grid=(t // tt,), in_specs=in_specs, out_specs=out_specs,
                 out_shape=out_shape, semantics=("arbitrary",))(*operands)


def _ple_bwd(dh2, gpre, e, name):
    t, d = dh2.shape
    tt = _row_tile(t)

    def body(d_ref, g_ref, e_ref, de_ref, dp_ref):
        gate = _sigmoid(g_ref[...])
        dv = d_ref[...]
        de_ref[...] = (dv * gate).astype(BF16)
        dp_ref[...] = (dv * e_ref[...] * gate * (1.0 - gate)).astype(BF16)

    row = pl.BlockSpec((tt, d), lambda i: (i, 0))
    return _call(body, name=name, grid=(t // tt,), in_specs=[row, row, row], out_specs=[row, row],
                 out_shape=[jax.ShapeDtypeStruct((t, d), BF16)] * 2,
                 semantics=("parallel",))(dh2, gpre, e)


def _matmul_rows(pairs, rows_in, vec_in, epilogue, row_dtypes, name, tm):
    n_pairs = len(pairs)
    m_dim = pairs[0][0].shape[0]
    n_dim = pairs[0][1].shape[0]
    tm = min(tm, m_dim)
    in_specs = []
    for a, _, k0 in pairs:
        k_dim = a.shape[1]
        in_specs.append(pl.BlockSpec((tm, k_dim), lambda i: (i, 0)))
        in_specs.append(pl.BlockSpec((n_dim, k_dim), lambda i, kb=k0 // k_dim: (0, kb)))
    row = pl.BlockSpec((tm, n_dim), lambda i: (i, 0))
    vec = pl.BlockSpec((1, n_dim), lambda i: (0, 0))
    n_rows = len(rows_in)

    def body(*refs):
        ops = refs[:2 * n_pairs]
        row_refs = refs[2 * n_pairs:2 * n_pairs + n_rows]
        vec_ref = refs[2 * n_pairs + n_rows]
        outs = refs[2 * n_pairs + n_rows + 1:]
        total = _dot(ops[0][...], ops[1][...], NT)
        for idx in range(1, n_pairs):
            total = total + _dot(ops[2 * idx][...], ops[2 * idx + 1][...], NT)
        results, partial = epilogue(total, [r[...] for r in row_refs], vec_ref[...])
        for out_ref, val in zip(outs[:-1], results):
            out_ref[...] = val.astype(out_ref.dtype)

        @pl.when(pl.program_id(0) == 0)
        def _():
            outs[-1][...] = jnp.zeros_like(outs[-1])
        outs[-1][...] += partial

    flat = [t_ for a, b_, _ in pairs for t_ in (a, b_)]
    return _call(body, name=name, grid=(m_dim // tm,),
                 in_specs=in_specs + [row] * n_rows + [vec],
                 out_specs=[row] * len(row_dtypes) + [vec],
                 out_shape=[jax.ShapeDtypeStruct((m_dim, n_dim), dt) for dt in row_dtypes]
                 + [jax.ShapeDtypeStruct((1, n_dim), F32)],
                 semantics=("arbitrary",))(*flat, *rows_in, vec_in)


def _post_bwd_epilogue(t1, rows, g):
    dh2, mv = rows
    dh1 = dh2 + t1
    r = lax.rsqrt(jnp.mean(mv * mv, axis=-1, keepdims=True) + EPS)
    w = dh1 * g
    dot = jnp.mean(w * mv, axis=-1, keepdims=True)
    dmix = r * w - mv * (r * r * r) * dot
    return (dh1, dmix), jnp.sum(dh1 * mv * r, axis=0, keepdims=True)


def _pre_bwd_epilogue(dhn, rows, g):
    hv, dh1 = rows
    r = lax.rsqrt(jnp.mean(hv * hv, axis=-1, keepdims=True) + EPS)
    w = dhn * g
    dot = jnp.mean(w * hv, axis=-1, keepdims=True)
    return (dh1 + r * w - hv * (r * r * r) * dot,), jnp.sum(dhn * hv * r, axis=0, keepdims=True)


def _split3(v):
    hi = v.astype(BF16)
    r1 = v - hi.astype(F32)
    mid = r1.astype(BF16)
    lo = (r1 - mid.astype(F32)).astype(BF16)
    return hi, mid, lo


def _scan_tile(s):
    return min(256, s)


def _gates_fwd(fl, bf, name):
    b, s, _ = fl.shape

    tb = _scan_tile(s)

    def body(f_ref, b_ref, c_ref):
        dst = lax.broadcasted_iota(jnp.int32, (tb, tb), 0)
        src = lax.broadcasted_iota(jnp.int32, (tb, tb), 1)
        lower = (src <= dst).astype(BF16)
        carry = jnp.zeros((1, LANES), F32)
        for blk_i in range(s // tb):
            rows = slice(blk_i * tb, (blk_i + 1) * tb)
            xv = f_ref[0, rows, :] + b_ref[...]
            lf = jnp.minimum(xv, 0.0) - jnp.log(1.0 + jnp.exp(-jnp.abs(xv)))
            acc = carry
            for part in _split3(lf):
                acc = acc + _dot(lower, part, NN)
            c_ref[0, rows, :] = acc
            carry = acc[tb - 1:tb, :]

    blk = pl.BlockSpec((1, s, LANES), lambda i: (i, 0, 0))
    return _call(body, name=name, grid=(b,),
                 in_specs=[blk, pl.BlockSpec((1, LANES), lambda i: (0, 0))],
                 out_specs=blk, out_shape=jax.ShapeDtypeStruct((b, s, LANES), F32),
                 semantics=("parallel",))(fl, bf)


def _gates_bwd(dc, fl, bf, heads, name):
    b, s, _ = fl.shape

    tb = _scan_tile(s)

    def body(d_ref, f_ref, b_ref, o_ref, db_ref):
        dst = lax.broadcasted_iota(jnp.int32, (tb, tb), 0)
        src = lax.broadcasted_iota(jnp.int32, (tb, tb), 1)
        later = (src >= dst).astype(BF16)
        lane = lax.broadcasted_iota(jnp.int32, (tb, LANES), 1)
        carry = jnp.zeros((1, LANES), F32)
        db = jnp.zeros((1, LANES), F32)
        for blk_i in reversed(range(s // tb)):
            rows = slice(blk_i * tb, (blk_i + 1) * tb)
            dlf = carry
            for part in _split3(d_ref[0, rows, :]):
                dlf = dlf + _dot(later, part, NN)
            carry = dlf[0:1, :]
            xv = f_ref[0, rows, :] + b_ref[...]
            dfl = jnp.where(lane < heads, dlf * _sigmoid(-xv), 0.0)
            o_ref[0, rows, :] = dfl.astype(BF16)
            db = db + jnp.sum(dfl, axis=0, keepdims=True)
        db_ref[0] = db

    blk = pl.BlockSpec((1, s, LANES), lambda i: (i, 0, 0))
    return _call(body, name=name, grid=(b,),
                 in_specs=[blk, blk, pl.BlockSpec((1, LANES), lambda i: (0, 0))],
                 out_specs=[blk, pl.BlockSpec((1, 1, LANES), lambda i: (i, 0, 0))],
                 out_shape=[jax.ShapeDtypeStruct((b, s, LANES), BF16),
                            jax.ShapeDtypeStruct((b, 1, LANES), F32)],
                 semantics=("parallel",))(dc, fl, bf)


LANE_CQ = 64
LANE_CK = 67
LANE_LSE = 70
LANE_D = 64
N_PARTS = 3


def _attn_tiles(s):
    return min(512, s), min(256, s)


def _lanes_in(lane, first):
    return (lane >= first) & (lane < first + N_PARTS)


def _attn_prep_fwd(pa, c, name):
    b, s, a4 = pa.shape
    pairs = a4 // (4 * LANES)
    scale = 1.0 / math.sqrt(HEAD_DIM)

    def body(q_ref, k_ref, v_ref, c_ref, qa_ref, ka_ref, kat_ref, va_ref, vt_ref):
        hp = pl.program_id(1)
        cv = c_ref[0]
        vv = v_ref[0]
        lane = lax.broadcasted_iota(jnp.int32, (s, LANES), 1)
        r128 = lax.broadcasted_iota(jnp.int32, (LANES, LANES), 0)
        c128 = lax.broadcasted_iota(jnp.int32, (LANES, LANES), 1)
        ident = (r128 == c128).astype(BF16)
        for j in range(2):
            head = 2 * hp + j
            move128 = (r128 == c128 + HEAD_DIM * j) & (c128 < HEAD_DIM)
            cparts = _split3(jnp.sum(jnp.where(lane == head, cv, 0.0), axis=1, keepdims=True))
            qa = _dot(q_ref[0], jnp.where(move128, scale, 0.0).astype(BF16), NN)
            ka = _dot(k_ref[0], move128.astype(BF16), NN)
            for i in range(N_PARTS):
                qa = jnp.where(lane == LANE_CQ + i, cparts[i].astype(F32), qa)
                ka = jnp.where(lane == LANE_CK + i, -cparts[i].astype(F32), ka)
            qa = jnp.where(_lanes_in(lane, LANE_CK), 1.0, qa)
            ka = jnp.where(_lanes_in(lane, LANE_CQ) | _lanes_in(lane, LANE_LSE), 1.0, ka)
            va = _dot(vv, move128.astype(BF16), NN) + jnp.where(_lanes_in(lane, LANE_D), 1.0, 0.0)
            kab = ka.astype(BF16)
            qa_ref[0, 0, j] = qa.astype(BF16)
            ka_ref[0, 0, j] = kab
            kat_ref[0, 0, j] = _dot(ident, kab, NT).astype(BF16)
            va_ref[0, 0, j] = va.astype(BF16)
        vt_ref[0, 0] = _dot(ident, vv, NT).astype(BF16)

    col_blk = lambda cidx: pl.BlockSpec((1, s, LANES), lambda bi, hp: (bi, 0, cidx * pairs + hp))
    tok = pl.BlockSpec((1, 1, 2, s, LANES), lambda bi, hp: (bi, hp, 0, 0, 0))
    tok_t = pl.BlockSpec((1, 1, 2, LANES, s), lambda bi, hp: (bi, hp, 0, 0, 0))
    tok_shape = jax.ShapeDtypeStruct((b, pairs, 2, s, LANES), BF16)
    return _call(
        body, name=name, grid=(b, pairs),
        in_specs=[col_blk(0), col_blk(1), col_blk(2),
                  pl.BlockSpec((1, s, LANES), lambda bi, hp: (bi, 0, 0))],
        out_specs=[tok, tok, tok_t, tok,
                   pl.BlockSpec((1, 1, LANES, s), lambda bi, hp: (bi, hp, 0, 0))],
        out_shape=[tok_shape, tok_shape, jax.ShapeDtypeStruct((b, pairs, 2, LANES, s), BF16),
                   tok_shape, jax.ShapeDtypeStruct((b, pairs, LANES, s), BF16)],
        semantics=("parallel", "parallel"))(pa, pa, pa, c)


def _attn_fwd(qa, ka, vt, pa, name):
    b, pairs, _, s, _ = qa.shape
    a = pairs * LANES
    tq = min(1024, s)
    nq = s // tq

    def body(q_ref, k_ref, vt_ref, z_ref, o_ref, g_ref, lse_ref):
        key_i = lax.broadcasted_iota(jnp.int32, (tq, tq), 0)
        qry_i = lax.broadcasted_iota(jnp.int32, (tq, tq), 1)

        def query_block(c):
            past = tq * c
            heads_out = []
            for j in range(2):
                qv = q_ref[0, 0, j]
                vrows = slice(HEAD_DIM * j, HEAD_DIM * (j + 1))
                sd = _dot(k_ref[0, 0, j, past:past + tq, :], qv, NT)
                sd = jnp.where(key_i <= qry_i, sd, NEG_INF)
                m = jnp.max(sd, axis=0, keepdims=True)
                if c > 0:
                    sp = _dot(k_ref[0, 0, j, 0:past, :], qv, NT)
                    m = jnp.maximum(m, jnp.max(sp, axis=0, keepdims=True))
                pd = jnp.exp(sd - m)
                l = jnp.sum(pd, axis=0, keepdims=True)
                acc = _dot(vt_ref[0, 0, vrows, past:past + tq], pd.astype(BF16), NN)
                if c > 0:
                    pp = jnp.exp(sp - m)
                    l = l + jnp.sum(pp, axis=0, keepdims=True)
                    acc = acc + _dot(vt_ref[0, 0, vrows, 0:past], pp.astype(BF16), NN)
                heads_out.append(acc / l)
                lse_ref[0, 0, j:j + 1, :] = m + jnp.log(l)
            ov = jnp.transpose(jnp.concatenate(heads_out, axis=0))
            o_ref[0] = ov.astype(BF16)
            zv = z_ref[0].astype(F32)
            g_ref[0] = (ov * zv * _sigmoid(zv)).astype(BF16)

        for c in range(nq):
            pl.when(pl.program_id(2) == c)(functools.partial(query_block, c))

    return _call(
        body, name=name, grid=(b, pairs, s // tq),
        in_specs=[pl.BlockSpec((1, 1, 2, tq, LANES), lambda bi, hp, qi: (bi, hp, 0, qi, 0)),
                  pl.BlockSpec((1, 1, 2, s, LANES), lambda bi, hp, qi: (bi, hp, 0, 0, 0)),
                  pl.BlockSpec((1, 1, LANES, s), lambda bi, hp, qi: (bi, hp, 0, 0)),
                  pl.BlockSpec((1, tq, LANES), lambda bi, hp, qi: (bi, qi, 3 * pairs + hp))],
        out_specs=[pl.BlockSpec((1, tq, LANES), lambda bi, hp, qi: (bi, qi, hp)),
                   pl.BlockSpec((1, tq, LANES), lambda bi, hp, qi: (bi, qi, hp)),
                   pl.BlockSpec((1, 1, 2, tq), lambda bi, hp, qi: (bi, hp, 0, qi))],
        out_shape=[jax.ShapeDtypeStruct((b, s, a), BF16), jax.ShapeDtypeStruct((b, s, a), BF16),
                   jax.ShapeDtypeStruct((b, pairs, 2, s), F32)],
        semantics=("parallel", "parallel", "arbitrary"))(qa, ka, vt, pa)


def _attn_prep_bwd(dcat, pa, o, lse, qa, name):
    b, pairs, _, s, _ = qa.shape
    a = pairs * LANES
    sub = 16

    def body(da_ref, z_ref, o_ref, lse_ref, qa_ref, qab_ref, doa_ref, dz_ref):
        zv = z_ref[0].astype(F32)
        dav = da_ref[0].astype(F32)
        ov = o_ref[0].astype(F32)
        sg = _sigmoid(zv)
        dov = dav * zv * sg
        dz_ref[0] = (dav * ov * sg * (1.0 + zv * (1.0 - sg))).astype(BF16)
        prod = dov * ov
        dob = dov.astype(BF16)
        lane = lax.broadcasted_iota(jnp.int32, (s, LANES), 1)
        r128 = lax.broadcasted_iota(jnp.int32, (LANES, LANES), 0)
        c128 = lax.broadcasted_iota(jnp.int32, (LANES, LANES), 1)
        prow = lax.broadcasted_iota(jnp.int32, (sub, s), 0)
        srow = lax.broadcasted_iota(jnp.int32, (sub, LANES), 0)
        scol = lax.broadcasted_iota(jnp.int32, (sub, LANES), 1)
        place = ((scol == srow + LANE_LSE) & (srow < N_PARTS)).astype(BF16)
        for j in range(2):
            in_head = (lane >= HEAD_DIM * j) & (lane < HEAD_DIM * (j + 1))
            dparts = _split3(jnp.sum(jnp.where(in_head, prod, 0.0), axis=1, keepdims=True))
            move128 = ((r128 == c128 + HEAD_DIM * j) & (c128 < HEAD_DIM)).astype(BF16)
            doa = _dot(dob, move128, NN)
            for i in range(N_PARTS):
                doa = jnp.where(lane == LANE_D + i, -dparts[i].astype(F32), doa)
            doa_ref[0, 0, j] = doa.astype(BF16)
            lparts = _split3(lse_ref[0, 0, j:j + 1, :])
            pmat = jnp.zeros((sub, s), BF16)
            for i in range(N_PARTS):
                pmat = jnp.where(prow == i, lparts[i], pmat)
            lcol = _dot(pmat, place, TN)
            qab_ref[0, 0, j] = (qa_ref[0, 0, j].astype(F32) - lcol).astype(BF16)

    tok = pl.BlockSpec((1, 1, 2, s, LANES), lambda bi, hp: (bi, hp, 0, 0, 0))
    tok_shape = jax.ShapeDtypeStruct((b, pairs, 2, s, LANES), BF16)
    pair_blk = pl.BlockSpec((1, s, LANES), lambda bi, hp: (bi, 0, hp))
    return _call(
        body, name=name, grid=(b, pairs),
        in_specs=[pair_blk,
                  pl.BlockSpec((1, s, LANES), lambda bi, hp: (bi, 0, 3 * pairs + hp)),
                  pair_blk,
                  pl.BlockSpec((1, 1, 2, s), lambda bi, hp: (bi, hp, 0, 0)),
                  tok],
        out_specs=[tok, tok, pair_blk],
        out_shape=[tok_shape, tok_shape, jax.ShapeDtypeStruct((b, s, a), BF16)],
        semantics=("parallel", "parallel"))(dcat, pa, o, lse, qa)


def _attn_bwd(ka, kat, va, qab, doa, name):
    b, pairs, _, s, _ = ka.shape
    a = pairs * LANES
    tq, tk = _attn_tiles(s)
    ratio = tq // tk
    nq, nk = s // tq, s // tk
    scale = 1.0 / math.sqrt(HEAD_DIM)

    def body(k_ref, kt_ref, v_ref, q_ref, do_ref, dq_ref, dk_ref, dv_ref, dc_ref,
             dqt_acc, dk_s, dv_s):
        key_i = lax.broadcasted_iota(jnp.int32, (tk, tq), 0)
        qry_i = lax.broadcasted_iota(jnp.int32, (tk, tq), 1)
        lane = lax.broadcasted_iota(jnp.int32, (tq, LANES), 1)
        low = lane < HEAD_DIM

        def key_block(kj):
            krows = slice(kj * tk, (kj + 1) * tk)
            q0 = (kj // ratio) * tq
            spans = [(slice(q0, q0 + tq), kj * tk - q0)]
            if q0 + tq < s:
                spans.append((slice(q0 + tq, s), None))
            for j in range(2):
                kb = k_ref[0, 0, j, krows, :]
                vb = v_ref[0, 0, j, krows, :]
                ktb = kt_ref[0, 0, j, :, krows]
                dk = dv = None
                for qrows, diag in spans:
                    qb = q_ref[0, 0, j, qrows, :]
                    dob = do_ref[0, 0, j, qrows, :]
                    pt = jnp.exp(_dot(kb, qb, NT))
                    if diag is not None:
                        pt = jnp.where(key_i + diag <= qry_i, pt, 0.0)
                    dsb = (pt * _dot(vb, dob, NT)).astype(BF16)
                    dv_part = _dot(pt.astype(BF16), dob, NN)
                    dk_part = _dot(dsb, qb, NN)
                    dv = dv_part if dv is None else dv + dv_part
                    dk = dk_part if dk is None else dk + dk_part
                    dq_part = _dot(ktb, dsb, NN)
                    if kj == 0:
                        dqt_acc[j, :, qrows] = dq_part
                    else:
                        dqt_acc[j, :, qrows] += dq_part
                dk_s[j, krows, :] = dk
                dv_s[j, krows, :] = dv

        for kj in range(nk):
            key_block(kj)

        def finish(i, _):
            rows = pl.ds(pl.multiple_of(i * tq, tq), tq)
            dq = [jnp.transpose(dqt_acc[j, :, rows]) for j in range(2)]
            dk = [dk_s[j, rows, :] for j in range(2)]
            dv = [dv_s[j, rows, :] for j in range(2)]
            dcol = [dq[j][:, LANE_CQ:LANE_CQ + 1] - dk[j][:, LANE_CK:LANE_CK + 1] for j in range(2)]
            dq = [dq[j] * scale for j in range(2)]
            for out_ref, val in ((dq_ref, dq), (dk_ref, dk), (dv_ref, dv)):
                merged = jnp.where(low, val[0], pltpu.roll(val[1], HEAD_DIM, 1))
                out_ref[0, rows, :] = merged.astype(BF16)
            hp = pl.program_id(1)
            prev = jnp.where(hp == 0, 0.0, dc_ref[0, rows, :])
            dc_ref[0, rows, :] = jnp.where(lane == 2 * hp, dcol[0],
                                           jnp.where(lane == 2 * hp + 1, dcol[1], prev))
            return 0

        lax.fori_loop(0, nq, finish, 0)

    tok = pl.BlockSpec((1, 1, 2, s, LANES), lambda bi, hp: (bi, hp, 0, 0, 0))
    tok_t = pl.BlockSpec((1, 1, 2, LANES, s), lambda bi, hp: (bi, hp, 0, 0, 0))
    pair_blk = pl.BlockSpec((1, s, LANES), lambda bi, hp: (bi, 0, hp))
    pair_shape = jax.ShapeDtypeStruct((b, s, a), BF16)
    return _call(
        body, name=name, grid=(b, pairs),
        in_specs=[tok, tok_t, tok, tok, tok],
        out_specs=[pair_blk, pair_blk, pair_blk,
                   pl.BlockSpec((1, s, LANES), lambda bi, hp: (bi, 0, 0))],
        out_shape=[pair_shape, pair_shape, pair_shape,
                   jax.ShapeDtypeStruct((b, s, LANES), F32)],
        scratch_shapes=[pltpu.VMEM((2, LANES, s), F32), pltpu.VMEM((2, s, LANES), F32),
                        pltpu.VMEM((2, s, LANES), F32)],
        semantics=("parallel", "arbitrary"))(ka, kat, va, qab, doa)


def _pool_tile(s):
    return min(256, s)


def _band(tb, window, shift):
    tgt = lax.broadcasted_iota(jnp.int32, (tb, tb), 0)
    src = lax.broadcasted_iota(jnp.int32, (tb, tb), 1) + shift
    return ((src <= tgt) & (src > tgt - window)).astype(BF16)


def _band_t(tb, window, shift):
    src = lax.broadcasted_iota(jnp.int32, (tb, tb), 0)
    tgt = lax.broadcasted_iota(jnp.int32, (tb, tb), 1) + shift
    return ((src <= tgt) & (src > tgt - window)).astype(BF16)


def _pool_fwd(pp, w_pool, scale, name):
    b, s, pw2 = pp.shape
    pw = pw2 // 2
    pg = pw // N_POOL_GROUPS
    tb = _pool_tile(s)
    nb = s // tb

    def body(u_ref, z_ref, w_ref, s_ref, o_ref):
        window = 2 << pl.program_id(1)
        band0 = _band(tb, window, 0)
        band1 = _band(tb, window, -tb)
        pos = lax.broadcasted_iota(jnp.int32, (tb, pg), 0)

        def block(i, _):
            rows = pl.ds(pl.multiple_of(i * tb, tb), tb)
            prev = pl.ds(pl.multiple_of(jnp.maximum(i - 1, 0) * tb, tb), tb)
            ub = u_ref[0, rows, :]
            up = u_ref[0, prev, :]
            up = jnp.where(i > 0, up, jnp.zeros_like(up))
            count = jnp.minimum(pos + i * tb + 1, window).astype(F32)
            pooled = (_dot(band0, ub, NN) + _dot(band1, up, NN)) / count - ub.astype(F32)
            mixed = _dot(pooled.astype(BF16), w_ref[0], NN) * s_ref[...]
            zv = z_ref[0, rows, :].astype(F32)
            o_ref[0, rows, :] = (mixed * zv * _sigmoid(zv)).astype(BF16)
            return 0

        lax.fori_loop(0, nb, block, 0)

    return _call(
        body, name=name, grid=(b, N_POOL_GROUPS),
        in_specs=[pl.BlockSpec((1, s, pg), lambda bi, g: (bi, 0, g)),
                  pl.BlockSpec((1, s, pg), lambda bi, g: (bi, 0, N_POOL_GROUPS + g)),
                  pl.BlockSpec((1, pg, pg), lambda bi, g: (g, 0, 0)),
                  pl.BlockSpec((1, pg), lambda bi, g: (0, g))],
        out_specs=pl.BlockSpec((1, s, pg), lambda bi, g: (bi, 0, g)),
        out_shape=jax.ShapeDtypeStruct((b, s, pw), BF16),
        semantics=("parallel", "parallel"))(pp, pp, w_pool, scale)


def _pool_bwd(pp, dcat, w_pool, scale, first_block, name):
    b, s, pw2 = pp.shape
    pw = pw2 // 2
    pg = pw // N_POOL_GROUPS
    tb = _pool_tile(s)
    nb = s // tb

    def body(u_ref, z_ref, d_ref, w_ref, s_ref, du_ref, dz_ref, dw_ref, ds_ref, dpool_s):
        @pl.when(pl.program_id(1) == 0)
        def _():
            dw_ref[...] = jnp.zeros_like(dw_ref)
            ds_ref[...] = jnp.zeros_like(ds_ref)

        window = 2 << pl.program_id(0)
        band0 = _band(tb, window, 0)
        band1 = _band(tb, window, -tb)
        band0_t = _band_t(tb, window, 0)
        band1_t = _band_t(tb, window, tb)
        pos = lax.broadcasted_iota(jnp.int32, (tb, pg), 0)

        def first(i, _):
            rows = pl.ds(pl.multiple_of(i * tb, tb), tb)
            prev = pl.ds(pl.multiple_of(jnp.maximum(i - 1, 0) * tb, tb), tb)
            ub = u_ref[0, rows, :]
            up = u_ref[0, prev, :]
            up = jnp.where(i > 0, up, jnp.zeros_like(up))
            count = jnp.minimum(pos + i * tb + 1, window).astype(F32)
            pooled = ((_dot(band0, ub, NN) + _dot(band1, up, NN)) / count
                      - ub.astype(F32)).astype(BF16)
            mixed = _dot(pooled, w_ref[0], NN)
            pm = mixed * s_ref[...]
            zv = z_ref[0, rows, :].astype(F32)
            sg = _sigmoid(zv)
            dpl = d_ref[0, rows, :].astype(F32)
            dpm = dpl * zv * sg
            dz_ref[0, rows, :] = (dpl * pm * sg * (1.0 + zv * (1.0 - sg))).astype(BF16)
            ds_ref[...] += jnp.sum(dpm * mixed, axis=0, keepdims=True)
            dmixed = (dpm * s_ref[...]).astype(BF16)
            dw_ref[0] += _dot(pooled, dmixed, TN)
            dpool_s[rows, :] = _dot(dmixed, w_ref[0], NT)
            return 0

        lax.fori_loop(0, nb, first, 0)

        def second(i, _):
            rows = pl.ds(pl.multiple_of(i * tb, tb), tb)
            nxt_i = jnp.minimum(i + 1, nb - 1)
            nxt = pl.ds(pl.multiple_of(nxt_i * tb, tb), tb)
            count = jnp.minimum(pos + i * tb + 1, window).astype(F32)
            count_n = jnp.minimum(pos + nxt_i * tb + 1, window).astype(F32)
            dpb = dpool_s[rows, :]
            cur = (dpb / count).astype(BF16)
            nx = dpool_s[nxt, :] / count_n
            nx = jnp.where(i < nb - 1, nx, 0.0).astype(BF16)
            du = _dot(band0_t, cur, NN) + _dot(band1_t, nx, NN) - dpb
            du_ref[0, rows, :] = du.astype(BF16)
            return 0

        lax.fori_loop(0, nb, second, 0)

    return _call(
        body, name=name, grid=(N_POOL_GROUPS, b),
        in_specs=[pl.BlockSpec((1, s, pg), lambda g, bi: (bi, 0, g)),
                  pl.BlockSpec((1, s, pg), lambda g, bi: (bi, 0, N_POOL_GROUPS + g)),
                  pl.BlockSpec((1, s, pg), lambda g, bi: (bi, 0, first_block + g)),
                  pl.BlockSpec((1, pg, pg), lambda g, bi: (g, 0, 0)),
                  pl.BlockSpec((1, pg), lambda g, bi: (0, g))],
        out_specs=[pl.BlockSpec((1, s, pg), lambda g, bi: (bi, 0, g)),
                   pl.BlockSpec((1, s, pg), lambda g, bi: (bi, 0, g)),
                   pl.BlockSpec((1, pg, pg), lambda g, bi: (g, 0, 0)),
                   pl.BlockSpec((1, pg), lambda g, bi: (0, g))],
        out_shape=[jax.ShapeDtypeStruct((b, s, pw), BF16), jax.ShapeDtypeStruct((b, s, pw), BF16),
                   jax.ShapeDtypeStruct((N_POOL_GROUPS, pg, pg), F32),
                   jax.ShapeDtypeStruct((1, pw), F32)],
        scratch_shapes=[pltpu.VMEM((s, pg), F32)],
        semantics=("parallel", "arbitrary"))(pp, pp, dcat, w_pool, scale)


def _adamw(recvs, sent, me, w, m, v, name):
    depth = len(recvs)
    r, c = w.shape[1:]
    tr = min(128, r)
    nb = r // tr
    c1 = 1.0 - ADAM_B1 ** ADAM_STEP
    c2 = 1.0 - ADAM_B2 ** ADAM_STEP
    slotted = sent[0].ndim == 3

    def body(me_ref, *refs):
        recv_refs, own_refs = refs[:depth], refs[depth:2 * depth]
        w_ref, m_ref, v_ref, g_ref, d_ref, nm_ref, nv_ref = refs[2 * depth:]
        me = me_ref[0]
        for layer in range(depth):
            @pl.when(pl.program_id(0) == layer)
            def _(layer=layer):
                own = (own_refs[layer][0] if slotted else own_refs[layer][...]).astype(F32)
                g = jnp.where(me == 0, own, recv_refs[layer][0].astype(F32))
                for sl in range(1, N_DEV):
                    g = g + jnp.where(me == sl, own, recv_refs[layer][sl].astype(F32))
                mn = ADAM_B1 * m_ref[0] + (1.0 - ADAM_B1) * g
                vn = ADAM_B2 * v_ref[0] + (1.0 - ADAM_B2) * (g * g)
                m_hat = mn / c1
                v_hat = vn / c2
                g_ref[0] = g
                d_ref[0] = -ADAM_LR * (m_hat / (jnp.sqrt(v_hat) + ADAM_EPS) + ADAM_WD * w_ref[0])
                nm_ref[0] = mn
                nv_ref[0] = vn

    def blk(layer):
        return lambda l, i: jnp.clip(i + (l - layer) * nb, 0, nb - 1)

    in_specs = [pl.BlockSpec((N_DEV, tr, c), lambda l, i, me_ref, f=blk(layer): (0, f(l, i), 0))
                for layer in range(depth)]
    if slotted:
        in_specs += [pl.BlockSpec((1, tr, c),
                                  lambda l, i, me_ref, f=blk(layer): (me_ref[0], f(l, i), 0))
                     for layer in range(depth)]
    else:
        in_specs += [pl.BlockSpec((tr, c), lambda l, i, me_ref, f=blk(layer): (f(l, i), 0))
                     for layer in range(depth)]
    row = pl.BlockSpec((1, tr, c), lambda l, i, me_ref: (l, i, 0))
    return pl.pallas_call(
        body, name=name, out_shape=[jax.ShapeDtypeStruct((depth, r, c), F32)] * 4,
        grid_spec=pltpu.PrefetchScalarGridSpec(
            num_scalar_prefetch=1, grid=(depth, nb), in_specs=in_specs + [row, row, row],
            out_specs=[row] * 4),
        compiler_params=pltpu.CompilerParams(dimension_semantics=("arbitrary", "arbitrary"),
                                             vmem_limit_bytes=VMEM_LIMIT_BYTES),
    )(me, *recvs, *sent, w, m, v)


def _pack_w_in(gathered, a, heads, pw):
    d = gathered.shape[1]
    w_full = jnp.transpose(gathered, (1, 0, 2)).reshape(d, -1)
    wf = jnp.pad(w_full[:, 4 * a:4 * a + heads], ((0, 0), (0, LANES - heads)))
    return w_full, w_full[:, 4 * a + heads:], wf


def _unpack_dw_in(parts, heads):
    dq, dk, dv, dz, dwf, du, dzp = parts
    d = dq.shape[0]
    full = jnp.concatenate([dq, dk, dv, dz, dwf[:, :heads], du, dzp], axis=1)
    return jnp.transpose(full.reshape(d, N_DEV, -1), (1, 0, 2))


def kernel(x, p, norm_pre, norm_post, w_in, b_f, w_pool, pool_scale, w_out, w_pg, w_pe, loss_target, m_norm_pre, m_norm_post, m_w_in, m_b_f, m_w_pool, m_pool_scale, m_w_out, m_w_pg, m_w_pe, v_norm_pre, v_norm_post, v_w_in, v_b_f, v_w_pool, v_pool_scale, v_w_out, v_w_pg, v_w_pe):
    depth = w_in.shape[0]
    b, s, d = x.shape
    t = b * s
    heads = b_f.shape[1]
    a = heads * HEAD_DIM
    pairs = a // LANES
    pw = pool_scale.shape[1]
    pg = pw // N_POOL_GROUPS
    ple = p.shape[-1]
    mix_w = a + pw

    me = 4 * lax.axis_index("x") + 2 * lax.axis_index("y") + lax.axis_index("c")
    shard = {
        "w_in": [w_in[i].astype(BF16) for i in range(depth)],
        "w_pool": [w_pool[i].reshape(N_POOL_GROUPS * (pg // N_DEV), pg).astype(BF16)
                   for i in range(depth)],
        "w_out": [w_out[i].astype(BF16) for i in range(depth)],
        "w_pg": [w_pg[i].astype(BF16) for i in range(depth)],
        "w_pe": [w_pe[i].astype(BF16) for i in range(depth)],
    }
    names = list(shard)
    rest = names[1:]

    def unpack_rest(lands, layer, which):
        g = {nm: _with_own(ld, shard[nm][layer], me) for nm, ld in zip(which, lands)}
        g_pool = g["w_pool"].reshape(N_DEV, N_POOL_GROUPS, pg // N_DEV, pg)
        return dict(wpool=jnp.transpose(g_pool, (1, 0, 2, 3)).reshape(N_POOL_GROUPS, pg, pg),
                    wout=g["w_out"].reshape(mix_w, d), wpg=g["w_pg"].reshape(d, d),
                    wpe=jnp.transpose(g["w_pe"], (1, 0, 2)).reshape(ple, d))

    g_in0 = _gather_two_level(shard["w_in"][0], "gather_w_in0")
    rest0, tok_rest0 = _exchange_start([(shard[nm][0], False) for nm in rest], g_in0,
                                       "gather_rest0_start")
    later, tok = [], tok_rest0
    for i in range(1, depth):
        hdl, tk_i = _exchange_start([(shard[nm][i], False) for nm in names], g_in0,
                                    "gather_layer%d_start" % i)
        later.append(hdl)
        tok = tok + tk_i

    h = x.reshape(t, d)
    saved = []
    layers = []
    for i in range(depth):
        sv = dict(h=h)
        g_pre = norm_pre[i:i + 1]
        g_post = norm_post[i:i + 1]
        bf = jnp.pad(b_f[i:i + 1], ((0, 0), (0, LANES - heads)))
        scale = pool_scale[i:i + 1]
        if i == 0:
            lw = dict(zip(("wa", "wp", "wf"), _pack_w_in(g_in0, a, heads, pw)))
            g_pre = g_pre + tok
        else:
            lands = _exchange_wait(later[i - 1], h, "gather_layer%d_wait" % i)
            g_in = _with_own(lands[0], shard["w_in"][i], me)
            lw = dict(zip(("wa", "wp", "wf"), _pack_w_in(g_in, a, heads, pw)))
            lw.update(unpack_rest(lands[1:], i, rest))
        hn = _rms_fwd(h, g_pre, "rms_pre")
        pa = _matmul([(hn, lw["wa"])], "nn", BF16, "proj_attn", n_dim=4 * a, tn=2048,
                     n_outer=True).reshape(b, s, 4 * a)
        pp = _matmul([(hn, lw["wp"])], "nn", BF16, "proj_pool", tn=2048,
                     n_outer=True).reshape(b, s, 2 * pw)
        fl = _matmul([(hn, lw["wf"])], "nn", F32, "proj_gate").reshape(b, s, LANES)
        c = _gates_fwd(fl, bf, "gates_fwd")
        qa, ka, kat, va, vt = _attn_prep_fwd(pa, c, "attn_prep_fwd")
        o, ga, lse = _attn_fwd(qa, ka, vt, pa, "attn_fwd")
        if i == 0:
            lw.update(unpack_rest(_exchange_wait(rest0, lse, "gather_rest0_wait"), 0, rest))
        layers.append(lw)
        gp = _pool_fwd(pp, lw["wpool"], scale, "pool_fwd")
        ga2 = ga.reshape(t, a)
        gp2 = gp.reshape(t, pw)
        mix = _matmul([(ga2, lw["wout"], 0, 0), (gp2, lw["wout"], a, 0)], "nn", F32, "mix_out")
        h1, h1b = _post_fwd(h, mix, g_post, "post_fwd")
        pb = p[i].reshape(t, ple).astype(BF16)
        if i < depth - 1:
            gpre, e, h = _ple_fwd(h1, h1b, lw["wpg"], pb, lw["wpe"], None, "ple_fwd")
        else:
            gpre, e, dh, sq = _ple_fwd(h1, h1b, lw["wpg"], pb, lw["wpe"],
                                       loss_target.reshape(t, d), "ple_loss")
        sv.update(hn=hn, pa=pa, pp=pp, fl=fl, bf=bf, qa=qa, ka=ka, kat=kat, va=va, o=o, lse=lse, ga=ga2,
                  gp=gp2, mix=mix,
                  h1b=h1b, pb=pb, gpre=gpre, e=e, g_pre=g_pre, g_post=g_post, scale=scale)
        saved.append(sv)

    loss = lax.psum(0.5 * jnp.sum(sq) / d, MESH_AXES)

    big = {nm: [None] * depth for nm in names}
    small = {nm: [None] * depth for nm in ("norm_pre", "norm_post", "b_f", "pool_scale")}
    grad_handles = [None] * depth
    rest_handles = [None] * depth
    for i in reversed(range(depth)):
        lw, sv = layers[i], saved[i]
        de, dpre = _ple_bwd(dh, sv["gpre"], sv["e"], "ple_bwd")
        dwpe = _matmul([(sv["pb"], de)], "tn", BF16, "dw_pe", tm=1024)
        dwpg = _matmul([(sv["h1b"], dpre)], "tn", BF16, "dw_pg", tm=1024)
        dh1, dmix, dg_post = _matmul_rows(
            [(dpre, lw["wpg"], 0)], [dh, sv["mix"]], sv["g_post"], _post_bwd_epilogue,
            (F32, BF16), "d_h1_post_bwd", tm=512)
        dwout = jnp.concatenate(
            [_matmul([(sv["ga"], dmix)], "tn", BF16, "dw_out_attn", tm=1024),
             _matmul([(sv["gp"], dmix)], "tn", BF16, "dw_out_pool", tm=1024)], axis=0)
        dcat = _matmul([(dmix, lw["wout"])], "nt", BF16, "d_cat", tn=2048).reshape(b, s, mix_w)
        du, dzp, dwpool, dscale = _pool_bwd(sv["pp"], dcat, lw["wpool"], sv["scale"], a // pg,
                                            "pool_bwd")
        big["w_pool"][i] = jnp.transpose(
            dwpool.astype(BF16).reshape(N_POOL_GROUPS, N_DEV, pg // N_DEV, pg), (1, 0, 2, 3)
        ).reshape(N_DEV, N_POOL_GROUPS * (pg // N_DEV), pg)
        big["w_out"][i] = dwout.reshape(N_DEV, mix_w // N_DEV, d)
        big["w_pg"][i] = dwpg.reshape(N_DEV, d // N_DEV, d)
        big["w_pe"][i] = jnp.transpose(dwpe.reshape(ple, N_DEV, d // N_DEV), (1, 0, 2))
        rest_handles[i], tok = _exchange_start([(big[nm][i], True) for nm in rest], du,
                                               "grads_rest%d_start" % i)
        qab, doa, dz = _attn_prep_bwd(dcat, sv["pa"], sv["o"], sv["lse"] + tok, sv["qa"],
                                      "attn_prep_bwd")
        dq, dk, dv, dc = _attn_bwd(sv["ka"], sv["kat"], sv["va"], qab, doa, "attn_bwd")
        dfl, dbf = _gates_bwd(dc, sv["fl"], sv["bf"], heads, "gates_bwd")
        dproj = [g_.reshape(t, -1) for g_ in (dq, dk, dv, dz, dfl, du, dzp)]
        dw_parts = [_matmul([(sv["hn"], g_)], "tn", BF16, "dw_in_%d" % n_, tm=1024)
                    for n_, g_ in enumerate(dproj)]

        big["w_in"][i] = _unpack_dw_in(dw_parts, heads)
        grad_handles[i], tok = _exchange_start([(big["w_in"][i], True)], dw_parts[-1],
                                               "grads_w_in%d_start" % i)

        dq2, dk2, dv2, dz2, dfl2, du2, dzp2 = dproj
        dh, dg_pre = _matmul_rows(
            [(dq2, lw["wa"], 0), (dk2, lw["wa"], a), (dv2, lw["wa"], 2 * a), (dz2, lw["wa"], 3 * a),
             (du2, lw["wp"], 0), (dzp2, lw["wp"], pw), (dfl2, lw["wf"] + tok.astype(BF16), 0)],
            [sv["h"], dh1], sv["g_pre"] + tok, _pre_bwd_epilogue, (F32,), "d_hn_pre_bwd", tm=256)
        small["norm_pre"][i] = dg_pre
        small["norm_post"][i] = dg_post
        small["b_f"][i] = jnp.sum(dbf, axis=0)
        small["pool_scale"][i] = dscale
    grad_x = dh.reshape(b, s, d)

    width = max(d, pw)
    small_names = ("norm_pre", "norm_post", "pool_scale", "b_f")

    def small_rows(get):
        rows = []
        for nm in small_names:
            for i in range(depth):
                v_ = get(nm, i)
                rows.append(jnp.pad(v_, ((0, 0), (0, width - v_.shape[1]))))
        return jnp.concatenate(rows, axis=0)

    small_g = small_rows(lambda nm, i: small[nm][i])
    me1 = jnp.reshape(me, (1,)).astype(jnp.int32)

    weights = dict(norm_pre=norm_pre, norm_post=norm_post, w_in=w_in, b_f=b_f, w_pool=w_pool,
                   pool_scale=pool_scale, w_out=w_out, w_pg=w_pg, w_pe=w_pe)
    mom1 = dict(norm_pre=m_norm_pre, norm_post=m_norm_post, w_in=m_w_in, b_f=m_b_f, w_pool=m_w_pool,
                pool_scale=m_pool_scale, w_out=m_w_out, w_pg=m_w_pg, w_pe=m_w_pe)
    mom2 = dict(norm_pre=v_norm_pre, norm_post=v_norm_post, w_in=v_w_in, b_f=v_b_f, w_pool=v_w_pool,
                pool_scale=v_pool_scale, w_out=v_w_out, w_pg=v_w_pg, w_pe=v_w_pe)

    results = {}

    def update(nm, recvs):
        shp = weights[nm].shape
        sent = [big[nm][i] for i in range(depth)]
        flat = lambda arr: arr.reshape((depth,) + sent[0].shape[1:])
        outs = _adamw(recvs, sent, me1, flat(weights[nm]), flat(mom1[nm]), flat(mom2[nm]),
                      "adamw_" + nm)
        results[nm] = [o_.reshape(shp) for o_ in outs]
        return outs[0]

    got_rest = [_exchange_wait(rest_handles[i], dh, "grads_rest%d_wait" % i) for i in range(depth)]
    for j, nm in enumerate(rest):
        last = update(nm, [got_rest[i][j] for i in range(depth)])

    small_g, last = lax.optimization_barrier((small_g, last))
    (small_recv,) = _exchange([(small_g, False)], "exchange_small")
    small_w = small_rows(lambda nm, i: weights[nm][i:i + 1])[None]
    small_m = small_rows(lambda nm, i: mom1[nm][i:i + 1])[None]
    small_v = small_rows(lambda nm, i: mom2[nm][i:i + 1])[None]
    outs = _adamw([small_recv], [small_g], me1, small_w, small_m, small_v, "adamw_small")
    for j, nm in enumerate(small_names):
        cols = weights[nm].shape[1]
        results[nm] = [o_[0, j * depth:(j + 1) * depth, :cols] for o_ in outs]

    got_w_in = [_exchange_wait(grad_handles[i], last + outs[0][0, 0, 0], "grads_w_in%d_wait" % i)[0]
                for i in range(depth)]
    update("w_in", got_w_in)

    order = ("norm_pre", "norm_post", "w_in", "b_f", "w_pool", "pool_scale", "w_out", "w_pg", "w_pe")
    return (loss, grad_x, *[results[nm][0] for nm in order], *[results[nm][1] for nm in order],
            *[results[nm][2] for nm in order], *[results[nm][3] for nm in order])
```

```python
import functools
import math

import jax
import jax.numpy as jnp
from jax import lax
from jax.experimental import pallas as pl
from jax.experimental.pallas import tpu as pltpu

N_DEV = 8
MESH_AXES = ("x", "y", "c")
HEAD_DIM = 64
LANES = 128
N_POOL_GROUPS = 4
EPS = 1e-6
ADAM_LR = 0.001
ADAM_B1 = 0.9
ADAM_B2 = 0.999
ADAM_EPS = 1e-08
ADAM_WD = 0.01
ADAM_STEP = 10
VMEM_LIMIT_BYTES = 56 * 1024 * 1024
F32 = jnp.float32
BF16 = jnp.bfloat16
NEG_INF = float("-inf")


def _call(body, *, name, grid, in_specs, out_specs, out_shape, scratch_shapes=(), semantics=None):
    return pl.pallas_call(
        body, name=name, grid=grid, in_specs=in_specs, out_specs=out_specs, out_shape=out_shape,
        scratch_shapes=list(scratch_shapes),
        compiler_params=pltpu.CompilerParams(dimension_semantics=semantics,
                                             vmem_limit_bytes=VMEM_LIMIT_BYTES))


def _sigmoid(z):
    return 1.0 / (1.0 + jnp.exp(-z))


def _dot(a, b, dims):
    return lax.dot_general(a, b, (dims, ((), ())), preferred_element_type=F32)


NN = ((1,), (0,))
NT = ((1,), (1,))
TN = ((0,), (0,))


def _exchange(items, name):
    n = len(items)
    modes = [s for _, s in items]
    out_shapes = []
    for a, s in items:
        shp = a.shape[1:] if s else a.shape
        out_shapes.append(jax.ShapeDtypeStruct((N_DEV,) + tuple(shp), a.dtype))

    def body(*refs):
        ins = refs[:n]
        outs = refs[n:2 * n]
        send_sems, recv_sems, local_sems = refs[2 * n:]
        x, y, c = (lax.axis_index(ax) for ax in MESH_AXES)
        me = 4 * x + 2 * y + c
        started = []
        for i in range(n):
            mine = ins[i].at[me] if modes[i] else ins[i]
            loc = pltpu.make_async_copy(mine, outs[i].at[me], local_sems.at[i])
            loc.start()
            started.append(loc)
        remote = []
        for k in range(1, N_DEV):
            px = x ^ ((k >> 2) & 1)
            py = y ^ ((k >> 1) & 1)
            pc = c ^ (k & 1)
            peer = me ^ k
            for i in range(n):
                src = ins[i].at[peer] if modes[i] else ins[i]
                cp = pltpu.make_async_remote_copy(
                    src_ref=src, dst_ref=outs[i].at[me],
                    send_sem=send_sems.at[i, k - 1], recv_sem=recv_sems.at[i, k - 1],
                    device_id=(px, py, pc), device_id_type=pl.DeviceIdType.MESH)
                cp.start()
                remote.append(cp)
        for cp in remote:
            cp.wait()
        for loc in started:
            loc.wait()

    hbm = pl.BlockSpec(memory_space=pltpu.HBM)
    return pl.pallas_call(
        body, name=name, out_shape=out_shapes,
        in_specs=[hbm] * n, out_specs=[hbm] * n,
        scratch_shapes=[pltpu.SemaphoreType.DMA((n, N_DEV - 1)),
                        pltpu.SemaphoreType.DMA((n, N_DEV - 1)),
                        pltpu.SemaphoreType.DMA((n,))],
    )(*[a for a, _ in items])


def _gather_two_level(shard, name):
    def body(x_ref, out_ref, send_sems, recv_sems, local_sem):
        x, y, c = (lax.axis_index(ax) for ax in MESH_AXES)
        sibling = (x, y, 1 - c)
        chips = [(1 - x, y), (x, 1 - y), (1 - x, 1 - y)]

        def slot(px, py, pc):
            return out_ref.at[4 * px + 2 * py + pc]

        def copy(k, block, to, src=None):
            return pltpu.make_async_remote_copy(
                src_ref=slot(*block) if src is None else src, dst_ref=slot(*block),
                send_sem=send_sems.at[k], recv_sem=recv_sems.at[k],
                device_id=to, device_id_type=pl.DeviceIdType.MESH)

        mine = pltpu.make_async_copy(x_ref, slot(x, y, c), local_sem)
        mine.start()
        first = [copy(0, (x, y, c), sibling, src=x_ref)]
        first += [copy(1 + j, (x, y, c), (*chip, c), src=x_ref) for j, chip in enumerate(chips)]
        for cp in first:
            cp.start()
        passed = [copy(4 + j, (*chip, c), sibling) for j, chip in enumerate(chips)]
        for j, chip in enumerate(chips):
            copy(1 + j, (*chip, c), (x, y, c)).wait_recv()
            passed[j].start()
        copy(0, (x, y, 1 - c), (x, y, c)).wait_recv()
        for j, chip in enumerate(chips):
            copy(4 + j, (*chip, 1 - c), (x, y, c)).wait_recv()
        for cp in first + passed:
            cp.wait_send()
        mine.wait()

    hbm = pl.BlockSpec(memory_space=pltpu.HBM)
    return pl.pallas_call(
        body, name=name, out_shape=jax.ShapeDtypeStruct((N_DEV,) + shard.shape, shard.dtype),
        in_specs=[hbm], out_specs=hbm,
        scratch_shapes=[pltpu.SemaphoreType.DMA((N_DEV - 1,)), pltpu.SemaphoreType.DMA((N_DEV - 1,)),
                        pltpu.SemaphoreType.DMA],
    )(shard)


def _peer_copies(srcs, lands, modes, send_sems, recv_sems):
    x, y, c = (lax.axis_index(ax) for ax in MESH_AXES)
    me = 4 * x + 2 * y + c
    copies = []
    for k in range(1, N_DEV):
        peer_id = (x ^ ((k >> 2) & 1), y ^ ((k >> 1) & 1), c ^ (k & 1))
        for i, scatter in enumerate(modes):
            src = srcs[i].at[me ^ k] if scatter else srcs[i]
            pair = i * (N_DEV - 1) + k - 1
            copies.append(pltpu.make_async_remote_copy(
                src_ref=src, dst_ref=lands[i].at[me],
                send_sem=send_sems.at[pair], recv_sem=recv_sems.at[pair],
                device_id=peer_id, device_id_type=pl.DeviceIdType.MESH))
    return copies


def _exchange_start(items, after, name):
    n = len(items)
    modes = [s for _, s in items]
    srcs = [pltpu.with_memory_space_constraint(a, pltpu.HBM) for a, _ in items]
    lands = []
    for a, s in items:
        shp = (N_DEV,) + tuple(a.shape[1:] if s else a.shape)
        lands.append(pltpu.with_memory_space_constraint(lax.empty(shp, a.dtype), pltpu.HBM))

    def body(*refs):
        send_sems, recv_sems = refs[2 * n + 1], refs[2 * n + 2]
        token = refs[-1]
        for cp in _peer_copies(refs[:n], refs[n:2 * n], modes, send_sems, recv_sems):
            cp.start()
        token[...] = jnp.zeros_like(token)

    hbm = pl.BlockSpec(memory_space=pltpu.HBM)
    sem = pl.BlockSpec(memory_space=pltpu.SEMAPHORE)
    outs = pl.pallas_call(
        body, name=name,
        out_shape=(pltpu.SemaphoreType.DMA((n * (N_DEV - 1),)),
                   pltpu.SemaphoreType.DMA((n * (N_DEV - 1),)),
                   *[pltpu.HBM(a.shape, a.dtype) for a in srcs + lands],
                   jax.ShapeDtypeStruct((8, LANES), F32)),
        in_specs=[hbm] * (2 * n) + [pl.BlockSpec(memory_space=pl.ANY)],
        out_specs=(sem, sem, *([hbm] * (2 * n)), pl.BlockSpec(memory_space=pltpu.VMEM)),
        input_output_aliases={i: 2 + i for i in range(2 * n)},
        compiler_params=pltpu.CompilerParams(
            has_side_effects=pltpu.SideEffectType.DATAFLOW_SIDE_EFFECTING),
    )(*srcs, *lands, after)
    handle = (modes, outs[0], outs[1], list(outs[2:2 + n]), list(outs[2 + n:2 + 2 * n]))
    return handle, outs[-1][0, 0]


def _exchange_wait(handle, after, name):
    modes, send_sems, recv_sems, srcs, lands = handle
    n = len(modes)

    def body(*refs):
        for cp in _peer_copies(refs[:n], refs[n:2 * n], modes, refs[2 * n], refs[2 * n + 1]):
            cp.wait_send()
            cp.wait_recv()

    hbm = pl.BlockSpec(memory_space=pltpu.HBM)
    sem = pl.BlockSpec(memory_space=pltpu.SEMAPHORE)
    outs = pl.pallas_call(
        body, name=name,
        out_shape=tuple(pltpu.HBM(a.shape, a.dtype) for a in srcs + lands),
        in_specs=[hbm] * (2 * n) + [sem, sem, pl.BlockSpec(memory_space=pl.ANY)],
        out_specs=tuple([hbm] * (2 * n)),
        input_output_aliases={i: i for i in range(2 * n)},
        compiler_params=pltpu.CompilerParams(
            has_side_effects=pltpu.SideEffectType.DATAFLOW_SIDE_EFFECTING),
    )(*srcs, *lands, send_sems, recv_sems, after)
    return list(outs[n:])


def _with_own(slots, own, me):
    idx = lax.broadcasted_iota(jnp.int32, (N_DEV,) + (1,) * own.ndim, 0)
    return jnp.where(idx == me, own[None], slots)


def _matmul(pairs, mode, out_dtype, name, n_dim=None, tm=512, tn=1024, tk=1024, n_outer=False):
    dims = {"nn": NN, "nt": NT, "tn": TN}[mode]
    pairs = [tuple(pr) + (0, 0) * (len(pr) == 2) for pr in pairs]
    a0, b0 = pairs[0][:2]
    m_dim = a0.shape[1] if mode == "tn" else a0.shape[0]
    if n_dim is None:
        n_dim = b0.shape[0] if mode == "nt" else b0.shape[1]
    tm = min(tm, m_dim)
    tn = min(tn, n_dim)
    segs = []
    off = 0
    for a, _, k0, n0 in pairs:
        k_dim = a.shape[0] if mode == "tn" else a.shape[1]
        t = min(tk, k_dim)
        segs.append((off, k_dim // t, t, k0 // t, n0 // tn))
        off += k_dim // t
    nk = off
    n_pairs = len(pairs)

    def ij(g0, g1):
        return (g1, g0) if n_outer else (g0, g1)

    in_specs = []
    for (o, cnt, t, kb, nb) in segs:
        def kc(kk, o=o, cnt=cnt):
            return jnp.clip(kk - o, 0, cnt - 1)
        if mode == "tn":
            in_specs.append(pl.BlockSpec((t, tm), lambda g0, g1, kk, kc=kc: (kc(kk), ij(g0, g1)[0])))
        else:
            in_specs.append(pl.BlockSpec((tm, t), lambda g0, g1, kk, kc=kc: (ij(g0, g1)[0], kc(kk))))
        if mode == "nt":
            in_specs.append(pl.BlockSpec((tn, t), lambda g0, g1, kk, kc=kc, kb=kb, nb=nb:
                                         (nb + ij(g0, g1)[1], kb + kc(kk))))
        else:
            in_specs.append(pl.BlockSpec((t, tn), lambda g0, g1, kk, kc=kc, kb=kb, nb=nb:
                                         (kb + kc(kk), nb + ij(g0, g1)[1])))

    one_shot = all(sg[1] == 1 for sg in segs)

    def body_sum(*refs):
        total = _dot(refs[0][...], refs[1][...], dims)
        for idx in range(1, n_pairs):
            total = total + _dot(refs[2 * idx][...], refs[2 * idx + 1][...], dims)
        refs[2 * n_pairs][...] = total.astype(out_dtype)

    def body(*refs):
        out_ref = refs[2 * n_pairs]
        acc = refs[2 * n_pairs + 1]
        kk = pl.program_id(2)

        @pl.when(kk == 0)
        def _():
            acc[...] = jnp.zeros_like(acc)

        for idx, (o, cnt) in enumerate(sg[:2] for sg in segs):
            @pl.when((kk >= o) & (kk < o + cnt))
            def _(idx=idx):
                acc[...] += _dot(refs[2 * idx][...], refs[2 * idx + 1][...], dims)

        @pl.when(kk == nk - 1)
        def _():
            out_ref[...] = acc[...].astype(out_dtype)

    flat = [t for pr in pairs for t in pr[:2]]
    tiles = (m_dim // tm, n_dim // tn)
    return _call(body_sum if one_shot else body, name=name,
                 grid=ij(*tiles) + (1 if one_shot else nk,), in_specs=in_specs,
                 out_specs=pl.BlockSpec((tm, tn), lambda g0, g1, kk: ij(g0, g1)),
                 out_shape=jax.ShapeDtypeStruct((m_dim, n_dim), out_dtype),
                 scratch_shapes=[] if one_shot else [pltpu.VMEM((tm, tn), F32)],
                 semantics=("parallel", "parallel", "arbitrary"))(*flat)


def _row_tile(t):
    return min(512, t)


def _rms_fwd(h, g, name):
    t, d = h.shape
    tt = _row_tile(t)

    def body(h_ref, g_ref, o_ref):
        hv = h_ref[...]
        r = lax.rsqrt(jnp.mean(hv * hv, axis=-1, keepdims=True) + EPS)
        o_ref[...] = (hv * r * g_ref[...]).astype(BF16)

    row = pl.BlockSpec((tt, d), lambda i: (i, 0))
    vec = pl.BlockSpec((1, d), lambda i: (0, 0))
    return _call(body, name=name, grid=(t // tt,), in_specs=[row, vec], out_specs=row,
                 out_shape=jax.ShapeDtypeStruct((t, d), BF16), semantics=("parallel",))(h, g)


def _post_fwd(h, mix, g, name):
    t, d = h.shape
    tt = _row_tile(t)

    def body(h_ref, m_ref, g_ref, o_ref, ob_ref):
        mv = m_ref[...]
        r = lax.rsqrt(jnp.mean(mv * mv, axis=-1, keepdims=True) + EPS)
        h1 = h_ref[...] + mv * r * g_ref[...]
        o_ref[...] = h1
        ob_ref[...] = h1.astype(BF16)

    row = pl.BlockSpec((tt, d), lambda i: (i, 0))
    vec = pl.BlockSpec((1, d), lambda i: (0, 0))
    return _call(body, name=name, grid=(t // tt,), in_specs=[row, row, vec], out_specs=[row, row],
                 out_shape=[jax.ShapeDtypeStruct((t, d), F32), jax.ShapeDtypeStruct((t, d), BF16)],
                 semantics=("parallel",))(h, mix, g)


def _ple_fwd(h1, h1b, wpg, pb, wpe, target, name):
    t, d = h1.shape
    tt = _row_tile(t)
    last = target is not None

    def body(h_ref, hb_ref, wg_ref, p_ref, we_ref, *rest):
        gpre = _dot(hb_ref[...], wg_ref[...], NN)
        e = _dot(p_ref[...], we_ref[...], NN)
        y = h_ref[...] + _sigmoid(gpre) * e
        if last:
            t_ref, g_ref, e_ref, dy_ref, s_ref = rest

            @pl.when(pl.program_id(0) == 0)
            def _():
                s_ref[...] = jnp.zeros_like(s_ref)
            diff = y - t_ref[...]
            dy_ref[...] = diff * (1.0 / d)
            s_ref[...] += jnp.sum(diff * diff, axis=0, keepdims=True)
        else:
            g_ref, e_ref, y_ref = rest
            y_ref[...] = y
        g_ref[...] = gpre
        e_ref[...] = e

    row = pl.BlockSpec((tt, d), lambda i: (i, 0))
    whole = lambda arr: pl.BlockSpec(arr.shape, lambda i: (0, 0))
    rows_f32 = jax.ShapeDtypeStruct((t, d), F32)
    in_specs = [row, row, whole(wpg), pl.BlockSpec((tt, pb.shape[1]), lambda i: (i, 0)), whole(wpe)]
    operands = [h1, h1b, wpg, pb, wpe]
    out_specs, out_shape = [row, row, row], [rows_f32, rows_f32, rows_f32]
    if last:
        in_specs.append(row)
        operands.append(target)
        out_specs.append(pl.BlockSpec((1, d), lambda i: (0, 0)))
        out_shape.append(jax.ShapeDtypeStruct((1, d), F32))
    return _call(body, name=name, grid=(t // tt,), in_specs=in_specs, out_specs=out_specs,
                 out_shape=out_shape, semantics=("arbitrary",))(*operands)


def _ple_bwd(dh2, gpre, e, name):
    t, d = dh2.shape
    tt = _row_tile(t)

    def body(d_ref, g_ref, e_ref, de_ref, dp_ref):
        gate = _sigmoid(g_ref[...])
        dv = d_ref[...]
        de_ref[...] = (dv * gate).astype(BF16)
        dp_ref[...] = (dv * e_ref[...] * gate * (1.0 - gate)).astype(BF16)

    row = pl.BlockSpec((tt, d), lambda i: (i, 0))
    return _call(body, name=name, grid=(t // tt,), in_specs=[row, row, row], out_specs=[row, row],
                 out_shape=[jax.ShapeDtypeStruct((t, d), BF16)] * 2,
                 semantics=("parallel",))(dh2, gpre, e)


def _matmul_rows(pairs, rows_in, vec_in, epilogue, row_dtypes, name, tm):
    n_pairs = len(pairs)
    m_dim = pairs[0][0].shape[0]
    n_dim = pairs[0][1].shape[0]
    tm = min(tm, m_dim)
    in_specs = []
    for a, _, k0 in pairs:
        k_dim = a.shape[1]
        in_specs.append(pl.BlockSpec((tm, k_dim), lambda i: (i, 0)))
        in_specs.append(pl.BlockSpec((n_dim, k_dim), lambda i, kb=k0 // k_dim: (0, kb)))
    row = pl.BlockSpec((tm, n_dim), lambda i: (i, 0))
    vec = pl.BlockSpec((1, n_dim), lambda i: (0, 0))
    n_rows = len(rows_in)

    def body(*refs):
        ops = refs[:2 * n_pairs]
        row_refs = refs[2 * n_pairs:2 * n_pairs + n_rows]
        vec_ref = refs[2 * n_pairs + n_rows]
        outs = refs[2 * n_pairs + n_rows + 1:]
        total = _dot(ops[0][...], ops[1][...], NT)
        for idx in range(1, n_pairs):
            total = total + _dot(ops[2 * idx][...], ops[2 * idx + 1][...], NT)
        results, partial = epilogue(total, [r[...] for r in row_refs], vec_ref[...])
        for out_ref, val in zip(outs[:-1], results):
            out_ref[...] = val.astype(out_ref.dtype)

        @pl.when(pl.program_id(0) == 0)
        def _():
            outs[-1][...] = jnp.zeros_like(outs[-1])
        outs[-1][...] += partial

    flat = [t_ for a, b_, _ in pairs for t_ in (a, b_)]
    return _call(body, name=name, grid=(m_dim // tm,),
                 in_specs=in_specs + [row] * n_rows + [vec],
                 out_specs=[row] * len(row_dtypes) + [vec],
                 out_shape=[jax.ShapeDtypeStruct((m_dim, n_dim), dt) for dt in row_dtypes]
                 + [jax.ShapeDtypeStruct((1, n_dim), F32)],
                 semantics=("arbitrary",))(*flat, *rows_in, vec_in)


def _post_bwd_epilogue(t1, rows, g):
    dh2, mv = rows
    dh1 = dh2 + t1
    r = lax.rsqrt(jnp.mean(mv * mv, axis=-1, keepdims=True) + EPS)
    w = dh1 * g
    dot = jnp.mean(w * mv, axis=-1, keepdims=True)
    dmix = r * w - mv * (r * r * r) * dot
    return (dh1, dmix), jnp.sum(dh1 * mv * r, axis=0, keepdims=True)


def _pre_bwd_epilogue(dhn, rows, g):
    hv, dh1 = rows
    r = lax.rsqrt(jnp.mean(hv * hv, axis=-1, keepdims=True) + EPS)
    w = dhn * g
    dot = jnp.mean(w * hv, axis=-1, keepdims=True)
    return (dh1 + r * w - hv * (r * r * r) * dot,), jnp.sum(dhn * hv * r, axis=0, keepdims=True)


def _split3(v):
    hi = v.astype(BF16)
    r1 = v - hi.astype(F32)
    mid = r1.astype(BF16)
    lo = (r1 - mid.astype(F32)).astype(BF16)
    return hi, mid, lo


def _scan_tile(s):
    return min(256, s)


def _gates_fwd(fl, bf, name):
    b, s, _ = fl.shape

    tb = _scan_tile(s)

    def body(f_ref, b_ref, c_ref):
        dst = lax.broadcasted_iota(jnp.int32, (tb, tb), 0)
        src = lax.broadcasted_iota(jnp.int32, (tb, tb), 1)
        lower = (src <= dst).astype(BF16)
        carry = jnp.zeros((1, LANES), F32)
        for blk_i in range(s // tb):
            rows = slice(blk_i * tb, (blk_i + 1) * tb)
            xv = f_ref[0, rows, :] + b_ref[...]
            lf = jnp.minimum(xv, 0.0) - jnp.log(1.0 + jnp.exp(-jnp.abs(xv)))
            acc = carry
            for part in _split3(lf):
                acc = acc + _dot(lower, part, NN)
            c_ref[0, rows, :] = acc
            carry = acc[tb - 1:tb, :]

    blk = pl.BlockSpec((1, s, LANES), lambda i: (i, 0, 0))
    return _call(body, name=name, grid=(b,),
                 in_specs=[blk, pl.BlockSpec((1, LANES), lambda i: (0, 0))],
                 out_specs=blk, out_shape=jax.ShapeDtypeStruct((b, s, LANES), F32),
                 semantics=("parallel",))(fl, bf)


def _gates_bwd(dc, fl, bf, heads, name):
    b, s, _ = fl.shape

    tb = _scan_tile(s)

    def body(d_ref, f_ref, b_ref, o_ref, db_ref):
        dst = lax.broadcasted_iota(jnp.int32, (tb, tb), 0)
        src = lax.broadcasted_iota(jnp.int32, (tb, tb), 1)
        later = (src >= dst).astype(BF16)
        lane = lax.broadcasted_iota(jnp.int32, (tb, LANES), 1)
        carry = jnp.zeros((1, LANES), F32)
        db = jnp.zeros((1, LANES), F32)
        for blk_i in reversed(range(s // tb)):
            rows = slice(blk_i * tb, (blk_i + 1) * tb)
            dlf = carry
            for part in _split3(d_ref[0, rows, :]):
                dlf = dlf + _dot(later, part, NN)
            carry = dlf[0:1, :]
            xv = f_ref[0, rows, :] + b_ref[...]
            dfl = jnp.where(lane < heads, dlf * _sigmoid(-xv), 0.0)
            o_ref[0, rows, :] = dfl.astype(BF16)
            db = db + jnp.sum(dfl, axis=0, keepdims=True)
        db_ref[0] = db

    blk = pl.BlockSpec((1, s, LANES), lambda i: (i, 0, 0))
    return _call(body, name=name, grid=(b,),
                 in_specs=[blk, blk, pl.BlockSpec((1, LANES), lambda i: (0, 0))],
                 out_specs=[blk, pl.BlockSpec((1, 1, LANES), lambda i: (i, 0, 0))],
                 out_shape=[jax.ShapeDtypeStruct((b, s, LANES), BF16),
                            jax.ShapeDtypeStruct((b, 1, LANES), F32)],
                 semantics=("parallel",))(dc, fl, bf)


LANE_CQ = 64
LANE_CK = 67
LANE_LSE = 70
LANE_D = 64
N_PARTS = 3


def _attn_tiles(s):
    return min(512, s), min(256, s)


def _lanes_in(lane, first):
    return (lane >= first) & (lane < first + N_PARTS)


def _attn_prep_fwd(pa, c, name):
    b, s, a4 = pa.shape
    pairs = a4 // (4 * LANES)
    scale = 1.0 / math.sqrt(HEAD_DIM)

    def body(q_ref, k_ref, v_ref, c_ref, qa_ref, ka_ref, kat_ref, va_ref, vt_ref):
        hp = pl.program_id(1)
        cv = c_ref[0]
        vv = v_ref[0]
        lane = lax.broadcasted_iota(jnp.int32, (s, LANES), 1)
        r128 = lax.broadcasted_iota(jnp.int32, (LANES, LANES), 0)
        c128 = lax.broadcasted_iota(jnp.int32, (LANES, LANES), 1)
        ident = (r128 == c128).astype(BF16)
        for j in range(2):
            head = 2 * hp + j
            move128 = (r128 == c128 + HEAD_DIM * j) & (c128 < HEAD_DIM)
            cparts = _split3(jnp.sum(jnp.where(lane == head, cv, 0.0), axis=1, keepdims=True))
            qa = _dot(q_ref[0], jnp.where(move128, scale, 0.0).astype(BF16), NN)
            ka = _dot(k_ref[0], move128.astype(BF16), NN)
            for i in range(N_PARTS):
                qa = jnp.where(lane == LANE_CQ + i, cparts[i].astype(F32), qa)
                ka = jnp.where(lane == LANE_CK + i, -cparts[i].astype(F32), ka)
            qa = jnp.where(_lanes_in(lane, LANE_CK), 1.0, qa)
            ka = jnp.where(_lanes_in(lane, LANE_CQ) | _lanes_in(lane, LANE_LSE), 1.0, ka)
            va = _dot(vv, move128.astype(BF16), NN) + jnp.where(_lanes_in(lane, LANE_D), 1.0, 0.0)
            kab = ka.astype(BF16)
            qa_ref[0, 0, j] = qa.astype(BF16)
            ka_ref[0, 0, j] = kab
            kat_ref[0, 0, j] = _dot(ident, kab, NT).astype(BF16)
            va_ref[0, 0, j] = va.astype(BF16)
        vt_ref[0, 0] = _dot(ident, vv, NT).astype(BF16)

    col_blk = lambda cidx: pl.BlockSpec((1, s, LANES), lambda bi, hp: (bi, 0, cidx * pairs + hp))
    tok = pl.BlockSpec((1, 1, 2, s, LANES), lambda bi, hp: (bi, hp, 0, 0, 0))
    tok_t = pl.BlockSpec((1, 1, 2, LANES, s), lambda bi, hp: (bi, hp, 0, 0, 0))
    tok_shape = jax.ShapeDtypeStruct((b, pairs, 2, s, LANES), BF16)
    return _call(
        body, name=name, grid=(b, pairs),
        in_specs=[col_blk(0), col_blk(1), col_blk(2),
                  pl.BlockSpec((1, s, LANES), lambda bi, hp: (bi, 0, 0))],
        out_specs=[tok, tok, tok_t, tok,
                   pl.BlockSpec((1, 1, LANES, s), lambda bi, hp: (bi, hp, 0, 0))],
        out_shape=[tok_shape, tok_shape, jax.ShapeDtypeStruct((b, pairs, 2, LANES, s), BF16),
                   tok_shape, jax.ShapeDtypeStruct((b, pairs, LANES, s), BF16)],
        semantics=("parallel", "parallel"))(pa, pa, pa, c)


def _attn_fwd(qa, ka, vt, pa, name):
    b, pairs, _, s, _ = qa.shape
    a = pairs * LANES
    tq = min(1024, s)
    nq = s // tq
    chunk = min(512, tq)

    def body(q_ref, k_ref, vt_ref, z_ref, o_ref, g_ref, lse_ref):
        key_i = lax.broadcasted_iota(jnp.int32, (tq, tq), 0)
        qry_i = lax.broadcasted_iota(jnp.int32, (tq, tq), 1)

        def query_block(c):
            past = tq * c
            heads_out = []
            for j in range(2):
                qv = q_ref[0, 0, j]
                vrows = slice(HEAD_DIM * j, HEAD_DIM * (j + 1))
                sd = _dot(k_ref[0, 0, j, past:past + tq, :], qv, NT)
                sd = jnp.where(key_i <= qry_i, sd, NEG_INF)
                chunks = [slice(k0, k0 + chunk) for k0 in range(0, past, chunk)]
                m = jnp.max(sd, axis=0, keepdims=True)
                for ch in chunks:
                    sp = _dot(k_ref[0, 0, j, ch, :], qv, NT)
                    m = jnp.maximum(m, jnp.max(sp, axis=0, keepdims=True))
                pd = jnp.exp(sd - m)
                l = jnp.sum(pd, axis=0, keepdims=True)
                acc = _dot(vt_ref[0, 0, vrows, past:past + tq], pd.astype(BF16), NN)
                for ch in chunks:
                    pp = jnp.exp(_dot(k_ref[0, 0, j, ch, :], qv, NT) - m)
                    l = l + jnp.sum(pp, axis=0, keepdims=True)
                    acc = acc + _dot(vt_ref[0, 0, vrows, ch], pp.astype(BF16), NN)
                heads_out.append(acc / l)
                lse_ref[0, 0, j:j + 1, :] = m + jnp.log(l)
            ov = jnp.transpose(jnp.concatenate(heads_out, axis=0))
            o_ref[0] = ov.astype(BF16)
            zv = z_ref[0].astype(F32)
            g_ref[0] = (ov * zv * _sigmoid(zv)).astype(BF16)

        for c in range(nq):
            pl.when(pl.program_id(2) == c)(functools.partial(query_block, c))

    return _call(
        body, name=name, grid=(b, pairs, s // tq),
        in_specs=[pl.BlockSpec((1, 1, 2, tq, LANES), lambda bi, hp, qi: (bi, hp, 0, qi, 0)),
                  pl.BlockSpec((1, 1, 2, s, LANES), lambda bi, hp, qi: (bi, hp, 0, 0, 0)),
                  pl.BlockSpec((1, 1, LANES, s), lambda bi, hp, qi: (bi, hp, 0, 0)),
                  pl.BlockSpec((1, tq, LANES), lambda bi, hp, qi: (bi, qi, 3 * pairs + hp))],
        out_specs=[pl.BlockSpec((1, tq, LANES), lambda bi, hp, qi: (bi, qi, hp)),
                   pl.BlockSpec((1, tq, LANES), lambda bi, hp, qi: (bi, qi, hp)),
                   pl.BlockSpec((1, 1, 2, tq), lambda bi, hp, qi: (bi, hp, 0, qi))],
        out_shape=[jax.ShapeDtypeStruct((b, s, a), BF16), jax.ShapeDtypeStruct((b, s, a), BF16),
                   jax.ShapeDtypeStruct((b, pairs, 2, s), F32)],
        semantics=("parallel", "parallel", "arbitrary"))(qa, ka, vt, pa)


def _attn_prep_bwd(dcat, pa, o, lse, qa, name):
    b, pairs, _, s, _ = qa.shape
    a = pairs * LANES
    sub = 16

    def body(da_ref, z_ref, o_ref, lse_ref, qa_ref, qab_ref, doa_ref, dz_ref):
        zv = z_ref[0].astype(F32)
        dav = da_ref[0].astype(F32)
        ov = o_ref[0].astype(F32)
        sg = _sigmoid(zv)
        dov = dav * zv * sg
        dz_ref[0] = (dav * ov * sg * (1.0 + zv * (1.0 - sg))).astype(BF16)
        prod = dov * ov
        dob = dov.astype(BF16)
        lane = lax.broadcasted_iota(jnp.int32, (s, LANES), 1)
        r128 = lax.broadcasted_iota(jnp.int32, (LANES, LANES), 0)
        c128 = lax.broadcasted_iota(jnp.int32, (LANES, LANES), 1)
        prow = lax.broadcasted_iota(jnp.int32, (sub, s), 0)
        srow = lax.broadcasted_iota(jnp.int32, (sub, LANES), 0)
        scol = lax.broadcasted_iota(jnp.int32, (sub, LANES), 1)
        place = ((scol == srow + LANE_LSE) & (srow < N_PARTS)).astype(BF16)
        for j in range(2):
            in_head = (lane >= HEAD_DIM * j) & (lane < HEAD_DIM * (j + 1))
            dparts = _split3(jnp.sum(jnp.where(in_head, prod, 0.0), axis=1, keepdims=True))
            move128 = ((r128 == c128 + HEAD_DIM * j) & (c128 < HEAD_DIM)).astype(BF16)
            doa = _dot(dob, move128, NN)
            for i in range(N_PARTS):
                doa = jnp.where(lane == LANE_D + i, -dparts[i].astype(F32), doa)
            doa_ref[0, 0, j] = doa.astype(BF16)
            lparts = _split3(lse_ref[0, 0, j:j + 1, :])
            pmat = jnp.zeros((sub, s), BF16)
            for i in range(N_PARTS):
                pmat = jnp.where(prow == i, lparts[i], pmat)
            lcol = _dot(pmat, place, TN)
            qab_ref[0, 0, j] = (qa_ref[0, 0, j].astype(F32) - lcol).astype(BF16)

    tok = pl.BlockSpec((1, 1, 2, s, LANES), lambda bi, hp: (bi, hp, 0, 0, 0))
    tok_shape = jax.ShapeDtypeStruct((b, pairs, 2, s, LANES), BF16)
    pair_blk = pl.BlockSpec((1, s, LANES), lambda bi, hp: (bi, 0, hp))
    return _call(
        body, name=name, grid=(b, pairs),
        in_specs=[pair_blk,
                  pl.BlockSpec((1, s, LANES), lambda bi, hp: (bi, 0, 3 * pairs + hp)),
                  pair_blk,
                  pl.BlockSpec((1, 1, 2, s), lambda bi, hp: (bi, hp, 0, 0)),
                  tok],
        out_specs=[tok, tok, pair_blk],
        out_shape=[tok_shape, tok_shape, jax.ShapeDtypeStruct((b, s, a), BF16)],
        semantics=("parallel", "parallel"))(dcat, pa, o, lse, qa)


def _attn_bwd(ka, kat, va, qab, doa, name):
    b, pairs, _, s, _ = ka.shape
    a = pairs * LANES
    tq, tk = _attn_tiles(s)
    ratio = tq // tk
    nq, nk = s // tq, s // tk
    scale = 1.0 / math.sqrt(HEAD_DIM)

    def body(k_ref, kt_ref, v_ref, q_ref, do_ref, dq_ref, dk_ref, dv_ref, dc_ref,
             dqt_acc, dk_s, dv_s):
        key_i = lax.broadcasted_iota(jnp.int32, (tk, tq), 0)
        qry_i = lax.broadcasted_iota(jnp.int32, (tk, tq), 1)
        lane = lax.broadcasted_iota(jnp.int32, (tq, LANES), 1)
        low = lane < HEAD_DIM

        def key_block(kj):
            krows = slice(kj * tk, (kj + 1) * tk)
            q0 = (kj // ratio) * tq
            spans = [(slice(q0, q0 + tq), kj * tk - q0)]
            if q0 + tq < s:
                spans.append((slice(q0 + tq, s), None))
            for j in range(2):
                kb = k_ref[0, 0, j, krows, :]
                vb = v_ref[0, 0, j, krows, :]
                ktb = kt_ref[0, 0, j, :, krows]
                dk = dv = None
                for qrows, diag in spans:
                    qb = q_ref[0, 0, j, qrows, :]
                    dob = do_ref[0, 0, j, qrows, :]
                    pt = jnp.exp(_dot(kb, qb, NT))
                    if diag is not None:
                        pt = jnp.where(key_i + diag <= qry_i, pt, 0.0)
                    dsb = (pt * _dot(vb, dob, NT)).astype(BF16)
                    dv_part = _dot(pt.astype(BF16), dob, NN)
                    dk_part = _dot(dsb, qb, NN)
                    dv = dv_part if dv is None else dv + dv_part
                    dk = dk_part if dk is None else dk + dk_part
                    dq_part = _dot(ktb, dsb, NN)
                    if kj == 0:
                        dqt_acc[j, :, qrows] = dq_part
                    else:
                        dqt_acc[j, :, qrows] += dq_part
                dk_s[j, krows, :] = dk
                dv_s[j, krows, :] = dv

        for kj in range(nk):
            key_block(kj)

        def finish(i, _):
            rows = pl.ds(pl.multiple_of(i * tq, tq), tq)
            dq = [jnp.transpose(dqt_acc[j, :, rows]) for j in range(2)]
            dk = [dk_s[j, rows, :] for j in range(2)]
            dv = [dv_s[j, rows, :] for j in range(2)]
            dcol = [dq[j][:, LANE_CQ:LANE_CQ + 1] - dk[j][:, LANE_CK:LANE_CK + 1] for j in range(2)]
            dq = [dq[j] * scale for j in range(2)]
            for out_ref, val in ((dq_ref, dq), (dk_ref, dk), (dv_ref, dv)):
                merged = jnp.where(low, val[0], pltpu.roll(val[1], HEAD_DIM, 1))
                out_ref[0, rows, :] = merged.astype(BF16)
            hp = pl.program_id(1)
            prev = jnp.where(hp == 0, 0.0, dc_ref[0, rows, :])
            dc_ref[0, rows, :] = jnp.where(lane == 2 * hp, dcol[0],
                                           jnp.where(lane == 2 * hp + 1, dcol[1], prev))
            return 0

        lax.fori_loop(0, nq, finish, 0)

    tok = pl.BlockSpec((1, 1, 2, s, LANES), lambda bi, hp: (bi, hp, 0, 0, 0))
    tok_t = pl.BlockSpec((1, 1, 2, LANES, s), lambda bi, hp: (bi, hp, 0, 0, 0))
    pair_blk = pl.BlockSpec((1, s, LANES), lambda bi, hp: (bi, 0, hp))
    pair_shape = jax.ShapeDtypeStruct((b, s, a), BF16)
    return _call(
        body, name=name, grid=(b, pairs),
        in_specs=[tok, tok_t, tok, tok, tok],
        out_specs=[pair_blk, pair_blk, pair_blk,
                   pl.BlockSpec((1, s, LANES), lambda bi, hp: (bi, 0, 0))],
        out_shape=[pair_shape, pair_shape, pair_shape,
                   jax.ShapeDtypeStruct((b, s, LANES), F32)],
        scratch_shapes=[pltpu.VMEM((2, LANES, s), F32), pltpu.VMEM((2, s, LANES), F32),
                        pltpu.VMEM((2, s, LANES), F32)],
        semantics=("parallel", "arbitrary"))(ka, kat, va, qab, doa)


def _pool_tile(s):
    return min(256, s)


def _band(tb, window, shift):
    tgt = lax.broadcasted_iota(jnp.int32, (tb, tb), 0)
    src = lax.broadcasted_iota(jnp.int32, (tb, tb), 1) + shift
    return ((src <= tgt) & (src > tgt - window)).astype(BF16)


def _band_t(tb, window, shift):
    src = lax.broadcasted_iota(jnp.int32, (tb, tb), 0)
    tgt = lax.broadcasted_iota(jnp.int32, (tb, tb), 1) + shift
    return ((src <= tgt) & (src > tgt - window)).astype(BF16)


def _pool_fwd(pp, w_pool, scale, name):
    b, s, pw2 = pp.shape
    pw = pw2 // 2
    pg = pw // N_POOL_GROUPS
    tb = _pool_tile(s)
    nb = s // tb

    def body(u_ref, z_ref, w_ref, s_ref, o_ref):
        window = 2 << pl.program_id(1)
        band0 = _band(tb, window, 0)
        band1 = _band(tb, window, -tb)
        pos = lax.broadcasted_iota(jnp.int32, (tb, pg), 0)

        def block(i, _):
            rows = pl.ds(pl.multiple_of(i * tb, tb), tb)
            prev = pl.ds(pl.multiple_of(jnp.maximum(i - 1, 0) * tb, tb), tb)
            ub = u_ref[0, rows, :]
            up = u_ref[0, prev, :]
            up = jnp.where(i > 0, up, jnp.zeros_like(up))
            count = jnp.minimum(pos + i * tb + 1, window).astype(F32)
            pooled = (_dot(band0, ub, NN) + _dot(band1, up, NN)) / count - ub.astype(F32)
            mixed = _dot(pooled.astype(BF16), w_ref[0], NN) * s_ref[...]
            zv = z_ref[0, rows, :].astype(F32)
            o_ref[0, rows, :] = (mixed * zv * _sigmoid(zv)).astype(BF16)
            return 0

        lax.fori_loop(0, nb, block, 0)

    return _call(
        body, name=name, grid=(b, N_POOL_GROUPS),
        in_specs=[pl.BlockSpec((1, s, pg), lambda bi, g: (bi, 0, g)),
                  pl.BlockSpec((1, s, pg), lambda bi, g: (bi, 0, N_POOL_GROUPS + g)),
                  pl.BlockSpec((1, pg, pg), lambda bi, g: (g, 0, 0)),
                  pl.BlockSpec((1, pg), lambda bi, g: (0, g))],
        out_specs=pl.BlockSpec((1, s, pg), lambda bi, g: (bi, 0, g)),
        out_shape=jax.ShapeDtypeStruct((b, s, pw), BF16),
        semantics=("parallel", "parallel"))(pp, pp, w_pool, scale)


def _pool_bwd(pp, dcat, w_pool, scale, first_block, name):
    b, s, pw2 = pp.shape
    pw = pw2 // 2
    pg = pw // N_POOL_GROUPS
    tb = _pool_tile(s)
    nb = s // tb

    def body(u_ref, z_ref, d_ref, w_ref, s_ref, du_ref, dz_ref, dw_ref, ds_ref, dpool_s):
        @pl.when(pl.program_id(1) == 0)
        def _():
            dw_ref[...] = jnp.zeros_like(dw_ref)
            ds_ref[...] = jnp.zeros_like(ds_ref)

        window = 2 << pl.program_id(0)
        band0 = _band(tb, window, 0)
        band1 = _band(tb, window, -tb)
        band0_t = _band_t(tb, window, 0)
        band1_t = _band_t(tb, window, tb)
        pos = lax.broadcasted_iota(jnp.int32, (tb, pg), 0)

        def first(i, _):
            rows = pl.ds(pl.multiple_of(i * tb, tb), tb)
            prev = pl.ds(pl.multiple_of(jnp.maximum(i - 1, 0) * tb, tb), tb)
            ub = u_ref[0, rows, :]
            up = u_ref[0, prev, :]
            up = jnp.where(i > 0, up, jnp.zeros_like(up))
            count = jnp.minimum(pos + i * tb + 1, window).astype(F32)
            pooled = ((_dot(band0, ub, NN) + _dot(band1, up, NN)) / count
                      - ub.astype(F32)).astype(BF16)
            mixed = _dot(pooled, w_ref[0], NN)
            pm = mixed * s_ref[...]
            zv = z_ref[0, rows, :].astype(F32)
            sg = _sigmoid(zv)
            dpl = d_ref[0, rows, :].astype(F32)
            dpm = dpl * zv * sg
            dz_ref[0, rows, :] = (dpl * pm * sg * (1.0 + zv * (1.0 - sg))).astype(BF16)
            ds_ref[...] += jnp.sum(dpm * mixed, axis=0, keepdims=True)
            dmixed = (dpm * s_ref[...]).astype(BF16)
            dw_ref[0] += _dot(pooled, dmixed, TN)
            dpool_s[rows, :] = _dot(dmixed, w_ref[0], NT)
            return 0

        lax.fori_loop(0, nb, first, 0)

        def second(i, _):
            rows = pl.ds(pl.multiple_of(i * tb, tb), tb)
            nxt_i = jnp.minimum(i + 1, nb - 1)
            nxt = pl.ds(pl.multiple_of(nxt_i * tb, tb), tb)
            count = jnp.minimum(pos + i * tb + 1, window).astype(F32)
            count_n = jnp.minimum(pos + nxt_i * tb + 1, window).astype(F32)
            dpb = dpool_s[rows, :]
            cur = (dpb / count).astype(BF16)
            nx = dpool_s[nxt, :] / count_n
            nx = jnp.where(i < nb - 1, nx, 0.0).astype(BF16)
            du = _dot(band0_t, cur, NN) + _dot(band1_t, nx, NN) - dpb
            du_ref[0, rows, :] = du.astype(BF16)
            return 0

        lax.fori_loop(0, nb, second, 0)

    return _call(
        body, name=name, grid=(N_POOL_GROUPS, b),
        in_specs=[pl.BlockSpec((1, s, pg), lambda g, bi: (bi, 0, g)),
                  pl.BlockSpec((1, s, pg), lambda g, bi: (bi, 0, N_POOL_GROUPS + g)),
                  pl.BlockSpec((1, s, pg), lambda g, bi: (bi, 0, first_block + g)),
                  pl.BlockSpec((1, pg, pg), lambda g, bi: (g, 0, 0)),
                  pl.BlockSpec((1, pg), lambda g, bi: (0, g))],
        out_specs=[pl.BlockSpec((1, s, pg), lambda g, bi: (bi, 0, g)),
                   pl.BlockSpec((1, s, pg), lambda g, bi: (bi, 0, g)),
                   pl.BlockSpec((1, pg, pg), lambda g, bi: (g, 0, 0)),
                   pl.BlockSpec((1, pg), lambda g, bi: (0, g))],
        out_shape=[jax.ShapeDtypeStruct((b, s, pw), BF16), jax.ShapeDtypeStruct((b, s, pw), BF16),
                   jax.ShapeDtypeStruct((N_POOL_GROUPS, pg, pg), F32),
                   jax.ShapeDtypeStruct((1, pw), F32)],
        scratch_shapes=[pltpu.VMEM((s, pg), F32)],
        semantics=("parallel", "arbitrary"))(pp, pp, dcat, w_pool, scale)


def _adamw(recvs, sent, me, w, m, v, name):
    depth = len(recvs)
    r, c = w.shape[1:]
    tr = min(128, r)
    nb = r // tr
    c1 = 1.0 - ADAM_B1 ** ADAM_STEP
    c2 = 1.0 - ADAM_B2 ** ADAM_STEP
    slotted = sent[0].ndim == 3

    def body(me_ref, *refs):
        recv_refs, own_refs = refs[:depth], refs[depth:2 * depth]
        w_ref, m_ref, v_ref, g_ref, d_ref, nm_ref, nv_ref = refs[2 * depth:]
        me = me_ref[0]
        for layer in range(depth):
            @pl.when(pl.program_id(0) == layer)
            def _(layer=layer):
                own = (own_refs[layer][0] if slotted else own_refs[layer][...]).astype(F32)
                g = jnp.where(me == 0, own, recv_refs[layer][0].astype(F32))
                for sl in range(1, N_DEV):
                    g = g + jnp.where(me == sl, own, recv_refs[layer][sl].astype(F32))
                mn = ADAM_B1 * m_ref[0] + (1.0 - ADAM_B1) * g
                vn = ADAM_B2 * v_ref[0] + (1.0 - ADAM_B2) * (g * g)
                m_hat = mn / c1
                v_hat = vn / c2
                g_ref[0] = g
                d_ref[0] = -ADAM_LR * (m_hat / (jnp.sqrt(v_hat) + ADAM_EPS) + ADAM_WD * w_ref[0])
                nm_ref[0] = mn
                nv_ref[0] = vn

    def blk(layer):
        return lambda l, i: jnp.clip(i + (l - layer) * nb, 0, nb - 1)

    in_specs = [pl.BlockSpec((N_DEV, tr, c), lambda l, i, me_ref, f=blk(layer): (0, f(l, i), 0))
                for layer in range(depth)]
    if slotted:
        in_specs += [pl.BlockSpec((1, tr, c),
                                  lambda l, i, me_ref, f=blk(layer): (me_ref[0], f(l, i), 0))
                     for layer in range(depth)]
    else:
        in_specs += [pl.BlockSpec((tr, c), lambda l, i, me_ref, f=blk(layer): (f(l, i), 0))
                     for layer in range(depth)]
    row = pl.BlockSpec((1, tr, c), lambda l, i, me_ref: (l, i, 0))
    return pl.pallas_call(
        body, name=name, out_shape=[jax.ShapeDtypeStruct((depth, r, c), F32)] * 4,
        grid_spec=pltpu.PrefetchScalarGridSpec(
            num_scalar_prefetch=1, grid=(depth, nb), in_specs=in_specs + [row, row, row],
            out_specs=[row] * 4),
        compiler_params=pltpu.CompilerParams(dimension_semantics=("arbitrary", "arbitrary"),
                                             vmem_limit_bytes=VMEM_LIMIT_BYTES),
    )(me, *recvs, *sent, w, m, v)


def _pack_w_in(gathered, a, heads, pw):
    d = gathered.shape[1]
    w_full = jnp.transpose(gathered, (1, 0, 2)).reshape(d, -1)
    wf = jnp.pad(w_full[:, 4 * a:4 * a + heads], ((0, 0), (0, LANES - heads)))
    return w_full, w_full[:, 4 * a + heads:], wf


def _unpack_dw_in(parts, heads):
    dq, dk, dv, dz, dwf, du, dzp = parts
    d = dq.shape[0]
    full = jnp.concatenate([dq, dk, dv, dz, dwf[:, :heads], du, dzp], axis=1)
    return jnp.transpose(full.reshape(d, N_DEV, -1), (1, 0, 2))


def kernel(x, p, norm_pre, norm_post, w_in, b_f, w_pool, pool_scale, w_out, w_pg, w_pe, loss_target, m_norm_pre, m_norm_post, m_w_in, m_b_f, m_w_pool, m_pool_scale, m_w_out, m_w_pg, m_w_pe, v_norm_pre, v_norm_post, v_w_in, v_b_f, v_w_pool, v_pool_scale, v_w_out, v_w_pg, v_w_pe):
    depth = w_in.shape[0]
    b, s, d = x.shape
    t = b * s
    heads = b_f.shape[1]
    a = heads * HEAD_DIM
    pairs = a // LANES
    pw = pool_scale.shape[1]
    pg = pw // N_POOL_GROUPS
    ple = p.shape[-1]
    mix_w = a + pw

    me = 4 * lax.axis_index("x") + 2 * lax.axis_index("y") + lax.axis_index("c")
    shard = {
        "w_in": [w_in[i].astype(BF16) for i in range(depth)],
        "w_pool": [w_pool[i].reshape(N_POOL_GROUPS * (pg // N_DEV), pg).astype(BF16)
                   for i in range(depth)],
        "w_out": [w_out[i].astype(BF16) for i in range(depth)],
        "w_pg": [w_pg[i].astype(BF16) for i in range(depth)],
        "w_pe": [w_pe[i].astype(BF16) for i in range(depth)],
    }
    names = list(shard)
    rest = names[1:]

    def unpack_rest(lands, layer, which):
        g = {nm: _with_own(ld, shard[nm][layer], me) for nm, ld in zip(which, lands)}
        g_pool = g["w_pool"].reshape(N_DEV, N_POOL_GROUPS, pg // N_DEV, pg)
        return dict(wpool=jnp.transpose(g_pool, (1, 0, 2, 3)).reshape(N_POOL_GROUPS, pg, pg),
                    wout=g["w_out"].reshape(mix_w, d), wpg=g["w_pg"].reshape(d, d),
                    wpe=jnp.transpose(g["w_pe"], (1, 0, 2)).reshape(ple, d))

    g_in0 = _gather_two_level(shard["w_in"][0], "gather_w_in0")
    rest0, tok_rest0 = _exchange_start([(shard[nm][0], False) for nm in rest], g_in0,
                                       "gather_rest0_start")
    later, tok = [], tok_rest0
    for i in range(1, depth):
        hdl, tk_i = _exchange_start([(shard[nm][i], False) for nm in names], g_in0,
                                    "gather_layer%d_start" % i)
        later.append(hdl)
        tok = tok + tk_i

    h = x.reshape(t, d)
    saved = []
    layers = []
    for i in range(depth):
        sv = dict(h=h)
        g_pre = norm_pre[i:i + 1]
        g_post = norm_post[i:i + 1]
        bf = jnp.pad(b_f[i:i + 1], ((0, 0), (0, LANES - heads)))
        scale = pool_scale[i:i + 1]
        if i == 0:
            lw = dict(zip(("wa", "wp", "wf"), _pack_w_in(g_in0, a, heads, pw)))
            g_pre = g_pre + tok
        else:
            lands = _exchange_wait(later[i - 1], h, "gather_layer%d_wait" % i)
            g_in = _with_own(lands[0], shard["w_in"][i], me)
            lw = dict(zip(("wa", "wp", "wf"), _pack_w_in(g_in, a, heads, pw)))
            lw.update(unpack_rest(lands[1:], i, rest))
        hn = _rms_fwd(h, g_pre, "rms_pre")
        pa = _matmul([(hn, lw["wa"])], "nn", BF16, "proj_attn", n_dim=4 * a, tn=2048,
                     n_outer=True).reshape(b, s, 4 * a)
        pp = _matmul([(hn, lw["wp"])], "nn", BF16, "proj_pool", tn=2048,
                     n_outer=True).reshape(b, s, 2 * pw)
        fl = _matmul([(hn, lw["wf"])], "nn", F32, "proj_gate").reshape(b, s, LANES)
        c = _gates_fwd(fl, bf, "gates_fwd")
        qa, ka, kat, va, vt = _attn_prep_fwd(pa, c, "attn_prep_fwd")
        o, ga, lse = _attn_fwd(qa, ka, vt, pa, "attn_fwd")
        if i == 0:
            lw.update(unpack_rest(_exchange_wait(rest0, lse, "gather_rest0_wait"), 0, rest))
        layers.append(lw)
        gp = _pool_fwd(pp, lw["wpool"], scale, "pool_fwd")
        ga2 = ga.reshape(t, a)
        gp2 = gp.reshape(t, pw)
        mix = _matmul([(ga2, lw["wout"], 0, 0), (gp2, lw["wout"], a, 0)], "nn", F32, "mix_out")
        h1, h1b = _post_fwd(h, mix, g_post, "post_fwd")
        pb = p[i].reshape(t, ple).astype(BF16)
        if i < depth - 1:
            gpre, e, h = _ple_fwd(h1, h1b, lw["wpg"], pb, lw["wpe"], None, "ple_fwd")
        else:
            gpre, e, dh, sq = _ple_fwd(h1, h1b, lw["wpg"], pb, lw["wpe"],
                                       loss_target.reshape(t, d), "ple_loss")
        sv.update(hn=hn, pa=pa, pp=pp, fl=fl, bf=bf, qa=qa, ka=ka, kat=kat, va=va, o=o, lse=lse, ga=ga2,
                  gp=gp2, mix=mix,
                  h1b=h1b, pb=pb, gpre=gpre, e=e, g_pre=g_pre, g_post=g_post, scale=scale)
        saved.append(sv)

    loss = lax.psum(0.5 * jnp.sum(sq) / d, MESH_AXES)

    big = {nm: [None] * depth for nm in names}
    small = {nm: [None] * depth for nm in ("norm_pre", "norm_post", "b_f", "pool_scale")}
    grad_handles = [None] * depth
    rest_handles = [None] * depth
    for i in reversed(range(depth)):
        lw, sv = layers[i], saved[i]
        de, dpre = _ple_bwd(dh, sv["gpre"], sv["e"], "ple_bwd")
        dwpe = _matmul([(sv["pb"], de)], "tn", BF16, "dw_pe", tm=1024)
        dwpg = _matmul([(sv["h1b"], dpre)], "tn", BF16, "dw_pg", tm=1024)
        dh1, dmix, dg_post = _matmul_rows(
            [(dpre, lw["wpg"], 0)], [dh, sv["mix"]], sv["g_post"], _post_bwd_epilogue,
            (F32, BF16), "d_h1_post_bwd", tm=512)
        dwout = jnp.concatenate(
            [_matmul([(sv["ga"], dmix)], "tn", BF16, "dw_out_attn", tm=1024),
             _matmul([(sv["gp"], dmix)], "tn", BF16, "dw_out_pool", tm=1024)], axis=0)
        dcat = _matmul([(dmix, lw["wout"])], "nt", BF16, "d_cat", tn=2048).reshape(b, s, mix_w)
        du, dzp, dwpool, dscale = _pool_bwd(sv["pp"], dcat, lw["wpool"], sv["scale"], a // pg,
                                            "pool_bwd")
        big["w_pool"][i] = jnp.transpose(
            dwpool.astype(BF16).reshape(N_POOL_GROUPS, N_DEV, pg // N_DEV, pg), (1, 0, 2, 3)
        ).reshape(N_DEV, N_POOL_GROUPS * (pg // N_DEV), pg)
        big["w_out"][i] = dwout.reshape(N_DEV, mix_w // N_DEV, d)
        big["w_pg"][i] = dwpg.reshape(N_DEV, d // N_DEV, d)
        big["w_pe"][i] = jnp.transpose(dwpe.reshape(ple, N_DEV, d // N_DEV), (1, 0, 2))
        rest_handles[i], tok = _exchange_start([(big[nm][i], True) for nm in rest], du,
                                               "grads_rest%d_start" % i)
        qab, doa, dz = _attn_prep_bwd(dcat, sv["pa"], sv["o"], sv["lse"] + tok, sv["qa"],
                                      "attn_prep_bwd")
        dq, dk, dv, dc = _attn_bwd(sv["ka"], sv["kat"], sv["va"], qab, doa, "attn_bwd")
        dfl, dbf = _gates_bwd(dc, sv["fl"], sv["bf"], heads, "gates_bwd")
        dproj = [g_.reshape(t, -1) for g_ in (dq, dk, dv, dz, dfl, du, dzp)]
        dw_parts = [_matmul([(sv["hn"], g_)], "tn", BF16, "dw_in_%d" % n_, tm=1024)
                    for n_, g_ in enumerate(dproj)]

        big["w_in"][i] = _unpack_dw_in(dw_parts, heads)
        grad_handles[i], tok = _exchange_start([(big["w_in"][i], True)], dw_parts[-1],
                                               "grads_w_in%d_start" % i)

        dq2, dk2, dv2, dz2, dfl2, du2, dzp2 = dproj
        dh, dg_pre = _matmul_rows(
            [(dq2, lw["wa"], 0), (dk2, lw["wa"], a), (dv2, lw["wa"], 2 * a), (dz2, lw["wa"], 3 * a),
             (du2, lw["wp"], 0), (dzp2, lw["wp"], pw), (dfl2, lw["wf"] + tok.astype(BF16), 0)],
            [sv["h"], dh1], sv["g_pre"] + tok, _pre_bwd_epilogue, (F32,), "d_hn_pre_bwd", tm=256)
        small["norm_pre"][i] = dg_pre
        small["norm_post"][i] = dg_post
        small["b_f"][i] = jnp.sum(dbf, axis=0)
        small["pool_scale"][i] = dscale
    grad_x = dh.reshape(b, s, d)

    width = max(d, pw)
    small_names = ("norm_pre", "norm_post", "pool_scale", "b_f")

    def small_rows(get):
        rows = []
        for nm in small_names:
            for i in range(depth):
                v_ = get(nm, i)
                rows.append(jnp.pad(v_, ((0, 0), (0, width - v_.shape[1]))))
        return jnp.concatenate(rows, axis=0)

    small_g = small_rows(lambda nm, i: small[nm][i])
    me1 = jnp.reshape(me, (1,)).astype(jnp.int32)

    weights = dict(norm_pre=norm_pre, norm_post=norm_post, w_in=w_in, b_f=b_f, w_pool=w_pool,
                   pool_scale=pool_scale, w_out=w_out, w_pg=w_pg, w_pe=w_pe)
    mom1 = dict(norm_pre=m_norm_pre, norm_post=m_norm_post, w_in=m_w_in, b_f=m_b_f, w_pool=m_w_pool,
                pool_scale=m_pool_scale, w_out=m_w_out, w_pg=m_w_pg, w_pe=m_w_pe)
    mom2 = dict(norm_pre=v_norm_pre, norm_post=v_norm_post, w_in=v_w_in, b_f=v_b_f, w_pool=v_w_pool,
                pool_scale=v_pool_scale, w_out=v_w_out, w_pg=v_w_pg, w_pe=v_w_pe)

    results = {}

    def update(nm, recvs):
        shp = weights[nm].shape
        sent = [big[nm][i] for i in range(depth)]
        flat = lambda arr: arr.reshape((depth,) + sent[0].shape[1:])
        outs = _adamw(recvs, sent, me1, flat(weights[nm]), flat(mom1[nm]), flat(mom2[nm]),
                      "adamw_" + nm)
        results[nm] = [o_.reshape(shp) for o_ in outs]
        return outs[0]

    got_rest = [_exchange_wait(rest_handles[i], dh, "grads_rest%d_wait" % i) for i in range(depth)]
    for j, nm in enumerate(rest):
        last = update(nm, [got_rest[i][j] for i in range(depth)])

    small_g, last = lax.optimization_barrier((small_g, last))
    (small_recv,) = _exchange([(small_g, False)], "exchange_small")
    small_w = small_rows(lambda nm, i: weights[nm][i:i + 1])[None]
    small_m = small_rows(lambda nm, i: mom1[nm][i:i + 1])[None]
    small_v = small_rows(lambda nm, i: mom2[nm][i:i + 1])[None]
    outs = _adamw([small_recv], [small_g], me1, small_w, small_m, small_v, "adamw_small")
    for j, nm in enumerate(small_names):
        cols = weights[nm].shape[1]
        results[nm] = [o_[0, j * depth:(j + 1) * depth, :cols] for o_ in outs]

    got_w_in = [_exchange_wait(grad_handles[i], last + outs[0][0, 0, 0], "grads_w_in%d_wait" % i)[0]
                for i in range(depth)]
    update("w_in", got_w_in)

    order = ("norm_pre", "norm_post", "w_in", "b_f", "w_pool", "pool_scale", "w_out", "w_pg", "w_pe")
    return (loss, grad_x, *[results[nm][0] for nm in order], *[results[nm][1] for nm in order],
            *[results[nm][2] for nm in order], *[results[nm][3] for nm in order])
```
